```python
import math
import jax, jax.numpy as jnp
from jax import lax
import numpy as np

D_MODEL = 1024
BATCH = 4
SEQ = 4096
DEPTH = 1

N_META = 16
HEAD_DIM = 64
SWA_Q_HEADS = D_MODEL // (2 * HEAD_DIM)
SWA_KV_HEADS = max(SWA_Q_HEADS // 4, 1)
SWA_GROUP = SWA_Q_HEADS // SWA_KV_HEADS
FOX_HEADS = D_MODEL // (2 * HEAD_DIM)
SWA_Q_W = SWA_Q_HEADS * HEAD_DIM
SWA_KV_W = SWA_KV_HEADS * HEAD_DIM
FOX_W = FOX_HEADS * HEAD_DIM
D_MIX = SWA_Q_W + FOX_W
OFF_QA = SWA_Q_W
OFF_KA = OFF_QA + SWA_KV_W
OFF_VA = OFF_KA + SWA_KV_W
OFF_QB = OFF_VA + FOX_W
OFF_KB = OFF_QB + FOX_W
OFF_VB = OFF_KB + FOX_W
D_PROJ = OFF_VB + FOX_HEADS
WINDOW = 128
BLOCK = 128
N_BUCKETS = 32
MAX_DISTANCE = 128
D_FF = -(-8 * D_MODEL // (3 * 256)) * 256
EPS = 1e-6
NEG_INF = -1e30

kernel_name = "hymba_swa_sink_fox_t5bias_sandwich"


def rms_norm(x, g):
    xf = x.astype(jnp.float32)
    y = xf * lax.rsqrt(jnp.mean(xf * xf, axis=-1, keepdims=True) + EPS)
    return (y * g.astype(jnp.float32)).astype(x.dtype)


def t5_bucket(dist):
    n = jnp.maximum(dist, 0).astype(jnp.int32)
    max_exact = N_BUCKETS // 2
    nf = jnp.maximum(n, 1).astype(jnp.float32)
    large = max_exact + (jnp.log(nf / max_exact) / math.log(MAX_DISTANCE / max_exact)
                         * (N_BUCKETS - max_exact)).astype(jnp.int32)
    large = jnp.minimum(large, N_BUCKETS - 1)
    return jnp.where(n < max_exact, n, large)


def softmax_with_sink(s, sink):
    sink_r = sink.reshape((1,) + sink.shape + (1,) * (s.ndim - 3))
    col = jnp.broadcast_to(sink_r, s.shape[:-1] + (1,))
    p = jax.nn.softmax(jnp.concatenate([s, col], axis=-1), axis=-1)
    return p[..., :-1]


def swa_sink_attention(q, k, v, sinks, rel_bias):
    B, L = q.shape[0], q.shape[1]
    n_blk = (L - N_META) // BLOCK
    scale = HEAD_DIM ** -0.5
    sink = sinks.astype(jnp.float32).reshape(SWA_KV_HEADS, SWA_GROUP)
    tab = rel_bias.astype(jnp.float32)
    mi = jnp.arange(N_META)
    km, vm = k[:, :N_META], v[:, :N_META]

    qm = q[:, :N_META].reshape(B, N_META, SWA_KV_HEADS, SWA_GROUP, HEAD_DIM)
    d_mm = mi[:, None] - mi[None, :]
    b_mm = tab[t5_bucket(d_mm)].transpose(2, 0, 1).reshape(SWA_KV_HEADS, SWA_GROUP, N_META, N_META)
    s_mm = jnp.einsum('bqhgd,bkhd->bhgqk', qm, km, preferred_element_type=jnp.float32) * scale + b_mm
    s_mm = jnp.where(d_mm >= 0, s_mm, NEG_INF)
    p_mm = softmax_with_sink(s_mm, sink).astype(v.dtype)
    o_meta = jnp.einsum('bhgqk,bkhd->bqhgd', p_mm, vm).reshape(B, N_META, SWA_Q_HEADS, HEAD_DIM)

    qr = q[:, N_META:].reshape(B, n_blk, BLOCK, SWA_KV_HEADS, SWA_GROUP, HEAD_DIM)
    kr = k[:, N_META:].reshape(B, n_blk, BLOCK, SWA_KV_HEADS, HEAD_DIM)
    vr = v[:, N_META:].reshape(B, n_blk, BLOCK, SWA_KV_HEADS, HEAD_DIM)

    def with_prev(t):
        prev = jnp.pad(t, ((0, 0), (1, 0), (0, 0), (0, 0), (0, 0)))[:, :-1]
        return jnp.concatenate([prev, t], axis=2)

    kw, vw = with_prev(kr), with_prev(vr)
    qi = jnp.arange(BLOCK)[:, None]
    ki = jnp.arange(2 * BLOCK)[None, :]
    d_w = qi + BLOCK - ki
    b_w = tab[t5_bucket(d_w)].transpose(2, 0, 1).reshape(SWA_KV_HEADS, SWA_GROUP, 1, BLOCK, 2 * BLOCK)
    blk = jnp.arange(n_blk)[:, None, None]
    valid_w = ((d_w >= 0) & (d_w < WINDOW))[None] & ((blk > 0) | (ki >= BLOCK)[None])
    s_w = jnp.einsum('bnqhgd,bnkhd->bhgnqk', qr, kw, preferred_element_type=jnp.float32) * scale + b_w
    s_w = jnp.where(valid_w, s_w, NEG_INF)
    d_m = N_META + blk * BLOCK + qi[None] - mi[None, None, :]
    b_m = tab[t5_bucket(d_m)].transpose(3, 0, 1, 2).reshape(SWA_KV_HEADS, SWA_GROUP, n_blk, BLOCK, N_META)
    s_m = jnp.einsum('bnqhgd,bmhd->bhgnqm', qr, km, preferred_element_type=jnp.float32) * scale + b_m
    p = softmax_with_sink(jnp.concatenate([s_m, s_w], axis=-1), sink).astype(v.dtype)
    o = (jnp.einsum('bhgnqm,bmhd->bnqhgd', p[..., :N_META], vm)
         + jnp.einsum('bhgnqk,bnkhd->bnqhgd', p[..., N_META:], vw))
    o_real = o.reshape(B, n_blk * BLOCK, SWA_Q_HEADS, HEAD_DIM)
    return jnp.concatenate([o_meta, o_real], axis=1)


def forgetting_attention(q, k, v, f_logit):
    B, L = q.shape[0], q.shape[1]
    n_blk = (L - N_META) // BLOCK
    scale = HEAD_DIM ** -0.5
    cum = jnp.cumsum(jax.nn.log_sigmoid(f_logit.astype(jnp.float32)), axis=1).transpose(0, 2, 1)

    mi = jnp.arange(N_META)
    s = jnp.einsum('bqhd,bkhd->bhqk', q[:, :N_META], k[:, :N_META], preferred_element_type=jnp.float32) * scale
    s = s + cum[:, :, :N_META, None] - cum[:, :, None, :N_META]
    s = jnp.where(mi[:, None] >= mi[None, :], s, NEG_INF)
    p = jax.nn.softmax(s, axis=-1).astype(v.dtype)
    o_meta = jnp.einsum('bhqk,bkhd->bqhd', p, v[:, :N_META])

    qb = q[:, N_META:].reshape(B, n_blk, BLOCK, FOX_HEADS, HEAD_DIM).transpose(1, 0, 2, 3, 4)
    cb = cum[:, :, N_META:].reshape(B, FOX_HEADS, n_blk, BLOCK).transpose(2, 0, 1, 3)
    k_pos = jnp.arange(L)

    def one_block(args):
        q_blk, c_blk, b = args
        q_pos = N_META + b * BLOCK + jnp.arange(BLOCK)
        sb = jnp.einsum('bqhd,bkhd->bhqk', q_blk, k, preferred_element_type=jnp.float32) * scale
        sb = sb + c_blk[..., None] - cum[:, :, None, :]
        sb = jnp.where(k_pos[None, :] <= q_pos[:, None], sb, NEG_INF)
        pb = jax.nn.softmax(sb, axis=-1).astype(v.dtype)
        return jnp.einsum('bhqk,bkhd->bqhd', pb, v)

    o = lax.map(one_block, (qb, cb, jnp.arange(n_blk)))
    o_real = o.transpose(1, 0, 2, 3, 4).reshape(B, n_blk * BLOCK, FOX_HEADS, HEAD_DIM)
    return jnp.concatenate([o_meta, o_real], axis=1)


def setup_inputs(seed: int = 0) -> dict:
    key = jax.random.key(seed)
    ks = jax.random.split(key, 14)
    f32 = jnp.float32
    gain = lambda k: 1.0 + 0.05 * jax.random.normal(k, (DEPTH, D_MODEL), f32)
    return {
        "x": jax.random.normal(ks[0], (BATCH, SEQ, D_MODEL), f32),
        "meta_tokens": jax.random.normal(ks[1], (N_META, D_MODEL), f32),
        "rel_bias": 0.5 * jax.random.normal(ks[2], (N_BUCKETS, SWA_Q_HEADS), f32),
        "ln_pre_mix": gain(ks[3]),
        "ln_post_mix": gain(ks[4]),
        "ln_pre_ffn": gain(ks[5]),
        "ln_post_ffn": gain(ks[6]),
        "w_in": jax.random.normal(ks[7], (DEPTH, D_MODEL, D_PROJ), f32) * D_MODEL ** -0.5,
        "b_forget": jax.random.uniform(ks[8], (DEPTH, FOX_HEADS), f32, minval=1.0, maxval=5.0),
        "sinks": 0.5 * jax.random.normal(ks[9], (DEPTH, SWA_Q_HEADS), f32),
        "w_out": jax.random.normal(ks[10], (DEPTH, D_MIX, D_MODEL), f32) * D_MIX ** -0.5,
        "w_gate_up": jax.random.normal(ks[11], (DEPTH, D_MODEL, 2 * D_FF), f32) * D_MODEL ** -0.5,
        "w_down": jax.random.normal(ks[12], (DEPTH, D_FF, D_MODEL), f32) * D_FF ** -0.5,
    }


def reference(x, meta_tokens, rel_bias, ln_pre_mix, ln_post_mix, ln_pre_ffn, ln_post_ffn,
              w_in, b_forget, sinks, w_out, w_gate_up, w_down):
    B = x.shape[0]
    meta = jnp.broadcast_to(meta_tokens[None].astype(x.dtype), (B, N_META, D_MODEL))
    h = jnp.concatenate([meta, x], axis=1)
    L = h.shape[1]
    for layer in range(DEPTH):
        hn = rms_norm(h, ln_pre_mix[layer])
        proj = jnp.einsum('bld,dc->blc', hn, w_in[layer])
        q_a = proj[..., :OFF_QA].reshape(B, L, SWA_Q_HEADS, HEAD_DIM)
        k_a = proj[..., OFF_QA:OFF_KA].reshape(B, L, SWA_KV_HEADS, HEAD_DIM)
        v_a = proj[..., OFF_KA:OFF_VA].reshape(B, L, SWA_KV_HEADS, HEAD_DIM)
        q_b = proj[..., OFF_VA:OFF_QB].reshape(B, L, FOX_HEADS, HEAD_DIM)
        k_b = proj[..., OFF_QB:OFF_KB].reshape(B, L, FOX_HEADS, HEAD_DIM)
        v_b = proj[..., OFF_KB:OFF_VB].reshape(B, L, FOX_HEADS, HEAD_DIM)
        f_b = proj[..., OFF_VB:] + b_forget[layer].astype(proj.dtype)
        o_a = swa_sink_attention(q_a, k_a, v_a, sinks[layer], rel_bias)
        o_b = forgetting_attention(q_b, k_b, v_b, f_b)
        mix = jnp.concatenate([o_a.reshape(B, L, SWA_Q_W), o_b.reshape(B, L, FOX_W)], axis=-1)
        h = h + rms_norm(jnp.einsum('blc,cd->bld', mix, w_out[layer]), ln_post_mix[layer])
        hn = rms_norm(h, ln_pre_ffn[layer])
        gu = jnp.einsum('bld,df->blf', hn, w_gate_up[layer])
        ff = jnp.einsum('blf,fd->bld', jax.nn.silu(gu[..., :D_FF]) * gu[..., D_FF:], w_down[layer])
        h = h + rms_norm(ff, ln_post_ffn[layer])
    return h[:, N_META:]
```

```python
import functools
import math

import numpy as np
import jax
import jax.numpy as jnp
from jax import lax
from jax.experimental import pallas as pl
from jax.experimental.pallas import tpu as pltpu

D_MODEL = 1024
N_META = 16
HEAD_DIM = 64
SWA_Q_HEADS = 8
SWA_KV_HEADS = 2
SWA_GROUP = SWA_Q_HEADS // SWA_KV_HEADS
FOX_HEADS = 8
SWA_Q_W = SWA_Q_HEADS * HEAD_DIM
SWA_KV_W = SWA_KV_HEADS * HEAD_DIM
FOX_W = FOX_HEADS * HEAD_DIM
OFF_QA = SWA_Q_W
OFF_KA = OFF_QA + SWA_KV_W
OFF_VA = OFF_KA + SWA_KV_W
OFF_QB = OFF_VA + FOX_W
OFF_KB = OFF_QB + FOX_W
OFF_VB = OFF_KB + FOX_W
WINDOW = 128
BLOCK = 128
N_BUCKETS = 32
MAX_DISTANCE = 128
D_FF = 2816
EPS = 1e-6
NEG_INF = -1e30
SCALE = HEAD_DIM ** -0.5

LANES = 128
PAIR_W = 2 * HEAD_DIM
VMEM_LIMIT = 56 * 1024 * 1024

IN_TM = 512
FOX_T = 256
FFN_TM = 256

F32 = jnp.float32
BF16 = jnp.bfloat16


def _dot(a, b):
    return jnp.dot(a, b, preferred_element_type=F32)


def _dot_nt(a, b):
    return lax.dot_general(a, b, (((1,), (1,)), ((), ())), preferred_element_type=F32)


def _rms(t, g):
    return t * lax.rsqrt(jnp.mean(t * t, axis=-1, keepdims=True) + EPS) * g


def _t5_bucket_np(dist):
    n = np.maximum(dist, 0).astype(np.int32)
    max_exact = N_BUCKETS // 2
    nf = np.maximum(n, 1).astype(np.float32)
    large = max_exact + (np.log(nf / np.float32(max_exact)) / np.float32(math.log(MAX_DISTANCE / max_exact))
                         * np.float32(N_BUCKETS - max_exact)).astype(np.int32)
    large = np.minimum(large, N_BUCKETS - 1)
    return np.where(n < max_exact, n, large).astype(np.int32)


def _bias_kernel(tab_ref, bkt_w_ref, valid_w_ref, bkt_m_ref, bw_ref, bm_ref):
    h = pl.program_id(0)
    bkt_w = bkt_w_ref[...]
    bkt_m = bkt_m_ref[...]
    acc_w = jnp.zeros(bkt_w.shape, F32)
    acc_m = jnp.zeros(bkt_m.shape, F32)
    for b in range(N_BUCKETS):
        t = tab_ref[b, h]
        acc_w = jnp.where(bkt_w == b, t, acc_w)
        acc_m = jnp.where(bkt_m == b, t, acc_m)
    bw_ref[...] = jnp.where(valid_w_ref[...] > 0, acc_w, NEG_INF)
    bm_ref[...] = acc_m


def _bias_tables(rel_bias):
    qi = np.arange(BLOCK)[:, None]
    ki = np.arange(2 * BLOCK)[None, :]
    d_w = qi + BLOCK - ki
    bkt_w = _t5_bucket_np(d_w)
    valid_w = ((d_w >= 0) & (d_w < WINDOW)).astype(np.int32)
    mi = np.arange(N_META)[None, :]
    bkt_m0 = _t5_bucket_np(N_META + qi - mi)
    far = _t5_bucket_np(np.asarray([[N_META + BLOCK - (N_META - 1)]]))
    assert far[0, 0] == N_BUCKETS - 1, "meta keys of later blocks must share the last bucket"
    bkt_m = np.stack([bkt_m0, np.full_like(bkt_m0, far[0, 0])]).reshape(2 * BLOCK, N_META)
    return pl.pallas_call(
        _bias_kernel,
        grid=(SWA_Q_HEADS,),
        in_specs=[
            pl.BlockSpec(memory_space=pltpu.SMEM),
            pl.BlockSpec((BLOCK, 2 * BLOCK), lambda h: (0, 0)),
            pl.BlockSpec((BLOCK, 2 * BLOCK), lambda h: (0, 0)),
            pl.BlockSpec((2 * BLOCK, N_META), lambda h: (0, 0)),
        ],
        out_specs=[
            pl.BlockSpec((None, BLOCK, 2 * BLOCK), lambda h: (h, 0, 0)),
            pl.BlockSpec((None, 2 * BLOCK, N_META), lambda h: (h, 0, 0)),
        ],
        out_shape=[
            jax.ShapeDtypeStruct((SWA_Q_HEADS, BLOCK, 2 * BLOCK), F32),
            jax.ShapeDtypeStruct((SWA_Q_HEADS, 2 * BLOCK, N_META), F32),
        ],
        name="bias_tables",
    )(rel_bias.astype(F32), jnp.asarray(bkt_w), jnp.asarray(valid_w), jnp.asarray(bkt_m))


def _inproj_kernel(x_ref, g_ref, w_ref, wf_ref, bf_ref, tri_ref, c0_ref, proj_ref, cum_ref, carry_ref):
    @pl.when(pl.program_id(1) == 0)
    def _():
        carry_ref[...] = c0_ref[...]

    y = _rms(x_ref[...], g_ref[...]).astype(BF16)
    acc = _dot(y, w_ref[...])
    proj_ref[:, 0:OFF_QA] = (acc[:, 0:OFF_QA] * SCALE).astype(BF16)
    proj_ref[:, OFF_QA:OFF_VA] = acc[:, OFF_QA:OFF_VA].astype(BF16)
    proj_ref[:, OFF_VA:OFF_QB] = (acc[:, OFF_VA:OFF_QB] * SCALE).astype(BF16)
    proj_ref[:, OFF_QB:OFF_VB] = acc[:, OFF_QB:OFF_VB].astype(BF16)

    f = _dot(y, wf_ref[...]) + bf_ref[...]
    ls = jnp.minimum(f, 0.0) - jnp.log1p(jnp.exp(-jnp.abs(f)))
    hi = ls.astype(BF16)
    r1 = ls - hi.astype(F32)
    mid = r1.astype(BF16)
    lo = (r1 - mid.astype(F32)).astype(BF16)
    tri = tri_ref[...]
    cum = _dot(tri, hi) + _dot(tri, mid) + _dot(tri, lo) + carry_ref[...]
    cum_ref[...] = cum[:, :FOX_HEADS]
    tm = cum.shape[0]
    carry_ref[...] = cum[tm - 1:tm, :]


def _inproj(x3, g, w, wf, bf, c0, tm):
    nb, rows, _ = x3.shape
    tri = jnp.asarray(np.tril(np.ones((tm, tm), np.float32)), BF16)
    const = lambda b, t: (0, 0)
    return pl.pallas_call(
        _inproj_kernel,
        grid=(nb, rows // tm),
        in_specs=[
            pl.BlockSpec((None, tm, D_MODEL), lambda b, t: (b, t, 0)),
            pl.BlockSpec((1, D_MODEL), const),
            pl.BlockSpec((D_MODEL, OFF_VB), const),
            pl.BlockSpec((D_MODEL, LANES), const),
            pl.BlockSpec((1, LANES), const),
            pl.BlockSpec((tm, tm), const),
            pl.BlockSpec((1, LANES), const),
        ],
        out_specs=[
            pl.BlockSpec((None, tm, OFF_VB), lambda b, t: (b, t, 0)),
            pl.BlockSpec((None, tm, FOX_HEADS), lambda b, t: (b, t, 0)),
        ],
        out_shape=[
            jax.ShapeDtypeStruct((nb, rows, OFF_VB), BF16),
            jax.ShapeDtypeStruct((nb, rows, FOX_HEADS), F32),
        ],
        scratch_shapes=[pltpu.VMEM((1, LANES), F32)],
        compiler_params=pltpu.CompilerParams(
            dimension_semantics=("arbitrary", "arbitrary"), vmem_limit_bytes=VMEM_LIMIT),
        name="inproj",
    )(x3, g, w, wf, bf, tri, c0)


def _swa_kernel(sink_ref, q_ref, kc_ref, kp_ref, vc_ref, vp_ref, km_ref, vm_ref, bw_ref, bm_ref, o_ref):
    n = pl.program_id(1)
    lane = lax.broadcasted_iota(jnp.int32, (1, LANES), 1)
    half = [lane < HEAD_DIM, lane >= HEAD_DIM]

    def swapped(t):
        return pltpu.roll(t.astype(F32), HEAD_DIM, 1).astype(BF16)

    kw = jnp.concatenate([kp_ref[...], kc_ref[...]], axis=0)
    vw = jnp.concatenate([vp_ref[...], vc_ref[...]], axis=0)
    km = km_ref[...]
    vm = vm_ref[...]
    keys = [(kw, km), (swapped(kw), swapped(km))]
    vals = [(vw, vm), (swapped(vw), swapped(vm))]
    col = lax.broadcasted_iota(jnp.int32, (1, 2 * BLOCK), 1)
    no_prev = jnp.where((col < BLOCK) & (n == 0), NEG_INF, 0.0)

    for pair in range(SWA_Q_HEADS // 2):
        qpair = q_ref[:, pair * PAIR_W:(pair + 1) * PAIR_W]
        outs = []
        for hp in range(2):
            h = 2 * pair + hp
            g = h // SWA_GROUP
            k_w, k_m = keys[0] if hp == g else keys[1]
            v_w, v_m = vals[0] if hp == g else vals[1]
            qh = jnp.where(half[hp], qpair, jnp.zeros_like(qpair))
            s_w = _dot_nt(qh, k_w) + bw_ref[h] + no_prev
            s_m = _dot_nt(qh, k_m) + bm_ref[h]
            sink = sink_ref[h]
            m = jnp.maximum(jnp.maximum(jnp.max(s_w, axis=1, keepdims=True),
                                        jnp.max(s_m, axis=1, keepdims=True)), sink)
            p_w = jnp.exp(s_w - m)
            p_m = jnp.exp(s_m - m)
            l = (jnp.sum(p_w, axis=1, keepdims=True) + jnp.sum(p_m, axis=1, keepdims=True)
                 + jnp.exp(sink - m))
            o = _dot(p_w.astype(BF16), v_w) + _dot(p_m.astype(BF16), v_m)
            outs.append(o / l)
        o_ref[:, pair * PAIR_W:(pair + 1) * PAIR_W] = jnp.where(half[0], outs[0], outs[1]).astype(BF16)


def _swa(proj, projm, bias_w, bias_m, sinks):
    nb, rows, _ = proj.shape
    nblk = rows // BLOCK
    kcol, vcol = OFF_QA // SWA_KV_W, OFF_KA // SWA_KV_W
    cur = lambda c: (lambda b, n: (b, n, c))
    prev = lambda c: (lambda b, n: (b, jnp.maximum(n - 1, 0), c))
    return pl.pallas_call(
        _swa_kernel,
        grid=(nb, nblk),
        in_specs=[
            pl.BlockSpec(memory_space=pltpu.SMEM),
            pl.BlockSpec((None, BLOCK, SWA_Q_W), lambda b, n: (b, n, 0)),
            pl.BlockSpec((None, BLOCK, SWA_KV_W), cur(kcol)),
            pl.BlockSpec((None, BLOCK, SWA_KV_W), prev(kcol)),
            pl.BlockSpec((None, BLOCK, SWA_KV_W), cur(vcol)),
            pl.BlockSpec((None, BLOCK, SWA_KV_W), prev(vcol)),
            pl.BlockSpec((N_META, SWA_KV_W), lambda b, n: (0, kcol)),
            pl.BlockSpec((N_META, SWA_KV_W), lambda b, n: (0, vcol)),
            pl.BlockSpec((SWA_Q_HEADS, BLOCK, 2 * BLOCK), lambda b, n: (0, 0, 0)),
            pl.BlockSpec((SWA_Q_HEADS, BLOCK, N_META), lambda b, n: (0, jnp.minimum(n, 1), 0)),
        ],
        out_specs=pl.BlockSpec((None, BLOCK, SWA_Q_W), lambda b, n: (b, n, 0)),
        out_shape=jax.ShapeDtypeStruct((nb, rows, SWA_Q_W), BF16),
        compiler_params=pltpu.CompilerParams(
            dimension_semantics=("arbitrary", "arbitrary"), vmem_limit_bytes=VMEM_LIMIT),
        name="swa",
    )(sinks, proj, proj, proj, proj, proj, projm, projm, bias_w, bias_m)


def _fox_kernel(q_ref, k_ref, v_ref, km_ref, vm_ref, c_ref, cm_ref, o_ref, m_scr, l_scr, acc_scr):
    i = pl.program_id(2)
    t = q_ref.shape[0]
    lane = lax.broadcasted_iota(jnp.int32, (1, LANES), 1)
    half = [lane < HEAD_DIM, lane >= HEAD_DIM]
    q = q_ref[...]
    qs = [jnp.where(half[e], q, jnp.zeros_like(q)) for e in range(2)]
    reps = t // LANES

    for e in range(2):
        s = _dot_nt(qs[e], km_ref[...]) - cm_ref[e:e + 1, :]
        m = jnp.max(s, axis=1, keepdims=True)
        p = jnp.exp(s - m)
        m_scr[e] = jnp.broadcast_to(m, (t, LANES))
        l_scr[e] = jnp.broadcast_to(jnp.sum(p, axis=1, keepdims=True), (t, LANES))
        acc_scr[e] = _dot(p.astype(BF16), vm_ref[...])

    def step(j, diagonal):
        start = pl.multiple_of(j * t, t)
        kj = k_ref[pl.ds(start, t), :]
        vj = v_ref[pl.ds(start, t), :]
        for e in range(2):
            s = _dot_nt(qs[e], kj) - c_ref[e:e + 1, pl.ds(start, t)]
            if diagonal:
                row = lax.broadcasted_iota(jnp.int32, (t, t), 0)
                colk = lax.broadcasted_iota(jnp.int32, (t, t), 1)
                s = jnp.where(colk <= row, s, NEG_INF)
            m_prev = m_scr[e]
            m_next = jnp.maximum(m_prev, jnp.max(s, axis=1, keepdims=True))
            p = jnp.exp(s - jnp.concatenate([m_next] * reps, axis=1))
            alpha = jnp.exp(m_prev - m_next)
            l_scr[e] = alpha * l_scr[e] + jnp.sum(p, axis=1, keepdims=True)
            acc_scr[e] = alpha * acc_scr[e] + _dot(p.astype(BF16), vj)
            m_scr[e] = m_next

    def full_step(j, carry):
        step(j, False)
        return carry

    lax.fori_loop(0, i, full_step, 0)
    step(i, True)
    o = jnp.where(half[0], acc_scr[0] / l_scr[0], acc_scr[1] / l_scr[1])
    o_ref[...] = o.astype(BF16)


def _fox(proj, projm, cum_t, cum_mt):
    nb, rows, _ = proj.shape
    t = FOX_T
    npair = FOX_HEADS // 2
    qcol, kcol, vcol = OFF_VA // PAIR_W, OFF_QB // PAIR_W, OFF_KB // PAIR_W
    return pl.pallas_call(
        _fox_kernel,
        grid=(nb, npair, rows // t),
        in_specs=[
            pl.BlockSpec((None, t, PAIR_W), lambda b, p, i: (b, i, qcol + p)),
            pl.BlockSpec((None, rows, PAIR_W), lambda b, p, i: (b, 0, kcol + p)),
            pl.BlockSpec((None, rows, PAIR_W), lambda b, p, i: (b, 0, vcol + p)),
            pl.BlockSpec((N_META, PAIR_W), lambda b, p, i: (0, kcol + p)),
            pl.BlockSpec((N_META, PAIR_W), lambda b, p, i: (0, vcol + p)),
            pl.BlockSpec((None, None, 2, rows), lambda b, p, i: (b, p, 0, 0)),
            pl.BlockSpec((None, 2, N_META), lambda b, p, i: (p, 0, 0)),
        ],
        out_specs=pl.BlockSpec((None, t, PAIR_W), lambda b, p, i: (b, i, p)),
        out_shape=jax.ShapeDtypeStruct((nb, rows, FOX_W), BF16),
        scratch_shapes=[
            pltpu.VMEM((2, t, LANES), F32),
            pltpu.VMEM((2, t, LANES), F32),
            pltpu.VMEM((2, t, PAIR_W), F32),
        ],
        compiler_params=pltpu.CompilerParams(
            dimension_semantics=("arbitrary", "arbitrary", "arbitrary"), vmem_limit_bytes=VMEM_LIMIT),
        name="fox",
    )(proj, proj, proj, projm, projm, cum_t, cum_mt)


def _ffn_kernel(oa_ref, ob_ref, x_ref, wo_ref, g1_ref, g2_ref, wg_ref, wu_ref, wd_ref, g3_ref, out_ref):
    mix = jnp.concatenate([oa_ref[...], ob_ref[...]], axis=1)
    h1 = x_ref[...] + _rms(_dot(mix, wo_ref[...]), g1_ref[...])
    hn = _rms(h1, g2_ref[...]).astype(BF16)
    gate = _dot(hn, wg_ref[...])
    up = _dot(hn, wu_ref[...])
    act = (gate / (1.0 + jnp.exp(-gate)) * up).astype(BF16)
    out_ref[...] = h1 + _rms(_dot(act, wd_ref[...]), g3_ref[...])


def _ffn(o_a, o_b, x3, wo, g1, g2, wg, wu, wd, g3):
    nb, rows, _ = x3.shape
    tm = FFN_TM
    const = lambda b, t: (0, 0)
    resident = lambda shape: pl.BlockSpec(shape, const, pipeline_mode=pl.Buffered(1))
    row = lambda w: pl.BlockSpec((None, tm, w), lambda b, t: (b, t, 0))
    return pl.pallas_call(
        _ffn_kernel,
        grid=(nb, rows // tm),
        in_specs=[
            row(SWA_Q_W), row(FOX_W), row(D_MODEL),
            resident((D_MODEL, D_MODEL)), resident((1, D_MODEL)), resident((1, D_MODEL)),
            resident((D_MODEL, D_FF)), resident((D_MODEL, D_FF)), resident((D_FF, D_MODEL)),
            resident((1, D_MODEL)),
        ],
        out_specs=row(D_MODEL),
        out_shape=jax.ShapeDtypeStruct((nb, rows, D_MODEL), F32),
        compiler_params=pltpu.CompilerParams(
            dimension_semantics=("arbitrary", "arbitrary"), vmem_limit_bytes=VMEM_LIMIT),
        name="outproj_ffn",
    )(o_a, o_b, x3, wo, g1, g2, wg, wu, wd, g3)


def kernel(x, meta_tokens, rel_bias, ln_pre_mix, ln_post_mix, ln_pre_ffn, ln_post_ffn,
           w_in, b_forget, sinks, w_out, w_gate_up, w_down):
    nb, seq, d = x.shape
    assert d == D_MODEL and seq % IN_TM == 0 and seq % FOX_T == 0 and seq % FFN_TM == 0
    assert w_in.shape[0] == 1, "single-layer block"
    assert meta_tokens.shape == (N_META, D_MODEL)
    x = x.astype(F32)

    w_qkv = w_in[0, :, :OFF_VB].astype(BF16)
    w_f = jnp.pad(w_in[0, :, OFF_VB:], ((0, 0), (0, LANES - FOX_HEADS))).astype(BF16)
    b_f = jnp.pad(b_forget[0].astype(F32), (0, LANES - FOX_HEADS)).reshape(1, LANES)
    g_pre = ln_pre_mix[0].astype(F32).reshape(1, D_MODEL)

    bias_w, bias_m = _bias_tables(rel_bias)

    zero_c = jnp.zeros((1, LANES), F32)
    projm, cum_m = _inproj(meta_tokens.astype(F32)[None], g_pre, w_qkv, w_f, b_f, zero_c, N_META)
    projm, cum_m = projm[0], cum_m[0]
    c0 = jnp.pad(cum_m[N_META - 1], (0, LANES - FOX_HEADS)).reshape(1, LANES)
    proj, cum = _inproj(x, g_pre, w_qkv, w_f, b_f, c0, IN_TM)

    o_a = _swa(proj, projm, bias_w, bias_m, sinks[0].astype(F32))

    cum_t = jnp.swapaxes(cum, 1, 2).reshape(nb, FOX_HEADS // 2, 2, seq)
    cum_mt = cum_m.T.reshape(FOX_HEADS // 2, 2, N_META)
    o_b = _fox(proj, projm, cum_t, cum_mt)

    w_gu = w_gate_up[0]
    return _ffn(o_a, o_b, x,
                w_out[0].astype(BF16),
                ln_post_mix[0].astype(F32).reshape(1, D_MODEL),
                ln_pre_ffn[0].astype(F32).reshape(1, D_MODEL),
                w_gu[:, :D_FF].astype(BF16), w_gu[:, D_FF:].astype(BF16),
                w_down[0].astype(BF16),
                ln_post_ffn[0].astype(F32).reshape(1, D_MODEL))
```

```python
import math

import numpy as np
import jax
import jax.numpy as jnp
from jax import lax
from jax.experimental import pallas as pl
from jax.experimental.pallas import tpu as pltpu

D_MODEL = 1024
N_META = 16
HEAD_DIM = 64
SWA_Q_HEADS = 8
SWA_KV_HEADS = 2
SWA_GROUP = SWA_Q_HEADS // SWA_KV_HEADS
FOX_HEADS = 8
SWA_Q_W = SWA_Q_HEADS * HEAD_DIM
SWA_KV_W = SWA_KV_HEADS * HEAD_DIM
FOX_W = FOX_HEADS * HEAD_DIM
OFF_QA = SWA_Q_W
OFF_KA = OFF_QA + SWA_KV_W
OFF_VA = OFF_KA + SWA_KV_W
OFF_QB = OFF_VA + FOX_W
OFF_KB = OFF_QB + FOX_W
OFF_VB = OFF_KB + FOX_W
WINDOW = 128
BLOCK = 128
N_BUCKETS = 32
MAX_DISTANCE = 128
D_FF = 2816
EPS = 1e-6
NEG_INF = -1e30
SCALE = HEAD_DIM ** -0.5
LOG2E = math.log2(math.e)

LANES = 128
PAIR_W = 2 * HEAD_DIM
VMEM_LIMIT = 56 * 1024 * 1024

IN_TM = 512
FOX_T = 512
FOX_CHUNK = 512
FFN_TM = 256
N_PIECES = 3

F32 = jnp.float32
BF16 = jnp.bfloat16


def _dot(a, b):
    return jnp.dot(a, b, preferred_element_type=F32)


def _dot_nt(a, b):
    return lax.dot_general(a, b, (((1,), (1,)), ((), ())), preferred_element_type=F32)


def _rms(t, g):
    return t * lax.rsqrt(jnp.mean(t * t, axis=-1, keepdims=True) + EPS) * g


def _bf16_pieces(v):
    pieces = []
    for _ in range(N_PIECES - 1):
        p = v.astype(BF16)
        pieces.append(p)
        v = v - p.astype(F32)
    pieces.append(v.astype(BF16))
    return pieces


def _t5_bucket_np(dist):
    n = np.maximum(dist, 0).astype(np.int32)
    max_exact = N_BUCKETS // 2
    nf = np.maximum(n, 1).astype(np.float32)
    large = max_exact + (np.log(nf / np.float32(max_exact)) / np.float32(math.log(MAX_DISTANCE / max_exact))
                         * np.float32(N_BUCKETS - max_exact)).astype(np.int32)
    large = np.minimum(large, N_BUCKETS - 1)
    return np.where(n < max_exact, n, large).astype(np.int32)


def _bias_kernel(tab_ref, bkt_w_ref, valid_w_ref, bkt_m_ref, bw_ref, bm_ref):
    h = pl.program_id(0)
    bkt_w = bkt_w_ref[...]
    bkt_m = bkt_m_ref[...]
    acc_w = jnp.zeros(bkt_w.shape, F32)
    acc_m = jnp.zeros(bkt_m.shape, F32)
    for b in range(N_BUCKETS):
        t = tab_ref[b, h]
        acc_w = jnp.where(bkt_w == b, t, acc_w)
        acc_m = jnp.where(bkt_m == b, t, acc_m)
    bw_ref[...] = jnp.where(valid_w_ref[...] > 0, acc_w, NEG_INF)
    bm_ref[...] = acc_m


def _bias_tables(rel_bias):
    qi = np.arange(BLOCK)[:, None]
    ki = np.arange(2 * BLOCK)[None, :]
    d_w = qi + BLOCK - ki
    bkt_w = _t5_bucket_np(d_w)
    valid_w = ((d_w >= 0) & (d_w < WINDOW)).astype(np.int32)
    mi = np.arange(N_META)[None, :]
    bkt_m0 = _t5_bucket_np(N_META + qi - mi)
    far = _t5_bucket_np(np.asarray([[N_META + BLOCK - (N_META - 1)]]))
    assert far[0, 0] == N_BUCKETS - 1, "meta keys of later blocks must share the last bucket"
    bkt_m = np.stack([bkt_m0, np.full_like(bkt_m0, far[0, 0])]).reshape(2 * BLOCK, N_META)
    return pl.pallas_call(
        _bias_kernel,
        grid=(SWA_Q_HEADS,),
        in_specs=[
            pl.BlockSpec(memory_space=pltpu.SMEM),
            pl.BlockSpec((BLOCK, 2 * BLOCK), lambda h: (0, 0)),
            pl.BlockSpec((BLOCK, 2 * BLOCK), lambda h: (0, 0)),
            pl.BlockSpec((2 * BLOCK, N_META), lambda h: (0, 0)),
        ],
        out_specs=[
            pl.BlockSpec((None, BLOCK, 2 * BLOCK), lambda h: (h, 0, 0)),
            pl.BlockSpec((None, 2 * BLOCK, N_META), lambda h: (h, 0, 0)),
        ],
        out_shape=[
            jax.ShapeDtypeStruct((SWA_Q_HEADS, BLOCK, 2 * BLOCK), F32),
            jax.ShapeDtypeStruct((SWA_Q_HEADS, 2 * BLOCK, N_META), F32),
        ],
        name="bias_tables",
    )(rel_bias.astype(F32), jnp.asarray(bkt_w), jnp.asarray(valid_w), jnp.asarray(bkt_m))


def _inproj_kernel(x_ref, g_ref, w_ref, wf_ref, bf_ref, tri_ref, c0_ref, proj_ref, cum_ref, carry_ref):
    @pl.when(pl.program_id(1) == 0)
    def _():
        carry_ref[...] = c0_ref[...]

    y = _rms(x_ref[...], g_ref[...]).astype(BF16)
    acc = _dot(y, w_ref[...])
    proj_ref[:, 0:OFF_QA] = (acc[:, 0:OFF_QA] * SCALE).astype(BF16)
    proj_ref[:, OFF_QA:OFF_VA] = acc[:, OFF_QA:OFF_VA].astype(BF16)
    proj_ref[:, OFF_VA:OFF_QB] = (acc[:, OFF_VA:OFF_QB] * (SCALE * LOG2E)).astype(BF16)
    proj_ref[:, OFF_QB:OFF_VB] = acc[:, OFF_QB:OFF_VB].astype(BF16)

    f = _dot(y, wf_ref[...]) + bf_ref[...]
    ls = jnp.minimum(f, 0.0) - jnp.log1p(jnp.exp(-jnp.abs(f)))
    tri = tri_ref[...]
    cum = carry_ref[...]
    for piece in _bf16_pieces(ls):
        cum = cum + _dot(tri, piece)
    cum_ref[...] = cum[:, :FOX_HEADS]
    tm = cum.shape[0]
    carry_ref[...] = cum[tm - 1:tm, :]


def _inproj(x3, g, w, wf, bf, c0, tm):
    nb, rows, _ = x3.shape
    tri = jnp.asarray(np.tril(np.ones((tm, tm), np.float32)), BF16)
    const = lambda b, t: (0, 0)
    return pl.pallas_call(
        _inproj_kernel,
        grid=(nb, rows // tm),
        in_specs=[
            pl.BlockSpec((None, tm, D_MODEL), lambda b, t: (b, t, 0)),
            pl.BlockSpec((1, D_MODEL), const),
            pl.BlockSpec((D_MODEL, OFF_VB), const),
            pl.BlockSpec((D_MODEL, LANES), const),
            pl.BlockSpec((1, LANES), const),
            pl.BlockSpec((tm, tm), const),
            pl.BlockSpec((1, LANES), const),
        ],
        out_specs=[
            pl.BlockSpec((None, tm, OFF_VB), lambda b, t: (b, t, 0)),
            pl.BlockSpec((None, tm, FOX_HEADS), lambda b, t: (b, t, 0)),
        ],
        out_shape=[
            jax.ShapeDtypeStruct((nb, rows, OFF_VB), BF16),
            jax.ShapeDtypeStruct((nb, rows, FOX_HEADS), F32),
        ],
        scratch_shapes=[pltpu.VMEM((1, LANES), F32)],
        compiler_params=pltpu.CompilerParams(
            dimension_semantics=("arbitrary", "arbitrary"), vmem_limit_bytes=VMEM_LIMIT),
        name="inproj",
    )(x3, g, w, wf, bf, tri, c0)


def _swa_kernel(sink_ref, q_ref, kc_ref, kp_ref, vc_ref, vp_ref, km_ref, vm_ref, bw_ref, bm_ref, o_ref):
    n = pl.program_id(1)
    lane = lax.broadcasted_iota(jnp.int32, (1, LANES), 1)
    half = [lane < HEAD_DIM, lane >= HEAD_DIM]

    def swapped(t):
        return pltpu.roll(t.astype(F32), HEAD_DIM, 1).astype(BF16)

    kw = jnp.concatenate([kp_ref[...], kc_ref[...]], axis=0)
    vw = jnp.concatenate([vp_ref[...], vc_ref[...]], axis=0)
    km = km_ref[...]
    vm = vm_ref[...]
    keys = [(kw, km), (swapped(kw), swapped(km))]
    vals = [(vw, vm), (swapped(vw), swapped(vm))]
    col = lax.broadcasted_iota(jnp.int32, (1, 2 * BLOCK), 1)
    no_prev = jnp.where((col < BLOCK) & (n == 0), NEG_INF, 0.0)

    for pair in range(SWA_Q_HEADS // 2):
        qpair = q_ref[:, pair * PAIR_W:(pair + 1) * PAIR_W]
        outs = []
        for hp in range(2):
            h = 2 * pair + hp
            g = h // SWA_GROUP
            k_w, k_m = keys[0] if hp == g else keys[1]
            v_w, v_m = vals[0] if hp == g else vals[1]
            qh = jnp.where(half[hp], qpair, jnp.zeros_like(qpair))
            s_w = _dot_nt(qh, k_w) + bw_ref[h] + no_prev
            s_m = _dot_nt(qh, k_m) + bm_ref[h]
            sink = sink_ref[h]
            m = jnp.maximum(jnp.maximum(jnp.max(s_w, axis=1, keepdims=True),
                                        jnp.max(s_m, axis=1, keepdims=True)), sink)
            p_w = jnp.exp(s_w - m)
            p_m = jnp.exp(s_m - m)
            l = (jnp.sum(p_w, axis=1, keepdims=True) + jnp.sum(p_m, axis=1, keepdims=True)
                 + jnp.exp(sink - m))
            o = _dot(p_w.astype(BF16), v_w) + _dot(p_m.astype(BF16), v_m)
            outs.append(o / l)
        o_ref[:, pair * PAIR_W:(pair + 1) * PAIR_W] = jnp.where(half[0], outs[0], outs[1]).astype(BF16)


def _swa(proj, projm, bias_w, bias_m, sinks):
    nb, rows, _ = proj.shape
    nblk = rows // BLOCK
    kcol, vcol = OFF_QA // SWA_KV_W, OFF_KA // SWA_KV_W
    cur = lambda c: (lambda b, n: (b, n, c))
    prev = lambda c: (lambda b, n: (b, jnp.maximum(n - 1, 0), c))
    return pl.pallas_call(
        _swa_kernel,
        grid=(nb, nblk),
        in_specs=[
            pl.BlockSpec(memory_space=pltpu.SMEM),
            pl.BlockSpec((None, BLOCK, SWA_Q_W), lambda b, n: (b, n, 0)),
            pl.BlockSpec((None, BLOCK, SWA_KV_W), cur(kcol)),
            pl.BlockSpec((None, BLOCK, SWA_KV_W), prev(kcol)),
            pl.BlockSpec((None, BLOCK, SWA_KV_W), cur(vcol)),
            pl.BlockSpec((None, BLOCK, SWA_KV_W), prev(vcol)),
            pl.BlockSpec((N_META, SWA_KV_W), lambda b, n: (0, kcol)),
            pl.BlockSpec((N_META, SWA_KV_W), lambda b, n: (0, vcol)),
            pl.BlockSpec((SWA_Q_HEADS, BLOCK, 2 * BLOCK), lambda b, n: (0, 0, 0)),
            pl.BlockSpec((SWA_Q_HEADS, BLOCK, N_META), lambda b, n: (0, jnp.minimum(n, 1), 0)),
        ],
        out_specs=pl.BlockSpec((None, BLOCK, SWA_Q_W), lambda b, n: (b, n, 0)),
        out_shape=jax.ShapeDtypeStruct((nb, rows, SWA_Q_W), BF16),
        compiler_params=pltpu.CompilerParams(
            dimension_semantics=("arbitrary", "arbitrary"), vmem_limit_bytes=VMEM_LIMIT),
        name="swa",
    )(sinks, proj, proj, proj, proj, proj, projm, projm, bias_w, bias_m)


def _fox_aug_lanes(e):
    lane = lax.broadcasted_iota(jnp.int32, (1, LANES), 1)
    own = (lane < HEAD_DIM) if e == 0 else (lane >= HEAD_DIM)
    return lane, own, (HEAD_DIM if e == 0 else 0)


def _fox_key_aug(kpair, c, e):
    lane, own, a0 = _fox_aug_lanes(e)
    aug = jnp.zeros(kpair.shape, F32)
    for i, piece in enumerate(_bf16_pieces(c * (-LOG2E))):
        aug = jnp.where(lane == a0 + i, piece.astype(F32), aug)
    return jnp.where(own, kpair, aug.astype(BF16))


def _fox_query_aug(qpair, e):
    lane, own, a0 = _fox_aug_lanes(e)
    ones = jnp.where((lane >= a0) & (lane < a0 + N_PIECES), 1.0, 0.0).astype(BF16)
    return jnp.where(own, qpair, jnp.broadcast_to(ones, qpair.shape))


def _fox_value_aug_t(vpair, e):
    vt = vpair.astype(F32).T
    top = vt[e * HEAD_DIM:(e + 1) * HEAD_DIM]
    return jnp.concatenate([top, jnp.ones_like(top)], axis=0).astype(BF16)


def _fox_kernel(q_ref, k_ref, v_ref, km_ref, vm_ref, c_ref, cm_ref, o_ref, ka_scr, vt_scr, acc_scr):
    i = pl.program_id(2)
    t = q_ref.shape[0]
    nk = k_ref.shape[0] // t

    @pl.when(i == 0)
    def _():
        for j in range(nk):
            rows = slice(j * t, (j + 1) * t)
            for e in range(2):
                ka_scr[e, j] = _fox_key_aug(k_ref[rows, :], c_ref[rows, e:e + 1], e)
                vt_scr[e, j] = _fox_value_aug_t(v_ref[rows, :], e)

    q = q_ref[...]
    qt = [_fox_query_aug(q, e).astype(F32).T.astype(BF16) for e in range(2)]

    m0 = []
    for e in range(2):
        s = _dot(_fox_key_aug(km_ref[...], cm_ref[:, e:e + 1], e), qt[e])
        m = jnp.max(s, axis=0, keepdims=True)
        p = jnp.exp2(s - m).astype(BF16)
        acc_scr[e] = _dot(_fox_value_aug_t(vm_ref[...], e), p)
        m0.append(m)

    def step(j, ms, diagonal):
        out = []
        for e in range(2):
            s = _dot(ka_scr[e, j], qt[e])
            if diagonal:
                key = lax.broadcasted_iota(jnp.int32, (t, t), 0)
                qry = lax.broadcasted_iota(jnp.int32, (t, t), 1)
                s = jnp.where(key <= qry, s, NEG_INF)
            m_next = jnp.maximum(ms[e], jnp.max(s, axis=0, keepdims=True))
            p = jnp.exp2(s - m_next).astype(BF16)
            acc_scr[e] = jnp.exp2(ms[e] - m_next) * acc_scr[e] + _dot(vt_scr[e, j], p)
            out.append(m_next)
        return tuple(out)

    ms = lax.fori_loop(0, i, lambda j, ms: step(j, ms, False), tuple(m0))
    step(i, ms, True)
    o_t = jnp.concatenate([acc_scr[e, 0:HEAD_DIM] / acc_scr[e, HEAD_DIM:PAIR_W] for e in range(2)], axis=0)
    o_ref[...] = o_t.T.astype(BF16)


def _fox(proj, projm, cum_p, cum_mp):
    nb, rows, _ = proj.shape
    t = FOX_T
    nk = rows // t
    npair = FOX_HEADS // 2
    qcol, kcol, vcol = OFF_VA // PAIR_W, OFF_QB // PAIR_W, OFF_KB // PAIR_W
    return pl.pallas_call(
        _fox_kernel,
        grid=(nb, npair, rows // t),
        in_specs=[
            pl.BlockSpec((None, t, PAIR_W), lambda b, p, i: (b, i, qcol + p)),
            pl.BlockSpec((None, rows, PAIR_W), lambda b, p, i: (b, 0, kcol + p)),
            pl.BlockSpec((None, rows, PAIR_W), lambda b, p, i: (b, 0, vcol + p)),
            pl.BlockSpec((N_META, PAIR_W), lambda b, p, i: (0, kcol + p)),
            pl.BlockSpec((N_META, PAIR_W), lambda b, p, i: (0, vcol + p)),
            pl.BlockSpec((None, None, rows, 2), lambda b, p, i: (b, p, 0, 0)),
            pl.BlockSpec((None, N_META, 2), lambda b, p, i: (p, 0, 0)),
        ],
        out_specs=pl.BlockSpec((None, t, PAIR_W), lambda b, p, i: (b, i, p)),
        out_shape=jax.ShapeDtypeStruct((nb, rows, FOX_W), BF16),
        scratch_shapes=[
            pltpu.VMEM((2, nk, t, PAIR_W), BF16),
            pltpu.VMEM((2, nk, PAIR_W, t), BF16),
            pltpu.VMEM((2, PAIR_W, t), F32),
        ],
        compiler_params=pltpu.CompilerParams(
            dimension_semantics=("arbitrary", "arbitrary", "arbitrary"), vmem_limit_bytes=VMEM_LIMIT),
        name="fox",
    )(proj, proj, proj, projm, projm, cum_p, cum_mp)


def _ffn_kernel(oa_ref, ob_ref, x_ref, wo_ref, g1_ref, g2_ref, wg_ref, wu_ref, wd_ref, g3_ref, out_ref):
    mix = jnp.concatenate([oa_ref[...], ob_ref[...]], axis=1)
    h1 = x_ref[...] + _rms(_dot(mix, wo_ref[...]), g1_ref[...])
    hn = _rms(h1, g2_ref[...]).astype(BF16)
    gate = _dot(hn, wg_ref[...])
    up = _dot(hn, wu_ref[...])
    act = (gate / (1.0 + jnp.exp(-gate)) * up).astype(BF16)
    out_ref[...] = h1 + _rms(_dot(act, wd_ref[...]), g3_ref[...])


def _ffn(o_a, o_b, x3, wo, g1, g2, wg, wu, wd, g3):
    nb, rows, _ = x3.shape
    tm = FFN_TM
    const = lambda b, t: (0, 0)
    resident = lambda shape: pl.BlockSpec(shape, const, pipeline_mode=pl.Buffered(1))
    row = lambda w: pl.BlockSpec((None, tm, w), lambda b, t: (b, t, 0))
    return pl.pallas_call(
        _ffn_kernel,
        grid=(nb, rows // tm),
        in_specs=[
            row(SWA_Q_W), row(FOX_W), row(D_MODEL),
            resident((D_MODEL, D_MODEL)), resident((1, D_MODEL)), resident((1, D_MODEL)),
            resident((D_MODEL, D_FF)), resident((D_MODEL, D_FF)), resident((D_FF, D_MODEL)),
            resident((1, D_MODEL)),
        ],
        out_specs=row(D_MODEL),
        out_shape=jax.ShapeDtypeStruct((nb, rows, D_MODEL), F32),
        compiler_params=pltpu.CompilerParams(
            dimension_semantics=("arbitrary", "arbitrary"), vmem_limit_bytes=VMEM_LIMIT),
        name="outproj_ffn",
    )(o_a, o_b, x3, wo, g1, g2, wg, wu, wd, g3)


def kernel(x, meta_tokens, rel_bias, ln_pre_mix, ln_post_mix, ln_pre_ffn, ln_post_ffn,
           w_in, b_forget, sinks, w_out, w_gate_up, w_down):
    nb, seq, d = x.shape
    assert d == D_MODEL and seq % IN_TM == 0 and seq % FOX_T == 0 and seq % FFN_TM == 0
    assert w_in.shape[0] == 1, "single-layer block"
    assert meta_tokens.shape == (N_META, D_MODEL)
    x = x.astype(F32)

    w_qkv = w_in[0, :, :OFF_VB].astype(BF16)
    w_f = jnp.pad(w_in[0, :, OFF_VB:], ((0, 0), (0, LANES - FOX_HEADS))).astype(BF16)
    b_f = jnp.pad(b_forget[0].astype(F32), (0, LANES - FOX_HEADS)).reshape(1, LANES)
    g_pre = ln_pre_mix[0].astype(F32).reshape(1, D_MODEL)

    bias_w, bias_m = _bias_tables(rel_bias)

    zero_c = jnp.zeros((1, LANES), F32)
    projm, cum_m = _inproj(meta_tokens.astype(F32)[None], g_pre, w_qkv, w_f, b_f, zero_c, N_META)
    projm, cum_m = projm[0], cum_m[0]
    c0 = jnp.pad(cum_m[N_META - 1], (0, LANES - FOX_HEADS)).reshape(1, LANES)
    proj, cum = _inproj(x, g_pre, w_qkv, w_f, b_f, c0, IN_TM)

    o_a = _swa(proj, projm, bias_w, bias_m, sinks[0].astype(F32))

    npair = FOX_HEADS // 2
    cum_p = cum.reshape(nb, seq, npair, 2).transpose(0, 2, 1, 3)
    cum_mp = cum_m.reshape(N_META, npair, 2).transpose(1, 0, 2)
    o_b = _fox(proj, projm, cum_p, cum_mp)

    w_gu = w_gate_up[0]
    return _ffn(o_a, o_b, x,
                w_out[0].astype(BF16),
                ln_post_mix[0].astype(F32).reshape(1, D_MODEL),
                ln_pre_ffn[0].astype(F32).reshape(1, D_MODEL),
                w_gu[:, :D_FF].astype(BF16), w_gu[:, D_FF:].astype(BF16),
                w_down[0].astype(BF16),
                ln_post_ffn[0].astype(F32).reshape(1, D_MODEL))
```

```python
import math

import numpy as np
import jax
import jax.numpy as jnp
from jax import lax
from jax.experimental import pallas as pl
from jax.experimental.pallas import tpu as pltpu

D_MODEL = 1024
N_META = 16
HEAD_DIM = 64
SWA_Q_HEADS = 8
SWA_KV_HEADS = 2
SWA_GROUP = SWA_Q_HEADS // SWA_KV_HEADS
FOX_HEADS = 8
SWA_Q_W = SWA_Q_HEADS * HEAD_DIM
SWA_KV_W = SWA_KV_HEADS * HEAD_DIM
FOX_W = FOX_HEADS * HEAD_DIM
OFF_QA = SWA_Q_W
OFF_KA = OFF_QA + SWA_KV_W
OFF_VA = OFF_KA + SWA_KV_W
OFF_QB = OFF_VA + FOX_W
OFF_KB = OFF_QB + FOX_W
OFF_VB = OFF_KB + FOX_W
WINDOW = 128
BLOCK = 128
N_BUCKETS = 32
MAX_DISTANCE = 128
D_FF = 2816
EPS = 1e-6
NEG_INF = -1e30
SCALE = HEAD_DIM ** -0.5
LOG2E = math.log2(math.e)

LANES = 128
PAIR_W = 2 * HEAD_DIM
VMEM_LIMIT = 56 * 1024 * 1024

IN_TM = 512
FOX_T = 512
FOX_GROUP = 4
FOX_SAFE_LOG2 = 40.0
FFN_TM = 256
N_PIECES = 3

F32 = jnp.float32
BF16 = jnp.bfloat16


def _dot(a, b):
    return jnp.dot(a, b, preferred_element_type=F32)


def _dot_nt(a, b):
    return lax.dot_general(a, b, (((1,), (1,)), ((), ())), preferred_element_type=F32)


def _rms(t, g):
    return t * lax.rsqrt(jnp.mean(t * t, axis=-1, keepdims=True) + EPS) * g


def _bf16_pieces(v):
    pieces = []
    for _ in range(N_PIECES - 1):
        p = v.astype(BF16)
        pieces.append(p)
        v = v - p.astype(F32)
    pieces.append(v.astype(BF16))
    return pieces


def _t5_bucket_np(dist):
    n = np.maximum(dist, 0).astype(np.int32)
    max_exact = N_BUCKETS // 2
    nf = np.maximum(n, 1).astype(np.float32)
    large = max_exact + (np.log(nf / np.float32(max_exact)) / np.float32(math.log(MAX_DISTANCE / max_exact))
                         * np.float32(N_BUCKETS - max_exact)).astype(np.int32)
    large = np.minimum(large, N_BUCKETS - 1)
    return np.where(n < max_exact, n, large).astype(np.int32)


def _bias_kernel(tab_ref, bkt_w_ref, valid_w_ref, bkt_m_ref, bw_ref, bm_ref):
    h = pl.program_id(0)
    bkt_w = bkt_w_ref[...]
    bkt_m = bkt_m_ref[...]
    acc_w = jnp.zeros(bkt_w.shape, F32)
    acc_m = jnp.zeros(bkt_m.shape, F32)
    for b in range(N_BUCKETS):
        t = tab_ref[b, h]
        acc_w = jnp.where(bkt_w == b, t, acc_w)
        acc_m = jnp.where(bkt_m == b, t, acc_m)
    bw_ref[...] = jnp.where(valid_w_ref[...] > 0, acc_w, NEG_INF)
    bm_ref[...] = acc_m


def _bias_tables(rel_bias):
    qi = np.arange(BLOCK)[:, None]
    ki = np.arange(2 * BLOCK)[None, :]
    d_w = qi + BLOCK - ki
    bkt_w = _t5_bucket_np(d_w)
    valid_w = ((d_w >= 0) & (d_w < WINDOW)).astype(np.int32)
    mi = np.arange(N_META)[None, :]
    bkt_m0 = _t5_bucket_np(N_META + qi - mi)
    far = _t5_bucket_np(np.asarray([[N_META + BLOCK - (N_META - 1)]]))
    assert far[0, 0] == N_BUCKETS - 1, "meta keys of later blocks must share the last bucket"
    bkt_m = np.stack([bkt_m0, np.full_like(bkt_m0, far[0, 0])]).reshape(2 * BLOCK, N_META)
    return pl.pallas_call(
        _bias_kernel,
        grid=(SWA_Q_HEADS,),
        in_specs=[
            pl.BlockSpec(memory_space=pltpu.SMEM),
            pl.BlockSpec((BLOCK, 2 * BLOCK), lambda h: (0, 0)),
            pl.BlockSpec((BLOCK, 2 * BLOCK), lambda h: (0, 0)),
            pl.BlockSpec((2 * BLOCK, N_META), lambda h: (0, 0)),
        ],
        out_specs=[
            pl.BlockSpec((None, BLOCK, 2 * BLOCK), lambda h: (h, 0, 0)),
            pl.BlockSpec((None, 2 * BLOCK, N_META), lambda h: (h, 0, 0)),
        ],
        out_shape=[
            jax.ShapeDtypeStruct((SWA_Q_HEADS, BLOCK, 2 * BLOCK), F32),
            jax.ShapeDtypeStruct((SWA_Q_HEADS, 2 * BLOCK, N_META), F32),
        ],
        name="bias_tables",
    )(rel_bias.astype(F32), jnp.asarray(bkt_w), jnp.asarray(valid_w), jnp.asarray(bkt_m))


def _inproj_kernel(x_ref, g_ref, w_ref, wf_ref, bf_ref, tri_ref, c0_ref, proj_ref, cum_ref, carry_ref):
    @pl.when(pl.program_id(1) == 0)
    def _():
        carry_ref[...] = c0_ref[...]

    y = _rms(x_ref[...], g_ref[...]).astype(BF16)
    acc = _dot(y, w_ref[...])
    proj_ref[:, 0:OFF_QA] = (acc[:, 0:OFF_QA] * SCALE).astype(BF16)
    proj_ref[:, OFF_QA:OFF_VA] = acc[:, OFF_QA:OFF_VA].astype(BF16)
    proj_ref[:, OFF_VA:OFF_QB] = (acc[:, OFF_VA:OFF_QB] * (SCALE * LOG2E)).astype(BF16)
    proj_ref[:, OFF_QB:OFF_VB] = acc[:, OFF_QB:OFF_VB].astype(BF16)

    f = _dot(y, wf_ref[...]) + bf_ref[...]
    ls = jnp.minimum(f, 0.0) - jnp.log1p(jnp.exp(-jnp.abs(f)))
    tri = tri_ref[...]
    cum = carry_ref[...]
    for piece in _bf16_pieces(ls):
        cum = cum + _dot(tri, piece)
    cum_ref[...] = cum[:, :FOX_HEADS]
    tm = cum.shape[0]
    carry_ref[...] = cum[tm - 1:tm, :]


def _inproj(x3, g, w, wf, bf, c0, tm):
    nb, rows, _ = x3.shape
    tri = jnp.asarray(np.tril(np.ones((tm, tm), np.float32)), BF16)
    const = lambda b, t: (0, 0)
    return pl.pallas_call(
        _inproj_kernel,
        grid=(nb, rows // tm),
        in_specs=[
            pl.BlockSpec((None, tm, D_MODEL), lambda b, t: (b, t, 0)),
            pl.BlockSpec((1, D_MODEL), const),
            pl.BlockSpec((D_MODEL, OFF_VB), const),
            pl.BlockSpec((D_MODEL, LANES), const),
            pl.BlockSpec((1, LANES), const),
            pl.BlockSpec((tm, tm), const),
            pl.BlockSpec((1, LANES), const),
        ],
        out_specs=[
            pl.BlockSpec((None, tm, OFF_VB), lambda b, t: (b, t, 0)),
            pl.BlockSpec((None, tm, FOX_HEADS), lambda b, t: (b, t, 0)),
        ],
        out_shape=[
            jax.ShapeDtypeStruct((nb, rows, OFF_VB), BF16),
            jax.ShapeDtypeStruct((nb, rows, FOX_HEADS), F32),
        ],
        scratch_shapes=[pltpu.VMEM((1, LANES), F32)],
        compiler_params=pltpu.CompilerParams(
            dimension_semantics=("arbitrary", "arbitrary"), vmem_limit_bytes=VMEM_LIMIT),
        name="inproj",
    )(x3, g, w, wf, bf, tri, c0)


def _swa_kernel(sink_ref, q_ref, kc_ref, kp_ref, vc_ref, vp_ref, km_ref, vm_ref, bw_ref, bm_ref, o_ref):
    n = pl.program_id(1)
    lane = lax.broadcasted_iota(jnp.int32, (1, LANES), 1)
    half = [lane < HEAD_DIM, lane >= HEAD_DIM]

    def swapped(t):
        return pltpu.roll(t.astype(F32), HEAD_DIM, 1).astype(BF16)

    kw = jnp.concatenate([kp_ref[...], kc_ref[...]], axis=0)
    vw = jnp.concatenate([vp_ref[...], vc_ref[...]], axis=0)
    km = km_ref[...]
    vm = vm_ref[...]
    keys = [(kw, km), (swapped(kw), swapped(km))]
    vals = [(vw, vm), (swapped(vw), swapped(vm))]
    col = lax.broadcasted_iota(jnp.int32, (1, 2 * BLOCK), 1)
    no_prev = jnp.where((col < BLOCK) & (n == 0), NEG_INF, 0.0)

    for pair in range(SWA_Q_HEADS // 2):
        qpair = q_ref[:, pair * PAIR_W:(pair + 1) * PAIR_W]
        outs = []
        for hp in range(2):
            h = 2 * pair + hp
            g = h // SWA_GROUP
            k_w, k_m = keys[0] if hp == g else keys[1]
            v_w, v_m = vals[0] if hp == g else vals[1]
            qh = jnp.where(half[hp], qpair, jnp.zeros_like(qpair))
            s_w = _dot_nt(qh, k_w) + bw_ref[h] + no_prev
            s_m = _dot_nt(qh, k_m) + bm_ref[h]
            sink = sink_ref[h]
            m = jnp.maximum(jnp.maximum(jnp.max(s_w, axis=1, keepdims=True),
                                        jnp.max(s_m, axis=1, keepdims=True)), sink)
            p_w = jnp.exp(s_w - m)
            p_m = jnp.exp(s_m - m)
            l = (jnp.sum(p_w, axis=1, keepdims=True) + jnp.sum(p_m, axis=1, keepdims=True)
                 + jnp.exp(sink - m))
            o = _dot(p_w.astype(BF16), v_w) + _dot(p_m.astype(BF16), v_m)
            outs.append(o / l)
        o_ref[:, pair * PAIR_W:(pair + 1) * PAIR_W] = jnp.where(half[0], outs[0], outs[1]).astype(BF16)


def _swa(proj, projm, bias_w, bias_m, sinks):
    nb, rows, _ = proj.shape
    nblk = rows // BLOCK
    kcol, vcol = OFF_QA // SWA_KV_W, OFF_KA // SWA_KV_W
    cur = lambda c: (lambda b, n: (b, n, c))
    prev = lambda c: (lambda b, n: (b, jnp.maximum(n - 1, 0), c))
    return pl.pallas_call(
        _swa_kernel,
        grid=(nb, nblk),
        in_specs=[
            pl.BlockSpec(memory_space=pltpu.SMEM),
            pl.BlockSpec((None, BLOCK, SWA_Q_W), lambda b, n: (b, n, 0)),
            pl.BlockSpec((None, BLOCK, SWA_KV_W), cur(kcol)),
            pl.BlockSpec((None, BLOCK, SWA_KV_W), prev(kcol)),
            pl.BlockSpec((None, BLOCK, SWA_KV_W), cur(vcol)),
            pl.BlockSpec((None, BLOCK, SWA_KV_W), prev(vcol)),
            pl.BlockSpec((N_META, SWA_KV_W), lambda b, n: (0, kcol)),
            pl.BlockSpec((N_META, SWA_KV_W), lambda b, n: (0, vcol)),
            pl.BlockSpec((SWA_Q_HEADS, BLOCK, 2 * BLOCK), lambda b, n: (0, 0, 0)),
            pl.BlockSpec((SWA_Q_HEADS, BLOCK, N_META), lambda b, n: (0, jnp.minimum(n, 1), 0)),
        ],
        out_specs=pl.BlockSpec((None, BLOCK, SWA_Q_W), lambda b, n: (b, n, 0)),
        out_shape=jax.ShapeDtypeStruct((nb, rows, SWA_Q_W), BF16),
        compiler_params=pltpu.CompilerParams(
            dimension_semantics=("arbitrary", "arbitrary"), vmem_limit_bytes=VMEM_LIMIT),
        name="swa",
    )(sinks, proj, proj, proj, proj, proj, projm, projm, bias_w, bias_m)


def _fox_aug_lanes(e):
    lane = lax.broadcasted_iota(jnp.int32, (1, LANES), 1)
    own = (lane < HEAD_DIM) if e == 0 else (lane >= HEAD_DIM)
    return lane, own, (HEAD_DIM if e == 0 else 0)


def _fox_operand_aug(pair, col, e, col_first):
    lane, own, a0 = _fox_aug_lanes(e)
    p0, o0 = (a0, a0 + N_PIECES) if col_first else (a0 + N_PIECES, a0)
    aug = jnp.broadcast_to(jnp.where((lane >= o0) & (lane < o0 + N_PIECES), 1.0, 0.0), pair.shape)
    for i, piece in enumerate(_bf16_pieces(col)):
        aug = jnp.where(lane == p0 + i, piece.astype(F32), aug)
    return jnp.where(own, pair, aug.astype(BF16))


def _fox_own_norm2(pair, e):
    _, own, _ = _fox_aug_lanes(e)
    x = jnp.where(own, pair.astype(F32), 0.0)
    return jnp.sum(x * x, axis=1, keepdims=True)


def _fox_value_aug_t(vpair, e):
    vt = vpair.astype(F32).T
    top = vt[e * HEAD_DIM:(e + 1) * HEAD_DIM]
    return jnp.concatenate([top, jnp.ones_like(top)], axis=0).astype(BF16)


def _fox_kernel(q_ref, k_ref, v_ref, km_ref, vm_ref, c_ref, cm_ref, o_ref, ka_scr, vt_scr, acc_scr, kmax_scr):
    i = pl.program_id(2)
    t = q_ref.shape[0]
    nk = k_ref.shape[0] // t
    nh = acc_scr.shape[0]
    pair_cols = lambda h: slice((h // 2) * PAIR_W, (h // 2 + 1) * PAIR_W)
    colmax = lambda x: jnp.max(x, axis=0, keepdims=True)

    @pl.when(i == 0)
    def _():
        for h in range(nh):
            e = h % 2
            kn2 = colmax(_fox_own_norm2(km_ref[:, pair_cols(h)], e))
            for j in range(nk):
                rows = slice(j * t, (j + 1) * t)
                kpair = k_ref[rows, pair_cols(h)]
                ka_scr[h, j] = _fox_operand_aug(kpair, c_ref[rows, h:h + 1] * (-LOG2E), e, True)
                vt_scr[h, j] = _fox_value_aug_t(v_ref[rows, pair_cols(h)], e)
                kn2 = jnp.maximum(kn2, colmax(_fox_own_norm2(kpair, e)))
            kmax_scr[h] = jnp.broadcast_to(jnp.sqrt(kn2), kmax_scr.shape[1:])

    q_rows = pl.ds(pl.multiple_of(i * t, t), t)
    r = [jnp.sqrt(_fox_own_norm2(q_ref[:, pair_cols(h)], h % 2)) * kmax_scr[h, 0:1, 0:1] * 1.001
         for h in range(nh)]
    worst = r[0]
    for rh in r[1:]:
        worst = jnp.maximum(worst, rh)
    safe = jnp.max(worst) <= FOX_SAFE_LOG2

    qt, kma, vmt = [], [], []
    for h in range(nh):
        e = h % 2
        u = c_ref[q_rows, h:h + 1] * LOG2E - jnp.where(safe, r[h], 0.0)
        qt.append(_fox_operand_aug(q_ref[:, pair_cols(h)], u, e, False).astype(F32).T.astype(BF16))
        kma.append(_fox_operand_aug(km_ref[:, pair_cols(h)], cm_ref[:, h:h + 1] * (-LOG2E), e, True))
        vmt.append(_fox_value_aug_t(vm_ref[:, pair_cols(h)], e))

    def scores(h, ka, diagonal):
        s = _dot(ka, qt[h])
        if diagonal:
            key = lax.broadcasted_iota(jnp.int32, (t, t), 0)
            qry = lax.broadcasted_iota(jnp.int32, (t, t), 1)
            s = jnp.where(key <= qry, s, NEG_INF)
        return s

    @pl.when(safe)
    def _():
        for h in range(nh):
            acc_scr[h] = _dot(vmt[h], jnp.exp2(_dot(kma[h], qt[h])).astype(BF16))

        def tile(j, diagonal):
            s_prev = None
            for h in range(nh):
                s = scores(h, ka_scr[h, j], diagonal)
                if h > 0:
                    acc_scr[h - 1] += _dot(vt_scr[h - 1, j], jnp.exp2(s_prev).astype(BF16))
                s_prev = s
            acc_scr[nh - 1] += _dot(vt_scr[nh - 1, j], jnp.exp2(s_prev).astype(BF16))

        def full_tile(j, carry):
            tile(j, False)
            return carry

        lax.fori_loop(0, i, full_tile, 0)
        tile(i, True)

    @pl.when(jnp.logical_not(safe))
    def _():
        m0 = []
        for h in range(nh):
            s = _dot(kma[h], qt[h])
            m = colmax(s)
            acc_scr[h] = _dot(vmt[h], jnp.exp2(s - m).astype(BF16))
            m0.append(m)

        def tile(j, ms, diagonal):
            out = []
            for h in range(nh):
                s = scores(h, ka_scr[h, j], diagonal)
                m_next = jnp.maximum(ms[h], colmax(s))
                p = jnp.exp2(s - m_next).astype(BF16)
                acc_scr[h] = jnp.exp2(ms[h] - m_next) * acc_scr[h] + _dot(vt_scr[h, j], p)
                out.append(m_next)
            return tuple(out)

        ms = lax.fori_loop(0, i, lambda j, ms: tile(j, ms, False), tuple(m0))
        tile(i, ms, True)

    for pp in range(nh // 2):
        o_t = jnp.concatenate([acc_scr[h, 0:HEAD_DIM] / acc_scr[h, HEAD_DIM:PAIR_W]
                               for h in (2 * pp, 2 * pp + 1)], axis=0)
        o_ref[:, pp * PAIR_W:(pp + 1) * PAIR_W] = o_t.T.astype(BF16)


def _fox(proj, projm, cum_g, cum_mg):
    nb, rows, _ = proj.shape
    t = FOX_T
    nk = rows // t
    nh = cum_g.shape[-1]
    gw = nh * HEAD_DIM
    ngroup = FOX_HEADS // nh
    qcol, kcol, vcol = OFF_VA // gw, OFF_QB // gw, OFF_KB // gw
    return pl.pallas_call(
        _fox_kernel,
        grid=(nb, ngroup, rows // t),
        in_specs=[
            pl.BlockSpec((None, t, gw), lambda b, g, i: (b, i, qcol + g)),
            pl.BlockSpec((None, rows, gw), lambda b, g, i: (b, 0, kcol + g)),
            pl.BlockSpec((None, rows, gw), lambda b, g, i: (b, 0, vcol + g)),
            pl.BlockSpec((N_META, gw), lambda b, g, i: (0, kcol + g)),
            pl.BlockSpec((N_META, gw), lambda b, g, i: (0, vcol + g)),
            pl.BlockSpec((None, None, rows, nh), lambda b, g, i: (b, g, 0, 0)),
            pl.BlockSpec((None, N_META, nh), lambda b, g, i: (g, 0, 0)),
        ],
        out_specs=pl.BlockSpec((None, t, gw), lambda b, g, i: (b, i, g)),
        out_shape=jax.ShapeDtypeStruct((nb, rows, FOX_W), BF16),
        scratch_shapes=[
            pltpu.VMEM((nh, nk, t, PAIR_W), BF16),
            pltpu.VMEM((nh, nk, PAIR_W, t), BF16),
            pltpu.VMEM((nh, PAIR_W, t), F32),
            pltpu.VMEM((nh, 8, LANES), F32),
        ],
        compiler_params=pltpu.CompilerParams(
            dimension_semantics=("arbitrary", "arbitrary", "arbitrary"), vmem_limit_bytes=VMEM_LIMIT),
        name="fox",
    )(proj, proj, proj, projm, projm, cum_g, cum_mg)


def _ffn_kernel(oa_ref, ob_ref, x_ref, wo_ref, g1_ref, g2_ref, wg_ref, wu_ref, wd_ref, g3_ref, out_ref):
    mix = jnp.concatenate([oa_ref[...], ob_ref[...]], axis=1)
    h1 = x_ref[...] + _rms(_dot(mix, wo_ref[...]), g1_ref[...])
    hn = _rms(h1, g2_ref[...]).astype(BF16)
    gate = _dot(hn, wg_ref[...])
    up = _dot(hn, wu_ref[...])
    act = (gate / (1.0 + jnp.exp(-gate)) * up).astype(BF16)
    out_ref[...] = h1 + _rms(_dot(act, wd_ref[...]), g3_ref[...])


def _ffn(o_a, o_b, x3, wo, g1, g2, wg, wu, wd, g3):
    nb, rows, _ = x3.shape
    tm = FFN_TM
    const = lambda b, t: (0, 0)
    resident = lambda shape: pl.BlockSpec(shape, const, pipeline_mode=pl.Buffered(1))
    row = lambda w: pl.BlockSpec((None, tm, w), lambda b, t: (b, t, 0))
    return pl.pallas_call(
        _ffn_kernel,
        grid=(nb, rows // tm),
        in_specs=[
            row(SWA_Q_W), row(FOX_W), row(D_MODEL),
            resident((D_MODEL, D_MODEL)), resident((1, D_MODEL)), resident((1, D_MODEL)),
            resident((D_MODEL, D_FF)), resident((D_MODEL, D_FF)), resident((D_FF, D_MODEL)),
            resident((1, D_MODEL)),
        ],
        out_specs=row(D_MODEL),
        out_shape=jax.ShapeDtypeStruct((nb, rows, D_MODEL), F32),
        compiler_params=pltpu.CompilerParams(
            dimension_semantics=("arbitrary", "arbitrary"), vmem_limit_bytes=VMEM_LIMIT),
        name="outproj_ffn",
    )(o_a, o_b, x3, wo, g1, g2, wg, wu, wd, g3)


def kernel(x, meta_tokens, rel_bias, ln_pre_mix, ln_post_mix, ln_pre_ffn, ln_post_ffn,
           w_in, b_forget, sinks, w_out, w_gate_up, w_down):
    nb, seq, d = x.shape
    assert d == D_MODEL and seq % IN_TM == 0 and seq % FOX_T == 0 and seq % FFN_TM == 0
    assert w_in.shape[0] == 1, "single-layer block"
    assert meta_tokens.shape == (N_META, D_MODEL)
    x = x.astype(F32)

    w_qkv = w_in[0, :, :OFF_VB].astype(BF16)
    w_f = jnp.pad(w_in[0, :, OFF_VB:], ((0, 0), (0, LANES - FOX_HEADS))).astype(BF16)
    b_f = jnp.pad(b_forget[0].astype(F32), (0, LANES - FOX_HEADS)).reshape(1, LANES)
    g_pre = ln_pre_mix[0].astype(F32).reshape(1, D_MODEL)

    bias_w, bias_m = _bias_tables(rel_bias)

    zero_c = jnp.zeros((1, LANES), F32)
    projm, cum_m = _inproj(meta_tokens.astype(F32)[None], g_pre, w_qkv, w_f, b_f, zero_c, N_META)
    projm, cum_m = projm[0], cum_m[0]
    c0 = jnp.pad(cum_m[N_META - 1], (0, LANES - FOX_HEADS)).reshape(1, LANES)
    proj, cum = _inproj(x, g_pre, w_qkv, w_f, b_f, c0, IN_TM)

    o_a = _swa(proj, projm, bias_w, bias_m, sinks[0].astype(F32))

    ngroup = FOX_HEADS // FOX_GROUP
    cum_g = cum.reshape(nb, seq, ngroup, FOX_GROUP).transpose(0, 2, 1, 3)
    cum_mg = cum_m.reshape(N_META, ngroup, FOX_GROUP).transpose(1, 0, 2)
    o_b = _fox(proj, projm, cum_g, cum_mg)

    w_gu = w_gate_up[0]
    return _ffn(o_a, o_b, x,
                w_out[0].astype(BF16),
                ln_post_mix[0].astype(F32).reshape(1, D_MODEL),
                ln_pre_ffn[0].astype(F32).reshape(1, D_MODEL),
                w_gu[:, :D_FF].astype(BF16), w_gu[:, D_FF:].astype(BF16),
                w_down[0].astype(BF16),
                ln_post_ffn[0].astype(F32).reshape(1, D_MODEL))
```

```python
import math

import numpy as np
import jax
import jax.numpy as jnp
from jax import lax
from jax.experimental import pallas as pl
from jax.experimental.pallas import tpu as pltpu

D_MODEL = 1024
N_META = 16
HEAD_DIM = 64
SWA_Q_HEADS = 8
SWA_KV_HEADS = 2
SWA_GROUP = SWA_Q_HEADS // SWA_KV_HEADS
FOX_HEADS = 8
SWA_Q_W = SWA_Q_HEADS * HEAD_DIM
SWA_KV_W = SWA_KV_HEADS * HEAD_DIM
FOX_W = FOX_HEADS * HEAD_DIM
OFF_QA = SWA_Q_W
OFF_KA = OFF_QA + SWA_KV_W
OFF_VA = OFF_KA + SWA_KV_W
OFF_QB = OFF_VA + FOX_W
OFF_KB = OFF_QB + FOX_W
OFF_VB = OFF_KB + FOX_W
WINDOW = 128
BLOCK = 128
N_BUCKETS = 32
MAX_DISTANCE = 128
D_FF = 2816
EPS = 1e-6
NEG_INF = -1e30
SCALE = HEAD_DIM ** -0.5
LOG2E = math.log2(math.e)

LANES = 128
PAIR_W = 2 * HEAD_DIM
VMEM_LIMIT = 56 * 1024 * 1024

SWA_KEYS = 2 * BLOCK + N_META
SWA_TQ = 512
SWA_ONES = 16
IN_TM = 512
FOX_T = 512
FOX_GROUP = 4
FOX_SAFE_LOG2 = 40.0
FFN_TM = 256
N_PIECES = 3

F32 = jnp.float32
BF16 = jnp.bfloat16


def _dot(a, b):
    return jnp.dot(a, b, preferred_element_type=F32)


def _rms(t, g):
    return t * lax.rsqrt(jnp.mean(t * t, axis=-1, keepdims=True) + EPS) * g


def _bf16_pieces(v):
    pieces = []
    for _ in range(N_PIECES - 1):
        p = v.astype(BF16)
        pieces.append(p)
        v = v - p.astype(F32)
    pieces.append(v.astype(BF16))
    return pieces


def _t5_bucket_np(dist):
    n = np.maximum(dist, 0).astype(np.int32)
    max_exact = N_BUCKETS // 2
    nf = np.maximum(n, 1).astype(np.float32)
    large = max_exact + (np.log(nf / np.float32(max_exact)) / np.float32(math.log(MAX_DISTANCE / max_exact))
                         * np.float32(N_BUCKETS - max_exact)).astype(np.int32)
    large = np.minimum(large, N_BUCKETS - 1)
    return np.where(n < max_exact, n, large).astype(np.int32)


def _bias_kernel(tab_ref, bkt_ref, valid_ref, out_ref):
    h = pl.program_id(1)
    bkt = bkt_ref[...]
    acc = jnp.zeros(bkt.shape, F32)
    for b in range(N_BUCKETS):
        acc = jnp.where(bkt == b, tab_ref[b, h], acc)
    out_ref[...] = jnp.where(valid_ref[...] > 0, acc * LOG2E, NEG_INF)


def _bias_tables(rel_bias):
    ki = np.arange(2 * BLOCK)[:, None]
    qi = np.arange(BLOCK)[None, :]
    d_w = qi + BLOCK - ki
    bkt_w = _t5_bucket_np(d_w)
    in_window = (d_w >= 0) & (d_w < WINDOW)
    mi = np.arange(N_META)[:, None]
    bkt_m0 = _t5_bucket_np(N_META + qi - mi)
    far = _t5_bucket_np(np.asarray([[N_META + BLOCK - (N_META - 1)]]))
    assert far[0, 0] == N_BUCKETS - 1, "meta keys of later blocks must share the last bucket"
    bkt = np.stack([np.concatenate([bkt_w, bkt_m0]), np.concatenate([bkt_w, np.full_like(bkt_m0, far[0, 0])])])
    meta_ok = np.ones((N_META, BLOCK), bool)
    valid = np.stack([np.concatenate([in_window & (ki >= BLOCK), meta_ok]), np.concatenate([in_window, meta_ok])])
    spec = pl.BlockSpec((None, SWA_KEYS, BLOCK), lambda a, h: (a, 0, 0))
    return pl.pallas_call(
        _bias_kernel,
        grid=(2, SWA_Q_HEADS),
        in_specs=[pl.BlockSpec(memory_space=pltpu.SMEM), spec, spec],
        out_specs=pl.BlockSpec((None, SWA_KEYS, BLOCK), lambda a, h: (a, 0, h)),
        out_shape=jax.ShapeDtypeStruct((2, SWA_KEYS, SWA_Q_HEADS * BLOCK), F32),
        name="bias_tables",
    )(rel_bias.astype(F32), jnp.asarray(bkt.astype(np.int32)), jnp.asarray(valid.astype(np.int32)))


def _inproj_kernel(x_ref, g_ref, w_ref, wf_ref, bf_ref, tri_ref, c0_ref, proj_ref, cum_ref, carry_ref):
    @pl.when(pl.program_id(1) == 0)
    def _():
        carry_ref[...] = c0_ref[...]

    y = _rms(x_ref[...], g_ref[...]).astype(BF16)
    acc = _dot(y, w_ref[...])
    proj_ref[:, 0:OFF_QA] = (acc[:, 0:OFF_QA] * (SCALE * LOG2E)).astype(BF16)
    proj_ref[:, OFF_QA:OFF_VA] = acc[:, OFF_QA:OFF_VA].astype(BF16)
    proj_ref[:, OFF_VA:OFF_QB] = (acc[:, OFF_VA:OFF_QB] * (SCALE * LOG2E)).astype(BF16)
    proj_ref[:, OFF_QB:OFF_VB] = acc[:, OFF_QB:OFF_VB].astype(BF16)

    f = _dot(y, wf_ref[...]) + bf_ref[...]
    ls = jnp.minimum(f, 0.0) - jnp.log1p(jnp.exp(-jnp.abs(f)))
    tri = tri_ref[...]
    cum = carry_ref[...]
    for piece in _bf16_pieces(ls):
        cum = cum + _dot(tri, piece)
    cum_ref[...] = cum[:, :FOX_HEADS]
    tm = cum.shape[0]
    carry_ref[...] = cum[tm - 1:tm, :]


def _inproj(x3, g, w, wf, bf, c0, tm):
    nb, rows, _ = x3.shape
    tri = jnp.asarray(np.tril(np.ones((tm, tm), np.float32)), BF16)
    const = lambda b, t: (0, 0)
    return pl.pallas_call(
        _inproj_kernel,
        grid=(nb, rows // tm),
        in_specs=[
            pl.BlockSpec((None, tm, D_MODEL), lambda b, t: (b, t, 0)),
            pl.BlockSpec((1, D_MODEL), const),
            pl.BlockSpec((D_MODEL, OFF_VB), const),
            pl.BlockSpec((D_MODEL, LANES), const),
            pl.BlockSpec((1, LANES), const),
            pl.BlockSpec((tm, tm), const),
            pl.BlockSpec((1, LANES), const),
        ],
        out_specs=[
            pl.BlockSpec((None, tm, OFF_VB), lambda b, t: (b, t, 0)),
            pl.BlockSpec((None, tm, FOX_HEADS), lambda b, t: (b, t, 0)),
        ],
        out_shape=[
            jax.ShapeDtypeStruct((nb, rows, OFF_VB), BF16),
            jax.ShapeDtypeStruct((nb, rows, FOX_HEADS), F32),
        ],
        scratch_shapes=[pltpu.VMEM((1, LANES), F32)],
        compiler_params=pltpu.CompilerParams(
            dimension_semantics=("arbitrary", "arbitrary"), vmem_limit_bytes=VMEM_LIMIT),
        name="inproj",
    )(x3, g, w, wf, bf, tri, c0)


def _swa_scores(q, k_all, bias):
    nq = q.shape[0]
    zeros = jnp.zeros((HEAD_DIM, nq), F32)
    blocks = []
    for pair in range(SWA_Q_HEADS // 2):
        q_t = q[:, pair * PAIR_W:(pair + 1) * PAIR_W].astype(F32).T
        for hp in range(2):
            qh = q_t[hp * HEAD_DIM:(hp + 1) * HEAD_DIM]
            g = (2 * pair + hp) // SWA_GROUP
            blocks.append(jnp.concatenate([qh, zeros] if g == 0 else [zeros, qh], axis=0))
    q_bd = jnp.concatenate(blocks, axis=1).astype(BF16)
    return _dot(k_all, q_bd) + bias


def _swa_outputs(s, v_all, sink_ref):
    nq = s.shape[1] // SWA_Q_HEADS
    ps, sink_terms = [], []
    for h in range(SWA_Q_HEADS):
        sh = s[:, h * nq:(h + 1) * nq]
        sink = sink_ref[h] * LOG2E
        m = jnp.maximum(jnp.max(sh, axis=0, keepdims=True), sink)
        ps.append(jnp.exp2(sh - m).astype(BF16))
        sink_terms.append(jnp.exp2(sink - m))
    n = BLOCK
    v_t = jnp.concatenate([v_all[0:n].astype(F32).T, v_all[n:2 * n].astype(F32).T,
                           v_all[2 * n:].astype(F32).T], axis=1)
    ones = jnp.ones((SWA_ONES, v_t.shape[1]), F32)
    outs = []
    for g in range(SWA_KV_HEADS):
        v_aug = jnp.concatenate([v_t[g * HEAD_DIM:(g + 1) * HEAD_DIM], ones], axis=0).astype(BF16)
        p_g = jnp.concatenate(ps[g * SWA_GROUP:(g + 1) * SWA_GROUP], axis=1)
        o_g = _dot(v_aug, p_g)
        for j in range(SWA_GROUP):
            cols = slice(j * nq, (j + 1) * nq)
            l = o_g[HEAD_DIM:HEAD_DIM + 1, cols] + sink_terms[g * SWA_GROUP + j]
            outs.append(o_g[0:HEAD_DIM, cols] / l)
    pairs = [jnp.concatenate(outs[2 * p:2 * p + 2], axis=0).T for p in range(SWA_Q_HEADS // 2)]
    return jnp.concatenate(pairs, axis=1).astype(BF16)


def _swa_kernel(sink_ref, q_ref, kc_ref, kp_ref, vc_ref, vp_ref, km_ref, vm_ref, bias0_ref, bias1_ref, o_ref):
    n = BLOCK
    nsub = q_ref.shape[0] // n
    k_rows = [kp_ref[...]] + [kc_ref[c * n:(c + 1) * n] for c in range(nsub)]
    v_rows = [vp_ref[...]] + [vc_ref[c * n:(c + 1) * n] for c in range(nsub)]
    s = []
    for c in range(nsub):
        k_all = jnp.concatenate([k_rows[c], k_rows[c + 1], km_ref[...]], axis=0)
        bias = bias0_ref[...] if c == 0 else bias1_ref[...]
        s.append(_swa_scores(q_ref[c * n:(c + 1) * n], k_all, bias))
    for c in range(nsub):
        v_all = jnp.concatenate([v_rows[c], v_rows[c + 1], vm_ref[...]], axis=0)
        o_ref[c * n:(c + 1) * n] = _swa_outputs(s[c], v_all, sink_ref)


def _swa(proj, projm, bias, sinks):
    nb, rows, _ = proj.shape
    tq = SWA_TQ
    ratio = tq // BLOCK
    kcol, vcol = OFF_QA // SWA_KV_W, OFF_KA // SWA_KV_W
    cur = lambda c: (lambda b, n: (b, n, c))
    prev = lambda c: (lambda b, n: (b, jnp.maximum(n * ratio - 1, 0), c))
    bias_spec = lambda idx: pl.BlockSpec((None, SWA_KEYS, SWA_Q_HEADS * BLOCK), idx)
    return pl.pallas_call(
        _swa_kernel,
        grid=(nb, rows // tq),
        in_specs=[
            pl.BlockSpec(memory_space=pltpu.SMEM),
            pl.BlockSpec((None, tq, SWA_Q_W), lambda b, n: (b, n, 0)),
            pl.BlockSpec((None, tq, SWA_KV_W), cur(kcol)),
            pl.BlockSpec((None, BLOCK, SWA_KV_W), prev(kcol)),
            pl.BlockSpec((None, tq, SWA_KV_W), cur(vcol)),
            pl.BlockSpec((None, BLOCK, SWA_KV_W), prev(vcol)),
            pl.BlockSpec((N_META, SWA_KV_W), lambda b, n: (0, kcol)),
            pl.BlockSpec((N_META, SWA_KV_W), lambda b, n: (0, vcol)),
            bias_spec(lambda b, n: (jnp.minimum(n, 1), 0, 0)),
            bias_spec(lambda b, n: (1, 0, 0)),
        ],
        out_specs=pl.BlockSpec((None, tq, SWA_Q_W), lambda b, n: (b, n, 0)),
        out_shape=jax.ShapeDtypeStruct((nb, rows, SWA_Q_W), BF16),
        compiler_params=pltpu.CompilerParams(
            dimension_semantics=("arbitrary", "arbitrary"), vmem_limit_bytes=VMEM_LIMIT),
        name="swa",
    )(sinks, proj, proj, proj, proj, proj, projm, projm, bias, bias)


def _fox_aug_lanes(e):
    lane = lax.broadcasted_iota(jnp.int32, (1, LANES), 1)
    own = (lane < HEAD_DIM) if e == 0 else (lane >= HEAD_DIM)
    return lane, own, (HEAD_DIM if e == 0 else 0)


def _fox_operand_aug(pair, col, e, col_first):
    lane, own, a0 = _fox_aug_lanes(e)
    p0, o0 = (a0, a0 + N_PIECES) if col_first else (a0 + N_PIECES, a0)
    aug = jnp.broadcast_to(jnp.where((lane >= o0) & (lane < o0 + N_PIECES), 1.0, 0.0), pair.shape)
    for i, piece in enumerate(_bf16_pieces(col)):
        aug = jnp.where(lane == p0 + i, piece.astype(F32), aug)
    return jnp.where(own, pair, aug.astype(BF16))


def _fox_own_norm2(pair, e):
    _, own, _ = _fox_aug_lanes(e)
    x = jnp.where(own, pair.astype(F32), 0.0)
    return jnp.sum(x * x, axis=1, keepdims=True)


def _fox_value_aug_t(vpair, e):
    vt = vpair.astype(F32).T
    top = vt[e * HEAD_DIM:(e + 1) * HEAD_DIM]
    return jnp.concatenate([top, jnp.ones_like(top)], axis=0).astype(BF16)


def _fox_kernel(q_ref, k_ref, v_ref, km_ref, vm_ref, c_ref, cm_ref, o_ref, ka_scr, vt_scr, acc_scr, kmax_scr):
    i = pl.program_id(2)
    t = q_ref.shape[0]
    nk = k_ref.shape[0] // t
    nh = acc_scr.shape[0]
    pair_cols = lambda h: slice((h // 2) * PAIR_W, (h // 2 + 1) * PAIR_W)
    colmax = lambda x: jnp.max(x, axis=0, keepdims=True)

    @pl.when(i == 0)
    def _():
        for h in range(nh):
            e = h % 2
            kn2 = colmax(_fox_own_norm2(km_ref[:, pair_cols(h)], e))
            for j in range(nk):
                rows = slice(j * t, (j + 1) * t)
                kpair = k_ref[rows, pair_cols(h)]
                ka_scr[h, j] = _fox_operand_aug(kpair, c_ref[rows, h:h + 1] * (-LOG2E), e, True)
                vt_scr[h, j] = _fox_value_aug_t(v_ref[rows, pair_cols(h)], e)
                kn2 = jnp.maximum(kn2, colmax(_fox_own_norm2(kpair, e)))
            kmax_scr[h] = jnp.broadcast_to(jnp.sqrt(kn2), kmax_scr.shape[1:])

    q_rows = pl.ds(pl.multiple_of(i * t, t), t)
    r = [jnp.sqrt(_fox_own_norm2(q_ref[:, pair_cols(h)], h % 2)) * kmax_scr[h, 0:1, 0:1] * 1.001
         for h in range(nh)]
    worst = r[0]
    for rh in r[1:]:
        worst = jnp.maximum(worst, rh)
    safe = jnp.max(worst) <= FOX_SAFE_LOG2

    qt, kma, vmt = [], [], []
    for h in range(nh):
        e = h % 2
        u = c_ref[q_rows, h:h + 1] * LOG2E - jnp.where(safe, r[h], 0.0)
        qt.append(_fox_operand_aug(q_ref[:, pair_cols(h)], u, e, False).astype(F32).T.astype(BF16))
        kma.append(_fox_operand_aug(km_ref[:, pair_cols(h)], cm_ref[:, h:h + 1] * (-LOG2E), e, True))
        vmt.append(_fox_value_aug_t(vm_ref[:, pair_cols(h)], e))

    def scores(h, ka, diagonal):
        s = _dot(ka, qt[h])
        if diagonal:
            key = lax.broadcasted_iota(jnp.int32, (t, t), 0)
            qry = lax.broadcasted_iota(jnp.int32, (t, t), 1)
            s = jnp.where(key <= qry, s, NEG_INF)
        return s

    @pl.when(safe)
    def _():
        for h in range(nh):
            acc_scr[h] = _dot(vmt[h], jnp.exp2(_dot(kma[h], qt[h])).astype(BF16))

        def tile(j, diagonal):
            s_prev = None
            for h in range(nh):
                s = scores(h, ka_scr[h, j], diagonal)
                if h > 0:
                    acc_scr[h - 1] += _dot(vt_scr[h - 1, j], jnp.exp2(s_prev).astype(BF16))
                s_prev = s
            acc_scr[nh - 1] += _dot(vt_scr[nh - 1, j], jnp.exp2(s_prev).astype(BF16))

        def full_tile(j, carry):
            tile(j, False)
            return carry

        lax.fori_loop(0, i, full_tile, 0)
        tile(i, True)

    @pl.when(jnp.logical_not(safe))
    def _():
        m0 = []
        for h in range(nh):
            s = _dot(kma[h], qt[h])
            m = colmax(s)
            acc_scr[h] = _dot(vmt[h], jnp.exp2(s - m).astype(BF16))
            m0.append(m)

        def tile(j, ms, diagonal):
            out = []
            for h in range(nh):
                s = scores(h, ka_scr[h, j], diagonal)
                m_next = jnp.maximum(ms[h], colmax(s))
                p = jnp.exp2(s - m_next).astype(BF16)
                acc_scr[h] = jnp.exp2(ms[h] - m_next) * acc_scr[h] + _dot(vt_scr[h, j], p)
                out.append(m_next)
            return tuple(out)

        ms = lax.fori_loop(0, i, lambda j, ms: tile(j, ms, False), tuple(m0))
        tile(i, ms, True)

    for pp in range(nh // 2):
        o_t = jnp.concatenate([acc_scr[h, 0:HEAD_DIM] / acc_scr[h, HEAD_DIM:PAIR_W]
                               for h in (2 * pp, 2 * pp + 1)], axis=0)
        o_ref[:, pp * PAIR_W:(pp + 1) * PAIR_W] = o_t.T.astype(BF16)


def _fox(proj, projm, cum_g, cum_mg):
    nb, rows, _ = proj.shape
    t = FOX_T
    nk = rows // t
    nh = cum_g.shape[-1]
    gw = nh * HEAD_DIM
    ngroup = FOX_HEADS // nh
    qcol, kcol, vcol = OFF_VA // gw, OFF_QB // gw, OFF_KB // gw
    return pl.pallas_call(
        _fox_kernel,
        grid=(nb, ngroup, rows // t),
        in_specs=[
            pl.BlockSpec((None, t, gw), lambda b, g, i: (b, i, qcol + g)),
            pl.BlockSpec((None, rows, gw), lambda b, g, i: (b, 0, kcol + g)),
            pl.BlockSpec((None, rows, gw), lambda b, g, i: (b, 0, vcol + g)),
            pl.BlockSpec((N_META, gw), lambda b, g, i: (0, kcol + g)),
            pl.BlockSpec((N_META, gw), lambda b, g, i: (0, vcol + g)),
            pl.BlockSpec((None, None, rows, nh), lambda b, g, i: (b, g, 0, 0)),
            pl.BlockSpec((None, N_META, nh), lambda b, g, i: (g, 0, 0)),
        ],
        out_specs=pl.BlockSpec((None, t, gw), lambda b, g, i: (b, i, g)),
        out_shape=jax.ShapeDtypeStruct((nb, rows, FOX_W), BF16),
        scratch_shapes=[
            pltpu.VMEM((nh, nk, t, PAIR_W), BF16),
            pltpu.VMEM((nh, nk, PAIR_W, t), BF16),
            pltpu.VMEM((nh, PAIR_W, t), F32),
            pltpu.VMEM((nh, 8, LANES), F32),
        ],
        compiler_params=pltpu.CompilerParams(
            dimension_semantics=("arbitrary", "arbitrary", "arbitrary"), vmem_limit_bytes=VMEM_LIMIT),
        name="fox",
    )(proj, proj, proj, projm, projm, cum_g, cum_mg)


def _ffn_kernel(oa_ref, ob_ref, x_ref, wo_ref, g1_ref, g2_ref, wg_ref, wu_ref, wd_ref, g3_ref, out_ref):
    mix = jnp.concatenate([oa_ref[...], ob_ref[...]], axis=1)
    h1 = x_ref[...] + _rms(_dot(mix, wo_ref[...]), g1_ref[...])
    hn = _rms(h1, g2_ref[...]).astype(BF16)
    gate = _dot(hn, wg_ref[...])
    up = _dot(hn, wu_ref[...])
    act = (gate / (1.0 + jnp.exp(-gate)) * up).astype(BF16)
    out_ref[...] = h1 + _rms(_dot(act, wd_ref[...]), g3_ref[...])


def _ffn(o_a, o_b, x3, wo, g1, g2, wg, wu, wd, g3):
    nb, rows, _ = x3.shape
    tm = FFN_TM
    const = lambda b, t: (0, 0)
    resident = lambda shape: pl.BlockSpec(shape, const, pipeline_mode=pl.Buffered(1))
    row = lambda w: pl.BlockSpec((None, tm, w), lambda b, t: (b, t, 0))
    return pl.pallas_call(
        _ffn_kernel,
        grid=(nb, rows // tm),
        in_specs=[
            row(SWA_Q_W), row(FOX_W), row(D_MODEL),
            resident((D_MODEL, D_MODEL)), resident((1, D_MODEL)), resident((1, D_MODEL)),
            resident((D_MODEL, D_FF)), resident((D_MODEL, D_FF)), resident((D_FF, D_MODEL)),
            resident((1, D_MODEL)),
        ],
        out_specs=row(D_MODEL),
        out_shape=jax.ShapeDtypeStruct((nb, rows, D_MODEL), F32),
        compiler_params=pltpu.CompilerParams(
            dimension_semantics=("arbitrary", "arbitrary"), vmem_limit_bytes=VMEM_LIMIT),
        name="outproj_ffn",
    )(o_a, o_b, x3, wo, g1, g2, wg, wu, wd, g3)


def kernel(x, meta_tokens, rel_bias, ln_pre_mix, ln_post_mix, ln_pre_ffn, ln_post_ffn,
           w_in, b_forget, sinks, w_out, w_gate_up, w_down):
    nb, seq, d = x.shape
    assert d == D_MODEL and seq % IN_TM == 0 and seq % FOX_T == 0 and seq % FFN_TM == 0 and seq % SWA_TQ == 0
    assert w_in.shape[0] == 1, "single-layer block"
    assert meta_tokens.shape == (N_META, D_MODEL)
    x = x.astype(F32)

    w_qkv = w_in[0, :, :OFF_VB].astype(BF16)
    w_f = jnp.pad(w_in[0, :, OFF_VB:], ((0, 0), (0, LANES - FOX_HEADS))).astype(BF16)
    b_f = jnp.pad(b_forget[0].astype(F32), (0, LANES - FOX_HEADS)).reshape(1, LANES)
    g_pre = ln_pre_mix[0].astype(F32).reshape(1, D_MODEL)

    swa_bias = _bias_tables(rel_bias)

    zero_c = jnp.zeros((1, LANES), F32)
    projm, cum_m = _inproj(meta_tokens.astype(F32)[None], g_pre, w_qkv, w_f, b_f, zero_c, N_META)
    projm, cum_m = projm[0], cum_m[0]
    c0 = jnp.pad(cum_m[N_META - 1], (0, LANES - FOX_HEADS)).reshape(1, LANES)
    proj, cum = _inproj(x, g_pre, w_qkv, w_f, b_f, c0, IN_TM)

    o_a = _swa(proj, projm, swa_bias, sinks[0].astype(F32))

    ngroup = FOX_HEADS // FOX_GROUP
    cum_g = cum.reshape(nb, seq, ngroup, FOX_GROUP).transpose(0, 2, 1, 3)
    cum_mg = cum_m.reshape(N_META, ngroup, FOX_GROUP).transpose(1, 0, 2)
    o_b = _fox(proj, projm, cum_g, cum_mg)

    w_gu = w_gate_up[0]
    return _ffn(o_a, o_b, x,
                w_out[0].astype(BF16),
                ln_post_mix[0].astype(F32).reshape(1, D_MODEL),
                ln_pre_ffn[0].astype(F32).reshape(1, D_MODEL),
                w_gu[:, :D_FF].astype(BF16), w_gu[:, D_FF:].astype(BF16),
                w_down[0].astype(BF16),
                ln_post_ffn[0].astype(F32).reshape(1, D_MODEL))
```

```python
import math

import numpy as np
import jax
import jax.numpy as jnp
from jax import lax
from jax.experimental import pallas as pl
from jax.experimental.pallas import tpu as pltpu

D_MODEL = 1024
N_META = 16
HEAD_DIM = 64
SWA_Q_HEADS = 8
SWA_KV_HEADS = 2
SWA_GROUP = SWA_Q_HEADS // SWA_KV_HEADS
FOX_HEADS = 8
SWA_Q_W = SWA_Q_HEADS * HEAD_DIM
SWA_KV_W = SWA_KV_HEADS * HEAD_DIM
FOX_W = FOX_HEADS * HEAD_DIM
OFF_QA = SWA_Q_W
OFF_KA = OFF_QA + SWA_KV_W
OFF_VA = OFF_KA + SWA_KV_W
OFF_QB = OFF_VA + FOX_W
OFF_KB = OFF_QB + FOX_W
OFF_VB = OFF_KB + FOX_W
WINDOW = 128
BLOCK = 128
N_BUCKETS = 32
MAX_DISTANCE = 128
D_FF = 2816
EPS = 1e-6
NEG_INF = -1e30
SCALE = HEAD_DIM ** -0.5
LOG2E = math.log2(math.e)

LANES = 128
PAIR_W = 2 * HEAD_DIM
VMEM_LIMIT = 56 * 1024 * 1024

SWA_KEYS = 2 * BLOCK + N_META
SWA_TQ = 512
SWA_ONES = 16
IN_TM = 512
FOX_T = 512
FOX_GROUP = 4
FOX_SAFE_LOG2 = 40.0
FFN_TM = 512
FFN_PARTS = 2
N_PIECES = 3

F32 = jnp.float32
BF16 = jnp.bfloat16


def _dot(a, b):
    return jnp.dot(a, b, preferred_element_type=F32)


def _rms(t, g):
    return t * lax.rsqrt(jnp.mean(t * t, axis=-1, keepdims=True) + EPS) * g


def _bf16_pieces(v):
    pieces = []
    for _ in range(N_PIECES - 1):
        p = v.astype(BF16)
        pieces.append(p)
        v = v - p.astype(F32)
    pieces.append(v.astype(BF16))
    return pieces


def _t5_bucket_np(dist):
    n = np.maximum(dist, 0).astype(np.int32)
    max_exact = N_BUCKETS // 2
    nf = np.maximum(n, 1).astype(np.float32)
    large = max_exact + (np.log(nf / np.float32(max_exact)) / np.float32(math.log(MAX_DISTANCE / max_exact))
                         * np.float32(N_BUCKETS - max_exact)).astype(np.int32)
    large = np.minimum(large, N_BUCKETS - 1)
    return np.where(n < max_exact, n, large).astype(np.int32)


def _bias_kernel(tab_ref, bkt_ref, valid_ref, out_ref):
    h = pl.program_id(1)
    bkt = bkt_ref[...]
    acc = jnp.zeros(bkt.shape, F32)
    for b in range(N_BUCKETS):
        acc = jnp.where(bkt == b, tab_ref[b, h], acc)
    out_ref[...] = jnp.where(valid_ref[...] > 0, acc * LOG2E, NEG_INF)


def _bias_tables(rel_bias):
    ki = np.arange(2 * BLOCK)[:, None]
    qi = np.arange(BLOCK)[None, :]
    d_w = qi + BLOCK - ki
    bkt_w = _t5_bucket_np(d_w)
    in_window = (d_w >= 0) & (d_w < WINDOW)
    mi = np.arange(N_META)[:, None]
    bkt_m0 = _t5_bucket_np(N_META + qi - mi)
    far = _t5_bucket_np(np.asarray([[N_META + BLOCK - (N_META - 1)]]))
    assert far[0, 0] == N_BUCKETS - 1, "meta keys of later blocks must share the last bucket"
    bkt = np.stack([np.concatenate([bkt_w, bkt_m0]), np.concatenate([bkt_w, np.full_like(bkt_m0, far[0, 0])])])
    meta_ok = np.ones((N_META, BLOCK), bool)
    valid = np.stack([np.concatenate([in_window & (ki >= BLOCK), meta_ok]), np.concatenate([in_window, meta_ok])])
    spec = pl.BlockSpec((None, SWA_KEYS, BLOCK), lambda a, h: (a, 0, 0))
    return pl.pallas_call(
        _bias_kernel,
        grid=(2, SWA_Q_HEADS),
        in_specs=[pl.BlockSpec(memory_space=pltpu.SMEM), spec, spec],
        out_specs=pl.BlockSpec((None, SWA_KEYS, BLOCK), lambda a, h: (a, 0, h)),
        out_shape=jax.ShapeDtypeStruct((2, SWA_KEYS, SWA_Q_HEADS * BLOCK), F32),
        name="bias_tables",
    )(rel_bias.astype(F32), jnp.asarray(bkt.astype(np.int32)), jnp.asarray(valid.astype(np.int32)))


def _inproj_kernel(x_ref, g_ref, w_ref, wf_ref, bf_ref, tri_ref, c0_ref, proj_ref, cum_ref, carry_ref):
    @pl.when(pl.program_id(1) == 0)
    def _():
        carry_ref[...] = c0_ref[...]

    y = _rms(x_ref[...], g_ref[...]).astype(BF16)
    acc = _dot(y, w_ref[...])
    proj_ref[:, 0:OFF_QA] = (acc[:, 0:OFF_QA] * (SCALE * LOG2E)).astype(BF16)
    proj_ref[:, OFF_QA:OFF_VA] = acc[:, OFF_QA:OFF_VA].astype(BF16)
    proj_ref[:, OFF_VA:OFF_QB] = (acc[:, OFF_VA:OFF_QB] * (SCALE * LOG2E)).astype(BF16)
    proj_ref[:, OFF_QB:OFF_VB] = acc[:, OFF_QB:OFF_VB].astype(BF16)

    f = _dot(y, wf_ref[...]) + bf_ref[...]
    ls = jnp.minimum(f, 0.0) - jnp.log1p(jnp.exp(-jnp.abs(f)))
    tri = tri_ref[...]
    cum = carry_ref[...]
    for piece in _bf16_pieces(ls):
        cum = cum + _dot(tri, piece)
    cum_ref[...] = cum[:, :FOX_HEADS]
    tm = cum.shape[0]
    carry_ref[...] = cum[tm - 1:tm, :]


def _inproj(x3, g, w, wf, bf, c0, tm):
    nb, rows, _ = x3.shape
    tri = jnp.asarray(np.tril(np.ones((tm, tm), np.float32)), BF16)
    const = lambda b, t: (0, 0)
    return pl.pallas_call(
        _inproj_kernel,
        grid=(nb, rows // tm),
        in_specs=[
            pl.BlockSpec((None, tm, D_MODEL), lambda b, t: (b, t, 0)),
            pl.BlockSpec((1, D_MODEL), const),
            pl.BlockSpec((D_MODEL, OFF_VB), const),
            pl.BlockSpec((D_MODEL, LANES), const),
            pl.BlockSpec((1, LANES), const),
            pl.BlockSpec((tm, tm), const),
            pl.BlockSpec((1, LANES), const),
        ],
        out_specs=[
            pl.BlockSpec((None, tm, OFF_VB), lambda b, t: (b, t, 0)),
            pl.BlockSpec((None, tm, FOX_HEADS), lambda b, t: (b, t, 0)),
        ],
        out_shape=[
            jax.ShapeDtypeStruct((nb, rows, OFF_VB), BF16),
            jax.ShapeDtypeStruct((nb, rows, FOX_HEADS), F32),
        ],
        scratch_shapes=[pltpu.VMEM((1, LANES), F32)],
        compiler_params=pltpu.CompilerParams(
            dimension_semantics=("arbitrary", "arbitrary"), vmem_limit_bytes=VMEM_LIMIT),
        name="inproj",
    )(x3, g, w, wf, bf, tri, c0)


def _swa_scores(q, k_all, bias):
    nq = q.shape[0]
    zeros = jnp.zeros((HEAD_DIM, nq), F32)
    blocks = []
    for pair in range(SWA_Q_HEADS // 2):
        q_t = q[:, pair * PAIR_W:(pair + 1) * PAIR_W].astype(F32).T
        for hp in range(2):
            qh = q_t[hp * HEAD_DIM:(hp + 1) * HEAD_DIM]
            g = (2 * pair + hp) // SWA_GROUP
            blocks.append(jnp.concatenate([qh, zeros] if g == 0 else [zeros, qh], axis=0))
    q_bd = jnp.concatenate(blocks, axis=1).astype(BF16)
    return _dot(k_all, q_bd) + bias


def _swa_outputs(s, v_all, sink_ref):
    nq = s.shape[1] // SWA_Q_HEADS
    ps, sink_terms = [], []
    for h in range(SWA_Q_HEADS):
        sh = s[:, h * nq:(h + 1) * nq]
        sink = sink_ref[h] * LOG2E
        m = jnp.maximum(jnp.max(sh, axis=0, keepdims=True), sink)
        ps.append(jnp.exp2(sh - m).astype(BF16))
        sink_terms.append(jnp.exp2(sink - m))
    n = BLOCK
    v_t = jnp.concatenate([v_all[0:n].astype(F32).T, v_all[n:2 * n].astype(F32).T,
                           v_all[2 * n:].astype(F32).T], axis=1)
    ones = jnp.ones((SWA_ONES, v_t.shape[1]), F32)
    outs = []
    for g in range(SWA_KV_HEADS):
        v_aug = jnp.concatenate([v_t[g * HEAD_DIM:(g + 1) * HEAD_DIM], ones], axis=0).astype(BF16)
        p_g = jnp.concatenate(ps[g * SWA_GROUP:(g + 1) * SWA_GROUP], axis=1)
        o_g = _dot(v_aug, p_g)
        for j in range(SWA_GROUP):
            cols = slice(j * nq, (j + 1) * nq)
            l = o_g[HEAD_DIM:HEAD_DIM + 1, cols] + sink_terms[g * SWA_GROUP + j]
            outs.append(o_g[0:HEAD_DIM, cols] / l)
    pairs = [jnp.concatenate(outs[2 * p:2 * p + 2], axis=0).T for p in range(SWA_Q_HEADS // 2)]
    return jnp.concatenate(pairs, axis=1).astype(BF16)


def _swa_kernel(sink_ref, q_ref, kc_ref, kp_ref, vc_ref, vp_ref, km_ref, vm_ref, bias0_ref, bias1_ref, o_ref):
    n = BLOCK
    nsub = q_ref.shape[0] // n
    k_rows = [kp_ref[...]] + [kc_ref[c * n:(c + 1) * n] for c in range(nsub)]
    v_rows = [vp_ref[...]] + [vc_ref[c * n:(c + 1) * n] for c in range(nsub)]
    s = []
    for c in range(nsub):
        k_all = jnp.concatenate([k_rows[c], k_rows[c + 1], km_ref[...]], axis=0)
        bias = bias0_ref[...] if c == 0 else bias1_ref[...]
        s.append(_swa_scores(q_ref[c * n:(c + 1) * n], k_all, bias))
    for c in range(nsub):
        v_all = jnp.concatenate([v_rows[c], v_rows[c + 1], vm_ref[...]], axis=0)
        o_ref[c * n:(c + 1) * n] = _swa_outputs(s[c], v_all, sink_ref)


def _swa(proj, projm, bias, sinks):
    nb, rows, _ = proj.shape
    tq = SWA_TQ
    ratio = tq // BLOCK
    kcol, vcol = OFF_QA // SWA_KV_W, OFF_KA // SWA_KV_W
    cur = lambda c: (lambda b, n: (b, n, c))
    prev = lambda c: (lambda b, n: (b, jnp.maximum(n * ratio - 1, 0), c))
    bias_spec = lambda idx: pl.BlockSpec((None, SWA_KEYS, SWA_Q_HEADS * BLOCK), idx)
    return pl.pallas_call(
        _swa_kernel,
        grid=(nb, rows // tq),
        in_specs=[
            pl.BlockSpec(memory_space=pltpu.SMEM),
            pl.BlockSpec((None, tq, SWA_Q_W), lambda b, n: (b, n, 0)),
            pl.BlockSpec((None, tq, SWA_KV_W), cur(kcol)),
            pl.BlockSpec((None, BLOCK, SWA_KV_W), prev(kcol)),
            pl.BlockSpec((None, tq, SWA_KV_W), cur(vcol)),
            pl.BlockSpec((None, BLOCK, SWA_KV_W), prev(vcol)),
            pl.BlockSpec((N_META, SWA_KV_W), lambda b, n: (0, kcol)),
            pl.BlockSpec((N_META, SWA_KV_W), lambda b, n: (0, vcol)),
            bias_spec(lambda b, n: (jnp.minimum(n, 1), 0, 0)),
            bias_spec(lambda b, n: (1, 0, 0)),
        ],
        out_specs=pl.BlockSpec((None, tq, SWA_Q_W), lambda b, n: (b, n, 0)),
        out_shape=jax.ShapeDtypeStruct((nb, rows, SWA_Q_W), BF16),
        compiler_params=pltpu.CompilerParams(
            dimension_semantics=("arbitrary", "arbitrary"), vmem_limit_bytes=VMEM_LIMIT),
        name="swa",
    )(sinks, proj, proj, proj, proj, proj, projm, projm, bias, bias)


def _fox_aug_lanes(e):
    lane = lax.broadcasted_iota(jnp.int32, (1, LANES), 1)
    own = (lane < HEAD_DIM) if e == 0 else (lane >= HEAD_DIM)
    return lane, own, (HEAD_DIM if e == 0 else 0)


def _fox_operand_aug(pair, col, e, col_first):
    lane, own, a0 = _fox_aug_lanes(e)
    p0, o0 = (a0, a0 + N_PIECES) if col_first else (a0 + N_PIECES, a0)
    aug = jnp.broadcast_to(jnp.where((lane >= o0) & (lane < o0 + N_PIECES), 1.0, 0.0), pair.shape)
    for i, piece in enumerate(_bf16_pieces(col)):
        aug = jnp.where(lane == p0 + i, piece.astype(F32), aug)
    return jnp.where(own, pair, aug.astype(BF16))


def _fox_own_norm2(pair, e):
    _, own, _ = _fox_aug_lanes(e)
    x = jnp.where(own, pair.astype(F32), 0.0)
    return jnp.sum(x * x, axis=1, keepdims=True)


def _fox_value_aug_t(vpair, e):
    vt = vpair.astype(F32).T
    top = vt[e * HEAD_DIM:(e + 1) * HEAD_DIM]
    return jnp.concatenate([top, jnp.ones_like(top)], axis=0).astype(BF16)


def _fox_kernel(q_ref, k_ref, v_ref, km_ref, vm_ref, c_ref, cm_ref, o_ref, ka_scr, vt_scr, acc_scr, kmax_scr):
    i = pl.program_id(2)
    t = q_ref.shape[0]
    nk = k_ref.shape[0] // t
    nh = acc_scr.shape[0]
    pair_cols = lambda h: slice((h // 2) * PAIR_W, (h // 2 + 1) * PAIR_W)
    colmax = lambda x: jnp.max(x, axis=0, keepdims=True)

    @pl.when(i == 0)
    def _():
        for h in range(nh):
            e = h % 2
            kn2 = colmax(_fox_own_norm2(km_ref[:, pair_cols(h)], e))
            for j in range(nk):
                rows = slice(j * t, (j + 1) * t)
                kpair = k_ref[rows, pair_cols(h)]
                ka_scr[h, j] = _fox_operand_aug(kpair, c_ref[rows, h:h + 1] * (-LOG2E), e, True)
                vt_scr[h, j] = _fox_value_aug_t(v_ref[rows, pair_cols(h)], e)
                kn2 = jnp.maximum(kn2, colmax(_fox_own_norm2(kpair, e)))
            kmax_scr[h] = jnp.broadcast_to(jnp.sqrt(kn2), kmax_scr.shape[1:])

    q_rows = pl.ds(pl.multiple_of(i * t, t), t)
    r = [jnp.sqrt(_fox_own_norm2(q_ref[:, pair_cols(h)], h % 2)) * kmax_scr[h, 0:1, 0:1] * 1.001
         for h in range(nh)]
    worst = r[0]
    for rh in r[1:]:
        worst = jnp.maximum(worst, rh)
    safe = jnp.max(worst) <= FOX_SAFE_LOG2

    qt, kma, vmt = [], [], []
    for h in range(nh):
        e = h % 2
        u = c_ref[q_rows, h:h + 1] * LOG2E - jnp.where(safe, r[h], 0.0)
        qt.append(_fox_operand_aug(q_ref[:, pair_cols(h)], u, e, False).astype(F32).T.astype(BF16))
        kma.append(_fox_operand_aug(km_ref[:, pair_cols(h)], cm_ref[:, h:h + 1] * (-LOG2E), e, True))
        vmt.append(_fox_value_aug_t(vm_ref[:, pair_cols(h)], e))

    def scores(h, ka, diagonal):
        s = _dot(ka, qt[h])
        if diagonal:
            key = lax.broadcasted_iota(jnp.int32, (t, t), 0)
            qry = lax.broadcasted_iota(jnp.int32, (t, t), 1)
            s = jnp.where(key <= qry, s, NEG_INF)
        return s

    @pl.when(safe)
    def _():
        for h in range(nh):
            acc_scr[h] = _dot(vmt[h], jnp.exp2(_dot(kma[h], qt[h])).astype(BF16))

        def tile(j, diagonal):
            s_prev = None
            for h in range(nh):
                s = scores(h, ka_scr[h, j], diagonal)
                if h > 0:
                    acc_scr[h - 1] += _dot(vt_scr[h - 1, j], jnp.exp2(s_prev).astype(BF16))
                s_prev = s
            acc_scr[nh - 1] += _dot(vt_scr[nh - 1, j], jnp.exp2(s_prev).astype(BF16))

        def full_tile(j, carry):
            tile(j, False)
            return carry

        lax.fori_loop(0, i, full_tile, 0)
        tile(i, True)

    @pl.when(jnp.logical_not(safe))
    def _():
        m0 = []
        for h in range(nh):
            s = _dot(kma[h], qt[h])
            m = colmax(s)
            acc_scr[h] = _dot(vmt[h], jnp.exp2(s - m).astype(BF16))
            m0.append(m)

        def tile(j, ms, diagonal):
            out = []
            for h in range(nh):
                s = scores(h, ka_scr[h, j], diagonal)
                m_next = jnp.maximum(ms[h], colmax(s))
                p = jnp.exp2(s - m_next).astype(BF16)
                acc_scr[h] = jnp.exp2(ms[h] - m_next) * acc_scr[h] + _dot(vt_scr[h, j], p)
                out.append(m_next)
            return tuple(out)

        ms = lax.fori_loop(0, i, lambda j, ms: tile(j, ms, False), tuple(m0))
        tile(i, ms, True)

    for pp in range(nh // 2):
        o_t = jnp.concatenate([acc_scr[h, 0:HEAD_DIM] / acc_scr[h, HEAD_DIM:PAIR_W]
                               for h in (2 * pp, 2 * pp + 1)], axis=0)
        o_ref[:, pp * PAIR_W:(pp + 1) * PAIR_W] = o_t.T.astype(BF16)


def _fox(proj, projm, cum_g, cum_mg):
    nb, rows, _ = proj.shape
    t = FOX_T
    nk = rows // t
    nh = cum_g.shape[-1]
    gw = nh * HEAD_DIM
    ngroup = FOX_HEADS // nh
    qcol, kcol, vcol = OFF_VA // gw, OFF_QB // gw, OFF_KB // gw
    return pl.pallas_call(
        _fox_kernel,
        grid=(nb, ngroup, rows // t),
        in_specs=[
            pl.BlockSpec((None, t, gw), lambda b, g, i: (b, i, qcol + g)),
            pl.BlockSpec((None, rows, gw), lambda b, g, i: (b, 0, kcol + g)),
            pl.BlockSpec((None, rows, gw), lambda b, g, i: (b, 0, vcol + g)),
            pl.BlockSpec((N_META, gw), lambda b, g, i: (0, kcol + g)),
            pl.BlockSpec((N_META, gw), lambda b, g, i: (0, vcol + g)),
            pl.BlockSpec((None, None, rows, nh), lambda b, g, i: (b, g, 0, 0)),
            pl.BlockSpec((None, N_META, nh), lambda b, g, i: (g, 0, 0)),
        ],
        out_specs=pl.BlockSpec((None, t, gw), lambda b, g, i: (b, i, g)),
        out_shape=jax.ShapeDtypeStruct((nb, rows, FOX_W), BF16),
        scratch_shapes=[
            pltpu.VMEM((nh, nk, t, PAIR_W), BF16),
            pltpu.VMEM((nh, nk, PAIR_W, t), BF16),
            pltpu.VMEM((nh, PAIR_W, t), F32),
            pltpu.VMEM((nh, 8, LANES), F32),
        ],
        compiler_params=pltpu.CompilerParams(
            dimension_semantics=("arbitrary", "arbitrary", "arbitrary"), vmem_limit_bytes=VMEM_LIMIT),
        name="fox",
    )(proj, proj, proj, projm, projm, cum_g, cum_mg)


def _ffn_kernel(oa_ref, ob_ref, x_ref, wo_ref, g1_ref, g2_ref, wg_ref, wu_ref, wd_ref, g3_ref, out_ref):
    tm = x_ref.shape[0]
    parts = [slice(c * tm // FFN_PARTS, (c + 1) * tm // FFN_PARTS) for c in range(FFN_PARTS)]
    mix = [jnp.concatenate([oa_ref[r, :], ob_ref[r, :]], axis=1) for r in parts]
    a = [_dot(m, wo_ref[...]) for m in mix]
    h1 = [x_ref[r, :] + _rms(ai, g1_ref[...]) for r, ai in zip(parts, a)]
    hn = [_rms(h, g2_ref[...]).astype(BF16) for h in h1]
    gate_up = [(_dot(h, wg_ref[...]), _dot(h, wu_ref[...])) for h in hn]
    act = [(g / (1.0 + jnp.exp(-g)) * u).astype(BF16) for g, u in gate_up]
    ff = [_dot(ac, wd_ref[...]) for ac in act]
    for r, h, f in zip(parts, h1, ff):
        out_ref[r, :] = h + _rms(f, g3_ref[...])


def _ffn(o_a, o_b, x3, wo, g1, g2, wg, wu, wd, g3):
    nb, rows, _ = x3.shape
    tm = FFN_TM
    const = lambda b, t: (0, 0)
    resident = lambda shape: pl.BlockSpec(shape, const, pipeline_mode=pl.Buffered(1))
    row = lambda w: pl.BlockSpec((None, tm, w), lambda b, t: (b, t, 0))
    return pl.pallas_call(
        _ffn_kernel,
        grid=(nb, rows // tm),
        in_specs=[
            row(SWA_Q_W), row(FOX_W), row(D_MODEL),
            resident((D_MODEL, D_MODEL)), resident((1, D_MODEL)), resident((1, D_MODEL)),
            resident((D_MODEL, D_FF)), resident((D_MODEL, D_FF)), resident((D_FF, D_MODEL)),
            resident((1, D_MODEL)),
        ],
        out_specs=row(D_MODEL),
        out_shape=jax.ShapeDtypeStruct((nb, rows, D_MODEL), F32),
        compiler_params=pltpu.CompilerParams(
            dimension_semantics=("arbitrary", "arbitrary"), vmem_limit_bytes=VMEM_LIMIT),
        name="outproj_ffn",
    )(o_a, o_b, x3, wo, g1, g2, wg, wu, wd, g3)


def kernel(x, meta_tokens, rel_bias, ln_pre_mix, ln_post_mix, ln_pre_ffn, ln_post_ffn,
           w_in, b_forget, sinks, w_out, w_gate_up, w_down):
    nb, seq, d = x.shape
    assert d == D_MODEL and seq % IN_TM == 0 and seq % FOX_T == 0 and seq % FFN_TM == 0 and seq % SWA_TQ == 0
    assert w_in.shape[0] == 1, "single-layer block"
    assert meta_tokens.shape == (N_META, D_MODEL)
    x = x.astype(F32)

    w_qkv = w_in[0, :, :OFF_VB].astype(BF16)
    w_f = jnp.pad(w_in[0, :, OFF_VB:], ((0, 0), (0, LANES - FOX_HEADS))).astype(BF16)
    b_f = jnp.pad(b_forget[0].astype(F32), (0, LANES - FOX_HEADS)).reshape(1, LANES)
    g_pre = ln_pre_mix[0].astype(F32).reshape(1, D_MODEL)

    swa_bias = _bias_tables(rel_bias)

    zero_c = jnp.zeros((1, LANES), F32)
    projm, cum_m = _inproj(meta_tokens.astype(F32)[None], g_pre, w_qkv, w_f, b_f, zero_c, N_META)
    projm, cum_m = projm[0], cum_m[0]
    c0 = jnp.pad(cum_m[N_META - 1], (0, LANES - FOX_HEADS)).reshape(1, LANES)
    proj, cum = _inproj(x, g_pre, w_qkv, w_f, b_f, c0, IN_TM)

    o_a = _swa(proj, projm, swa_bias, sinks[0].astype(F32))

    ngroup = FOX_HEADS // FOX_GROUP
    cum_g = cum.reshape(nb, seq, ngroup, FOX_GROUP).transpose(0, 2, 1, 3)
    cum_mg = cum_m.reshape(N_META, ngroup, FOX_GROUP).transpose(1, 0, 2)
    o_b = _fox(proj, projm, cum_g, cum_mg)

    w_gu = w_gate_up[0]
    return _ffn(o_a, o_b, x,
                w_out[0].astype(BF16),
                ln_post_mix[0].astype(F32).reshape(1, D_MODEL),
                ln_pre_ffn[0].astype(F32).reshape(1, D_MODEL),
                w_gu[:, :D_FF].astype(BF16), w_gu[:, D_FF:].astype(BF16),
                w_down[0].astype(BF16),
                ln_post_ffn[0].astype(F32).reshape(1, D_MODEL))
```

```python
import math

import numpy as np
import jax
import jax.numpy as jnp
from jax import lax
from jax.experimental import pallas as pl
from jax.experimental.pallas import tpu as pltpu

D_MODEL = 1024
N_META = 16
HEAD_DIM = 64
SWA_Q_HEADS = 8
SWA_KV_HEADS = 2
SWA_GROUP = SWA_Q_HEADS // SWA_KV_HEADS
FOX_HEADS = 8
SWA_Q_W = SWA_Q_HEADS * HEAD_DIM
SWA_KV_W = SWA_KV_HEADS * HEAD_DIM
FOX_W = FOX_HEADS * HEAD_DIM
OFF_QA = SWA_Q_W
OFF_KA = OFF_QA + SWA_KV_W
OFF_VA = OFF_KA + SWA_KV_W
OFF_QB = OFF_VA + FOX_W
OFF_KB = OFF_QB + FOX_W
OFF_VB = OFF_KB + FOX_W
WINDOW = 128
BLOCK = 128
N_BUCKETS = 32
MAX_DISTANCE = 128
D_FF = 2816
EPS = 1e-6
NEG_INF = -1e30
SCALE = HEAD_DIM ** -0.5
LOG2E = math.log2(math.e)

LANES = 128
PAIR_W = 2 * HEAD_DIM
VMEM_LIMIT = 56 * 1024 * 1024

SWA_KEYS = 2 * BLOCK + N_META
SWA_TQ = 512
SWA_ONES = 16
IN_TM = 512
META_TM = 128
GATE_ROWS = 16
FOX_T = 512
FOX_GROUP = 4
FOX_SAFE_LOG2 = 40.0
FFN_TM = 512
FFN_PARTS = 2
N_PIECES = 3

F32 = jnp.float32
BF16 = jnp.bfloat16


def _dot(a, b):
    return jnp.dot(a, b, preferred_element_type=F32)


def _dot_nt(a, b):
    return lax.dot_general(a, b, (((1,), (1,)), ((), ())), preferred_element_type=F32)


def _rms(t, g):
    return t * lax.rsqrt(jnp.mean(t * t, axis=-1, keepdims=True) + EPS) * g


def _bf16_pieces(v):
    pieces = []
    for _ in range(N_PIECES - 1):
        p = v.astype(BF16)
        pieces.append(p)
        v = v - p.astype(F32)
    pieces.append(v.astype(BF16))
    return pieces


def _t5_bucket_np(dist):
    n = np.maximum(dist, 0).astype(np.int32)
    max_exact = N_BUCKETS // 2
    nf = np.maximum(n, 1).astype(np.float32)
    large = max_exact + (np.log(nf / np.float32(max_exact)) / np.float32(math.log(MAX_DISTANCE / max_exact))
                         * np.float32(N_BUCKETS - max_exact)).astype(np.int32)
    large = np.minimum(large, N_BUCKETS - 1)
    return np.where(n < max_exact, n, large).astype(np.int32)


def _bias_kernel(tab_ref, bkt_ref, valid_ref, out_ref):
    h = pl.program_id(1)
    bkt = bkt_ref[...]
    acc = jnp.zeros(bkt.shape, F32)
    for b in range(N_BUCKETS):
        acc = jnp.where(bkt == b, tab_ref[b, h], acc)
    out_ref[...] = jnp.where(valid_ref[...] > 0, acc * LOG2E, NEG_INF)


def _bias_tables(rel_bias):
    ki = np.arange(2 * BLOCK)[:, None]
    qi = np.arange(BLOCK)[None, :]
    d_w = qi + BLOCK - ki
    bkt_w = _t5_bucket_np(d_w)
    in_window = (d_w >= 0) & (d_w < WINDOW)
    mi = np.arange(N_META)[:, None]
    bkt_m0 = _t5_bucket_np(N_META + qi - mi)
    far = _t5_bucket_np(np.asarray([[N_META + BLOCK - (N_META - 1)]]))
    assert far[0, 0] == N_BUCKETS - 1, "meta keys of later blocks must share the last bucket"
    bkt = np.stack([np.concatenate([bkt_w, bkt_m0]), np.concatenate([bkt_w, np.full_like(bkt_m0, far[0, 0])])])
    meta_ok = np.ones((N_META, BLOCK), bool)
    valid = np.stack([np.concatenate([in_window & (ki >= BLOCK), meta_ok]), np.concatenate([in_window, meta_ok])])
    spec = pl.BlockSpec((None, SWA_KEYS, BLOCK), lambda a, h: (a, 0, 0))
    return pl.pallas_call(
        _bias_kernel,
        grid=(2, SWA_Q_HEADS),
        in_specs=[pl.BlockSpec(memory_space=pltpu.SMEM), spec, spec],
        out_specs=pl.BlockSpec((None, SWA_KEYS, BLOCK), lambda a, h: (a, 0, h)),
        out_shape=jax.ShapeDtypeStruct((2, SWA_KEYS, SWA_Q_HEADS * BLOCK), F32),
        name="bias_tables",
    )(rel_bias.astype(F32), jnp.asarray(bkt.astype(np.int32)), jnp.asarray(valid.astype(np.int32)))


def _fox_aug8(val_row, pieces_first):
    n = val_row.shape[1]
    row = lax.broadcasted_iota(jnp.int32, (8, n), 0)
    p0, o0 = (0, N_PIECES) if pieces_first else (N_PIECES, 0)
    aug = jnp.where((row >= o0) & (row < o0 + N_PIECES), 1.0, 0.0)
    for i, piece in enumerate(_bf16_pieces(val_row)):
        aug = jnp.where(row == p0 + i, piece.astype(F32), aug)
    return aug


def _fox_head_t(own_t, aug8, e):
    pad = jnp.zeros((HEAD_DIM - 8, own_t.shape[1]), F32)
    return jnp.concatenate([own_t, aug8, pad] if e == 0 else [aug8, pad, own_t], axis=0)


def _inproj_kernel(x_ref, g_ref, w_ref, wft_ref, bft_ref, tri_ref, c0_ref,
                   pa_ref, qt_ref, ka_ref, vt_ref, cum_ref, qn2_ref, kmx_ref, carry_ref):
    @pl.when(pl.program_id(1) == 0)
    def _():
        carry_ref[...] = c0_ref[...]

    tm = x_ref.shape[0]
    y = _rms(x_ref[...], g_ref[...]).astype(BF16)
    acc = _dot(y, w_ref[...])
    pa_ref[:, 0:OFF_QA] = (acc[:, 0:OFF_QA] * (SCALE * LOG2E)).astype(BF16)
    pa_ref[:, OFF_QA:OFF_VA] = acc[:, OFF_QA:OFF_VA].astype(BF16)

    tile_lanes = lambda a: jnp.concatenate([a] * (tm // LANES), axis=1)
    f_t = _dot_nt(wft_ref[...], y) + tile_lanes(bft_ref[...])
    ls_t = jnp.minimum(f_t, 0.0) - jnp.log1p(jnp.exp(-jnp.abs(f_t)))
    cum_t = tile_lanes(carry_ref[...])
    for piece in _bf16_pieces(ls_t):
        cum_t = cum_t + _dot(piece, tri_ref[...])
    carry_ref[...] = jnp.broadcast_to(cum_t[:, tm - 1:tm], carry_ref.shape)
    cum_ref[...] = cum_t[0:FOX_HEADS]
    cb_t = cum_t * (-LOG2E)

    ones = jnp.ones((HEAD_DIM, tm), F32)
    for p in range(FOX_HEADS // 2):
        cols = lambda off: slice(off + p * PAIR_W, off + (p + 1) * PAIR_W)
        q_t = (acc[:, cols(OFF_VA)] * (SCALE * LOG2E)).astype(BF16).astype(F32).T
        k_t = acc[:, cols(OFF_QB)].astype(BF16).astype(F32).T
        v_t = acc[:, cols(OFF_KB)].astype(BF16).astype(F32).T
        qt_ref[p] = q_t.astype(BF16)
        for e in range(2):
            h = 2 * p + e
            own = slice(e * HEAD_DIM, (e + 1) * HEAD_DIM)
            qn2_ref[h:h + 1, :] = jnp.sum(q_t[own] * q_t[own], axis=0, keepdims=True)
            kn2 = jnp.sum(k_t[own] * k_t[own], axis=0, keepdims=True)
            kmx_ref[h:h + 1, :] = jnp.broadcast_to(jnp.max(kn2, axis=1, keepdims=True), (1, LANES))
            ka_ref[h] = _fox_head_t(k_t[own], _fox_aug8(cb_t[h:h + 1], True), e).T.astype(BF16)
            vt_ref[h] = jnp.concatenate([v_t[own], ones], axis=0).astype(BF16)


def _inproj(x3, g, w, wft, bft, c0, tm):
    nb, rows, _ = x3.shape
    nt = rows // tm
    tri = jnp.asarray(np.triu(np.ones((tm, tm), np.float32)), BF16)
    const = lambda b, t: (0, 0)
    npair = FOX_HEADS // 2
    return pl.pallas_call(
        _inproj_kernel,
        grid=(nb, nt),
        in_specs=[
            pl.BlockSpec((None, tm, D_MODEL), lambda b, t: (b, t, 0)),
            pl.BlockSpec((1, D_MODEL), const),
            pl.BlockSpec((D_MODEL, OFF_VB), const),
            pl.BlockSpec((GATE_ROWS, D_MODEL), const),
            pl.BlockSpec((GATE_ROWS, LANES), const),
            pl.BlockSpec((tm, tm), const),
            pl.BlockSpec((GATE_ROWS, LANES), const),
        ],
        out_specs=[
            pl.BlockSpec((None, tm, OFF_VA), lambda b, t: (b, t, 0)),
            pl.BlockSpec((None, npair, PAIR_W, tm), lambda b, t: (b, 0, 0, t)),
            pl.BlockSpec((None, FOX_HEADS, tm, PAIR_W), lambda b, t: (b, 0, t, 0)),
            pl.BlockSpec((None, FOX_HEADS, PAIR_W, tm), lambda b, t: (b, 0, 0, t)),
            pl.BlockSpec((None, FOX_HEADS, tm), lambda b, t: (b, 0, t)),
            pl.BlockSpec((None, FOX_HEADS, tm), lambda b, t: (b, 0, t)),
            pl.BlockSpec((None, None, FOX_HEADS, LANES), lambda b, t: (b, t, 0, 0)),
        ],
        out_shape=[
            jax.ShapeDtypeStruct((nb, rows, OFF_VA), BF16),
            jax.ShapeDtypeStruct((nb, npair, PAIR_W, rows), BF16),
            jax.ShapeDtypeStruct((nb, FOX_HEADS, rows, PAIR_W), BF16),
            jax.ShapeDtypeStruct((nb, FOX_HEADS, PAIR_W, rows), BF16),
            jax.ShapeDtypeStruct((nb, FOX_HEADS, rows), F32),
            jax.ShapeDtypeStruct((nb, FOX_HEADS, rows), F32),
            jax.ShapeDtypeStruct((nb, nt, FOX_HEADS, LANES), F32),
        ],
        scratch_shapes=[pltpu.VMEM((GATE_ROWS, LANES), F32)],
        compiler_params=pltpu.CompilerParams(
            dimension_semantics=("arbitrary", "arbitrary"), vmem_limit_bytes=VMEM_LIMIT),
        name="inproj",
    )(x3, g, w, wft, bft, tri, c0)


def _swa_scores(q, k_all, bias):
    nq = q.shape[0]
    zeros = jnp.zeros((HEAD_DIM, nq), F32)
    blocks = []
    for pair in range(SWA_Q_HEADS // 2):
        q_t = q[:, pair * PAIR_W:(pair + 1) * PAIR_W].astype(F32).T
        for hp in range(2):
            qh = q_t[hp * HEAD_DIM:(hp + 1) * HEAD_DIM]
            g = (2 * pair + hp) // SWA_GROUP
            blocks.append(jnp.concatenate([qh, zeros] if g == 0 else [zeros, qh], axis=0))
    q_bd = jnp.concatenate(blocks, axis=1).astype(BF16)
    return _dot(k_all, q_bd) + bias


def _swa_outputs(s, v_all, sink_ref):
    nq = s.shape[1] // SWA_Q_HEADS
    ps, sink_terms = [], []
    for h in range(SWA_Q_HEADS):
        sh = s[:, h * nq:(h + 1) * nq]
        sink = sink_ref[h] * LOG2E
        m = jnp.maximum(jnp.max(sh, axis=0, keepdims=True), sink)
        ps.append(jnp.exp2(sh - m).astype(BF16))
        sink_terms.append(jnp.exp2(sink - m))
    n = BLOCK
    v_t = jnp.concatenate([v_all[0:n].astype(F32).T, v_all[n:2 * n].astype(F32).T,
                           v_all[2 * n:].astype(F32).T], axis=1)
    ones = jnp.ones((SWA_ONES, v_t.shape[1]), F32)
    outs = []
    for g in range(SWA_KV_HEADS):
        v_aug = jnp.concatenate([v_t[g * HEAD_DIM:(g + 1) * HEAD_DIM], ones], axis=0).astype(BF16)
        p_g = jnp.concatenate(ps[g * SWA_GROUP:(g + 1) * SWA_GROUP], axis=1)
        o_g = _dot(v_aug, p_g)
        for j in range(SWA_GROUP):
            cols = slice(j * nq, (j + 1) * nq)
            l = o_g[HEAD_DIM:HEAD_DIM + 1, cols] + sink_terms[g * SWA_GROUP + j]
            outs.append(o_g[0:HEAD_DIM, cols] / l)
    pairs = [jnp.concatenate(outs[2 * p:2 * p + 2], axis=0).T for p in range(SWA_Q_HEADS // 2)]
    return jnp.concatenate(pairs, axis=1).astype(BF16)


def _swa_kernel(sink_ref, q_ref, kc_ref, kp_ref, vc_ref, vp_ref, km_ref, vm_ref, bias0_ref, bias1_ref, o_ref):
    n = BLOCK
    nsub = q_ref.shape[0] // n
    k_rows = [kp_ref[...]] + [kc_ref[c * n:(c + 1) * n] for c in range(nsub)]
    v_rows = [vp_ref[...]] + [vc_ref[c * n:(c + 1) * n] for c in range(nsub)]
    s = []
    for c in range(nsub):
        k_all = jnp.concatenate([k_rows[c], k_rows[c + 1], km_ref[...]], axis=0)
        bias = bias0_ref[...] if c == 0 else bias1_ref[...]
        s.append(_swa_scores(q_ref[c * n:(c + 1) * n], k_all, bias))
    for c in range(nsub):
        v_all = jnp.concatenate([v_rows[c], v_rows[c + 1], vm_ref[...]], axis=0)
        o_ref[c * n:(c + 1) * n] = _swa_outputs(s[c], v_all, sink_ref)


def _swa(proj, projm, bias, sinks):
    nb, rows, _ = proj.shape
    tq = SWA_TQ
    ratio = tq // BLOCK
    kcol, vcol = OFF_QA // SWA_KV_W, OFF_KA // SWA_KV_W
    cur = lambda c: (lambda b, n: (b, n, c))
    prev = lambda c: (lambda b, n: (b, jnp.maximum(n * ratio - 1, 0), c))
    bias_spec = lambda idx: pl.BlockSpec((None, SWA_KEYS, SWA_Q_HEADS * BLOCK), idx)
    return pl.pallas_call(
        _swa_kernel,
        grid=(nb, rows // tq),
        in_specs=[
            pl.BlockSpec(memory_space=pltpu.SMEM),
            pl.BlockSpec((None, tq, SWA_Q_W), lambda b, n: (b, n, 0)),
            pl.BlockSpec((None, tq, SWA_KV_W), cur(kcol)),
            pl.BlockSpec((None, BLOCK, SWA_KV_W), prev(kcol)),
            pl.BlockSpec((None, tq, SWA_KV_W), cur(vcol)),
            pl.BlockSpec((None, BLOCK, SWA_KV_W), prev(vcol)),
            pl.BlockSpec((N_META, SWA_KV_W), lambda b, n: (0, kcol)),
            pl.BlockSpec((N_META, SWA_KV_W), lambda b, n: (0, vcol)),
            bias_spec(lambda b, n: (jnp.minimum(n, 1), 0, 0)),
            bias_spec(lambda b, n: (1, 0, 0)),
        ],
        out_specs=pl.BlockSpec((None, tq, SWA_Q_W), lambda b, n: (b, n, 0)),
        out_shape=jax.ShapeDtypeStruct((nb, rows, SWA_Q_W), BF16),
        compiler_params=pltpu.CompilerParams(
            dimension_semantics=("arbitrary", "arbitrary"), vmem_limit_bytes=VMEM_LIMIT),
        name="swa",
    )(sinks, proj, proj, proj, proj, proj, projm, projm, bias, bias)


def _fox_kernel(qt_ref, ka_ref, vt_ref, kam_ref, vtm_ref, cq_ref, qn2_ref, kmx_ref, kmxm_ref, o_ref, acc_scr):
    i = pl.program_id(2)
    t = o_ref.shape[0]
    nh = acc_scr.shape[0]
    colmax = lambda x: jnp.max(x, axis=0, keepdims=True)
    tile_rows = lambda j: pl.ds(pl.multiple_of(j * t, t), t)

    kmax2 = jnp.maximum(jnp.max(kmx_ref[...], axis=0), kmxm_ref[...])
    r = [jnp.sqrt(qn2_ref[h:h + 1, :] * kmax2[h:h + 1, 0:1]) * 1.001 for h in range(nh)]
    worst = r[0]
    for rh in r[1:]:
        worst = jnp.maximum(worst, rh)
    safe = jnp.max(worst) <= FOX_SAFE_LOG2

    qt = []
    for h in range(nh):
        e = h % 2
        u = cq_ref[h:h + 1, :] * LOG2E - jnp.where(safe, r[h], 0.0)
        own = qt_ref[h // 2, e * HEAD_DIM:(e + 1) * HEAD_DIM, :].astype(F32)
        qt.append(_fox_head_t(own, _fox_aug8(u, False), e).astype(BF16))

    def scores(h, j, diagonal):
        s = _dot(ka_ref[h, tile_rows(j), :], qt[h])
        if diagonal:
            key = lax.broadcasted_iota(jnp.int32, (t, t), 0)
            qry = lax.broadcasted_iota(jnp.int32, (t, t), 1)
            s = jnp.where(key <= qry, s, NEG_INF)
        return s

    @pl.when(safe)
    def _():
        s_meta = [_dot(kam_ref[h], qt[h]) for h in range(nh)]
        for h in range(nh):
            acc_scr[h] = _dot(vtm_ref[h], jnp.exp2(s_meta[h]).astype(BF16))

        def tile(j, diagonal):
            s_prev = None
            for h in range(nh):
                s = scores(h, j, diagonal)
                if h > 0:
                    acc_scr[h - 1] += _dot(vt_ref[h - 1, :, tile_rows(j)], jnp.exp2(s_prev).astype(BF16))
                s_prev = s
            acc_scr[nh - 1] += _dot(vt_ref[nh - 1, :, tile_rows(j)], jnp.exp2(s_prev).astype(BF16))

        def full_tile(j, carry):
            tile(j, False)
            return carry

        lax.fori_loop(0, i, full_tile, 0)
        tile(i, True)

    @pl.when(jnp.logical_not(safe))
    def _():
        m0 = []
        for h in range(nh):
            s = _dot(kam_ref[h], qt[h])
            m = colmax(s)
            acc_scr[h] = _dot(vtm_ref[h], jnp.exp2(s - m).astype(BF16))
            m0.append(m)

        def tile(j, ms, diagonal):
            out = []
            for h in range(nh):
                s = scores(h, j, diagonal)
                m_next = jnp.maximum(ms[h], colmax(s))
                p = jnp.exp2(s - m_next).astype(BF16)
                acc_scr[h] = jnp.exp2(ms[h] - m_next) * acc_scr[h] + _dot(vt_ref[h, :, tile_rows(j)], p)
                out.append(m_next)
            return tuple(out)

        ms = lax.fori_loop(0, i, lambda j, ms: tile(j, ms, False), tuple(m0))
        tile(i, ms, True)

    for pp in range(nh // 2):
        o_t = jnp.concatenate([acc_scr[h, 0:HEAD_DIM] / acc_scr[h, HEAD_DIM:PAIR_W]
                               for h in (2 * pp, 2 * pp + 1)], axis=0)
        o_ref[:, pp * PAIR_W:(pp + 1) * PAIR_W] = o_t.T.astype(BF16)


def _fox(qt, ka, vt, ka_m, vt_m, cum, qn2, kmx, kmx_m):
    nb, _, rows, _ = ka.shape
    t = FOX_T
    nh = FOX_GROUP
    ngroup = FOX_HEADS // nh
    nt = kmx.shape[1]
    grouped = lambda a: a.reshape(a.shape[0], ngroup, nh, a.shape[-1])
    return pl.pallas_call(
        _fox_kernel,
        grid=(nb, ngroup, rows // t),
        in_specs=[
            pl.BlockSpec((None, nh // 2, PAIR_W, t), lambda b, g, i: (b, g, 0, i)),
            pl.BlockSpec((None, nh, rows, PAIR_W), lambda b, g, i: (b, g, 0, 0)),
            pl.BlockSpec((None, nh, PAIR_W, rows), lambda b, g, i: (b, g, 0, 0)),
            pl.BlockSpec((nh, N_META, PAIR_W), lambda b, g, i: (g, 0, 0)),
            pl.BlockSpec((nh, PAIR_W, N_META), lambda b, g, i: (g, 0, 0)),
            pl.BlockSpec((None, None, nh, t), lambda b, g, i: (b, g, 0, i)),
            pl.BlockSpec((None, None, nh, t), lambda b, g, i: (b, g, 0, i)),
            pl.BlockSpec((None, nt, None, nh, LANES), lambda b, g, i: (b, 0, g, 0, 0)),
            pl.BlockSpec((None, nh, LANES), lambda b, g, i: (g, 0, 0)),
        ],
        out_specs=pl.BlockSpec((None, t, nh * HEAD_DIM), lambda b, g, i: (b, i, g)),
        out_shape=jax.ShapeDtypeStruct((nb, rows, FOX_W), BF16),
        scratch_shapes=[pltpu.VMEM((nh, PAIR_W, t), F32)],
        compiler_params=pltpu.CompilerParams(
            dimension_semantics=("arbitrary", "arbitrary", "arbitrary"), vmem_limit_bytes=VMEM_LIMIT),
        name="fox",
    )(qt, ka, vt, ka_m, vt_m, grouped(cum), grouped(qn2),
      kmx.reshape(nb, nt, ngroup, nh, LANES), kmx_m.reshape(ngroup, nh, LANES))


def _ffn_kernel(oa_ref, ob_ref, x_ref, wo_ref, g1_ref, g2_ref, wg_ref, wu_ref, wd_ref, g3_ref, out_ref):
    tm = x_ref.shape[0]
    parts = [slice(c * tm // FFN_PARTS, (c + 1) * tm // FFN_PARTS) for c in range(FFN_PARTS)]
    mix = [jnp.concatenate([oa_ref[r, :], ob_ref[r, :]], axis=1) for r in parts]
    a = [_dot(m, wo_ref[...]) for m in mix]
    h1 = [x_ref[r, :] + _rms(ai, g1_ref[...]) for r, ai in zip(parts, a)]
    hn = [_rms(h, g2_ref[...]).astype(BF16) for h in h1]
    gate_up = [(_dot(h, wg_ref[...]), _dot(h, wu_ref[...])) for h in hn]
    act = [(g / (1.0 + jnp.exp(-g)) * u).astype(BF16) for g, u in gate_up]
    ff = [_dot(ac, wd_ref[...]) for ac in act]
    for r, h, f in zip(parts, h1, ff):
        out_ref[r, :] = h + _rms(f, g3_ref[...])


def _ffn(o_a, o_b, x3, wo, g1, g2, wg, wu, wd, g3):
    nb, rows, _ = x3.shape
    tm = FFN_TM
    const = lambda b, t: (0, 0)
    resident = lambda shape: pl.BlockSpec(shape, const, pipeline_mode=pl.Buffered(1))
    row = lambda w: pl.BlockSpec((None, tm, w), lambda b, t: (b, t, 0))
    return pl.pallas_call(
        _ffn_kernel,
        grid=(nb, rows // tm),
        in_specs=[
            row(SWA_Q_W), row(FOX_W), row(D_MODEL),
            resident((D_MODEL, D_MODEL)), resident((1, D_MODEL)), resident((1, D_MODEL)),
            resident((D_MODEL, D_FF)), resident((D_MODEL, D_FF)), resident((D_FF, D_MODEL)),
            resident((1, D_MODEL)),
        ],
        out_specs=row(D_MODEL),
        out_shape=jax.ShapeDtypeStruct((nb, rows, D_MODEL), F32),
        compiler_params=pltpu.CompilerParams(
            dimension_semantics=("arbitrary", "arbitrary"), vmem_limit_bytes=VMEM_LIMIT),
        name="outproj_ffn",
    )(o_a, o_b, x3, wo, g1, g2, wg, wu, wd, g3)


def kernel(x, meta_tokens, rel_bias, ln_pre_mix, ln_post_mix, ln_pre_ffn, ln_post_ffn,
           w_in, b_forget, sinks, w_out, w_gate_up, w_down):
    nb, seq, d = x.shape
    assert d == D_MODEL and seq % IN_TM == 0 and seq % FOX_T == 0 and seq % FFN_TM == 0 and seq % SWA_TQ == 0
    assert w_in.shape[0] == 1, "single-layer block"
    assert meta_tokens.shape == (N_META, D_MODEL)
    x = x.astype(F32)

    w_qkv = w_in[0, :, :OFF_VB].astype(BF16)
    pad_gate = GATE_ROWS - FOX_HEADS
    w_ft = jnp.pad(w_in[0, :, OFF_VB:].T, ((0, pad_gate), (0, 0))).astype(BF16)
    lane_rep = lambda v: jnp.broadcast_to(jnp.pad(v.astype(F32), (0, pad_gate))[:, None], (GATE_ROWS, LANES))
    b_ft = lane_rep(b_forget[0])
    g_pre = ln_pre_mix[0].astype(F32).reshape(1, D_MODEL)

    swa_bias = _bias_tables(rel_bias)

    x_m = jnp.pad(meta_tokens.astype(F32), ((0, META_TM - N_META), (0, 0)))[None]
    pa_m, _, ka_m, vt_m, cum_m, _, kmx_m = _inproj(x_m, g_pre, w_qkv, w_ft, b_ft,
                                                   jnp.zeros((GATE_ROWS, LANES), F32), META_TM)
    c0 = lane_rep(cum_m[0, :, N_META - 1])
    pa, qt, ka, vt, cum, qn2, kmx = _inproj(x, g_pre, w_qkv, w_ft, b_ft, c0, IN_TM)

    o_a = _swa(pa, pa_m[0, :N_META], swa_bias, sinks[0].astype(F32))
    o_b = _fox(qt, ka, vt, ka_m[0, :, :N_META], vt_m[0, :, :, :N_META], cum, qn2, kmx, kmx_m[0, 0])

    w_gu = w_gate_up[0]
    return _ffn(o_a, o_b, x,
                w_out[0].astype(BF16),
                ln_post_mix[0].astype(F32).reshape(1, D_MODEL),
                ln_pre_ffn[0].astype(F32).reshape(1, D_MODEL),
                w_gu[:, :D_FF].astype(BF16), w_gu[:, D_FF:].astype(BF16),
                w_down[0].astype(BF16),
                ln_post_ffn[0].astype(F32).reshape(1, D_MODEL))
```

```python
import math

import numpy as np
import jax
import jax.numpy as jnp
from jax import lax
from jax.experimental import pallas as pl
from jax.experimental.pallas import tpu as pltpu

D_MODEL = 1024
N_META = 16
HEAD_DIM = 64
SWA_Q_HEADS = 8
SWA_KV_HEADS = 2
SWA_GROUP = SWA_Q_HEADS // SWA_KV_HEADS
FOX_HEADS = 8
SWA_Q_W = SWA_Q_HEADS * HEAD_DIM
SWA_KV_W = SWA_KV_HEADS * HEAD_DIM
FOX_W = FOX_HEADS * HEAD_DIM
OFF_QA = SWA_Q_W
OFF_KA = OFF_QA + SWA_KV_W
OFF_VA = OFF_KA + SWA_KV_W
OFF_QB = OFF_VA + FOX_W
OFF_KB = OFF_QB + FOX_W
OFF_VB = OFF_KB + FOX_W
WINDOW = 128
BLOCK = 128
N_BUCKETS = 32
MAX_DISTANCE = 128
D_FF = 2816
EPS = 1e-6
NEG_INF = -1e30
SCALE = HEAD_DIM ** -0.5
LOG2E = math.log2(math.e)

LANES = 128
PAIR_W = 2 * HEAD_DIM
VMEM_LIMIT = 56 * 1024 * 1024

SWA_KEYS = 2 * BLOCK + N_META
SWA_TQ = 512
SWA_ONES = 16
IN_TM = 512
META_TM = 128
GATE_ROWS = 16
FOX_VROWS = HEAD_DIM + 16
FOX_T = 512
FOX_GROUP = 4
FOX_SAFE_LOG2 = 40.0
FFN_TM = 512
FFN_PARTS = 2
N_PIECES = 3

F32 = jnp.float32
BF16 = jnp.bfloat16


def _dot(a, b):
    return jnp.dot(a, b, preferred_element_type=F32)


def _dot_nt(a, b):
    return lax.dot_general(a, b, (((1,), (1,)), ((), ())), preferred_element_type=F32)


def _rms(t, g):
    return t * lax.rsqrt(jnp.mean(t * t, axis=-1, keepdims=True) + EPS) * g


def _bf16_pieces(v):
    pieces = []
    for _ in range(N_PIECES - 1):
        p = v.astype(BF16)
        pieces.append(p)
        v = v - p.astype(F32)
    pieces.append(v.astype(BF16))
    return pieces


def _t5_bucket_np(dist):
    n = np.maximum(dist, 0).astype(np.int32)
    max_exact = N_BUCKETS // 2
    nf = np.maximum(n, 1).astype(np.float32)
    large = max_exact + (np.log(nf / np.float32(max_exact)) / np.float32(math.log(MAX_DISTANCE / max_exact))
                         * np.float32(N_BUCKETS - max_exact)).astype(np.int32)
    large = np.minimum(large, N_BUCKETS - 1)
    return np.where(n < max_exact, n, large).astype(np.int32)


def _bias_kernel(tab_ref, bkt_ref, valid_ref, out_ref):
    bkt = bkt_ref[...]
    valid = valid_ref[...] > 0
    for h in range(SWA_Q_HEADS):
        acc = jnp.zeros(bkt.shape, F32)
        for b in range(N_BUCKETS):
            acc = jnp.where(bkt == b, tab_ref[b, h], acc)
        out_ref[:, h * BLOCK:(h + 1) * BLOCK] = jnp.where(valid, acc * LOG2E, NEG_INF)


def _bias_tables(rel_bias):
    ki = np.arange(2 * BLOCK)[:, None]
    qi = np.arange(BLOCK)[None, :]
    d_w = qi + BLOCK - ki
    bkt_w = _t5_bucket_np(d_w)
    in_window = (d_w >= 0) & (d_w < WINDOW)
    mi = np.arange(N_META)[:, None]
    bkt_m0 = _t5_bucket_np(N_META + qi - mi)
    far = _t5_bucket_np(np.asarray([[N_META + BLOCK - (N_META - 1)]]))
    assert far[0, 0] == N_BUCKETS - 1, "meta keys of later blocks must share the last bucket"
    bkt = np.stack([np.concatenate([bkt_w, bkt_m0]), np.concatenate([bkt_w, np.full_like(bkt_m0, far[0, 0])])])
    meta_ok = np.ones((N_META, BLOCK), bool)
    valid = np.stack([np.concatenate([in_window & (ki >= BLOCK), meta_ok]), np.concatenate([in_window, meta_ok])])
    spec = pl.BlockSpec((None, SWA_KEYS, BLOCK), lambda a: (a, 0, 0))
    return pl.pallas_call(
        _bias_kernel,
        grid=(2,),
        in_specs=[pl.BlockSpec(memory_space=pltpu.SMEM), spec, spec],
        out_specs=pl.BlockSpec((None, SWA_KEYS, SWA_Q_HEADS * BLOCK), lambda a: (a, 0, 0)),
        out_shape=jax.ShapeDtypeStruct((2, SWA_KEYS, SWA_Q_HEADS * BLOCK), F32),
        name="bias_tables",
    )(rel_bias.astype(F32), jnp.asarray(bkt.astype(np.int32)), jnp.asarray(valid.astype(np.int32)))


def _fox_aug8(val_row, pieces_first):
    n = val_row.shape[1]
    row = lax.broadcasted_iota(jnp.int32, (8, n), 0)
    p0, o0 = (0, N_PIECES) if pieces_first else (N_PIECES, 0)
    aug = jnp.where((row >= o0) & (row < o0 + N_PIECES), 1.0, 0.0)
    for i, piece in enumerate(_bf16_pieces(val_row)):
        aug = jnp.where(row == p0 + i, piece.astype(F32), aug)
    return aug


def _fox_head_t(own_t, aug8, e):
    pad = jnp.zeros((HEAD_DIM - 8, own_t.shape[1]), F32)
    return jnp.concatenate([own_t, aug8, pad] if e == 0 else [aug8, pad, own_t], axis=0)


def _inproj_kernel(x_ref, g_ref, w_ref, wft_ref, bft_ref, tri_ref, c0_ref,
                   pa_ref, qt_ref, ka_ref, vt_ref, cum_ref, qn2_ref, kmx_ref, carry_ref):
    @pl.when(pl.program_id(1) == 0)
    def _():
        carry_ref[...] = c0_ref[...]

    tm = x_ref.shape[0]
    y = _rms(x_ref[...], g_ref[...]).astype(BF16)

    tile_lanes = lambda a: jnp.concatenate([a] * (tm // LANES), axis=1)
    f_t = _dot_nt(wft_ref[...], y) + tile_lanes(bft_ref[...])
    acc_fox = _dot(y, w_ref[:, OFF_VA:OFF_VB])
    ls_t = jnp.minimum(f_t, 0.0) - jnp.log1p(jnp.exp(-jnp.abs(f_t)))
    cum_t = tile_lanes(carry_ref[...])
    for piece in _bf16_pieces(ls_t):
        cum_t = cum_t + _dot(piece, tri_ref[...])
    carry_ref[...] = jnp.broadcast_to(cum_t[:, tm - 1:tm], carry_ref.shape)
    for g in range(FOX_HEADS // FOX_GROUP):
        cum_ref[g] = cum_t[g * FOX_GROUP:(g + 1) * FOX_GROUP]
    cb_t = cum_t * (-LOG2E)

    acc_swa = _dot(y, w_ref[:, 0:OFF_VA])
    pa_ref[:, 0:OFF_QA] = (acc_swa[:, 0:OFF_QA] * (SCALE * LOG2E)).astype(BF16)
    pa_ref[:, OFF_QA:OFF_VA] = acc_swa[:, OFF_QA:OFF_VA].astype(BF16)

    ones = jnp.ones((FOX_VROWS - HEAD_DIM, tm), F32)
    for p in range(FOX_HEADS // 2):
        cols = lambda off: slice(off - OFF_VA + p * PAIR_W, off - OFF_VA + (p + 1) * PAIR_W)
        q_t = (acc_fox[:, cols(OFF_VA)] * (SCALE * LOG2E)).astype(BF16).astype(F32).T
        k_t = acc_fox[:, cols(OFF_QB)].astype(BF16).astype(F32).T
        v_t = acc_fox[:, cols(OFF_KB)].astype(BF16).astype(F32).T
        qt_ref[p] = q_t.astype(BF16)
        for e in range(2):
            h = 2 * p + e
            own = slice(e * HEAD_DIM, (e + 1) * HEAD_DIM)
            qn2_ref[h // FOX_GROUP, h % FOX_GROUP:h % FOX_GROUP + 1, :] = jnp.sum(q_t[own] * q_t[own], axis=0, keepdims=True)
            kn2 = jnp.sum(k_t[own] * k_t[own], axis=0, keepdims=True)
            kmx_ref[h:h + 1, :] = jnp.broadcast_to(jnp.max(kn2, axis=1, keepdims=True), (1, LANES))
            ka_ref[h] = _fox_head_t(k_t[own], _fox_aug8(cb_t[h:h + 1], True), e).T.astype(BF16)
            vt_ref[h] = jnp.concatenate([v_t[own], ones], axis=0).astype(BF16)


def _inproj(x3, g, w, wft, bft, c0, tm):
    nb, rows, _ = x3.shape
    nt = rows // tm
    tri = jnp.asarray(np.triu(np.ones((tm, tm), np.float32)), BF16)
    const = lambda b, t: (0, 0)
    npair = FOX_HEADS // 2
    ngroup = FOX_HEADS // FOX_GROUP
    return pl.pallas_call(
        _inproj_kernel,
        grid=(nb, nt),
        in_specs=[
            pl.BlockSpec((None, tm, D_MODEL), lambda b, t: (b, t, 0)),
            pl.BlockSpec((1, D_MODEL), const),
            pl.BlockSpec(w.shape, const),
            pl.BlockSpec((GATE_ROWS, D_MODEL), const),
            pl.BlockSpec((GATE_ROWS, LANES), const),
            pl.BlockSpec((tm, tm), const),
            pl.BlockSpec((GATE_ROWS, LANES), const),
        ],
        out_specs=[
            pl.BlockSpec((None, tm, OFF_VA), lambda b, t: (b, t, 0)),
            pl.BlockSpec((None, npair, PAIR_W, tm), lambda b, t: (b, 0, 0, t)),
            pl.BlockSpec((None, FOX_HEADS, tm, PAIR_W), lambda b, t: (b, 0, t, 0)),
            pl.BlockSpec((None, FOX_HEADS, FOX_VROWS, tm), lambda b, t: (b, 0, 0, t)),
            pl.BlockSpec((None, ngroup, FOX_GROUP, tm), lambda b, t: (b, 0, 0, t)),
            pl.BlockSpec((None, ngroup, FOX_GROUP, tm), lambda b, t: (b, 0, 0, t)),
            pl.BlockSpec((None, None, FOX_HEADS, LANES), lambda b, t: (b, t, 0, 0)),
        ],
        out_shape=[
            jax.ShapeDtypeStruct((nb, rows, OFF_VA), BF16),
            jax.ShapeDtypeStruct((nb, npair, PAIR_W, rows), BF16),
            jax.ShapeDtypeStruct((nb, FOX_HEADS, rows, PAIR_W), BF16),
            jax.ShapeDtypeStruct((nb, FOX_HEADS, FOX_VROWS, rows), BF16),
            jax.ShapeDtypeStruct((nb, ngroup, FOX_GROUP, rows), F32),
            jax.ShapeDtypeStruct((nb, ngroup, FOX_GROUP, rows), F32),
            jax.ShapeDtypeStruct((nb, nt, FOX_HEADS, LANES), F32),
        ],
        scratch_shapes=[pltpu.VMEM((GATE_ROWS, LANES), F32)],
        compiler_params=pltpu.CompilerParams(
            dimension_semantics=("arbitrary", "arbitrary"), vmem_limit_bytes=VMEM_LIMIT),
        name="inproj",
    )(x3, g, w, wft, bft, tri, c0)


def _swa_scores(q, k_all, bias):
    nq = q.shape[0]
    zeros = jnp.zeros((HEAD_DIM, nq), F32)
    blocks = []
    for pair in range(SWA_Q_HEADS // 2):
        q_t = q[:, pair * PAIR_W:(pair + 1) * PAIR_W].astype(F32).T
        for hp in range(2):
            qh = q_t[hp * HEAD_DIM:(hp + 1) * HEAD_DIM]
            g = (2 * pair + hp) // SWA_GROUP
            blocks.append(jnp.concatenate([qh, zeros] if g == 0 else [zeros, qh], axis=0))
    q_bd = jnp.concatenate(blocks, axis=1).astype(BF16)
    return _dot(k_all, q_bd) + bias


def _swa_outputs(s, v_all, sink_ref):
    nq = s.shape[1] // SWA_Q_HEADS
    ps, sink_terms = [], []
    for h in range(SWA_Q_HEADS):
        sh = s[:, h * nq:(h + 1) * nq]
        sink = sink_ref[h] * LOG2E
        m = jnp.maximum(jnp.max(sh, axis=0, keepdims=True), sink)
        ps.append(jnp.exp2(sh - m).astype(BF16))
        sink_terms.append(jnp.exp2(sink - m))
    n = BLOCK
    v_t = jnp.concatenate([v_all[0:n].astype(F32).T, v_all[n:2 * n].astype(F32).T,
                           v_all[2 * n:].astype(F32).T], axis=1)
    ones = jnp.ones((SWA_ONES, v_t.shape[1]), F32)
    outs = []
    for g in range(SWA_KV_HEADS):
        v_aug = jnp.concatenate([v_t[g * HEAD_DIM:(g + 1) * HEAD_DIM], ones], axis=0).astype(BF16)
        p_g = jnp.concatenate(ps[g * SWA_GROUP:(g + 1) * SWA_GROUP], axis=1)
        o_g = _dot(v_aug, p_g)
        for j in range(SWA_GROUP):
            cols = slice(j * nq, (j + 1) * nq)
            l = o_g[HEAD_DIM:HEAD_DIM + 1, cols] + sink_terms[g * SWA_GROUP + j]
            outs.append(o_g[0:HEAD_DIM, cols] / l)
    pairs = [jnp.concatenate(outs[2 * p:2 * p + 2], axis=0).T for p in range(SWA_Q_HEADS // 2)]
    return jnp.concatenate(pairs, axis=1).astype(BF16)


def _swa_kernel(sink_ref, q_ref, kc_ref, kp_ref, vc_ref, vp_ref, km_ref, vm_ref, bias0_ref, bias1_ref, o_ref):
    n = BLOCK
    nsub = q_ref.shape[0] // n
    k_rows = [kp_ref[...]] + [kc_ref[c * n:(c + 1) * n] for c in range(nsub)]
    v_rows = [vp_ref[...]] + [vc_ref[c * n:(c + 1) * n] for c in range(nsub)]
    s = []
    for c in range(nsub):
        k_all = jnp.concatenate([k_rows[c], k_rows[c + 1], km_ref[...]], axis=0)
        bias = bias0_ref[...] if c == 0 else bias1_ref[...]
        s.append(_swa_scores(q_ref[c * n:(c + 1) * n], k_all, bias))
    for c in range(nsub):
        v_all = jnp.concatenate([v_rows[c], v_rows[c + 1], vm_ref[...]], axis=0)
        o_ref[c * n:(c + 1) * n] = _swa_outputs(s[c], v_all, sink_ref)


def _swa(proj, projm, bias, sinks):
    nb, rows, _ = proj.shape
    tq = SWA_TQ
    ratio = tq // BLOCK
    kcol, vcol = OFF_QA // SWA_KV_W, OFF_KA // SWA_KV_W
    cur = lambda c: (lambda b, n: (b, n, c))
    prev = lambda c: (lambda b, n: (b, jnp.maximum(n * ratio - 1, 0), c))
    bias_spec = lambda idx: pl.BlockSpec((None, SWA_KEYS, SWA_Q_HEADS * BLOCK), idx)
    return pl.pallas_call(
        _swa_kernel,
        grid=(nb, rows // tq),
        in_specs=[
            pl.BlockSpec(memory_space=pltpu.SMEM),
            pl.BlockSpec((None, tq, SWA_Q_W), lambda b, n: (b, n, 0)),
            pl.BlockSpec((None, tq, SWA_KV_W), cur(kcol)),
            pl.BlockSpec((None, BLOCK, SWA_KV_W), prev(kcol)),
            pl.BlockSpec((None, tq, SWA_KV_W), cur(vcol)),
            pl.BlockSpec((None, BLOCK, SWA_KV_W), prev(vcol)),
            pl.BlockSpec((N_META, SWA_KV_W), lambda b, n: (0, kcol)),
            pl.BlockSpec((N_META, SWA_KV_W), lambda b, n: (0, vcol)),
            bias_spec(lambda b, n: (jnp.minimum(n, 1), 0, 0)),
            bias_spec(lambda b, n: (1, 0, 0)),
        ],
        out_specs=pl.BlockSpec((None, tq, SWA_Q_W), lambda b, n: (b, n, 0)),
        out_shape=jax.ShapeDtypeStruct((nb, rows, SWA_Q_W), BF16),
        compiler_params=pltpu.CompilerParams(
            dimension_semantics=("arbitrary", "arbitrary"), vmem_limit_bytes=VMEM_LIMIT),
        name="swa",
    )(sinks, proj, proj, proj, proj, proj, projm, projm, bias, bias)


def _fox_kernel(qt_ref, ka_ref, vt_ref, kam_ref, vtm_ref, cq_ref, qn2_ref, kmx_ref, kmxm_ref, o_ref, acc_scr):
    i = pl.program_id(2)
    t = o_ref.shape[0]
    nh = acc_scr.shape[0]
    colmax = lambda x: jnp.max(x, axis=0, keepdims=True)
    tile_rows = lambda j: pl.ds(pl.multiple_of(j * t, t), t)

    kmax2 = jnp.maximum(jnp.max(kmx_ref[...], axis=0), kmxm_ref[...])
    r = [jnp.sqrt(qn2_ref[h:h + 1, :] * kmax2[h:h + 1, 0:1]) * 1.001 for h in range(nh)]
    worst = r[0]
    for rh in r[1:]:
        worst = jnp.maximum(worst, rh)
    safe = jnp.max(worst) <= FOX_SAFE_LOG2

    qt = []
    for h in range(nh):
        e = h % 2
        u = cq_ref[h:h + 1, :] * LOG2E - jnp.where(safe, r[h], 0.0)
        own = qt_ref[h // 2, e * HEAD_DIM:(e + 1) * HEAD_DIM, :].astype(F32)
        qt.append(_fox_head_t(own, _fox_aug8(u, False), e).astype(BF16))

    def scores(h, j, diagonal):
        s = _dot(ka_ref[h, tile_rows(j), :], qt[h])
        if diagonal:
            key = lax.broadcasted_iota(jnp.int32, (t, t), 0)
            qry = lax.broadcasted_iota(jnp.int32, (t, t), 1)
            s = jnp.where(key <= qry, s, NEG_INF)
        return s

    @pl.when(safe)
    def _():
        s_meta = [_dot(kam_ref[h], qt[h]) for h in range(nh)]
        for h in range(nh):
            acc_scr[h] = _dot(vtm_ref[h], jnp.exp2(s_meta[h]).astype(BF16))

        def tile(j, diagonal):
            s_prev = None
            for h in range(nh):
                s = scores(h, j, diagonal)
                if h > 0:
                    acc_scr[h - 1] += _dot(vt_ref[h - 1, :, tile_rows(j)], jnp.exp2(s_prev).astype(BF16))
                s_prev = s
            acc_scr[nh - 1] += _dot(vt_ref[nh - 1, :, tile_rows(j)], jnp.exp2(s_prev).astype(BF16))

        def full_tile(j, carry):
            tile(j, False)
            return carry

        lax.fori_loop(0, i, full_tile, 0)
        tile(i, True)

    @pl.when(jnp.logical_not(safe))
    def _():
        m0 = []
        for h in range(nh):
            s = _dot(kam_ref[h], qt[h])
            m = colmax(s)
            acc_scr[h] = _dot(vtm_ref[h], jnp.exp2(s - m).astype(BF16))
            m0.append(m)

        def tile(j, ms, diagonal):
            out = []
            for h in range(nh):
                s = scores(h, j, diagonal)
                m_next = jnp.maximum(ms[h], colmax(s))
                p = jnp.exp2(s - m_next).astype(BF16)
                acc_scr[h] = jnp.exp2(ms[h] - m_next) * acc_scr[h] + _dot(vt_ref[h, :, tile_rows(j)], p)
                out.append(m_next)
            return tuple(out)

        ms = lax.fori_loop(0, i, lambda j, ms: tile(j, ms, False), tuple(m0))
        tile(i, ms, True)

    for pp in range(nh // 2):
        o_t = jnp.concatenate([acc_scr[h, 0:HEAD_DIM] / acc_scr[h, HEAD_DIM:HEAD_DIM + 1]
                               for h in (2 * pp, 2 * pp + 1)], axis=0)
        o_ref[:, pp * PAIR_W:(pp + 1) * PAIR_W] = o_t.T.astype(BF16)


def _fox(qt, ka, vt, ka_m, vt_m, cum, qn2, kmx, kmx_m):
    nb, _, rows, _ = ka.shape
    t = FOX_T
    nh = FOX_GROUP
    ngroup = FOX_HEADS // nh
    nt = kmx.shape[1]
    return pl.pallas_call(
        _fox_kernel,
        grid=(nb, ngroup, rows // t),
        in_specs=[
            pl.BlockSpec((None, nh // 2, PAIR_W, t), lambda b, g, i: (b, g, 0, i)),
            pl.BlockSpec((None, nh, rows, PAIR_W), lambda b, g, i: (b, g, 0, 0)),
            pl.BlockSpec((None, nh, FOX_VROWS, rows), lambda b, g, i: (b, g, 0, 0)),
            pl.BlockSpec((nh, N_META, PAIR_W), lambda b, g, i: (g, 0, 0)),
            pl.BlockSpec((nh, FOX_VROWS, N_META), lambda b, g, i: (g, 0, 0)),
            pl.BlockSpec((None, None, nh, t), lambda b, g, i: (b, g, 0, i)),
            pl.BlockSpec((None, None, nh, t), lambda b, g, i: (b, g, 0, i)),
            pl.BlockSpec((None, nt, None, nh, LANES), lambda b, g, i: (b, 0, g, 0, 0)),
            pl.BlockSpec((None, nh, LANES), lambda b, g, i: (g, 0, 0)),
        ],
        out_specs=pl.BlockSpec((None, t, nh * HEAD_DIM), lambda b, g, i: (b, i, g)),
        out_shape=jax.ShapeDtypeStruct((nb, rows, FOX_W), BF16),
        scratch_shapes=[pltpu.VMEM((nh, FOX_VROWS, t), F32)],
        compiler_params=pltpu.CompilerParams(
            dimension_semantics=("arbitrary", "arbitrary", "arbitrary"), vmem_limit_bytes=VMEM_LIMIT),
        name="fox",
    )(qt, ka, vt, ka_m, vt_m, cum, qn2,
      kmx.reshape(nb, nt, ngroup, nh, LANES), kmx_m.reshape(ngroup, nh, LANES))


def _ffn_kernel(oa_ref, ob_ref, x_ref, wo_ref, g1_ref, g2_ref, wg_ref, wu_ref, wd_ref, g3_ref, out_ref):
    tm = x_ref.shape[0]
    parts = [slice(c * tm // FFN_PARTS, (c + 1) * tm // FFN_PARTS) for c in range(FFN_PARTS)]
    mix = [jnp.concatenate([oa_ref[r, :], ob_ref[r, :]], axis=1) for r in parts]
    a = [_dot(m, wo_ref[...]) for m in mix]
    h1 = [x_ref[r, :] + _rms(ai, g1_ref[...]) for r, ai in zip(parts, a)]
    hn = [_rms(h, g2_ref[...]).astype(BF16) for h in h1]
    gate_up = [(_dot(h, wg_ref[...]), _dot(h, wu_ref[...])) for h in hn]
    act = [(g / (1.0 + jnp.exp(-g)) * u).astype(BF16) for g, u in gate_up]
    ff = [_dot(ac, wd_ref[...]) for ac in act]
    for r, h, f in zip(parts, h1, ff):
        out_ref[r, :] = h + _rms(f, g3_ref[...])


def _ffn(o_a, o_b, x3, wo, g1, g2, wgu, wd, g3):
    nb, rows, _ = x3.shape
    tm = FFN_TM
    const = lambda b, t: (0, 0)
    resident = lambda shape: pl.BlockSpec(shape, const, pipeline_mode=pl.Buffered(1))
    row = lambda w: pl.BlockSpec((None, tm, w), lambda b, t: (b, t, 0))
    return pl.pallas_call(
        _ffn_kernel,
        grid=(nb, rows // tm),
        in_specs=[
            row(SWA_Q_W), row(FOX_W), row(D_MODEL),
            resident((D_MODEL, D_MODEL)), resident((1, D_MODEL)), resident((1, D_MODEL)),
            pl.BlockSpec((D_MODEL, D_FF), lambda b, t: (0, 0), pipeline_mode=pl.Buffered(1)),
            pl.BlockSpec((D_MODEL, D_FF), lambda b, t: (0, 1), pipeline_mode=pl.Buffered(1)),
            resident((D_FF, D_MODEL)),
            resident((1, D_MODEL)),
        ],
        out_specs=row(D_MODEL),
        out_shape=jax.ShapeDtypeStruct((nb, rows, D_MODEL), F32),
        compiler_params=pltpu.CompilerParams(
            dimension_semantics=("arbitrary", "arbitrary"), vmem_limit_bytes=VMEM_LIMIT),
        name="outproj_ffn",
    )(o_a, o_b, x3, wo, g1, g2, wgu, wgu, wd, g3)


def kernel(x, meta_tokens, rel_bias, ln_pre_mix, ln_post_mix, ln_pre_ffn, ln_post_ffn,
           w_in, b_forget, sinks, w_out, w_gate_up, w_down):
    nb, seq, d = x.shape
    assert d == D_MODEL and seq % IN_TM == 0 and seq % FOX_T == 0 and seq % FFN_TM == 0 and seq % SWA_TQ == 0
    assert w_in.shape[0] == 1, "single-layer block"
    assert meta_tokens.shape == (N_META, D_MODEL)
    x = x.astype(F32)

    w_all = w_in[0].astype(BF16)
    pad_gate = GATE_ROWS - FOX_HEADS
    w_ft = jnp.pad(w_in[0, :, OFF_VB:].T, ((0, pad_gate), (0, 0))).astype(BF16)
    lane_rep = lambda v: jnp.broadcast_to(jnp.pad(v.astype(F32), (0, pad_gate))[:, None], (GATE_ROWS, LANES))
    b_ft = lane_rep(b_forget[0])
    g_pre = ln_pre_mix[0].astype(F32).reshape(1, D_MODEL)

    swa_bias = _bias_tables(rel_bias)

    x_m = jnp.pad(meta_tokens.astype(F32), ((0, META_TM - N_META), (0, 0)))[None]
    pa_m, _, ka_m, vt_m, cum_m, _, kmx_m = _inproj(x_m, g_pre, w_all, w_ft, b_ft,
                                                   jnp.zeros((GATE_ROWS, LANES), F32), META_TM)
    c0 = lane_rep(cum_m[0, :, :, N_META - 1].reshape(FOX_HEADS))
    pa, qt, ka, vt, cum, qn2, kmx = _inproj(x, g_pre, w_all, w_ft, b_ft, c0, IN_TM)

    o_a = _swa(pa, pa_m[0, :N_META], swa_bias, sinks[0].astype(F32))
    o_b = _fox(qt, ka, vt, ka_m[0, :, :N_META], vt_m[0, :, :, :N_META], cum, qn2, kmx, kmx_m[0, 0])

    return _ffn(o_a, o_b, x,
                w_out[0].astype(BF16),
                ln_post_mix[0].astype(F32).reshape(1, D_MODEL),
                ln_pre_ffn[0].astype(F32).reshape(1, D_MODEL),
                w_gate_up[0].astype(BF16),
                w_down[0].astype(BF16),
                ln_post_ffn[0].astype(F32).reshape(1, D_MODEL))
```

```python
import math

import numpy as np
import jax
import jax.numpy as jnp
from jax import lax
from jax.experimental import pallas as pl
from jax.experimental.pallas import tpu as pltpu

D_MODEL = 1024
N_META = 16
HEAD_DIM = 64
SWA_Q_HEADS = 8
SWA_KV_HEADS = 2
SWA_GROUP = SWA_Q_HEADS // SWA_KV_HEADS
FOX_HEADS = 8
SWA_Q_W = SWA_Q_HEADS * HEAD_DIM
SWA_KV_W = SWA_KV_HEADS * HEAD_DIM
FOX_W = FOX_HEADS * HEAD_DIM
OFF_QA = SWA_Q_W
OFF_KA = OFF_QA + SWA_KV_W
OFF_VA = OFF_KA + SWA_KV_W
OFF_QB = OFF_VA + FOX_W
OFF_KB = OFF_QB + FOX_W
OFF_VB = OFF_KB + FOX_W
WINDOW = 128
BLOCK = 128
N_BUCKETS = 32
MAX_DISTANCE = 128
D_FF = 2816
EPS = 1e-6
NEG_INF = -1e30
SCALE = HEAD_DIM ** -0.5
LOG2E = math.log2(math.e)

LANES = 128
PAIR_W = 2 * HEAD_DIM
VMEM_LIMIT = 56 * 1024 * 1024

SWA_KEYS = 2 * BLOCK + N_META
SWA_TQ = 512
SWA_ONES = 16
IN_TM = 512
META_TM = 128
GATE_ROWS = 16
FOX_VROWS = PAIR_W
FOX_T = 512
FOX_GROUP = 4
FOX_SAFE_LOG2 = 40.0
FFN_TM = 512
FFN_PARTS = 2
N_PIECES = 3

F32 = jnp.float32
BF16 = jnp.bfloat16


def _dot(a, b):
    return jnp.dot(a, b, preferred_element_type=F32)


def _dot_nt(a, b):
    return lax.dot_general(a, b, (((1,), (1,)), ((), ())), preferred_element_type=F32)


def _rms(t, g):
    return t * lax.rsqrt(jnp.mean(t * t, axis=-1, keepdims=True) + EPS) * g


def _bf16_pieces(v):
    pieces = []
    for _ in range(N_PIECES - 1):
        p = v.astype(BF16)
        pieces.append(p)
        v = v - p.astype(F32)
    pieces.append(v.astype(BF16))
    return pieces


def _t5_bucket_np(dist):
    n = np.maximum(dist, 0).astype(np.int32)
    max_exact = N_BUCKETS // 2
    nf = np.maximum(n, 1).astype(np.float32)
    large = max_exact + (np.log(nf / np.float32(max_exact)) / np.float32(math.log(MAX_DISTANCE / max_exact))
                         * np.float32(N_BUCKETS - max_exact)).astype(np.int32)
    large = np.minimum(large, N_BUCKETS - 1)
    return np.where(n < max_exact, n, large).astype(np.int32)


def _bias_kernel(tab_ref, bkt_ref, valid_ref, out_ref):
    bkt = bkt_ref[...]
    valid = valid_ref[...] > 0
    for h in range(SWA_Q_HEADS):
        acc = jnp.zeros(bkt.shape, F32)
        for b in range(N_BUCKETS):
            acc = jnp.where(bkt == b, tab_ref[b, h], acc)
        out_ref[:, h * BLOCK:(h + 1) * BLOCK] = jnp.where(valid, acc * LOG2E, NEG_INF)


def _bias_tables(rel_bias):
    ki = np.arange(2 * BLOCK)[:, None]
    qi = np.arange(BLOCK)[None, :]
    d_w = qi + BLOCK - ki
    bkt_w = _t5_bucket_np(d_w)
    in_window = (d_w >= 0) & (d_w < WINDOW)
    mi = np.arange(N_META)[:, None]
    bkt_m0 = _t5_bucket_np(N_META + qi - mi)
    far = _t5_bucket_np(np.asarray([[N_META + BLOCK - (N_META - 1)]]))
    assert far[0, 0] == N_BUCKETS - 1, "meta keys of later blocks must share the last bucket"
    bkt = np.stack([np.concatenate([bkt_w, bkt_m0]), np.concatenate([bkt_w, np.full_like(bkt_m0, far[0, 0])])])
    meta_ok = np.ones((N_META, BLOCK), bool)
    valid = np.stack([np.concatenate([in_window & (ki >= BLOCK), meta_ok]), np.concatenate([in_window, meta_ok])])
    spec = pl.BlockSpec((None, SWA_KEYS, BLOCK), lambda a: (a, 0, 0))
    return pl.pallas_call(
        _bias_kernel,
        grid=(2,),
        in_specs=[pl.BlockSpec(memory_space=pltpu.SMEM), spec, spec],
        out_specs=pl.BlockSpec((None, SWA_KEYS, SWA_Q_HEADS * BLOCK), lambda a: (a, 0, 0)),
        out_shape=jax.ShapeDtypeStruct((2, SWA_KEYS, SWA_Q_HEADS * BLOCK), F32),
        name="bias_tables",
    )(rel_bias.astype(F32), jnp.asarray(bkt.astype(np.int32)), jnp.asarray(valid.astype(np.int32)))


def _fox_aug8(val_row, pieces_first):
    n = val_row.shape[1]
    row = lax.broadcasted_iota(jnp.int32, (8, n), 0)
    p0, o0 = (0, N_PIECES) if pieces_first else (N_PIECES, 0)
    aug = jnp.where((row >= o0) & (row < o0 + N_PIECES), 1.0, 0.0)
    for i, piece in enumerate(_bf16_pieces(val_row)):
        aug = jnp.where(row == p0 + i, piece.astype(F32), aug)
    return aug


def _fox_head_t(own_t, aug8, e):
    pad = jnp.zeros((HEAD_DIM - 8, own_t.shape[1]), F32)
    return jnp.concatenate([own_t, aug8, pad] if e == 0 else [aug8, pad, own_t], axis=0)


def _inproj_kernel(x_ref, g_ref, w_ref, wf_ref, bft_ref, tri_ref, c0_ref,
                   pa_ref, qt_ref, ka_ref, vt_ref, cum_ref, qn2_ref, kmx_ref, carry_ref):
    @pl.when(pl.program_id(1) == 0)
    def _():
        carry_ref[...] = c0_ref[...]

    tm = x_ref.shape[0]
    y = _rms(x_ref[...], g_ref[...]).astype(BF16)

    tile_lanes = lambda a: jnp.concatenate([a] * (tm // LANES), axis=1)
    f_t = lax.dot_general(wf_ref[...], y, (((0,), (1,)), ((), ())), preferred_element_type=F32) + tile_lanes(bft_ref[...])
    acc_fox = _dot(y, w_ref[:, OFF_VA:OFF_VB])
    ls_t = jnp.minimum(f_t, 0.0) - jnp.log1p(jnp.exp(-jnp.abs(f_t)))
    cum_t = tile_lanes(carry_ref[...])
    for piece in _bf16_pieces(ls_t):
        cum_t = cum_t + _dot(piece, tri_ref[...])
    carry_ref[...] = jnp.broadcast_to(cum_t[:, tm - 1:tm], carry_ref.shape)
    for g in range(FOX_HEADS // FOX_GROUP):
        cum_ref[g] = cum_t[g * FOX_GROUP:(g + 1) * FOX_GROUP]
    cb_t = cum_t * (-LOG2E)

    acc_swa = _dot(y, w_ref[:, 0:OFF_VA])
    pa_ref[:, 0:OFF_QA] = (acc_swa[:, 0:OFF_QA] * (SCALE * LOG2E)).astype(BF16)
    pa_ref[:, OFF_QA:OFF_VA] = acc_swa[:, OFF_QA:OFF_VA].astype(BF16)

    ones = jnp.ones((FOX_VROWS - HEAD_DIM, tm), F32)
    for p in range(FOX_HEADS // 2):
        cols = lambda off: slice(off - OFF_VA + p * PAIR_W, off - OFF_VA + (p + 1) * PAIR_W)
        q_t = (acc_fox[:, cols(OFF_VA)] * (SCALE * LOG2E)).astype(BF16).astype(F32).T
        k_t = acc_fox[:, cols(OFF_QB)].astype(BF16).astype(F32).T
        v_t = acc_fox[:, cols(OFF_KB)].astype(BF16).astype(F32).T
        qt_ref[p] = q_t.astype(BF16)
        for e in range(2):
            h = 2 * p + e
            own = slice(e * HEAD_DIM, (e + 1) * HEAD_DIM)
            qn2_ref[h // FOX_GROUP, h % FOX_GROUP:h % FOX_GROUP + 1, :] = jnp.sum(q_t[own] * q_t[own], axis=0, keepdims=True)
            kn2 = jnp.sum(k_t[own] * k_t[own], axis=0, keepdims=True)
            kmx_ref[h:h + 1, :] = jnp.broadcast_to(jnp.max(kn2, axis=1, keepdims=True), (1, LANES))
            ka_ref[h] = _fox_head_t(k_t[own], _fox_aug8(cb_t[h:h + 1], True), e).T.astype(BF16)
            vt_ref[h] = jnp.concatenate([v_t[own], ones], axis=0).astype(BF16)


def _inproj(x3, g, w, wft, bft, c0, tm):
    nb, rows, _ = x3.shape
    nt = rows // tm
    tri = jnp.asarray(np.triu(np.ones((tm, tm), np.float32)), BF16)
    const = lambda b, t: (0, 0)
    npair = FOX_HEADS // 2
    ngroup = FOX_HEADS // FOX_GROUP
    return pl.pallas_call(
        _inproj_kernel,
        grid=(nb, nt),
        in_specs=[
            pl.BlockSpec((None, tm, D_MODEL), lambda b, t: (b, t, 0)),
            pl.BlockSpec((1, D_MODEL), const),
            pl.BlockSpec(w.shape, const),
            pl.BlockSpec((D_MODEL, GATE_ROWS), const),
            pl.BlockSpec((GATE_ROWS, LANES), const),
            pl.BlockSpec((tm, tm), const),
            pl.BlockSpec((GATE_ROWS, LANES), const),
        ],
        out_specs=[
            pl.BlockSpec((None, tm, OFF_VA), lambda b, t: (b, t, 0)),
            pl.BlockSpec((None, npair, PAIR_W, tm), lambda b, t: (b, 0, 0, t)),
            pl.BlockSpec((None, FOX_HEADS, tm, PAIR_W), lambda b, t: (b, 0, t, 0)),
            pl.BlockSpec((None, FOX_HEADS, FOX_VROWS, tm), lambda b, t: (b, 0, 0, t)),
            pl.BlockSpec((None, ngroup, FOX_GROUP, tm), lambda b, t: (b, 0, 0, t)),
            pl.BlockSpec((None, ngroup, FOX_GROUP, tm), lambda b, t: (b, 0, 0, t)),
            pl.BlockSpec((None, None, FOX_HEADS, LANES), lambda b, t: (b, t, 0, 0)),
        ],
        out_shape=[
            jax.ShapeDtypeStruct((nb, rows, OFF_VA), BF16),
            jax.ShapeDtypeStruct((nb, npair, PAIR_W, rows), BF16),
            jax.ShapeDtypeStruct((nb, FOX_HEADS, rows, PAIR_W), BF16),
            jax.ShapeDtypeStruct((nb, FOX_HEADS, FOX_VROWS, rows), BF16),
            jax.ShapeDtypeStruct((nb, ngroup, FOX_GROUP, rows), F32),
            jax.ShapeDtypeStruct((nb, ngroup, FOX_GROUP, rows), F32),
            jax.ShapeDtypeStruct((nb, nt, FOX_HEADS, LANES), F32),
        ],
        scratch_shapes=[pltpu.VMEM((GATE_ROWS, LANES), F32)],
        compiler_params=pltpu.CompilerParams(
            dimension_semantics=("arbitrary", "arbitrary"), vmem_limit_bytes=VMEM_LIMIT),
        name="inproj",
    )(x3, g, w, wft, bft, tri, c0)


def _swa_scores(q, k_all, bias):
    nq = q.shape[0]
    zeros = jnp.zeros((HEAD_DIM, nq), F32)
    blocks = []
    for pair in range(SWA_Q_HEADS // 2):
        q_t = q[:, pair * PAIR_W:(pair + 1) * PAIR_W].astype(F32).T
        for hp in range(2):
            qh = q_t[hp * HEAD_DIM:(hp + 1) * HEAD_DIM]
            g = (2 * pair + hp) // SWA_GROUP
            blocks.append(jnp.concatenate([qh, zeros] if g == 0 else [zeros, qh], axis=0))
    q_bd = jnp.concatenate(blocks, axis=1).astype(BF16)
    return _dot(k_all, q_bd) + bias


def _swa_outputs(s, v_all, sink_ref):
    nq = s.shape[1] // SWA_Q_HEADS
    ps, sink_terms = [], []
    for h in range(SWA_Q_HEADS):
        sh = s[:, h * nq:(h + 1) * nq]
        sink = sink_ref[h] * LOG2E
        m = jnp.maximum(jnp.max(sh, axis=0, keepdims=True), sink)
        ps.append(jnp.exp2(sh - m).astype(BF16))
        sink_terms.append(jnp.exp2(sink - m))
    n = BLOCK
    v_t = jnp.concatenate([v_all[0:n].astype(F32).T, v_all[n:2 * n].astype(F32).T,
                           v_all[2 * n:].astype(F32).T], axis=1)
    ones = jnp.ones((SWA_ONES, v_t.shape[1]), F32)
    outs = []
    for g in range(SWA_KV_HEADS):
        v_aug = jnp.concatenate([v_t[g * HEAD_DIM:(g + 1) * HEAD_DIM], ones], axis=0).astype(BF16)
        p_g = jnp.concatenate(ps[g * SWA_GROUP:(g + 1) * SWA_GROUP], axis=1)
        o_g = _dot(v_aug, p_g)
        for j in range(SWA_GROUP):
            cols = slice(j * nq, (j + 1) * nq)
            l = o_g[HEAD_DIM:HEAD_DIM + 1, cols] + sink_terms[g * SWA_GROUP + j]
            outs.append(o_g[0:HEAD_DIM, cols] / l)
    pairs = [jnp.concatenate(outs[2 * p:2 * p + 2], axis=0).T for p in range(SWA_Q_HEADS // 2)]
    return jnp.concatenate(pairs, axis=1).astype(BF16)


def _swa_kernel(sink_ref, q_ref, kc_ref, kp_ref, vc_ref, vp_ref, km_ref, vm_ref, bias0_ref, bias1_ref, o_ref):
    n = BLOCK
    nsub = q_ref.shape[0] // n
    k_rows = [kp_ref[...]] + [kc_ref[c * n:(c + 1) * n] for c in range(nsub)]
    v_rows = [vp_ref[...]] + [vc_ref[c * n:(c + 1) * n] for c in range(nsub)]
    s = []
    for c in range(nsub):
        k_all = jnp.concatenate([k_rows[c], k_rows[c + 1], km_ref[...]], axis=0)
        bias = bias0_ref[...] if c == 0 else bias1_ref[...]
        s.append(_swa_scores(q_ref[c * n:(c + 1) * n], k_all, bias))
    for c in range(nsub):
        v_all = jnp.concatenate([v_rows[c], v_rows[c + 1], vm_ref[...]], axis=0)
        o_ref[c * n:(c + 1) * n] = _swa_outputs(s[c], v_all, sink_ref)


def _swa(proj, projm, bias, sinks):
    nb, rows, _ = proj.shape
    tq = SWA_TQ
    ratio = tq // BLOCK
    kcol, vcol = OFF_QA // SWA_KV_W, OFF_KA // SWA_KV_W
    cur = lambda c: (lambda b, n: (b, n, c))
    prev = lambda c: (lambda b, n: (b, jnp.maximum(n * ratio - 1, 0), c))
    bias_spec = lambda idx: pl.BlockSpec((None, SWA_KEYS, SWA_Q_HEADS * BLOCK), idx)
    return pl.pallas_call(
        _swa_kernel,
        grid=(nb, rows // tq),
        in_specs=[
            pl.BlockSpec(memory_space=pltpu.SMEM),
            pl.BlockSpec((None, tq, SWA_Q_W), lambda b, n: (b, n, 0)),
            pl.BlockSpec((None, tq, SWA_KV_W), cur(kcol)),
            pl.BlockSpec((None, BLOCK, SWA_KV_W), prev(kcol)),
            pl.BlockSpec((None, tq, SWA_KV_W), cur(vcol)),
            pl.BlockSpec((None, BLOCK, SWA_KV_W), prev(vcol)),
            pl.BlockSpec((N_META, SWA_KV_W), lambda b, n: (0, kcol)),
            pl.BlockSpec((N_META, SWA_KV_W), lambda b, n: (0, vcol)),
            bias_spec(lambda b, n: (jnp.minimum(n, 1), 0, 0)),
            bias_spec(lambda b, n: (1, 0, 0)),
        ],
        out_specs=pl.BlockSpec((None, tq, SWA_Q_W), lambda b, n: (b, n, 0)),
        out_shape=jax.ShapeDtypeStruct((nb, rows, SWA_Q_W), BF16),
        compiler_params=pltpu.CompilerParams(
            dimension_semantics=("arbitrary", "arbitrary"), vmem_limit_bytes=VMEM_LIMIT),
        name="swa",
    )(sinks, proj, proj, proj, proj, proj, projm, projm, bias, bias)


def _fox_kernel(qt_ref, ka_ref, vt_ref, kam_ref, vtm_ref, cq_ref, qn2_ref, kmx_ref, kmxm_ref, o_ref, acc_scr):
    i = pl.program_id(2)
    t = o_ref.shape[0]
    nh = acc_scr.shape[0]
    colmax = lambda x: jnp.max(x, axis=0, keepdims=True)
    tile_rows = lambda j: pl.ds(pl.multiple_of(j * t, t), t)

    kmax2 = jnp.maximum(jnp.max(kmx_ref[...], axis=0), kmxm_ref[...])
    r = [jnp.sqrt(qn2_ref[h:h + 1, :] * kmax2[h:h + 1, 0:1]) * 1.001 for h in range(nh)]
    worst = r[0]
    for rh in r[1:]:
        worst = jnp.maximum(worst, rh)
    safe = jnp.max(worst) <= FOX_SAFE_LOG2

    qt = []
    for h in range(nh):
        e = h % 2
        u = cq_ref[h:h + 1, :] * LOG2E - jnp.where(safe, r[h], 0.0)
        own = qt_ref[h // 2, e * HEAD_DIM:(e + 1) * HEAD_DIM, :].astype(F32)
        qt.append(_fox_head_t(own, _fox_aug8(u, False), e).astype(BF16))

    def scores(h, j, diagonal):
        s = _dot(ka_ref[h, tile_rows(j), :], qt[h])
        if diagonal:
            key = lax.broadcasted_iota(jnp.int32, (t, t), 0)
            qry = lax.broadcasted_iota(jnp.int32, (t, t), 1)
            s = jnp.where(key <= qry, s, NEG_INF)
        return s

    @pl.when(safe)
    def _():
        s_meta = [_dot(kam_ref[h], qt[h]) for h in range(nh)]
        for h in range(nh):
            acc_scr[h] = _dot(vtm_ref[h], jnp.exp2(s_meta[h]).astype(BF16))

        def tile(j, diagonal):
            s_prev = None
            for h in range(nh):
                s = scores(h, j, diagonal)
                if h > 0:
                    acc_scr[h - 1] += _dot(vt_ref[h - 1, :, tile_rows(j)], jnp.exp2(s_prev).astype(BF16))
                s_prev = s
            acc_scr[nh - 1] += _dot(vt_ref[nh - 1, :, tile_rows(j)], jnp.exp2(s_prev).astype(BF16))

        def full_tile(j, carry):
            tile(j, False)
            return carry

        lax.fori_loop(0, i, full_tile, 0)
        tile(i, True)

    @pl.when(jnp.logical_not(safe))
    def _():
        m0 = []
        for h in range(nh):
            s = _dot(kam_ref[h], qt[h])
            m = colmax(s)
            acc_scr[h] = _dot(vtm_ref[h], jnp.exp2(s - m).astype(BF16))
            m0.append(m)

        def tile(j, ms, diagonal):
            out = []
            for h in range(nh):
                s = scores(h, j, diagonal)
                m_next = jnp.maximum(ms[h], colmax(s))
                p = jnp.exp2(s - m_next).astype(BF16)
                acc_scr[h] = jnp.exp2(ms[h] - m_next) * acc_scr[h] + _dot(vt_ref[h, :, tile_rows(j)], p)
                out.append(m_next)
            return tuple(out)

        ms = lax.fori_loop(0, i, lambda j, ms: tile(j, ms, False), tuple(m0))
        tile(i, ms, True)

    for pp in range(nh // 2):
        o_t = jnp.concatenate([acc_scr[h, 0:HEAD_DIM] / acc_scr[h, HEAD_DIM:HEAD_DIM + 1]
                               for h in (2 * pp, 2 * pp + 1)], axis=0)
        o_ref[:, pp * PAIR_W:(pp + 1) * PAIR_W] = o_t.T.astype(BF16)


def _fox(qt, ka, vt, ka_m, vt_m, cum, qn2, kmx, kmx_m):
    nb, _, rows, _ = ka.shape
    t = FOX_T
    nh = FOX_GROUP
    ngroup = FOX_HEADS // nh
    nt = kmx.shape[1]
    return pl.pallas_call(
        _fox_kernel,
        grid=(nb, ngroup, rows // t),
        in_specs=[
            pl.BlockSpec((None, nh // 2, PAIR_W, t), lambda b, g, i: (b, g, 0, i)),
            pl.BlockSpec((None, nh, rows, PAIR_W), lambda b, g, i: (b, g, 0, 0)),
            pl.BlockSpec((None, nh, FOX_VROWS, rows), lambda b, g, i: (b, g, 0, 0)),
            pl.BlockSpec((nh, N_META, PAIR_W), lambda b, g, i: (g, 0, 0)),
            pl.BlockSpec((nh, FOX_VROWS, N_META), lambda b, g, i: (g, 0, 0)),
            pl.BlockSpec((None, None, nh, t), lambda b, g, i: (b, g, 0, i)),
            pl.BlockSpec((None, None, nh, t), lambda b, g, i: (b, g, 0, i)),
            pl.BlockSpec((None, nt, None, nh, LANES), lambda b, g, i: (b, 0, g, 0, 0)),
            pl.BlockSpec((None, nh, LANES), lambda b, g, i: (g, 0, 0)),
        ],
        out_specs=pl.BlockSpec((None, t, nh * HEAD_DIM), lambda b, g, i: (b, i, g)),
        out_shape=jax.ShapeDtypeStruct((nb, rows, FOX_W), BF16),
        scratch_shapes=[pltpu.VMEM((nh, FOX_VROWS, t), F32)],
        compiler_params=pltpu.CompilerParams(
            dimension_semantics=("arbitrary", "arbitrary", "arbitrary"), vmem_limit_bytes=VMEM_LIMIT),
        name="fox",
    )(qt, ka, vt, ka_m, vt_m, cum, qn2,
      kmx.reshape(nb, nt, ngroup, nh, LANES), kmx_m.reshape(ngroup, nh, LANES))


def _ffn_kernel(oa_ref, ob_ref, x_ref, wo_ref, g1_ref, g2_ref, wg_ref, wu_ref, wd_ref, g3_ref, out_ref):
    tm = x_ref.shape[0]
    parts = [slice(c * tm // FFN_PARTS, (c + 1) * tm // FFN_PARTS) for c in range(FFN_PARTS)]
    mix = [jnp.concatenate([oa_ref[r, :], ob_ref[r, :]], axis=1) for r in parts]
    a = [_dot(m, wo_ref[...]) for m in mix]
    h1 = [x_ref[r, :] + _rms(ai, g1_ref[...]) for r, ai in zip(parts, a)]
    hn = [_rms(h, g2_ref[...]).astype(BF16) for h in h1]
    gate_up = [(_dot(h, wg_ref[...]), _dot(h, wu_ref[...])) for h in hn]
    act = [(g / (1.0 + jnp.exp(-g)) * u).astype(BF16) for g, u in gate_up]
    ff = [_dot(ac, wd_ref[...]) for ac in act]
    for r, h, f in zip(parts, h1, ff):
        out_ref[r, :] = h + _rms(f, g3_ref[...])


def _ffn(o_a, o_b, x3, wo, g1, g2, wgu, wd, g3):
    nb, rows, _ = x3.shape
    tm = FFN_TM
    const = lambda b, t: (0, 0)
    resident = lambda shape: pl.BlockSpec(shape, const, pipeline_mode=pl.Buffered(1))
    row = lambda w: pl.BlockSpec((None, tm, w), lambda b, t: (b, t, 0))
    return pl.pallas_call(
        _ffn_kernel,
        grid=(nb, rows // tm),
        in_specs=[
            row(SWA_Q_W), row(FOX_W), row(D_MODEL),
            resident((D_MODEL, D_MODEL)), resident((1, D_MODEL)), resident((1, D_MODEL)),
            pl.BlockSpec((D_MODEL, D_FF), lambda b, t: (0, 0), pipeline_mode=pl.Buffered(1)),
            pl.BlockSpec((D_MODEL, D_FF), lambda b, t: (0, 1), pipeline_mode=pl.Buffered(1)),
            resident((D_FF, D_MODEL)),
            resident((1, D_MODEL)),
        ],
        out_specs=row(D_MODEL),
        out_shape=jax.ShapeDtypeStruct((nb, rows, D_MODEL), F32),
        compiler_params=pltpu.CompilerParams(
            dimension_semantics=("arbitrary", "arbitrary"), vmem_limit_bytes=VMEM_LIMIT),
        name="outproj_ffn",
    )(o_a, o_b, x3, wo, g1, g2, wgu, wgu, wd, g3)


def kernel(x, meta_tokens, rel_bias, ln_pre_mix, ln_post_mix, ln_pre_ffn, ln_post_ffn,
           w_in, b_forget, sinks, w_out, w_gate_up, w_down):
    nb, seq, d = x.shape
    assert d == D_MODEL and seq % IN_TM == 0 and seq % FOX_T == 0 and seq % FFN_TM == 0 and seq % SWA_TQ == 0
    assert w_in.shape[0] == 1, "single-layer block"
    assert meta_tokens.shape == (N_META, D_MODEL)
    x = x.astype(F32)

    w_all = w_in[0].astype(BF16)
    pad_gate = GATE_ROWS - FOX_HEADS
    w_ft = jnp.pad(w_in[0, :, OFF_VB:], ((0, 0), (0, pad_gate))).astype(BF16)
    lane_rep = lambda v: jnp.broadcast_to(jnp.pad(v.astype(F32), (0, pad_gate))[:, None], (GATE_ROWS, LANES))
    b_ft = lane_rep(b_forget[0])
    g_pre = ln_pre_mix[0].astype(F32).reshape(1, D_MODEL)

    swa_bias = _bias_tables(rel_bias)

    x_m = jnp.pad(meta_tokens.astype(F32), ((0, META_TM - N_META), (0, 0)))[None]
    pa_m, _, ka_m, vt_m, cum_m, _, kmx_m = _inproj(x_m, g_pre, w_all, w_ft, b_ft,
                                                   jnp.zeros((GATE_ROWS, LANES), F32), META_TM)
    c0 = lane_rep(cum_m[0, :, :, N_META - 1].reshape(FOX_HEADS))
    pa, qt, ka, vt, cum, qn2, kmx = _inproj(x, g_pre, w_all, w_ft, b_ft, c0, IN_TM)

    o_a = _swa(pa, pa_m[0, :N_META], swa_bias, sinks[0].astype(F32))
    o_b = _fox(qt, ka, vt, ka_m[0, :, :N_META], vt_m[0, :, :, :N_META], cum, qn2, kmx, kmx_m[0, 0])

    return _ffn(o_a, o_b, x,
                w_out[0].astype(BF16),
                ln_post_mix[0].astype(F32).reshape(1, D_MODEL),
                ln_pre_ffn[0].astype(F32).reshape(1, D_MODEL),
                w_gate_up[0].astype(BF16),
                w_down[0].astype(BF16),
                ln_post_ffn[0].astype(F32).reshape(1, D_MODEL))
```

```python
import math

import numpy as np
import jax
import jax.numpy as jnp
from jax import lax
from jax.experimental import pallas as pl
from jax.experimental.pallas import tpu as pltpu

D_MODEL = 1024
N_META = 16
HEAD_DIM = 64
SWA_Q_HEADS = 8
SWA_KV_HEADS = 2
SWA_GROUP = SWA_Q_HEADS // SWA_KV_HEADS
FOX_HEADS = 8
SWA_Q_W = SWA_Q_HEADS * HEAD_DIM
SWA_KV_W = SWA_KV_HEADS * HEAD_DIM
FOX_W = FOX_HEADS * HEAD_DIM
OFF_QA = SWA_Q_W
OFF_KA = OFF_QA + SWA_KV_W
OFF_VA = OFF_KA + SWA_KV_W
OFF_QB = OFF_VA + FOX_W
OFF_KB = OFF_QB + FOX_W
OFF_VB = OFF_KB + FOX_W
WINDOW = 128
BLOCK = 128
N_BUCKETS = 32
MAX_DISTANCE = 128
D_FF = 2816
EPS = 1e-6
NEG_INF = -1e30
SCALE = HEAD_DIM ** -0.5
LOG2E = math.log2(math.e)

LANES = 128
PAIR_W = 2 * HEAD_DIM
VMEM_LIMIT = 56 * 1024 * 1024

SWA_KEYS = 2 * BLOCK + N_META
SWA_TQ = 512
SWA_ONES = 16
IN_TM = 512
META_TM = 128
GATE_ROWS = 16
FOX_VROWS = HEAD_DIM + 16
FOX_T = 512
FOX_GROUP = 4
FOX_SAFE_LOG2 = 40.0
FFN_TM = 512
FFN_PARTS = 2
N_PIECES = 3

F32 = jnp.float32
BF16 = jnp.bfloat16


def _dot(a, b):
    return jnp.dot(a, b, preferred_element_type=F32)


def _dot_nt(a, b):
    return lax.dot_general(a, b, (((1,), (1,)), ((), ())), preferred_element_type=F32)


def _rms(t, g):
    return t * lax.rsqrt(jnp.mean(t * t, axis=-1, keepdims=True) + EPS) * g


def _bf16_pieces(v):
    pieces = []
    for _ in range(N_PIECES - 1):
        p = v.astype(BF16)
        pieces.append(p)
        v = v - p.astype(F32)
    pieces.append(v.astype(BF16))
    return pieces


def _t5_bucket_np(dist):
    n = np.maximum(dist, 0).astype(np.int32)
    max_exact = N_BUCKETS // 2
    nf = np.maximum(n, 1).astype(np.float32)
    large = max_exact + (np.log(nf / np.float32(max_exact)) / np.float32(math.log(MAX_DISTANCE / max_exact))
                         * np.float32(N_BUCKETS - max_exact)).astype(np.int32)
    large = np.minimum(large, N_BUCKETS - 1)
    return np.where(n < max_exact, n, large).astype(np.int32)


def _bias_kernel(tab_ref, bkt_ref, valid_ref, out_ref):
    bkt = bkt_ref[...]
    valid = valid_ref[...] > 0
    for h in range(SWA_Q_HEADS):
        acc = jnp.zeros(bkt.shape, F32)
        for b in range(N_BUCKETS):
            acc = jnp.where(bkt == b, tab_ref[b, h], acc)
        out_ref[:, h * BLOCK:(h + 1) * BLOCK] = jnp.where(valid, acc * LOG2E, NEG_INF)


def _bias_tables(rel_bias):
    ki = np.arange(2 * BLOCK)[:, None]
    qi = np.arange(BLOCK)[None, :]
    d_w = qi + BLOCK - ki
    bkt_w = _t5_bucket_np(d_w)
    in_window = (d_w >= 0) & (d_w < WINDOW)
    mi = np.arange(N_META)[:, None]
    bkt_m0 = _t5_bucket_np(N_META + qi - mi)
    far = _t5_bucket_np(np.asarray([[N_META + BLOCK - (N_META - 1)]]))
    assert far[0, 0] == N_BUCKETS - 1, "meta keys of later blocks must share the last bucket"
    bkt = np.stack([np.concatenate([bkt_w, bkt_m0]), np.concatenate([bkt_w, np.full_like(bkt_m0, far[0, 0])])])
    meta_ok = np.ones((N_META, BLOCK), bool)
    valid = np.stack([np.concatenate([in_window & (ki >= BLOCK), meta_ok]), np.concatenate([in_window, meta_ok])])
    spec = pl.BlockSpec((None, SWA_KEYS, BLOCK), lambda a: (a, 0, 0))
    return pl.pallas_call(
        _bias_kernel,
        grid=(2,),
        in_specs=[pl.BlockSpec(memory_space=pltpu.SMEM), spec, spec],
        out_specs=pl.BlockSpec((None, SWA_KEYS, SWA_Q_HEADS * BLOCK), lambda a: (a, 0, 0)),
        out_shape=jax.ShapeDtypeStruct((2, SWA_KEYS, SWA_Q_HEADS * BLOCK), F32),
        name="bias_tables",
    )(rel_bias.astype(F32), jnp.asarray(bkt.astype(np.int32)), jnp.asarray(valid.astype(np.int32)))


def _fox_aug8(val_row, pieces_first):
    n = val_row.shape[1]
    row = lax.broadcasted_iota(jnp.int32, (8, n), 0)
    p0, o0 = (0, N_PIECES) if pieces_first else (N_PIECES, 0)
    aug = jnp.where((row >= o0) & (row < o0 + N_PIECES), 1.0, 0.0)
    for i, piece in enumerate(_bf16_pieces(val_row)):
        aug = jnp.where(row == p0 + i, piece.astype(F32), aug)
    return aug


def _fox_head_t(own_t, aug8, e):
    pad = jnp.zeros((HEAD_DIM - 8, own_t.shape[1]), F32)
    return jnp.concatenate([own_t, aug8, pad] if e == 0 else [aug8, pad, own_t], axis=0)


def _inproj_kernel(x_ref, g_ref, w_ref, wf_ref, bft_ref, tri_ref, c0_ref,
                   pa_ref, qt_ref, ka_ref, vt_ref, cum_ref, qn2_ref, kmx_ref, carry_ref):
    @pl.when(pl.program_id(1) == 0)
    def _():
        carry_ref[...] = c0_ref[...]

    tm = x_ref.shape[0]
    y = _rms(x_ref[...], g_ref[...]).astype(BF16)

    tile_lanes = lambda a: jnp.concatenate([a] * (tm // LANES), axis=1)
    f_t = lax.dot_general(wf_ref[...], y, (((0,), (1,)), ((), ())), preferred_element_type=F32) + tile_lanes(bft_ref[...])
    acc_fox = _dot(y, w_ref[:, OFF_VA:OFF_VB])
    ls_t = jnp.minimum(f_t, 0.0) - jnp.log1p(jnp.exp(-jnp.abs(f_t)))
    cum_t = tile_lanes(carry_ref[...])
    for piece in _bf16_pieces(ls_t):
        cum_t = cum_t + _dot(piece, tri_ref[...])
    carry_ref[...] = jnp.broadcast_to(cum_t[:, tm - 1:tm], carry_ref.shape)
    for g in range(FOX_HEADS // FOX_GROUP):
        cum_ref[g] = cum_t[g * FOX_GROUP:(g + 1) * FOX_GROUP]
    cb_t = cum_t * (-LOG2E)

    acc_swa = _dot(y, w_ref[:, 0:OFF_VA])
    pa_ref[:, 0:OFF_QA] = (acc_swa[:, 0:OFF_QA] * (SCALE * LOG2E)).astype(BF16)
    pa_ref[:, OFF_QA:OFF_VA] = acc_swa[:, OFF_QA:OFF_VA].astype(BF16)

    ones = jnp.ones((FOX_VROWS - HEAD_DIM, tm), F32)
    for p in range(FOX_HEADS // 2):
        cols = lambda off: slice(off - OFF_VA + p * PAIR_W, off - OFF_VA + (p + 1) * PAIR_W)
        q_t = (acc_fox[:, cols(OFF_VA)] * (SCALE * LOG2E)).astype(BF16).astype(F32).T
        k_t = acc_fox[:, cols(OFF_QB)].astype(BF16).astype(F32).T
        v_t = acc_fox[:, cols(OFF_KB)].astype(BF16).astype(F32).T
        qt_ref[p] = q_t.astype(BF16)
        for e in range(2):
            h = 2 * p + e
            own = slice(e * HEAD_DIM, (e + 1) * HEAD_DIM)
            qn2_ref[h // FOX_GROUP, h % FOX_GROUP:h % FOX_GROUP + 1, :] = jnp.sum(q_t[own] * q_t[own], axis=0, keepdims=True)
            kn2 = jnp.sum(k_t[own] * k_t[own], axis=0, keepdims=True)
            kmx_ref[h:h + 1, :] = jnp.broadcast_to(jnp.max(kn2, axis=1, keepdims=True), (1, LANES))
            ka_ref[h] = _fox_head_t(k_t[own], _fox_aug8(cb_t[h:h + 1], True), e).T.astype(BF16)
            vt_ref[h] = jnp.concatenate([v_t[own], ones], axis=0).astype(BF16)


def _inproj(x3, g, w, wft, bft, c0, tm):
    nb, rows, _ = x3.shape
    nt = rows // tm
    tri = jnp.asarray(np.triu(np.ones((tm, tm), np.float32)), BF16)
    const = lambda b, t: (0, 0)
    npair = FOX_HEADS // 2
    ngroup = FOX_HEADS // FOX_GROUP
    return pl.pallas_call(
        _inproj_kernel,
        grid=(nb, nt),
        in_specs=[
            pl.BlockSpec((None, tm, D_MODEL), lambda b, t: (b, t, 0)),
            pl.BlockSpec((1, D_MODEL), const),
            pl.BlockSpec(w.shape, const),
            pl.BlockSpec((D_MODEL, GATE_ROWS), const),
            pl.BlockSpec((GATE_ROWS, LANES), const),
            pl.BlockSpec((tm, tm), const),
            pl.BlockSpec((GATE_ROWS, LANES), const),
        ],
        out_specs=[
            pl.BlockSpec((None, tm, OFF_VA), lambda b, t: (b, t, 0)),
            pl.BlockSpec((None, npair, PAIR_W, tm), lambda b, t: (b, 0, 0, t)),
            pl.BlockSpec((None, FOX_HEADS, tm, PAIR_W), lambda b, t: (b, 0, t, 0)),
            pl.BlockSpec((None, FOX_HEADS, FOX_VROWS, tm), lambda b, t: (b, 0, 0, t)),
            pl.BlockSpec((None, ngroup, FOX_GROUP, tm), lambda b, t: (b, 0, 0, t)),
            pl.BlockSpec((None, ngroup, FOX_GROUP, tm), lambda b, t: (b, 0, 0, t)),
            pl.BlockSpec((None, None, FOX_HEADS, LANES), lambda b, t: (b, t, 0, 0)),
        ],
        out_shape=[
            jax.ShapeDtypeStruct((nb, rows, OFF_VA), BF16),
            jax.ShapeDtypeStruct((nb, npair, PAIR_W, rows), BF16),
            jax.ShapeDtypeStruct((nb, FOX_HEADS, rows, PAIR_W), BF16),
            jax.ShapeDtypeStruct((nb, FOX_HEADS, FOX_VROWS, rows), BF16),
            jax.ShapeDtypeStruct((nb, ngroup, FOX_GROUP, rows), F32),
            jax.ShapeDtypeStruct((nb, ngroup, FOX_GROUP, rows), F32),
            jax.ShapeDtypeStruct((nb, nt, FOX_HEADS, LANES), F32),
        ],
        scratch_shapes=[pltpu.VMEM((GATE_ROWS, LANES), F32)],
        compiler_params=pltpu.CompilerParams(
            dimension_semantics=("arbitrary", "arbitrary"), vmem_limit_bytes=VMEM_LIMIT),
        name="inproj",
    )(x3, g, w, wft, bft, tri, c0)


def _swa_scores(q, k_all, bias):
    nq = q.shape[0]
    zeros = jnp.zeros((HEAD_DIM, nq), F32)
    blocks = []
    for pair in range(SWA_Q_HEADS // 2):
        q_t = q[:, pair * PAIR_W:(pair + 1) * PAIR_W].astype(F32).T
        for hp in range(2):
            qh = q_t[hp * HEAD_DIM:(hp + 1) * HEAD_DIM]
            g = (2 * pair + hp) // SWA_GROUP
            blocks.append(jnp.concatenate([qh, zeros] if g == 0 else [zeros, qh], axis=0))
    q_bd = jnp.concatenate(blocks, axis=1).astype(BF16)
    return _dot(k_all, q_bd) + bias


def _swa_outputs(s, v_all, sink_ref):
    nq = s.shape[1] // SWA_Q_HEADS
    ps, sink_terms = [], []
    for h in range(SWA_Q_HEADS):
        sh = s[:, h * nq:(h + 1) * nq]
        sink = sink_ref[h] * LOG2E
        m = jnp.maximum(jnp.max(sh, axis=0, keepdims=True), sink)
        ps.append(jnp.exp2(sh - m).astype(BF16))
        sink_terms.append(jnp.exp2(sink - m))
    n = BLOCK
    v_t = jnp.concatenate([v_all[0:n].astype(F32).T, v_all[n:2 * n].astype(F32).T,
                           v_all[2 * n:].astype(F32).T], axis=1)
    ones = jnp.ones((SWA_ONES, v_t.shape[1]), F32)
    outs = []
    for g in range(SWA_KV_HEADS):
        v_aug = jnp.concatenate([v_t[g * HEAD_DIM:(g + 1) * HEAD_DIM], ones], axis=0).astype(BF16)
        p_g = jnp.concatenate(ps[g * SWA_GROUP:(g + 1) * SWA_GROUP], axis=1)
        o_g = _dot(v_aug, p_g)
        for j in range(SWA_GROUP):
            cols = slice(j * nq, (j + 1) * nq)
            l = o_g[HEAD_DIM:HEAD_DIM + 1, cols] + sink_terms[g * SWA_GROUP + j]
            outs.append(o_g[0:HEAD_DIM, cols] / l)
    pairs = [jnp.concatenate(outs[2 * p:2 * p + 2], axis=0).T for p in range(SWA_Q_HEADS // 2)]
    return jnp.concatenate(pairs, axis=1).astype(BF16)


def _swa_kernel(sink_ref, q_ref, kc_ref, kp_ref, vc_ref, vp_ref, km_ref, vm_ref, bias0_ref, bias1_ref, o_ref):
    n = BLOCK
    nsub = q_ref.shape[0] // n
    k_rows = [kp_ref[...]] + [kc_ref[c * n:(c + 1) * n] for c in range(nsub)]
    v_rows = [vp_ref[...]] + [vc_ref[c * n:(c + 1) * n] for c in range(nsub)]
    s = []
    for c in range(nsub):
        k_all = jnp.concatenate([k_rows[c], k_rows[c + 1], km_ref[...]], axis=0)
        bias = bias0_ref[...] if c == 0 else bias1_ref[...]
        s.append(_swa_scores(q_ref[c * n:(c + 1) * n], k_all, bias))
    for c in range(nsub):
        v_all = jnp.concatenate([v_rows[c], v_rows[c + 1], vm_ref[...]], axis=0)
        o_ref[c * n:(c + 1) * n] = _swa_outputs(s[c], v_all, sink_ref)


def _swa(proj, projm, bias, sinks):
    nb, rows, _ = proj.shape
    tq = SWA_TQ
    ratio = tq // BLOCK
    kcol, vcol = OFF_QA // SWA_KV_W, OFF_KA // SWA_KV_W
    cur = lambda c: (lambda b, n: (b, n, c))
    prev = lambda c: (lambda b, n: (b, jnp.maximum(n * ratio - 1, 0), c))
    bias_spec = lambda idx: pl.BlockSpec((None, SWA_KEYS, SWA_Q_HEADS * BLOCK), idx)
    return pl.pallas_call(
        _swa_kernel,
        grid=(nb, rows // tq),
        in_specs=[
            pl.BlockSpec(memory_space=pltpu.SMEM),
            pl.BlockSpec((None, tq, SWA_Q_W), lambda b, n: (b, n, 0)),
            pl.BlockSpec((None, tq, SWA_KV_W), cur(kcol)),
            pl.BlockSpec((None, BLOCK, SWA_KV_W), prev(kcol)),
            pl.BlockSpec((None, tq, SWA_KV_W), cur(vcol)),
            pl.BlockSpec((None, BLOCK, SWA_KV_W), prev(vcol)),
            pl.BlockSpec((N_META, SWA_KV_W), lambda b, n: (0, kcol)),
            pl.BlockSpec((N_META, SWA_KV_W), lambda b, n: (0, vcol)),
            bias_spec(lambda b, n: (jnp.minimum(n, 1), 0, 0)),
            bias_spec(lambda b, n: (1, 0, 0)),
        ],
        out_specs=pl.BlockSpec((None, tq, SWA_Q_W), lambda b, n: (b, n, 0)),
        out_shape=jax.ShapeDtypeStruct((nb, rows, SWA_Q_W), BF16),
        compiler_params=pltpu.CompilerParams(
            dimension_semantics=("arbitrary", "arbitrary"), vmem_limit_bytes=VMEM_LIMIT),
        name="swa",
    )(sinks, proj, proj, proj, proj, proj, projm, projm, bias, bias)


def _fox_kernel(qt_ref, ka_ref, vt_ref, kam_ref, vtm_ref, cq_ref, qn2_ref, kmx_ref, kmxm_ref, o_ref, acc_scr):
    i = pl.program_id(2)
    t = o_ref.shape[0]
    nh = acc_scr.shape[0]
    colmax = lambda x: jnp.max(x, axis=0, keepdims=True)
    tile_rows = lambda j: pl.ds(pl.multiple_of(j * t, t), t)

    kmax2 = jnp.maximum(jnp.max(kmx_ref[...], axis=0), kmxm_ref[...])
    r = [jnp.sqrt(qn2_ref[h:h + 1, :] * kmax2[h:h + 1, 0:1]) * 1.001 for h in range(nh)]
    worst = r[0]
    for rh in r[1:]:
        worst = jnp.maximum(worst, rh)
    safe = jnp.max(worst) <= FOX_SAFE_LOG2

    qt = []
    for h in range(nh):
        e = h % 2
        u = cq_ref[h:h + 1, :] * LOG2E - jnp.where(safe, r[h], 0.0)
        own = qt_ref[h // 2, e * HEAD_DIM:(e + 1) * HEAD_DIM, :].astype(F32)
        qt.append(_fox_head_t(own, _fox_aug8(u, False), e).astype(BF16))

    def scores(h, j, mask):
        s = _dot(ka_ref[h, tile_rows(j), :], qt[h])
        if mask is not None:
            key = lax.broadcasted_iota(jnp.int32, (t, t), 0)
            qry = lax.broadcasted_iota(jnp.int32, (t, t), 1)
            visible = (key <= qry) if mask == "diagonal" else (key + (j - i) * t <= qry)
            s = jnp.where(visible, s, NEG_INF)
        return s

    @pl.when(safe)
    def _():
        s_meta = [_dot(kam_ref[h], qt[h]) for h in range(nh)]
        for h in range(nh):
            acc_scr[h] = _dot(vtm_ref[h], jnp.exp2(s_meta[h]).astype(BF16))

        def run_tiles(tiles):
            prev = None
            for j, mask in tiles:
                for h in range(nh):
                    s = scores(h, j, mask)
                    if prev is not None:
                        ph, pj, ps = prev
                        acc_scr[ph] += _dot(vt_ref[ph, :, tile_rows(pj)], jnp.exp2(ps).astype(BF16))
                    prev = (h, j, s)
            ph, pj, ps = prev
            acc_scr[ph] += _dot(vt_ref[ph, :, tile_rows(pj)], jnp.exp2(ps).astype(BF16))

        def tile_pair(p, carry):
            run_tiles([(2 * p, None), (2 * p + 1, "causal")])
            return carry

        lax.fori_loop(0, (i + 1) // 2, tile_pair, 0)

        @pl.when(i % 2 == 0)
        def _():
            run_tiles([(i, "diagonal")])

    @pl.when(jnp.logical_not(safe))
    def _():
        m0 = []
        for h in range(nh):
            s = _dot(kam_ref[h], qt[h])
            m = colmax(s)
            acc_scr[h] = _dot(vtm_ref[h], jnp.exp2(s - m).astype(BF16))
            m0.append(m)

        def tile(j, ms, diagonal):
            out = []
            for h in range(nh):
                s = scores(h, j, "diagonal" if diagonal else None)
                m_next = jnp.maximum(ms[h], colmax(s))
                p = jnp.exp2(s - m_next).astype(BF16)
                acc_scr[h] = jnp.exp2(ms[h] - m_next) * acc_scr[h] + _dot(vt_ref[h, :, tile_rows(j)], p)
                out.append(m_next)
            return tuple(out)

        ms = lax.fori_loop(0, i, lambda j, ms: tile(j, ms, False), tuple(m0))
        tile(i, ms, True)

    for pp in range(nh // 2):
        o_t = jnp.concatenate([acc_scr[h, 0:HEAD_DIM] / acc_scr[h, HEAD_DIM:HEAD_DIM + 1]
                               for h in (2 * pp, 2 * pp + 1)], axis=0)
        o_ref[:, pp * PAIR_W:(pp + 1) * PAIR_W] = o_t.T.astype(BF16)


def _fox(qt, ka, vt, ka_m, vt_m, cum, qn2, kmx, kmx_m):
    nb, _, rows, _ = ka.shape
    t = FOX_T
    nh = FOX_GROUP
    ngroup = FOX_HEADS // nh
    nt = kmx.shape[1]
    return pl.pallas_call(
        _fox_kernel,
        grid=(nb, ngroup, rows // t),
        in_specs=[
            pl.BlockSpec((None, nh // 2, PAIR_W, t), lambda b, g, i: (b, g, 0, i)),
            pl.BlockSpec((None, nh, rows, PAIR_W), lambda b, g, i: (b, g, 0, 0)),
            pl.BlockSpec((None, nh, FOX_VROWS, rows), lambda b, g, i: (b, g, 0, 0)),
            pl.BlockSpec((nh, N_META, PAIR_W), lambda b, g, i: (g, 0, 0)),
            pl.BlockSpec((nh, FOX_VROWS, N_META), lambda b, g, i: (g, 0, 0)),
            pl.BlockSpec((None, None, nh, t), lambda b, g, i: (b, g, 0, i)),
            pl.BlockSpec((None, None, nh, t), lambda b, g, i: (b, g, 0, i)),
            pl.BlockSpec((None, nt, None, nh, LANES), lambda b, g, i: (b, 0, g, 0, 0)),
            pl.BlockSpec((None, nh, LANES), lambda b, g, i: (g, 0, 0)),
        ],
        out_specs=pl.BlockSpec((None, t, nh * HEAD_DIM), lambda b, g, i: (b, i, g)),
        out_shape=jax.ShapeDtypeStruct((nb, rows, FOX_W), BF16),
        scratch_shapes=[pltpu.VMEM((nh, FOX_VROWS, t), F32)],
        compiler_params=pltpu.CompilerParams(
            dimension_semantics=("arbitrary", "arbitrary", "arbitrary"), vmem_limit_bytes=VMEM_LIMIT),
        name="fox",
    )(qt, ka, vt, ka_m, vt_m, cum, qn2,
      kmx.reshape(nb, nt, ngroup, nh, LANES), kmx_m.reshape(ngroup, nh, LANES))


def _ffn_kernel(oa_ref, ob_ref, x_ref, wo_ref, g1_ref, g2_ref, wg_ref, wu_ref, wd_ref, g3_ref, out_ref):
    tm = x_ref.shape[0]
    parts = [slice(c * tm // FFN_PARTS, (c + 1) * tm // FFN_PARTS) for c in range(FFN_PARTS)]
    mix = [jnp.concatenate([oa_ref[r, :], ob_ref[r, :]], axis=1) for r in parts]
    a = [_dot(m, wo_ref[...]) for m in mix]
    h1 = [x_ref[r, :] + _rms(ai, g1_ref[...]) for r, ai in zip(parts, a)]
    hn = [_rms(h, g2_ref[...]).astype(BF16) for h in h1]
    gate_up = [(_dot(h, wg_ref[...]), _dot(h, wu_ref[...])) for h in hn]
    act = [(g / (1.0 + jnp.exp(-g)) * u).astype(BF16) for g, u in gate_up]
    ff = [_dot(ac, wd_ref[...]) for ac in act]
    for r, h, f in zip(parts, h1, ff):
        out_ref[r, :] = h + _rms(f, g3_ref[...])


def _ffn(o_a, o_b, x3, wo, g1, g2, wgu, wd, g3):
    nb, rows, _ = x3.shape
    tm = FFN_TM
    const = lambda b, t: (0, 0)
    resident = lambda shape: pl.BlockSpec(shape, const, pipeline_mode=pl.Buffered(1))
    row = lambda w: pl.BlockSpec((None, tm, w), lambda b, t: (b, t, 0))
    return pl.pallas_call(
        _ffn_kernel,
        grid=(nb, rows // tm),
        in_specs=[
            row(SWA_Q_W), row(FOX_W), row(D_MODEL),
            resident((D_MODEL, D_MODEL)), resident((1, D_MODEL)), resident((1, D_MODEL)),
            pl.BlockSpec((D_MODEL, D_FF), lambda b, t: (0, 0), pipeline_mode=pl.Buffered(1)),
            pl.BlockSpec((D_MODEL, D_FF), lambda b, t: (0, 1), pipeline_mode=pl.Buffered(1)),
            resident((D_FF, D_MODEL)),
            resident((1, D_MODEL)),
        ],
        out_specs=row(D_MODEL),
        out_shape=jax.ShapeDtypeStruct((nb, rows, D_MODEL), F32),
        compiler_params=pltpu.CompilerParams(
            dimension_semantics=("arbitrary", "arbitrary"), vmem_limit_bytes=VMEM_LIMIT),
        name="outproj_ffn",
    )(o_a, o_b, x3, wo, g1, g2, wgu, wgu, wd, g3)


def kernel(x, meta_tokens, rel_bias, ln_pre_mix, ln_post_mix, ln_pre_ffn, ln_post_ffn,
           w_in, b_forget, sinks, w_out, w_gate_up, w_down):
    nb, seq, d = x.shape
    assert d == D_MODEL and seq % IN_TM == 0 and seq % FOX_T == 0 and seq % FFN_TM == 0 and seq % SWA_TQ == 0
    assert w_in.shape[0] == 1, "single-layer block"
    assert meta_tokens.shape == (N_META, D_MODEL)
    x = x.astype(F32)

    w_all = w_in[0].astype(BF16)
    pad_gate = GATE_ROWS - FOX_HEADS
    w_ft = jnp.pad(w_in[0, :, OFF_VB:], ((0, 0), (0, pad_gate))).astype(BF16)
    lane_rep = lambda v: jnp.broadcast_to(jnp.pad(v.astype(F32), (0, pad_gate))[:, None], (GATE_ROWS, LANES))
    b_ft = lane_rep(b_forget[0])
    g_pre = ln_pre_mix[0].astype(F32).reshape(1, D_MODEL)

    swa_bias = _bias_tables(rel_bias)

    x_m = jnp.pad(meta_tokens.astype(F32), ((0, META_TM - N_META), (0, 0)))[None]
    pa_m, _, ka_m, vt_m, cum_m, _, kmx_m = _inproj(x_m, g_pre, w_all, w_ft, b_ft,
                                                   jnp.zeros((GATE_ROWS, LANES), F32), META_TM)
    c0 = lane_rep(cum_m[0, :, :, N_META - 1].reshape(FOX_HEADS))
    pa, qt, ka, vt, cum, qn2, kmx = _inproj(x, g_pre, w_all, w_ft, b_ft, c0, IN_TM)

    o_a = _swa(pa, pa_m[0, :N_META], swa_bias, sinks[0].astype(F32))
    o_b = _fox(qt, ka, vt, ka_m[0, :, :N_META], vt_m[0, :, :, :N_META], cum, qn2, kmx, kmx_m[0, 0])

    return _ffn(o_a, o_b, x,
                w_out[0].astype(BF16),
                ln_post_mix[0].astype(F32).reshape(1, D_MODEL),
                ln_pre_ffn[0].astype(F32).reshape(1, D_MODEL),
                w_gate_up[0].astype(BF16),
                w_down[0].astype(BF16),
                ln_post_ffn[0].astype(F32).reshape(1, D_MODEL))
```

```python
import math

import numpy as np
import jax
import jax.numpy as jnp
from jax import lax
from jax.experimental import pallas as pl
from jax.experimental.pallas import tpu as pltpu

D_MODEL = 1024
N_META = 16
HEAD_DIM = 64
SWA_Q_HEADS = 8
SWA_KV_HEADS = 2
SWA_GROUP = SWA_Q_HEADS // SWA_KV_HEADS
FOX_HEADS = 8
SWA_Q_W = SWA_Q_HEADS * HEAD_DIM
SWA_KV_W = SWA_KV_HEADS * HEAD_DIM
FOX_W = FOX_HEADS * HEAD_DIM
OFF_QA = SWA_Q_W
OFF_KA = OFF_QA + SWA_KV_W
OFF_VA = OFF_KA + SWA_KV_W
OFF_QB = OFF_VA + FOX_W
OFF_KB = OFF_QB + FOX_W
OFF_VB = OFF_KB + FOX_W
WINDOW = 128
BLOCK = 128
N_BUCKETS = 32
MAX_DISTANCE = 128
D_FF = 2816
EPS = 1e-6
NEG_INF = -1e30
SCALE = HEAD_DIM ** -0.5
LOG2E = math.log2(math.e)

LANES = 128
PAIR_W = 2 * HEAD_DIM
VMEM_LIMIT = 56 * 1024 * 1024

SWA_KEYS = 2 * BLOCK + N_META
SWA_TQ = 1024
SWA_ONES = 16
IN_TM = 512
META_TM = 128
GATE_ROWS = 16
FOX_VROWS = HEAD_DIM + 16
FOX_T = 512
FOX_GROUP = 4
FOX_SAFE_LOG2 = 40.0
FFN_TM = 512
FFN_PARTS = 2
SLAB_ALIGN = 16
N_PIECES = 3

F32 = jnp.float32
BF16 = jnp.bfloat16


def _dot(a, b):
    return jnp.dot(a, b, preferred_element_type=F32)


def _dot_nt(a, b):
    return lax.dot_general(a, b, (((1,), (1,)), ((), ())), preferred_element_type=F32)


def _rms(t, g):
    return t * lax.rsqrt(jnp.mean(t * t, axis=-1, keepdims=True) + EPS) * g


def _bf16_pieces(v):
    pieces = []
    for _ in range(N_PIECES - 1):
        p = v.astype(BF16)
        pieces.append(p)
        v = v - p.astype(F32)
    pieces.append(v.astype(BF16))
    return pieces


def _t5_bucket_np(dist):
    n = np.maximum(dist, 0).astype(np.int32)
    max_exact = N_BUCKETS // 2
    nf = np.maximum(n, 1).astype(np.float32)
    large = max_exact + (np.log(nf / np.float32(max_exact)) / np.float32(math.log(MAX_DISTANCE / max_exact))
                         * np.float32(N_BUCKETS - max_exact)).astype(np.int32)
    large = np.minimum(large, N_BUCKETS - 1)
    return np.where(n < max_exact, n, large).astype(np.int32)


def _bias_kernel(tab_ref, bkt_ref, valid_ref, out_ref):
    bkt = bkt_ref[...]
    valid = valid_ref[...] > 0
    for h in range(SWA_Q_HEADS):
        acc = jnp.zeros(bkt.shape, F32)
        for b in range(N_BUCKETS):
            acc = jnp.where(bkt == b, tab_ref[b, h], acc)
        out_ref[:, h * BLOCK:(h + 1) * BLOCK] = jnp.where(valid, acc * LOG2E, NEG_INF)


def _bias_tables(rel_bias):
    ki = np.arange(2 * BLOCK)[:, None]
    qi = np.arange(BLOCK)[None, :]
    d_w = qi + BLOCK - ki
    bkt_w = _t5_bucket_np(d_w)
    in_window = (d_w >= 0) & (d_w < WINDOW)
    mi = np.arange(N_META)[:, None]
    bkt_m0 = _t5_bucket_np(N_META + qi - mi)
    far = _t5_bucket_np(np.asarray([[N_META + BLOCK - (N_META - 1)]]))
    assert far[0, 0] == N_BUCKETS - 1, "meta keys of later blocks must share the last bucket"
    bkt = np.stack([np.concatenate([bkt_w, bkt_m0]), np.concatenate([bkt_w, np.full_like(bkt_m0, far[0, 0])])])
    meta_ok = np.ones((N_META, BLOCK), bool)
    valid = np.stack([np.concatenate([in_window & (ki >= BLOCK), meta_ok]), np.concatenate([in_window, meta_ok])])
    spec = pl.BlockSpec((None, SWA_KEYS, BLOCK), lambda a: (a, 0, 0))
    return pl.pallas_call(
        _bias_kernel,
        grid=(2,),
        in_specs=[pl.BlockSpec(memory_space=pltpu.SMEM), spec, spec],
        out_specs=pl.BlockSpec((None, SWA_KEYS, SWA_Q_HEADS * BLOCK), lambda a: (a, 0, 0)),
        out_shape=jax.ShapeDtypeStruct((2, SWA_KEYS, SWA_Q_HEADS * BLOCK), F32),
        name="bias_tables",
    )(rel_bias.astype(F32), jnp.asarray(bkt.astype(np.int32)), jnp.asarray(valid.astype(np.int32)))


def _fox_aug8(val_row, pieces_first):
    n = val_row.shape[1]
    row = lax.broadcasted_iota(jnp.int32, (8, n), 0)
    p0, o0 = (0, N_PIECES) if pieces_first else (N_PIECES, 0)
    aug = jnp.where((row >= o0) & (row < o0 + N_PIECES), 1.0, 0.0)
    for i, piece in enumerate(_bf16_pieces(val_row)):
        aug = jnp.where(row == p0 + i, piece.astype(F32), aug)
    return aug


def _fox_head_t(own_t, aug8, e):
    pad = jnp.zeros((HEAD_DIM - 8, own_t.shape[1]), F32)
    return jnp.concatenate([own_t, aug8, pad] if e == 0 else [aug8, pad, own_t], axis=0)


def _inproj_kernel(x_ref, g_ref, w_ref, wf_ref, bft_ref, tri_ref, c0_ref,
                   pa_ref, qt_ref, ka_ref, vt_ref, cum_ref, qn2_ref, kmx_ref, carry_ref):
    @pl.when(pl.program_id(1) == 0)
    def _():
        carry_ref[...] = c0_ref[...]

    tm = x_ref.shape[0]
    y = _rms(x_ref[...], g_ref[...]).astype(BF16)

    tile_lanes = lambda a: jnp.concatenate([a] * (tm // LANES), axis=1)
    f_t = lax.dot_general(wf_ref[...], y, (((0,), (1,)), ((), ())), preferred_element_type=F32) + tile_lanes(bft_ref[...])
    acc_fox = _dot(y, w_ref[:, OFF_VA:OFF_VB])
    ls_t = jnp.minimum(f_t, 0.0) - jnp.log1p(jnp.exp(-jnp.abs(f_t)))
    cum_t = tile_lanes(carry_ref[...])
    for piece in _bf16_pieces(ls_t):
        cum_t = cum_t + _dot(piece, tri_ref[...])
    carry_ref[...] = jnp.broadcast_to(cum_t[:, tm - 1:tm], carry_ref.shape)
    for g in range(FOX_HEADS // FOX_GROUP):
        cum_ref[g] = cum_t[g * FOX_GROUP:(g + 1) * FOX_GROUP]
    cb_t = cum_t * (-LOG2E)

    acc_swa = _dot(y, w_ref[:, 0:OFF_VA])
    pa_ref[:, 0:OFF_QA] = (acc_swa[:, 0:OFF_QA] * (SCALE * LOG2E)).astype(BF16)
    pa_ref[:, OFF_QA:OFF_VA] = acc_swa[:, OFF_QA:OFF_VA].astype(BF16)

    ones = jnp.ones((FOX_VROWS - HEAD_DIM, tm), F32)
    for p in range(FOX_HEADS // 2):
        cols = lambda off: slice(off - OFF_VA + p * PAIR_W, off - OFF_VA + (p + 1) * PAIR_W)
        q_t = (acc_fox[:, cols(OFF_VA)] * (SCALE * LOG2E)).astype(BF16).astype(F32).T
        k_t = acc_fox[:, cols(OFF_QB)].astype(BF16).astype(F32).T
        v_t = acc_fox[:, cols(OFF_KB)].astype(BF16).astype(F32).T
        qt_ref[p] = q_t.astype(BF16)
        for e in range(2):
            h = 2 * p + e
            own = slice(e * HEAD_DIM, (e + 1) * HEAD_DIM)
            qn2_ref[h // FOX_GROUP, h % FOX_GROUP:h % FOX_GROUP + 1, :] = jnp.sum(q_t[own] * q_t[own], axis=0, keepdims=True)
            kn2 = jnp.sum(k_t[own] * k_t[own], axis=0, keepdims=True)
            kmx_ref[h:h + 1, :] = jnp.broadcast_to(jnp.max(kn2, axis=1, keepdims=True), (1, LANES))
            ka_ref[h] = _fox_head_t(k_t[own], _fox_aug8(cb_t[h:h + 1], True), e).T.astype(BF16)
            vt_ref[h] = jnp.concatenate([v_t[own], ones], axis=0).astype(BF16)


def _inproj(x3, g, w, wft, bft, c0, tm):
    nb, rows, _ = x3.shape
    nt = rows // tm
    tri = jnp.asarray(np.triu(np.ones((tm, tm), np.float32)), BF16)
    const = lambda b, t: (0, 0)
    npair = FOX_HEADS // 2
    ngroup = FOX_HEADS // FOX_GROUP
    return pl.pallas_call(
        _inproj_kernel,
        grid=(nb, nt),
        in_specs=[
            pl.BlockSpec((None, tm, D_MODEL), lambda b, t: (b, t, 0)),
            pl.BlockSpec((1, D_MODEL), const),
            pl.BlockSpec(w.shape, const),
            pl.BlockSpec((D_MODEL, GATE_ROWS), const),
            pl.BlockSpec((GATE_ROWS, LANES), const),
            pl.BlockSpec((tm, tm), const),
            pl.BlockSpec((GATE_ROWS, LANES), const),
        ],
        out_specs=[
            pl.BlockSpec((None, tm, OFF_VA), lambda b, t: (b, t, 0)),
            pl.BlockSpec((None, npair, PAIR_W, tm), lambda b, t: (b, 0, 0, t)),
            pl.BlockSpec((None, FOX_HEADS, tm, PAIR_W), lambda b, t: (b, 0, t, 0)),
            pl.BlockSpec((None, FOX_HEADS, FOX_VROWS, tm), lambda b, t: (b, 0, 0, t)),
            pl.BlockSpec((None, ngroup, FOX_GROUP, tm), lambda b, t: (b, 0, 0, t)),
            pl.BlockSpec((None, ngroup, FOX_GROUP, tm), lambda b, t: (b, 0, 0, t)),
            pl.BlockSpec((None, None, FOX_HEADS, LANES), lambda b, t: (b, t, 0, 0)),
        ],
        out_shape=[
            jax.ShapeDtypeStruct((nb, rows, OFF_VA), BF16),
            jax.ShapeDtypeStruct((nb, npair, PAIR_W, rows), BF16),
            jax.ShapeDtypeStruct((nb, FOX_HEADS, rows, PAIR_W), BF16),
            jax.ShapeDtypeStruct((nb, FOX_HEADS, FOX_VROWS, rows), BF16),
            jax.ShapeDtypeStruct((nb, ngroup, FOX_GROUP, rows), F32),
            jax.ShapeDtypeStruct((nb, ngroup, FOX_GROUP, rows), F32),
            jax.ShapeDtypeStruct((nb, nt, FOX_HEADS, LANES), F32),
        ],
        scratch_shapes=[pltpu.VMEM((GATE_ROWS, LANES), F32)],
        compiler_params=pltpu.CompilerParams(
            dimension_semantics=("arbitrary", "arbitrary"), vmem_limit_bytes=VMEM_LIMIT),
        name="inproj",
    )(x3, g, w, wft, bft, tri, c0)


def _swa_scores(q, k_all, bias):
    nq = q.shape[0]
    zeros = jnp.zeros((HEAD_DIM, nq), F32)
    blocks = []
    for pair in range(SWA_Q_HEADS // 2):
        q_t = q[:, pair * PAIR_W:(pair + 1) * PAIR_W].astype(F32).T
        for hp in range(2):
            qh = q_t[hp * HEAD_DIM:(hp + 1) * HEAD_DIM]
            g = (2 * pair + hp) // SWA_GROUP
            blocks.append(jnp.concatenate([qh, zeros] if g == 0 else [zeros, qh], axis=0))
    q_bd = jnp.concatenate(blocks, axis=1).astype(BF16)
    return _dot(k_all, q_bd) + bias


def _swa_outputs(s, v_all, sink_ref):
    nq = s.shape[1] // SWA_Q_HEADS
    ps, sink_terms = [], []
    for h in range(SWA_Q_HEADS):
        sh = s[:, h * nq:(h + 1) * nq]
        sink = sink_ref[h] * LOG2E
        m = jnp.maximum(jnp.max(sh, axis=0, keepdims=True), sink)
        ps.append(jnp.exp2(sh - m).astype(BF16))
        sink_terms.append(jnp.exp2(sink - m))
    n = BLOCK
    v_t = jnp.concatenate([v_all[0:n].astype(F32).T, v_all[n:2 * n].astype(F32).T,
                           v_all[2 * n:].astype(F32).T], axis=1)
    ones = jnp.ones((SWA_ONES, v_t.shape[1]), F32)
    outs = []
    for g in range(SWA_KV_HEADS):
        v_aug = jnp.concatenate([v_t[g * HEAD_DIM:(g + 1) * HEAD_DIM], ones], axis=0).astype(BF16)
        p_g = jnp.concatenate(ps[g * SWA_GROUP:(g + 1) * SWA_GROUP], axis=1)
        o_g = _dot(v_aug, p_g)
        for j in range(SWA_GROUP):
            cols = slice(j * nq, (j + 1) * nq)
            l = o_g[HEAD_DIM:HEAD_DIM + 1, cols] + sink_terms[g * SWA_GROUP + j]
            outs.append(o_g[0:HEAD_DIM, cols] / l)
    pairs = [jnp.concatenate(outs[2 * p:2 * p + 2], axis=0).T for p in range(SWA_Q_HEADS // 2)]
    return jnp.concatenate(pairs, axis=1).astype(BF16)


def _swa_kernel(sink_ref, q_ref, kc_ref, kp_ref, vc_ref, vp_ref, km_ref, vm_ref, bias0_ref, bias1_ref, o_ref):
    n = BLOCK
    nsub = q_ref.shape[0] // n
    k_rows = [kp_ref[...]] + [kc_ref[c * n:(c + 1) * n] for c in range(nsub)]
    v_rows = [vp_ref[...]] + [vc_ref[c * n:(c + 1) * n] for c in range(nsub)]
    def block_scores(c):
        k_all = jnp.concatenate([k_rows[c], k_rows[c + 1], km_ref[...]], axis=0)
        bias = bias0_ref[...] if c == 0 else bias1_ref[...]
        return _swa_scores(q_ref[c * n:(c + 1) * n], k_all, bias)

    s_next = block_scores(0)
    for c in range(nsub):
        s_cur = s_next
        if c + 1 < nsub:
            s_next = block_scores(c + 1)
        v_all = jnp.concatenate([v_rows[c], v_rows[c + 1], vm_ref[...]], axis=0)
        o_ref[c * n:(c + 1) * n] = _swa_outputs(s_cur, v_all, sink_ref)


def _swa(proj, projm, bias, sinks):
    nb, rows, _ = proj.shape
    tq = SWA_TQ
    ratio = tq // BLOCK
    kcol, vcol = OFF_QA // SWA_KV_W, OFF_KA // SWA_KV_W
    cur = lambda c: (lambda b, n: (b, n, c))
    prev = lambda c: (lambda b, n: (b, jnp.maximum(n * ratio - 1, 0), c))
    bias_spec = lambda idx: pl.BlockSpec((None, SWA_KEYS, SWA_Q_HEADS * BLOCK), idx)
    return pl.pallas_call(
        _swa_kernel,
        grid=(nb, rows // tq),
        in_specs=[
            pl.BlockSpec(memory_space=pltpu.SMEM),
            pl.BlockSpec((None, tq, SWA_Q_W), lambda b, n: (b, n, 0)),
            pl.BlockSpec((None, tq, SWA_KV_W), cur(kcol)),
            pl.BlockSpec((None, BLOCK, SWA_KV_W), prev(kcol)),
            pl.BlockSpec((None, tq, SWA_KV_W), cur(vcol)),
            pl.BlockSpec((None, BLOCK, SWA_KV_W), prev(vcol)),
            pl.BlockSpec((N_META, SWA_KV_W), lambda b, n: (0, kcol)),
            pl.BlockSpec((N_META, SWA_KV_W), lambda b, n: (0, vcol)),
            bias_spec(lambda b, n: (jnp.minimum(n, 1), 0, 0)),
            bias_spec(lambda b, n: (1, 0, 0)),
        ],
        out_specs=pl.BlockSpec((None, tq, SWA_Q_W), lambda b, n: (b, n, 0)),
        out_shape=jax.ShapeDtypeStruct((nb, rows, SWA_Q_W), BF16),
        compiler_params=pltpu.CompilerParams(
            dimension_semantics=("arbitrary", "arbitrary"), vmem_limit_bytes=VMEM_LIMIT),
        name="swa",
    )(sinks, proj, proj, proj, proj, proj, projm, projm, bias, bias)


def _fox_kernel(qt_ref, ka_ref, vt_ref, kam_ref, vtm_ref, cq_ref, qn2_ref, kmx_ref, kmxm_ref,
                wo_ref, wgu_ref, wd_ref, o_ref, wo_bf_ref, wgu_bf_ref, wd_bf_ref, acc_scr):
    wo_bf_ref[...] = wo_ref[...].astype(BF16)
    wgu_bf_ref[...] = wgu_ref[...].astype(BF16)
    wd_bf_ref[...] = wd_ref[...].astype(BF16)

    i = pl.program_id(2)
    t = o_ref.shape[0]
    nh = acc_scr.shape[0]
    colmax = lambda x: jnp.max(x, axis=0, keepdims=True)
    tile_rows = lambda j: pl.ds(pl.multiple_of(j * t, t), t)

    kmax2 = jnp.maximum(jnp.max(kmx_ref[...], axis=0), kmxm_ref[...])
    r = [jnp.sqrt(qn2_ref[h:h + 1, :] * kmax2[h:h + 1, 0:1]) * 1.001 for h in range(nh)]
    worst = r[0]
    for rh in r[1:]:
        worst = jnp.maximum(worst, rh)
    safe = jnp.max(worst) <= FOX_SAFE_LOG2

    qt = []
    for h in range(nh):
        e = h % 2
        u = cq_ref[h:h + 1, :] * LOG2E - jnp.where(safe, r[h], 0.0)
        own = qt_ref[h // 2, e * HEAD_DIM:(e + 1) * HEAD_DIM, :].astype(F32)
        qt.append(_fox_head_t(own, _fox_aug8(u, False), e).astype(BF16))

    def scores(h, j, mask):
        s = _dot(ka_ref[h, tile_rows(j), :], qt[h])
        if mask is not None:
            key = lax.broadcasted_iota(jnp.int32, (t, t), 0)
            qry = lax.broadcasted_iota(jnp.int32, (t, t), 1)
            visible = (key <= qry) if mask == "diagonal" else (key + (j - i) * t <= qry)
            s = jnp.where(visible, s, NEG_INF)
        return s

    @pl.when(safe)
    def _():
        s_meta = [_dot(kam_ref[h], qt[h]) for h in range(nh)]
        for h in range(nh):
            acc_scr[h] = _dot(vtm_ref[h], jnp.exp2(s_meta[h]).astype(BF16))

        def run_tiles(tiles):
            prev = None
            for j, mask in tiles:
                for h in range(nh):
                    s = scores(h, j, mask)
                    if prev is not None:
                        ph, pj, ps = prev
                        acc_scr[ph] += _dot(vt_ref[ph, :, tile_rows(pj)], jnp.exp2(ps).astype(BF16))
                    prev = (h, j, s)
            ph, pj, ps = prev
            acc_scr[ph] += _dot(vt_ref[ph, :, tile_rows(pj)], jnp.exp2(ps).astype(BF16))

        def tile_pair(p, carry):
            run_tiles([(2 * p, None), (2 * p + 1, "causal")])
            return carry

        lax.fori_loop(0, (i + 1) // 2, tile_pair, 0)

        @pl.when(i % 2 == 0)
        def _():
            run_tiles([(i, "diagonal")])

    @pl.when(jnp.logical_not(safe))
    def _():
        m0 = []
        for h in range(nh):
            s = _dot(kam_ref[h], qt[h])
            m = colmax(s)
            acc_scr[h] = _dot(vtm_ref[h], jnp.exp2(s - m).astype(BF16))
            m0.append(m)

        def tile(j, ms, diagonal):
            out = []
            for h in range(nh):
                s = scores(h, j, "diagonal" if diagonal else None)
                m_next = jnp.maximum(ms[h], colmax(s))
                p = jnp.exp2(s - m_next).astype(BF16)
                acc_scr[h] = jnp.exp2(ms[h] - m_next) * acc_scr[h] + _dot(vt_ref[h, :, tile_rows(j)], p)
                out.append(m_next)
            return tuple(out)

        ms = lax.fori_loop(0, i, lambda j, ms: tile(j, ms, False), tuple(m0))
        tile(i, ms, True)

    for pp in range(nh // 2):
        o_t = jnp.concatenate([acc_scr[h, 0:HEAD_DIM] / acc_scr[h, HEAD_DIM:HEAD_DIM + 1]
                               for h in (2 * pp, 2 * pp + 1)], axis=0)
        o_ref[:, pp * PAIR_W:(pp + 1) * PAIR_W] = o_t.T.astype(BF16)


def _fox(qt, ka, vt, ka_m, vt_m, cum, qn2, kmx, kmx_m, w_out, w_gate_up, w_down):
    nb, _, rows, _ = ka.shape
    t = FOX_T
    nh = FOX_GROUP
    ngroup = FOX_HEADS // nh
    nt = kmx.shape[1]
    nq = rows // t
    nsteps = nb * ngroup * nq
    slab = lambda w: min(r for r in range(SLAB_ALIGN, w.shape[0] + 1, SLAB_ALIGN)
                         if w.shape[0] % r == 0 and r * nsteps >= w.shape[0])
    step = lambda b, g, i: (b * ngroup + g) * nq + i
    w_spec = lambda w: pl.BlockSpec((slab(w), w.shape[1]),
                                    lambda b, g, i: (jnp.minimum(step(b, g, i), w.shape[0] // slab(w) - 1), 0))
    return pl.pallas_call(
        _fox_kernel,
        grid=(nb, ngroup, nq),
        in_specs=[
            pl.BlockSpec((None, nh // 2, PAIR_W, t), lambda b, g, i: (b, g, 0, i)),
            pl.BlockSpec((None, nh, rows, PAIR_W), lambda b, g, i: (b, g, 0, 0)),
            pl.BlockSpec((None, nh, FOX_VROWS, rows), lambda b, g, i: (b, g, 0, 0)),
            pl.BlockSpec((nh, N_META, PAIR_W), lambda b, g, i: (g, 0, 0)),
            pl.BlockSpec((nh, FOX_VROWS, N_META), lambda b, g, i: (g, 0, 0)),
            pl.BlockSpec((None, None, nh, t), lambda b, g, i: (b, g, 0, i)),
            pl.BlockSpec((None, None, nh, t), lambda b, g, i: (b, g, 0, i)),
            pl.BlockSpec((None, nt, None, nh, LANES), lambda b, g, i: (b, 0, g, 0, 0)),
            pl.BlockSpec((None, nh, LANES), lambda b, g, i: (g, 0, 0)),
            w_spec(w_out), w_spec(w_gate_up), w_spec(w_down),
        ],
        out_specs=[pl.BlockSpec((None, t, nh * HEAD_DIM), lambda b, g, i: (b, i, g)),
                   w_spec(w_out), w_spec(w_gate_up), w_spec(w_down)],
        out_shape=[jax.ShapeDtypeStruct((nb, rows, FOX_W), BF16)]
        + [jax.ShapeDtypeStruct(w.shape, BF16) for w in (w_out, w_gate_up, w_down)],
        scratch_shapes=[pltpu.VMEM((nh, FOX_VROWS, t), F32)],
        compiler_params=pltpu.CompilerParams(
            dimension_semantics=("arbitrary", "arbitrary", "arbitrary"), vmem_limit_bytes=VMEM_LIMIT),
        name="fox",
    )(qt, ka, vt, ka_m, vt_m, cum, qn2,
      kmx.reshape(nb, nt, ngroup, nh, LANES), kmx_m.reshape(ngroup, nh, LANES), w_out, w_gate_up, w_down)


def _ffn_kernel(oa_ref, ob_ref, x_ref, wo_ref, g1_ref, g2_ref, wg_ref, wu_ref, wd_ref, g3_ref, out_ref):
    tm = x_ref.shape[0]
    parts = [slice(c * tm // FFN_PARTS, (c + 1) * tm // FFN_PARTS) for c in range(FFN_PARTS)]
    mix = [jnp.concatenate([oa_ref[r, :], ob_ref[r, :]], axis=1) for r in parts]
    a = [_dot(m, wo_ref[...]) for m in mix]
    h1 = [x_ref[r, :] + _rms(ai, g1_ref[...]) for r, ai in zip(parts, a)]
    hn = [_rms(h, g2_ref[...]).astype(BF16) for h in h1]
    gate_up = [(_dot(h, wg_ref[...]), _dot(h, wu_ref[...])) for h in hn]
    act = [(g / (1.0 + jnp.exp(-g)) * u).astype(BF16) for g, u in gate_up]
    ff = [_dot(ac, wd_ref[...]) for ac in act]
    for r, h, f in zip(parts, h1, ff):
        out_ref[r, :] = h + _rms(f, g3_ref[...])


def _ffn(o_a, o_b, x3, wo, g1, g2, wgu, wd, g3):
    nb, rows, _ = x3.shape
    tm = FFN_TM
    const = lambda b, t: (0, 0)
    resident = lambda shape: pl.BlockSpec(shape, const, pipeline_mode=pl.Buffered(1))
    row = lambda w: pl.BlockSpec((None, tm, w), lambda b, t: (b, t, 0))
    return pl.pallas_call(
        _ffn_kernel,
        grid=(nb, rows // tm),
        in_specs=[
            row(SWA_Q_W), row(FOX_W), row(D_MODEL),
            resident((D_MODEL, D_MODEL)), resident((1, D_MODEL)), resident((1, D_MODEL)),
            pl.BlockSpec((D_MODEL, D_FF), lambda b, t: (0, 0), pipeline_mode=pl.Buffered(1)),
            pl.BlockSpec((D_MODEL, D_FF), lambda b, t: (0, 1), pipeline_mode=pl.Buffered(1)),
            resident((D_FF, D_MODEL)),
            resident((1, D_MODEL)),
        ],
        out_specs=row(D_MODEL),
        out_shape=jax.ShapeDtypeStruct((nb, rows, D_MODEL), F32),
        compiler_params=pltpu.CompilerParams(
            dimension_semantics=("arbitrary", "arbitrary"), vmem_limit_bytes=VMEM_LIMIT),
        name="outproj_ffn",
    )(o_a, o_b, x3, wo, g1, g2, wgu, wgu, wd, g3)


def kernel(x, meta_tokens, rel_bias, ln_pre_mix, ln_post_mix, ln_pre_ffn, ln_post_ffn,
           w_in, b_forget, sinks, w_out, w_gate_up, w_down):
    nb, seq, d = x.shape
    assert d == D_MODEL and seq % IN_TM == 0 and seq % FOX_T == 0 and seq % FFN_TM == 0 and seq % SWA_TQ == 0
    assert w_in.shape[0] == 1, "single-layer block"
    assert meta_tokens.shape == (N_META, D_MODEL)
    x = x.astype(F32)

    w_all = w_in[0].astype(BF16)
    pad_gate = GATE_ROWS - FOX_HEADS
    w_ft = jnp.pad(w_in[0, :, OFF_VB:], ((0, 0), (0, pad_gate))).astype(BF16)
    lane_rep = lambda v: jnp.broadcast_to(jnp.pad(v.astype(F32), (0, pad_gate))[:, None], (GATE_ROWS, LANES))
    b_ft = lane_rep(b_forget[0])
    g_pre = ln_pre_mix[0].astype(F32).reshape(1, D_MODEL)

    swa_bias = _bias_tables(rel_bias)

    x_m = jnp.pad(meta_tokens.astype(F32), ((0, META_TM - N_META), (0, 0)))[None]
    pa_m, _, ka_m, vt_m, cum_m, _, kmx_m = _inproj(x_m, g_pre, w_all, w_ft, b_ft,
                                                   jnp.zeros((GATE_ROWS, LANES), F32), META_TM)
    c0 = lane_rep(cum_m[0, :, :, N_META - 1].reshape(FOX_HEADS))
    pa, qt, ka, vt, cum, qn2, kmx = _inproj(x, g_pre, w_all, w_ft, b_ft, c0, IN_TM)

    o_a = _swa(pa, pa_m[0, :N_META], swa_bias, sinks[0].astype(F32))
    o_b, w_out_bf, w_gu_bf, w_down_bf = _fox(qt, ka, vt, ka_m[0, :, :N_META], vt_m[0, :, :, :N_META], cum, qn2, kmx,
                                             kmx_m[0, 0], w_out[0].astype(F32), w_gate_up[0].astype(F32),
                                             w_down[0].astype(F32))

    return _ffn(o_a, o_b, x,
                w_out_bf,
                ln_post_mix[0].astype(F32).reshape(1, D_MODEL),
                ln_pre_ffn[0].astype(F32).reshape(1, D_MODEL),
                w_gu_bf,
                w_down_bf,
                ln_post_ffn[0].astype(F32).reshape(1, D_MODEL))
```

```python
import math

import numpy as np
import jax
import jax.numpy as jnp
from jax import lax
from jax.experimental import pallas as pl
from jax.experimental.pallas import tpu as pltpu

D_MODEL = 1024
N_META = 16
HEAD_DIM = 64
SWA_Q_HEADS = 8
SWA_KV_HEADS = 2
SWA_GROUP = SWA_Q_HEADS // SWA_KV_HEADS
FOX_HEADS = 8
SWA_Q_W = SWA_Q_HEADS * HEAD_DIM
SWA_KV_W = SWA_KV_HEADS * HEAD_DIM
FOX_W = FOX_HEADS * HEAD_DIM
OFF_QA = SWA_Q_W
OFF_KA = OFF_QA + SWA_KV_W
OFF_VA = OFF_KA + SWA_KV_W
OFF_QB = OFF_VA + FOX_W
OFF_KB = OFF_QB + FOX_W
OFF_VB = OFF_KB + FOX_W
WINDOW = 128
BLOCK = 128
N_BUCKETS = 32
MAX_DISTANCE = 128
D_FF = 2816
EPS = 1e-6
NEG_INF = -1e30
SCALE = HEAD_DIM ** -0.5
LOG2E = math.log2(math.e)

LANES = 128
PAIR_W = 2 * HEAD_DIM
VMEM_LIMIT = 56 * 1024 * 1024

SWA_KEYS = 2 * BLOCK + N_META
SWA_TQ = 1024
SWA_ONES = 16
IN_TM = 1024
META_TM = 128
GATE_ROWS = 16
FOX_VROWS = HEAD_DIM + 16
FOX_T = 512
FOX_GROUP = 4
FOX_SAFE_LOG2 = 40.0
FFN_TM = 1024
FFN_PARTS = 4
SLAB_ALIGN = 16
N_PIECES = 3

F32 = jnp.float32
BF16 = jnp.bfloat16


def _dot(a, b):
    return jnp.dot(a, b, preferred_element_type=F32)


def _dot_nt(a, b):
    return lax.dot_general(a, b, (((1,), (1,)), ((), ())), preferred_element_type=F32)


def _rms(t, g):
    return t * lax.rsqrt(jnp.mean(t * t, axis=-1, keepdims=True) + EPS) * g


def _bf16_pieces(v):
    pieces = []
    for _ in range(N_PIECES - 1):
        p = v.astype(BF16)
        pieces.append(p)
        v = v - p.astype(F32)
    pieces.append(v.astype(BF16))
    return pieces


def _t5_bucket_np(dist):
    n = np.maximum(dist, 0).astype(np.int32)
    max_exact = N_BUCKETS // 2
    nf = np.maximum(n, 1).astype(np.float32)
    large = max_exact + (np.log(nf / np.float32(max_exact)) / np.float32(math.log(MAX_DISTANCE / max_exact))
                         * np.float32(N_BUCKETS - max_exact)).astype(np.int32)
    large = np.minimum(large, N_BUCKETS - 1)
    return np.where(n < max_exact, n, large).astype(np.int32)


def _bias_kernel(tab_ref, bkt_ref, valid_ref, out_ref):
    bkt = bkt_ref[...]
    valid = valid_ref[...] > 0
    for h in range(SWA_Q_HEADS):
        acc = jnp.zeros(bkt.shape, F32)
        for b in range(N_BUCKETS):
            acc = jnp.where(bkt == b, tab_ref[b, h], acc)
        out_ref[:, h * BLOCK:(h + 1) * BLOCK] = jnp.where(valid, acc * LOG2E, NEG_INF)


def _bias_tables(rel_bias):
    ki = np.arange(2 * BLOCK)[:, None]
    qi = np.arange(BLOCK)[None, :]
    d_w = qi + BLOCK - ki
    bkt_w = _t5_bucket_np(d_w)
    in_window = (d_w >= 0) & (d_w < WINDOW)
    mi = np.arange(N_META)[:, None]
    bkt_m0 = _t5_bucket_np(N_META + qi - mi)
    far = _t5_bucket_np(np.asarray([[N_META + BLOCK - (N_META - 1)]]))
    assert far[0, 0] == N_BUCKETS - 1, "meta keys of later blocks must share the last bucket"
    bkt = np.stack([np.concatenate([bkt_w, bkt_m0]), np.concatenate([bkt_w, np.full_like(bkt_m0, far[0, 0])])])
    meta_ok = np.ones((N_META, BLOCK), bool)
    valid = np.stack([np.concatenate([in_window & (ki >= BLOCK), meta_ok]), np.concatenate([in_window, meta_ok])])
    spec = pl.BlockSpec((None, SWA_KEYS, BLOCK), lambda a: (a, 0, 0))
    return pl.pallas_call(
        _bias_kernel,
        grid=(2,),
        in_specs=[pl.BlockSpec(memory_space=pltpu.SMEM), spec, spec],
        out_specs=pl.BlockSpec((None, SWA_KEYS, SWA_Q_HEADS * BLOCK), lambda a: (a, 0, 0)),
        out_shape=jax.ShapeDtypeStruct((2, SWA_KEYS, SWA_Q_HEADS * BLOCK), F32),
        name="bias_tables",
    )(rel_bias.astype(F32), jnp.asarray(bkt.astype(np.int32)), jnp.asarray(valid.astype(np.int32)))


def _fox_aug8(val_row, pieces_first):
    n = val_row.shape[1]
    row = lax.broadcasted_iota(jnp.int32, (8, n), 0)
    p0, o0 = (0, N_PIECES) if pieces_first else (N_PIECES, 0)
    aug = jnp.where((row >= o0) & (row < o0 + N_PIECES), 1.0, 0.0)
    for i, piece in enumerate(_bf16_pieces(val_row)):
        aug = jnp.where(row == p0 + i, piece.astype(F32), aug)
    return aug


def _fox_head_t(own_t, aug8, e):
    pad = jnp.zeros((HEAD_DIM - 8, own_t.shape[1]), F32)
    return jnp.concatenate([own_t, aug8, pad] if e == 0 else [aug8, pad, own_t], axis=0)


def _inproj_kernel(x_ref, g_ref, w_ref, wf_ref, bft_ref, tri_ref, c0_ref,
                   pa_ref, qt_ref, ka_ref, vt_ref, cum_ref, qn2_ref, kmx_ref, carry_ref):
    @pl.when(pl.program_id(1) == 0)
    def _():
        carry_ref[...] = c0_ref[...]

    tm = x_ref.shape[0]
    y = _rms(x_ref[...], g_ref[...]).astype(BF16)

    tile_lanes = lambda a: jnp.concatenate([a] * (tm // LANES), axis=1)
    f_t = lax.dot_general(wf_ref[...], y, (((0,), (1,)), ((), ())), preferred_element_type=F32) + tile_lanes(bft_ref[...])
    acc_fox = _dot(y, w_ref[:, OFF_VA:OFF_VB])
    ls_t = jnp.minimum(f_t, 0.0) - jnp.log1p(jnp.exp(-jnp.abs(f_t)))
    cum_t = tile_lanes(carry_ref[...])
    for piece in _bf16_pieces(ls_t):
        cum_t = cum_t + _dot(piece, tri_ref[...])
    carry_ref[...] = jnp.broadcast_to(cum_t[:, tm - 1:tm], carry_ref.shape)
    for g in range(FOX_HEADS // FOX_GROUP):
        cum_ref[g] = cum_t[g * FOX_GROUP:(g + 1) * FOX_GROUP]
    cb_t = cum_t * (-LOG2E)

    acc_swa = _dot(y, w_ref[:, 0:OFF_VA])
    pa_ref[:, 0:OFF_QA] = (acc_swa[:, 0:OFF_QA] * (SCALE * LOG2E)).astype(BF16)
    pa_ref[:, OFF_QA:OFF_VA] = acc_swa[:, OFF_QA:OFF_VA].astype(BF16)

    ones = jnp.ones((FOX_VROWS - HEAD_DIM, tm), F32)
    for p in range(FOX_HEADS // 2):
        cols = lambda off: slice(off - OFF_VA + p * PAIR_W, off - OFF_VA + (p + 1) * PAIR_W)
        q_t = (acc_fox[:, cols(OFF_VA)] * (SCALE * LOG2E)).astype(BF16).astype(F32).T
        k_t = acc_fox[:, cols(OFF_QB)].astype(BF16).astype(F32).T
        v_t = acc_fox[:, cols(OFF_KB)].astype(BF16).astype(F32).T
        qt_ref[p] = q_t.astype(BF16)
        for e in range(2):
            h = 2 * p + e
            own = slice(e * HEAD_DIM, (e + 1) * HEAD_DIM)
            qn2_ref[h // FOX_GROUP, h % FOX_GROUP:h % FOX_GROUP + 1, :] = jnp.sum(q_t[own] * q_t[own], axis=0, keepdims=True)
            kn2 = jnp.sum(k_t[own] * k_t[own], axis=0, keepdims=True)
            kmx_ref[h:h + 1, :] = jnp.broadcast_to(jnp.max(kn2, axis=1, keepdims=True), (1, LANES))
            ka_ref[h] = _fox_head_t(k_t[own], _fox_aug8(cb_t[h:h + 1], True), e).T.astype(BF16)
            vt_ref[h] = jnp.concatenate([v_t[own], ones], axis=0).astype(BF16)


def _inproj(x3, g, w, wft, bft, c0, tm):
    nb, rows, _ = x3.shape
    nt = rows // tm
    tri = jnp.asarray(np.triu(np.ones((tm, tm), np.float32)), BF16)
    const = lambda b, t: (0, 0)
    npair = FOX_HEADS // 2
    ngroup = FOX_HEADS // FOX_GROUP
    return pl.pallas_call(
        _inproj_kernel,
        grid=(nb, nt),
        in_specs=[
            pl.BlockSpec((None, tm, D_MODEL), lambda b, t: (b, t, 0)),
            pl.BlockSpec((1, D_MODEL), const),
            pl.BlockSpec(w.shape, const),
            pl.BlockSpec((D_MODEL, GATE_ROWS), const),
            pl.BlockSpec((GATE_ROWS, LANES), const),
            pl.BlockSpec((tm, tm), const),
            pl.BlockSpec((GATE_ROWS, LANES), const),
        ],
        out_specs=[
            pl.BlockSpec((None, tm, OFF_VA), lambda b, t: (b, t, 0)),
            pl.BlockSpec((None, npair, PAIR_W, tm), lambda b, t: (b, 0, 0, t)),
            pl.BlockSpec((None, FOX_HEADS, tm, PAIR_W), lambda b, t: (b, 0, t, 0)),
            pl.BlockSpec((None, FOX_HEADS, FOX_VROWS, tm), lambda b, t: (b, 0, 0, t)),
            pl.BlockSpec((None, ngroup, FOX_GROUP, tm), lambda b, t: (b, 0, 0, t)),
            pl.BlockSpec((None, ngroup, FOX_GROUP, tm), lambda b, t: (b, 0, 0, t)),
            pl.BlockSpec((None, None, FOX_HEADS, LANES), lambda b, t: (b, t, 0, 0)),
        ],
        out_shape=[
            jax.ShapeDtypeStruct((nb, rows, OFF_VA), BF16),
            jax.ShapeDtypeStruct((nb, npair, PAIR_W, rows), BF16),
            jax.ShapeDtypeStruct((nb, FOX_HEADS, rows, PAIR_W), BF16),
            jax.ShapeDtypeStruct((nb, FOX_HEADS, FOX_VROWS, rows), BF16),
            jax.ShapeDtypeStruct((nb, ngroup, FOX_GROUP, rows), F32),
            jax.ShapeDtypeStruct((nb, ngroup, FOX_GROUP, rows), F32),
            jax.ShapeDtypeStruct((nb, nt, FOX_HEADS, LANES), F32),
        ],
        scratch_shapes=[pltpu.VMEM((GATE_ROWS, LANES), F32)],
        compiler_params=pltpu.CompilerParams(
            dimension_semantics=("arbitrary", "arbitrary"), vmem_limit_bytes=VMEM_LIMIT),
        name="inproj",
    )(x3, g, w, wft, bft, tri, c0)


def _swa_scores(q, k_all, bias):
    nq = q.shape[0]
    zeros = jnp.zeros((HEAD_DIM, nq), F32)
    blocks = []
    for pair in range(SWA_Q_HEADS // 2):
        q_t = q[:, pair * PAIR_W:(pair + 1) * PAIR_W].astype(F32).T
        for hp in range(2):
            qh = q_t[hp * HEAD_DIM:(hp + 1) * HEAD_DIM]
            g = (2 * pair + hp) // SWA_GROUP
            blocks.append(jnp.concatenate([qh, zeros] if g == 0 else [zeros, qh], axis=0))
    q_bd = jnp.concatenate(blocks, axis=1).astype(BF16)
    return _dot(k_all, q_bd) + bias


def _swa_outputs(s, v_all, sink_ref):
    nq = s.shape[1] // SWA_Q_HEADS
    ps, sink_terms = [], []
    for h in range(SWA_Q_HEADS):
        sh = s[:, h * nq:(h + 1) * nq]
        sink = sink_ref[h] * LOG2E
        m = jnp.maximum(jnp.max(sh, axis=0, keepdims=True), sink)
        ps.append(jnp.exp2(sh - m).astype(BF16))
        sink_terms.append(jnp.exp2(sink - m))
    n = BLOCK
    v_t = jnp.concatenate([v_all[0:n].astype(F32).T, v_all[n:2 * n].astype(F32).T,
                           v_all[2 * n:].astype(F32).T], axis=1)
    ones = jnp.ones((SWA_ONES, v_t.shape[1]), F32)
    outs = []
    for g in range(SWA_KV_HEADS):
        v_aug = jnp.concatenate([v_t[g * HEAD_DIM:(g + 1) * HEAD_DIM], ones], axis=0).astype(BF16)
        p_g = jnp.concatenate(ps[g * SWA_GROUP:(g + 1) * SWA_GROUP], axis=1)
        o_g = _dot(v_aug, p_g)
        for j in range(SWA_GROUP):
            cols = slice(j * nq, (j + 1) * nq)
            l = o_g[HEAD_DIM:HEAD_DIM + 1, cols] + sink_terms[g * SWA_GROUP + j]
            outs.append(o_g[0:HEAD_DIM, cols] / l)
    pairs = [jnp.concatenate(outs[2 * p:2 * p + 2], axis=0).T for p in range(SWA_Q_HEADS // 2)]
    return jnp.concatenate(pairs, axis=1).astype(BF16)


def _swa_kernel(sink_ref, q_ref, kc_ref, kp_ref, vc_ref, vp_ref, km_ref, vm_ref, bias0_ref, bias1_ref, o_ref):
    n = BLOCK
    nsub = q_ref.shape[0] // n
    k_rows = [kp_ref[...]] + [kc_ref[c * n:(c + 1) * n] for c in range(nsub)]
    v_rows = [vp_ref[...]] + [vc_ref[c * n:(c + 1) * n] for c in range(nsub)]
    def block_scores(c):
        k_all = jnp.concatenate([k_rows[c], k_rows[c + 1], km_ref[...]], axis=0)
        bias = bias0_ref[...] if c == 0 else bias1_ref[...]
        return _swa_scores(q_ref[c * n:(c + 1) * n], k_all, bias)

    s_next = block_scores(0)
    for c in range(nsub):
        s_cur = s_next
        if c + 1 < nsub:
            s_next = block_scores(c + 1)
        v_all = jnp.concatenate([v_rows[c], v_rows[c + 1], vm_ref[...]], axis=0)
        o_ref[c * n:(c + 1) * n] = _swa_outputs(s_cur, v_all, sink_ref)


def _swa(proj, projm, bias, sinks):
    nb, rows, _ = proj.shape
    tq = SWA_TQ
    ratio = tq // BLOCK
    kcol, vcol = OFF_QA // SWA_KV_W, OFF_KA // SWA_KV_W
    cur = lambda c: (lambda b, n: (b, n, c))
    prev = lambda c: (lambda b, n: (b, jnp.maximum(n * ratio - 1, 0), c))
    bias_spec = lambda idx: pl.BlockSpec((None, SWA_KEYS, SWA_Q_HEADS * BLOCK), idx)
    return pl.pallas_call(
        _swa_kernel,
        grid=(nb, rows // tq),
        in_specs=[
            pl.BlockSpec(memory_space=pltpu.SMEM),
            pl.BlockSpec((None, tq, SWA_Q_W), lambda b, n: (b, n, 0)),
            pl.BlockSpec((None, tq, SWA_KV_W), cur(kcol)),
            pl.BlockSpec((None, BLOCK, SWA_KV_W), prev(kcol)),
            pl.BlockSpec((None, tq, SWA_KV_W), cur(vcol)),
            pl.BlockSpec((None, BLOCK, SWA_KV_W), prev(vcol)),
            pl.BlockSpec((N_META, SWA_KV_W), lambda b, n: (0, kcol)),
            pl.BlockSpec((N_META, SWA_KV_W), lambda b, n: (0, vcol)),
            bias_spec(lambda b, n: (jnp.minimum(n, 1), 0, 0)),
            bias_spec(lambda b, n: (1, 0, 0)),
        ],
        out_specs=pl.BlockSpec((None, tq, SWA_Q_W), lambda b, n: (b, n, 0)),
        out_shape=jax.ShapeDtypeStruct((nb, rows, SWA_Q_W), BF16),
        compiler_params=pltpu.CompilerParams(
            dimension_semantics=("arbitrary", "arbitrary"), vmem_limit_bytes=VMEM_LIMIT),
        name="swa",
    )(sinks, proj, proj, proj, proj, proj, projm, projm, bias, bias)


def _fox_kernel(qt_ref, ka_ref, vt_ref, kam_ref, vtm_ref, cq_ref, qn2_ref, kmx_ref, kmxm_ref,
                wo_ref, wgu_ref, wd_ref, o_ref, wo_bf_ref, wgu_bf_ref, wd_bf_ref, acc_scr):
    wo_bf_ref[...] = wo_ref[...].astype(BF16)
    wgu_bf_ref[...] = wgu_ref[...].astype(BF16)
    wd_bf_ref[...] = wd_ref[...].astype(BF16)

    i = pl.program_id(2)
    t = o_ref.shape[0]
    nh = acc_scr.shape[0]
    colmax = lambda x: jnp.max(x, axis=0, keepdims=True)
    tile_rows = lambda j: pl.ds(pl.multiple_of(j * t, t), t)

    kmax2 = jnp.maximum(jnp.max(kmx_ref[...], axis=0), kmxm_ref[...])
    r = [jnp.sqrt(qn2_ref[h:h + 1, :] * kmax2[h:h + 1, 0:1]) * 1.001 for h in range(nh)]
    worst = r[0]
    for rh in r[1:]:
        worst = jnp.maximum(worst, rh)
    safe = jnp.max(worst) <= FOX_SAFE_LOG2

    qt = []
    for h in range(nh):
        e = h % 2
        u = cq_ref[h:h + 1, :] * LOG2E - jnp.where(safe, r[h], 0.0)
        own = qt_ref[h // 2, e * HEAD_DIM:(e + 1) * HEAD_DIM, :].astype(F32)
        qt.append(_fox_head_t(own, _fox_aug8(u, False), e).astype(BF16))

    def scores(h, j, mask):
        s = _dot(ka_ref[h, tile_rows(j), :], qt[h])
        if mask is not None:
            key = lax.broadcasted_iota(jnp.int32, (t, t), 0)
            qry = lax.broadcasted_iota(jnp.int32, (t, t), 1)
            visible = (key <= qry) if mask == "diagonal" else (key + (j - i) * t <= qry)
            s = jnp.where(visible, s, NEG_INF)
        return s

    @pl.when(safe)
    def _():
        s_meta = [_dot(kam_ref[h], qt[h]) for h in range(nh)]
        for h in range(nh):
            acc_scr[h] = _dot(vtm_ref[h], jnp.exp2(s_meta[h]).astype(BF16))

        def run_tiles(tiles):
            prev = None
            for j, mask in tiles:
                for h in range(nh):
                    s = scores(h, j, mask)
                    if prev is not None:
                        ph, pj, ps = prev
                        acc_scr[ph] += _dot(vt_ref[ph, :, tile_rows(pj)], jnp.exp2(ps).astype(BF16))
                    prev = (h, j, s)
            ph, pj, ps = prev
            acc_scr[ph] += _dot(vt_ref[ph, :, tile_rows(pj)], jnp.exp2(ps).astype(BF16))

        def tile_pair(p, carry):
            run_tiles([(2 * p, None), (2 * p + 1, "causal")])
            return carry

        lax.fori_loop(0, (i + 1) // 2, tile_pair, 0)

        @pl.when(i % 2 == 0)
        def _():
            run_tiles([(i, "diagonal")])

    @pl.when(jnp.logical_not(safe))
    def _():
        m0 = []
        for h in range(nh):
            s = _dot(kam_ref[h], qt[h])
            m = colmax(s)
            acc_scr[h] = _dot(vtm_ref[h], jnp.exp2(s - m).astype(BF16))
            m0.append(m)

        def tile(j, ms, diagonal):
            out = []
            for h in range(nh):
                s = scores(h, j, "diagonal" if diagonal else None)
                m_next = jnp.maximum(ms[h], colmax(s))
                p = jnp.exp2(s - m_next).astype(BF16)
                acc_scr[h] = jnp.exp2(ms[h] - m_next) * acc_scr[h] + _dot(vt_ref[h, :, tile_rows(j)], p)
                out.append(m_next)
            return tuple(out)

        ms = lax.fori_loop(0, i, lambda j, ms: tile(j, ms, False), tuple(m0))
        tile(i, ms, True)

    for pp in range(nh // 2):
        o_t = jnp.concatenate([acc_scr[h, 0:HEAD_DIM] / acc_scr[h, HEAD_DIM:HEAD_DIM + 1]
                               for h in (2 * pp, 2 * pp + 1)], axis=0)
        o_ref[:, pp * PAIR_W:(pp + 1) * PAIR_W] = o_t.T.astype(BF16)


def _fox(qt, ka, vt, ka_m, vt_m, cum, qn2, kmx, kmx_m, w_out, w_gate_up, w_down):
    nb, _, rows, _ = ka.shape
    t = FOX_T
    nh = FOX_GROUP
    ngroup = FOX_HEADS // nh
    nt = kmx.shape[1]
    nq = rows // t
    nsteps = nb * ngroup * nq
    slab = lambda w: min(r for r in range(SLAB_ALIGN, w.shape[0] + 1, SLAB_ALIGN)
                         if w.shape[0] % r == 0 and r * nsteps >= w.shape[0])
    step = lambda b, g, i: (b * ngroup + g) * nq + i
    w_spec = lambda w: pl.BlockSpec((slab(w), w.shape[1]),
                                    lambda b, g, i: (jnp.minimum(step(b, g, i), w.shape[0] // slab(w) - 1), 0))
    return pl.pallas_call(
        _fox_kernel,
        grid=(nb, ngroup, nq),
        in_specs=[
            pl.BlockSpec((None, nh // 2, PAIR_W, t), lambda b, g, i: (b, g, 0, i)),
            pl.BlockSpec((None, nh, rows, PAIR_W), lambda b, g, i: (b, g, 0, 0)),
            pl.BlockSpec((None, nh, FOX_VROWS, rows), lambda b, g, i: (b, g, 0, 0)),
            pl.BlockSpec((nh, N_META, PAIR_W), lambda b, g, i: (g, 0, 0)),
            pl.BlockSpec((nh, FOX_VROWS, N_META), lambda b, g, i: (g, 0, 0)),
            pl.BlockSpec((None, None, nh, t), lambda b, g, i: (b, g, 0, i)),
            pl.BlockSpec((None, None, nh, t), lambda b, g, i: (b, g, 0, i)),
            pl.BlockSpec((None, nt, None, nh, LANES), lambda b, g, i: (b, 0, g, 0, 0)),
            pl.BlockSpec((None, nh, LANES), lambda b, g, i: (g, 0, 0)),
            w_spec(w_out), w_spec(w_gate_up), w_spec(w_down),
        ],
        out_specs=[pl.BlockSpec((None, t, nh * HEAD_DIM), lambda b, g, i: (b, i, g)),
                   w_spec(w_out), w_spec(w_gate_up), w_spec(w_down)],
        out_shape=[jax.ShapeDtypeStruct((nb, rows, FOX_W), BF16)]
        + [jax.ShapeDtypeStruct(w.shape, BF16) for w in (w_out, w_gate_up, w_down)],
        scratch_shapes=[pltpu.VMEM((nh, FOX_VROWS, t), F32)],
        compiler_params=pltpu.CompilerParams(
            dimension_semantics=("arbitrary", "arbitrary", "arbitrary"), vmem_limit_bytes=VMEM_LIMIT),
        name="fox",
    )(qt, ka, vt, ka_m, vt_m, cum, qn2,
      kmx.reshape(nb, nt, ngroup, nh, LANES), kmx_m.reshape(ngroup, nh, LANES), w_out, w_gate_up, w_down)


def _ffn_kernel(oa_ref, ob_ref, x_ref, wo_ref, g1_ref, g2_ref, wg_ref, wu_ref, wd_ref, g3_ref, out_ref):
    tm = x_ref.shape[0]
    parts = [slice(c * tm // FFN_PARTS, (c + 1) * tm // FFN_PARTS) for c in range(FFN_PARTS)]
    mix = [jnp.concatenate([oa_ref[r, :], ob_ref[r, :]], axis=1) for r in parts]
    a = [_dot(m, wo_ref[...]) for m in mix]
    h1 = [x_ref[r, :] + _rms(ai, g1_ref[...]) for r, ai in zip(parts, a)]
    hn = [_rms(h, g2_ref[...]).astype(BF16) for h in h1]
    gate_up = [(_dot(h, wg_ref[...]), _dot(h, wu_ref[...])) for h in hn]
    act = [(g / (1.0 + jnp.exp(-g)) * u).astype(BF16) for g, u in gate_up]
    ff = [_dot(ac, wd_ref[...]) for ac in act]
    for r, h, f in zip(parts, h1, ff):
        out_ref[r, :] = h + _rms(f, g3_ref[...])


def _ffn(o_a, o_b, x3, wo, g1, g2, wgu, wd, g3):
    nb, rows, _ = x3.shape
    tm = FFN_TM
    const = lambda b, t: (0, 0)
    resident = lambda shape: pl.BlockSpec(shape, const, pipeline_mode=pl.Buffered(1))
    row = lambda w: pl.BlockSpec((None, tm, w), lambda b, t: (b, t, 0))
    return pl.pallas_call(
        _ffn_kernel,
        grid=(nb, rows // tm),
        in_specs=[
            row(SWA_Q_W), row(FOX_W), row(D_MODEL),
            resident((D_MODEL, D_MODEL)), resident((1, D_MODEL)), resident((1, D_MODEL)),
            pl.BlockSpec((D_MODEL, D_FF), lambda b, t: (0, 0), pipeline_mode=pl.Buffered(1)),
            pl.BlockSpec((D_MODEL, D_FF), lambda b, t: (0, 1), pipeline_mode=pl.Buffered(1)),
            resident((D_FF, D_MODEL)),
            resident((1, D_MODEL)),
        ],
        out_specs=row(D_MODEL),
        out_shape=jax.ShapeDtypeStruct((nb, rows, D_MODEL), F32),
        compiler_params=pltpu.CompilerParams(
            dimension_semantics=("arbitrary", "arbitrary"), vmem_limit_bytes=VMEM_LIMIT),
        name="outproj_ffn",
    )(o_a, o_b, x3, wo, g1, g2, wgu, wgu, wd, g3)


def kernel(x, meta_tokens, rel_bias, ln_pre_mix, ln_post_mix, ln_pre_ffn, ln_post_ffn,
           w_in, b_forget, sinks, w_out, w_gate_up, w_down):
    nb, seq, d = x.shape
    assert d == D_MODEL and seq % IN_TM == 0 and seq % FOX_T == 0 and seq % FFN_TM == 0 and seq % SWA_TQ == 0
    assert w_in.shape[0] == 1, "single-layer block"
    assert meta_tokens.shape == (N_META, D_MODEL)
    x = x.astype(F32)

    w_all = w_in[0].astype(BF16)
    pad_gate = GATE_ROWS - FOX_HEADS
    w_ft = jnp.pad(w_in[0, :, OFF_VB:], ((0, 0), (0, pad_gate))).astype(BF16)
    lane_rep = lambda v: jnp.broadcast_to(jnp.pad(v.astype(F32), (0, pad_gate))[:, None], (GATE_ROWS, LANES))
    b_ft = lane_rep(b_forget[0])
    g_pre = ln_pre_mix[0].astype(F32).reshape(1, D_MODEL)

    swa_bias = _bias_tables(rel_bias)

    x_m = jnp.pad(meta_tokens.astype(F32), ((0, META_TM - N_META), (0, 0)))[None]
    pa_m, _, ka_m, vt_m, cum_m, _, kmx_m = _inproj(x_m, g_pre, w_all, w_ft, b_ft,
                                                   jnp.zeros((GATE_ROWS, LANES), F32), META_TM)
    c0 = lane_rep(cum_m[0, :, :, N_META - 1].reshape(FOX_HEADS))
    pa, qt, ka, vt, cum, qn2, kmx = _inproj(x, g_pre, w_all, w_ft, b_ft, c0, IN_TM)

    o_a = _swa(pa, pa_m[0, :N_META], swa_bias, sinks[0].astype(F32))
    o_b, w_out_bf, w_gu_bf, w_down_bf = _fox(qt, ka, vt, ka_m[0, :, :N_META], vt_m[0, :, :, :N_META], cum, qn2, kmx,
                                             kmx_m[0, 0], w_out[0].astype(F32), w_gate_up[0].astype(F32),
                                             w_down[0].astype(F32))

    return _ffn(o_a, o_b, x,
                w_out_bf,
                ln_post_mix[0].astype(F32).reshape(1, D_MODEL),
                ln_pre_ffn[0].astype(F32).reshape(1, D_MODEL),
                w_gu_bf,
                w_down_bf,
                ln_post_ffn[0].astype(F32).reshape(1, D_MODEL))
```

```python
import math

import numpy as np
import jax
import jax.numpy as jnp
from jax import lax
from jax.experimental import pallas as pl
from jax.experimental.pallas import tpu as pltpu

D_MODEL = 1024
N_META = 16
HEAD_DIM = 64
SWA_Q_HEADS = 8
SWA_KV_HEADS = 2
SWA_GROUP = SWA_Q_HEADS // SWA_KV_HEADS
FOX_HEADS = 8
SWA_Q_W = SWA_Q_HEADS * HEAD_DIM
SWA_KV_W = SWA_KV_HEADS * HEAD_DIM
FOX_W = FOX_HEADS * HEAD_DIM
OFF_QA = SWA_Q_W
OFF_KA = OFF_QA + SWA_KV_W
OFF_VA = OFF_KA + SWA_KV_W
OFF_QB = OFF_VA + FOX_W
OFF_KB = OFF_QB + FOX_W
OFF_VB = OFF_KB + FOX_W
WINDOW = 128
BLOCK = 128
N_BUCKETS = 32
MAX_DISTANCE = 128
D_FF = 2816
EPS = 1e-6
NEG_INF = -1e30
SCALE = HEAD_DIM ** -0.5
LOG2E = math.log2(math.e)

LANES = 128
SUBLANES = 8
PAIR_W = 2 * HEAD_DIM
VMEM_LIMIT = 56 * 1024 * 1024

SWA_KEYS = 2 * BLOCK + N_META
SWA_TQ = 2048
SWA_ONES = 16
IN_TM = 512
META_TM = 128
GATE_ROWS = 16
FOX_VROWS = HEAD_DIM + 16
FOX_T = 512
FOX_GROUP = 4
FOX_SAFE_LOG2 = 40.0
FOX_BOUND_MARGIN = 1.001
FFN_TM = 512
FFN_PARTS = 2
SLAB_ALIGN = 16
N_PIECES = 3

F32 = jnp.float32
BF16 = jnp.bfloat16


def _dot(a, b):
    return jnp.dot(a, b, preferred_element_type=F32)


def _dot_nt(a, b):
    return lax.dot_general(a, b, (((1,), (1,)), ((), ())), preferred_element_type=F32)


def _rms(t, g):
    return t * lax.rsqrt(jnp.mean(t * t, axis=-1, keepdims=True) + EPS) * g


def _bf16_pieces(v):
    pieces = []
    for _ in range(N_PIECES - 1):
        p = v.astype(BF16)
        pieces.append(p)
        v = v - p.astype(F32)
    pieces.append(v.astype(BF16))
    return pieces


def _t5_bucket_np(dist):
    n = np.maximum(dist, 0).astype(np.int32)
    max_exact = N_BUCKETS // 2
    nf = np.maximum(n, 1).astype(np.float32)
    large = max_exact + (np.log(nf / np.float32(max_exact)) / np.float32(math.log(MAX_DISTANCE / max_exact))
                         * np.float32(N_BUCKETS - max_exact)).astype(np.int32)
    large = np.minimum(large, N_BUCKETS - 1)
    return np.where(n < max_exact, n, large).astype(np.int32)


def _bias_kernel(tab_ref, bkt_ref, valid_ref, out_ref):
    bkt = bkt_ref[...]
    valid = valid_ref[...] > 0
    for h in range(SWA_Q_HEADS):
        acc = jnp.zeros(bkt.shape, F32)
        for b in range(N_BUCKETS):
            acc = jnp.where(bkt == b, tab_ref[b, h], acc)
        out_ref[:, h * BLOCK:(h + 1) * BLOCK] = jnp.where(valid, acc * LOG2E, NEG_INF)


def _bias_tables(rel_bias):
    ki = np.arange(2 * BLOCK)[:, None]
    qi = np.arange(BLOCK)[None, :]
    d_w = qi + BLOCK - ki
    bkt_w = _t5_bucket_np(d_w)
    in_window = (d_w >= 0) & (d_w < WINDOW)
    mi = np.arange(N_META)[:, None]
    bkt_m0 = _t5_bucket_np(N_META + qi - mi)
    far = _t5_bucket_np(np.asarray([[N_META + BLOCK - (N_META - 1)]]))
    assert far[0, 0] == N_BUCKETS - 1, "meta keys of later blocks must share the last bucket"
    bkt = np.stack([np.concatenate([bkt_w, bkt_m0]), np.concatenate([bkt_w, np.full_like(bkt_m0, far[0, 0])])])
    meta_ok = np.ones((N_META, BLOCK), bool)
    valid = np.stack([np.concatenate([in_window & (ki >= BLOCK), meta_ok]), np.concatenate([in_window, meta_ok])])
    spec = pl.BlockSpec((None, SWA_KEYS, BLOCK), lambda a: (a, 0, 0))
    return pl.pallas_call(
        _bias_kernel,
        grid=(2,),
        in_specs=[pl.BlockSpec(memory_space=pltpu.SMEM), spec, spec],
        out_specs=pl.BlockSpec((None, SWA_KEYS, SWA_Q_HEADS * BLOCK), lambda a: (a, 0, 0)),
        out_shape=jax.ShapeDtypeStruct((2, SWA_KEYS, SWA_Q_HEADS * BLOCK), F32),
        name="bias_tables",
    )(rel_bias.astype(F32), jnp.asarray(bkt.astype(np.int32)), jnp.asarray(valid.astype(np.int32)))


def _fox_aug8(val_row, pieces_first):
    n = val_row.shape[1]
    row = lax.broadcasted_iota(jnp.int32, (SUBLANES, n), 0)
    p0, o0 = (0, N_PIECES) if pieces_first else (N_PIECES, 0)
    aug = jnp.where((row >= o0) & (row < o0 + N_PIECES), 1.0, 0.0)
    for i, piece in enumerate(_bf16_pieces(val_row)):
        aug = jnp.where(row == p0 + i, piece.astype(F32), aug)
    return aug


def _fox_head_t(own_t, aug8, e):
    pad = jnp.zeros((HEAD_DIM - SUBLANES, own_t.shape[1]), F32)
    return jnp.concatenate([own_t, aug8, pad] if e == 0 else [aug8, pad, own_t], axis=0)


def _inproj_kernel(x_ref, g_ref, w_ref, wf_ref, bft_ref, tri_ref, c0_ref,
                   pa_ref, qt_ref, ka_ref, vt_ref, cum_ref, qn2_ref, kmx_ref, carry_ref):
    @pl.when(pl.program_id(1) == 0)
    def _():
        carry_ref[...] = c0_ref[...]

    tm = x_ref.shape[0]
    y = _rms(x_ref[...], g_ref[...]).astype(BF16)

    tile_lanes = lambda a: jnp.concatenate([a] * (tm // LANES), axis=1)
    f_t = lax.dot_general(wf_ref[...], y, (((0,), (1,)), ((), ())), preferred_element_type=F32) + tile_lanes(bft_ref[...])
    acc_fox = _dot(y, w_ref[:, OFF_VA:OFF_VB])
    ls_t = jnp.minimum(f_t, 0.0) - jnp.log1p(jnp.exp(-jnp.abs(f_t)))
    cum_t = tile_lanes(carry_ref[...])
    for piece in _bf16_pieces(ls_t):
        cum_t = cum_t + _dot(piece, tri_ref[...])
    carry_ref[...] = jnp.broadcast_to(cum_t[:, tm - 1:tm], carry_ref.shape)
    for g in range(FOX_HEADS // FOX_GROUP):
        cum_ref[g] = cum_t[g * FOX_GROUP:(g + 1) * FOX_GROUP]
    cb_t = cum_t * (-LOG2E)

    acc_swa = _dot(y, w_ref[:, 0:OFF_VA])
    pa_ref[:, 0:OFF_QA] = (acc_swa[:, 0:OFF_QA] * (SCALE * LOG2E)).astype(BF16)
    pa_ref[:, OFF_QA:OFF_VA] = acc_swa[:, OFF_QA:OFF_VA].astype(BF16)

    ones = jnp.ones((FOX_VROWS - HEAD_DIM, tm), F32)
    for p in range(FOX_HEADS // 2):
        cols = lambda off: slice(off - OFF_VA + p * PAIR_W, off - OFF_VA + (p + 1) * PAIR_W)
        q_t = (acc_fox[:, cols(OFF_VA)] * (SCALE * LOG2E)).astype(BF16).astype(F32).T
        k_t = acc_fox[:, cols(OFF_QB)].astype(BF16).astype(F32).T
        v_t = acc_fox[:, cols(OFF_KB)].astype(BF16).astype(F32).T
        qt_ref[p] = q_t.astype(BF16)
        for e in range(2):
            h = 2 * p + e
            own = slice(e * HEAD_DIM, (e + 1) * HEAD_DIM)
            qn2_ref[h // FOX_GROUP, h % FOX_GROUP:h % FOX_GROUP + 1, :] = jnp.sum(q_t[own] * q_t[own], axis=0, keepdims=True)
            kn2 = jnp.sum(k_t[own] * k_t[own], axis=0, keepdims=True)
            kmx_ref[h:h + 1, :] = jnp.broadcast_to(jnp.max(kn2, axis=1, keepdims=True), (1, LANES))
            ka_ref[h] = _fox_head_t(k_t[own], _fox_aug8(cb_t[h:h + 1], True), e).T.astype(BF16)
            vt_ref[h] = jnp.concatenate([v_t[own], ones], axis=0).astype(BF16)


def _inproj(x3, g, w, wft, bft, c0, tm):
    nb, rows, _ = x3.shape
    nt = rows // tm
    tri = jnp.asarray(np.triu(np.ones((tm, tm), np.float32)), BF16)
    const = lambda b, t: (0, 0)
    npair = FOX_HEADS // 2
    ngroup = FOX_HEADS // FOX_GROUP
    return pl.pallas_call(
        _inproj_kernel,
        grid=(nb, nt),
        in_specs=[
            pl.BlockSpec((None, tm, D_MODEL), lambda b, t: (b, t, 0)),
            pl.BlockSpec((1, D_MODEL), const),
            pl.BlockSpec(w.shape, const),
            pl.BlockSpec((D_MODEL, GATE_ROWS), const),
            pl.BlockSpec((GATE_ROWS, LANES), const),
            pl.BlockSpec((tm, tm), const),
            pl.BlockSpec((GATE_ROWS, LANES), const),
        ],
        out_specs=[
            pl.BlockSpec((None, tm, OFF_VA), lambda b, t: (b, t, 0)),
            pl.BlockSpec((None, npair, PAIR_W, tm), lambda b, t: (b, 0, 0, t)),
            pl.BlockSpec((None, FOX_HEADS, tm, PAIR_W), lambda b, t: (b, 0, t, 0)),
            pl.BlockSpec((None, FOX_HEADS, FOX_VROWS, tm), lambda b, t: (b, 0, 0, t)),
            pl.BlockSpec((None, ngroup, FOX_GROUP, tm), lambda b, t: (b, 0, 0, t)),
            pl.BlockSpec((None, ngroup, FOX_GROUP, tm), lambda b, t: (b, 0, 0, t)),
            pl.BlockSpec((None, None, FOX_HEADS, LANES), lambda b, t: (b, t, 0, 0)),
        ],
        out_shape=[
            jax.ShapeDtypeStruct((nb, rows, OFF_VA), BF16),
            jax.ShapeDtypeStruct((nb, npair, PAIR_W, rows), BF16),
            jax.ShapeDtypeStruct((nb, FOX_HEADS, rows, PAIR_W), BF16),
            jax.ShapeDtypeStruct((nb, FOX_HEADS, FOX_VROWS, rows), BF16),
            jax.ShapeDtypeStruct((nb, ngroup, FOX_GROUP, rows), F32),
            jax.ShapeDtypeStruct((nb, ngroup, FOX_GROUP, rows), F32),
            jax.ShapeDtypeStruct((nb, nt, FOX_HEADS, LANES), F32),
        ],
        scratch_shapes=[pltpu.VMEM((GATE_ROWS, LANES), F32)],
        compiler_params=pltpu.CompilerParams(
            dimension_semantics=("arbitrary", "arbitrary"), vmem_limit_bytes=VMEM_LIMIT),
        name="inproj",
    )(x3, g, w, wft, bft, tri, c0)


def _swa_scores(q, k_all, bias):
    nq = q.shape[0]
    zeros = jnp.zeros((HEAD_DIM, nq), F32)
    blocks = []
    for pair in range(SWA_Q_HEADS // 2):
        q_t = q[:, pair * PAIR_W:(pair + 1) * PAIR_W].astype(F32).T
        for hp in range(2):
            qh = q_t[hp * HEAD_DIM:(hp + 1) * HEAD_DIM]
            g = (2 * pair + hp) // SWA_GROUP
            blocks.append(jnp.concatenate([qh, zeros] if g == 0 else [zeros, qh], axis=0))
    q_bd = jnp.concatenate(blocks, axis=1).astype(BF16)
    return _dot(k_all, q_bd) + bias


def _swa_outputs(s, v_all, sink_ref):
    nq = s.shape[1] // SWA_Q_HEADS
    ps, sink_terms = [], []
    for h in range(SWA_Q_HEADS):
        sh = s[:, h * nq:(h + 1) * nq]
        sink = sink_ref[h] * LOG2E
        m = jnp.maximum(jnp.max(sh, axis=0, keepdims=True), sink)
        ps.append(jnp.exp2(sh - m).astype(BF16))
        sink_terms.append(jnp.exp2(sink - m))
    n = BLOCK
    v_t = jnp.concatenate([v_all[0:n].astype(F32).T, v_all[n:2 * n].astype(F32).T,
                           v_all[2 * n:].astype(F32).T], axis=1)
    ones = jnp.ones((SWA_ONES, v_t.shape[1]), F32)
    outs = []
    for g in range(SWA_KV_HEADS):
        v_aug = jnp.concatenate([v_t[g * HEAD_DIM:(g + 1) * HEAD_DIM], ones], axis=0).astype(BF16)
        p_g = jnp.concatenate(ps[g * SWA_GROUP:(g + 1) * SWA_GROUP], axis=1)
        o_g = _dot(v_aug, p_g)
        for j in range(SWA_GROUP):
            cols = slice(j * nq, (j + 1) * nq)
            l = o_g[HEAD_DIM:HEAD_DIM + 1, cols] + sink_terms[g * SWA_GROUP + j]
            outs.append(o_g[0:HEAD_DIM, cols] / l)
    pairs = [jnp.concatenate(outs[2 * p:2 * p + 2], axis=0).T for p in range(SWA_Q_HEADS // 2)]
    return jnp.concatenate(pairs, axis=1).astype(BF16)


def _swa_kernel(sink_ref, q_ref, kc_ref, kp_ref, vc_ref, vp_ref, km_ref, vm_ref, bias0_ref, bias1_ref, o_ref):
    n = BLOCK
    nsub = q_ref.shape[0] // n
    k_rows = [kp_ref[...]] + [kc_ref[c * n:(c + 1) * n] for c in range(nsub)]
    v_rows = [vp_ref[...]] + [vc_ref[c * n:(c + 1) * n] for c in range(nsub)]
    def block_scores(c):
        k_all = jnp.concatenate([k_rows[c], k_rows[c + 1], km_ref[...]], axis=0)
        bias = bias0_ref[...] if c == 0 else bias1_ref[...]
        return _swa_scores(q_ref[c * n:(c + 1) * n], k_all, bias)

    s_next = block_scores(0)
    for c in range(nsub):
        s_cur = s_next
        if c + 1 < nsub:
            s_next = block_scores(c + 1)
        v_all = jnp.concatenate([v_rows[c], v_rows[c + 1], vm_ref[...]], axis=0)
        o_ref[c * n:(c + 1) * n] = _swa_outputs(s_cur, v_all, sink_ref)


def _swa(proj, projm, bias, sinks):
    nb, rows, _ = proj.shape
    tq = SWA_TQ
    ratio = tq // BLOCK
    kcol, vcol = OFF_QA // SWA_KV_W, OFF_KA // SWA_KV_W
    cur = lambda c: (lambda b, n: (b, n, c))
    prev = lambda c: (lambda b, n: (b, jnp.maximum(n * ratio - 1, 0), c))
    bias_spec = lambda idx: pl.BlockSpec((None, SWA_KEYS, SWA_Q_HEADS * BLOCK), idx)
    return pl.pallas_call(
        _swa_kernel,
        grid=(nb, rows // tq),
        in_specs=[
            pl.BlockSpec(memory_space=pltpu.SMEM),
            pl.BlockSpec((None, tq, SWA_Q_W), lambda b, n: (b, n, 0)),
            pl.BlockSpec((None, tq, SWA_KV_W), cur(kcol)),
            pl.BlockSpec((None, BLOCK, SWA_KV_W), prev(kcol)),
            pl.BlockSpec((None, tq, SWA_KV_W), cur(vcol)),
            pl.BlockSpec((None, BLOCK, SWA_KV_W), prev(vcol)),
            pl.BlockSpec((N_META, SWA_KV_W), lambda b, n: (0, kcol)),
            pl.BlockSpec((N_META, SWA_KV_W), lambda b, n: (0, vcol)),
            bias_spec(lambda b, n: (jnp.minimum(n, 1), 0, 0)),
            bias_spec(lambda b, n: (1, 0, 0)),
        ],
        out_specs=pl.BlockSpec((None, tq, SWA_Q_W), lambda b, n: (b, n, 0)),
        out_shape=jax.ShapeDtypeStruct((nb, rows, SWA_Q_W), BF16),
        compiler_params=pltpu.CompilerParams(
            dimension_semantics=("arbitrary", "arbitrary"), vmem_limit_bytes=VMEM_LIMIT),
        name="swa",
    )(sinks, proj, proj, proj, proj, proj, projm, projm, bias, bias)


def _fox_kernel(qt_ref, ka_ref, vt_ref, kam_ref, vtm_ref, cq_ref, qn2_ref, kmx_ref, kmxm_ref,
                wo_ref, wgu_ref, wd_ref, o_ref, wo_bf_ref, wgu_bf_ref, wd_bf_ref, acc_scr):
    wo_bf_ref[...] = wo_ref[...].astype(BF16)
    wgu_bf_ref[...] = wgu_ref[...].astype(BF16)
    wd_bf_ref[...] = wd_ref[...].astype(BF16)

    i = pl.program_id(2)
    t = o_ref.shape[0]
    nh = acc_scr.shape[0]
    colmax = lambda x: jnp.max(x, axis=0, keepdims=True)
    tile_rows = lambda j: pl.ds(pl.multiple_of(j * t, t), t)

    kmax2 = jnp.maximum(jnp.max(kmx_ref[...], axis=0), kmxm_ref[...])
    r = [jnp.sqrt(qn2_ref[h:h + 1, :] * kmax2[h:h + 1, 0:1]) * FOX_BOUND_MARGIN for h in range(nh)]
    worst = r[0]
    for rh in r[1:]:
        worst = jnp.maximum(worst, rh)
    safe = jnp.max(worst) <= FOX_SAFE_LOG2

    qt = []
    for h in range(nh):
        e = h % 2
        u = cq_ref[h:h + 1, :] * LOG2E - jnp.where(safe, r[h], 0.0)
        own = qt_ref[h // 2, e * HEAD_DIM:(e + 1) * HEAD_DIM, :].astype(F32)
        qt.append(_fox_head_t(own, _fox_aug8(u, False), e).astype(BF16))

    def scores(h, j, mask):
        s = _dot(ka_ref[h, tile_rows(j), :], qt[h])
        if mask is not None:
            key = lax.broadcasted_iota(jnp.int32, (t, t), 0)
            qry = lax.broadcasted_iota(jnp.int32, (t, t), 1)
            visible = (key <= qry) if mask == "diagonal" else (key + (j - i) * t <= qry)
            s = jnp.where(visible, s, NEG_INF)
        return s

    @pl.when(safe)
    def _():
        s_meta = [_dot(kam_ref[h], qt[h]) for h in range(nh)]
        for h in range(nh):
            acc_scr[h] = _dot(vtm_ref[h], jnp.exp2(s_meta[h]).astype(BF16))

        def run_tiles(tiles):
            prev = None
            for j, mask in tiles:
                for h in range(nh):
                    s = scores(h, j, mask)
                    if prev is not None:
                        ph, pj, ps = prev
                        acc_scr[ph] += _dot(vt_ref[ph, :, tile_rows(pj)], jnp.exp2(ps).astype(BF16))
                    prev = (h, j, s)
            ph, pj, ps = prev
            acc_scr[ph] += _dot(vt_ref[ph, :, tile_rows(pj)], jnp.exp2(ps).astype(BF16))

        def tile_pair(p, carry):
            run_tiles([(2 * p, None), (2 * p + 1, "causal")])
            return carry

        lax.fori_loop(0, (i + 1) // 2, tile_pair, 0)

        @pl.when(i % 2 == 0)
        def _():
            run_tiles([(i, "diagonal")])

    @pl.when(jnp.logical_not(safe))
    def _():
        m0 = []
        for h in range(nh):
            s = _dot(kam_ref[h], qt[h])
            m = colmax(s)
            acc_scr[h] = _dot(vtm_ref[h], jnp.exp2(s - m).astype(BF16))
            m0.append(m)

        def tile(j, ms, diagonal):
            out = []
            for h in range(nh):
                s = scores(h, j, "diagonal" if diagonal else None)
                m_next = jnp.maximum(ms[h], colmax(s))
                p = jnp.exp2(s - m_next).astype(BF16)
                acc_scr[h] = jnp.exp2(ms[h] - m_next) * acc_scr[h] + _dot(vt_ref[h, :, tile_rows(j)], p)
                out.append(m_next)
            return tuple(out)

        ms = lax.fori_loop(0, i, lambda j, ms: tile(j, ms, False), tuple(m0))
        tile(i, ms, True)

    for pp in range(nh // 2):
        o_t = jnp.concatenate([acc_scr[h, 0:HEAD_DIM] / acc_scr[h, HEAD_DIM:HEAD_DIM + 1]
                               for h in (2 * pp, 2 * pp + 1)], axis=0)
        o_ref[:, pp * PAIR_W:(pp + 1) * PAIR_W] = o_t.T.astype(BF16)


def _fox(qt, ka, vt, ka_m, vt_m, cum, qn2, kmx, kmx_m, w_out, w_gate_up, w_down):
    nb, _, rows, _ = ka.shape
    t = FOX_T
    nh = FOX_GROUP
    ngroup = FOX_HEADS // nh
    nt = kmx.shape[1]
    nq = rows // t
    nsteps = nb * ngroup * nq
    slab = lambda w: min(r for r in range(SLAB_ALIGN, w.shape[0] + 1, SLAB_ALIGN)
                         if w.shape[0] % r == 0 and r * nsteps >= w.shape[0])
    step = lambda b, g, i: (b * ngroup + g) * nq + i
    w_spec = lambda w: pl.BlockSpec((slab(w), w.shape[1]),
                                    lambda b, g, i: (jnp.minimum(step(b, g, i), w.shape[0] // slab(w) - 1), 0))
    return pl.pallas_call(
        _fox_kernel,
        grid=(nb, ngroup, nq),
        in_specs=[
            pl.BlockSpec((None, nh // 2, PAIR_W, t), lambda b, g, i: (b, g, 0, i)),
            pl.BlockSpec((None, nh, rows, PAIR_W), lambda b, g, i: (b, g, 0, 0)),
            pl.BlockSpec((None, nh, FOX_VROWS, rows), lambda b, g, i: (b, g, 0, 0)),
            pl.BlockSpec((nh, N_META, PAIR_W), lambda b, g, i: (g, 0, 0)),
            pl.BlockSpec((nh, FOX_VROWS, N_META), lambda b, g, i: (g, 0, 0)),
            pl.BlockSpec((None, None, nh, t), lambda b, g, i: (b, g, 0, i)),
            pl.BlockSpec((None, None, nh, t), lambda b, g, i: (b, g, 0, i)),
            pl.BlockSpec((None, nt, None, nh, LANES), lambda b, g, i: (b, 0, g, 0, 0)),
            pl.BlockSpec((None, nh, LANES), lambda b, g, i: (g, 0, 0)),
            w_spec(w_out), w_spec(w_gate_up), w_spec(w_down),
        ],
        out_specs=[pl.BlockSpec((None, t, nh * HEAD_DIM), lambda b, g, i: (b, i, g)),
                   w_spec(w_out), w_spec(w_gate_up), w_spec(w_down)],
        out_shape=[jax.ShapeDtypeStruct((nb, rows, FOX_W), BF16)]
        + [jax.ShapeDtypeStruct(w.shape, BF16) for w in (w_out, w_gate_up, w_down)],
        scratch_shapes=[pltpu.VMEM((nh, FOX_VROWS, t), F32)],
        compiler_params=pltpu.CompilerParams(
            dimension_semantics=("arbitrary", "arbitrary", "arbitrary"), vmem_limit_bytes=VMEM_LIMIT),
        name="fox",
    )(qt, ka, vt, ka_m, vt_m, cum, qn2,
      kmx.reshape(nb, nt, ngroup, nh, LANES), kmx_m.reshape(ngroup, nh, LANES), w_out, w_gate_up, w_down)


def _ffn_kernel(oa_ref, ob_ref, x_ref, wo_ref, g1_ref, g2_ref, wg_ref, wu_ref, wd_ref, g3_ref, out_ref):
    tm = x_ref.shape[0]
    parts = [slice(c * tm // FFN_PARTS, (c + 1) * tm // FFN_PARTS) for c in range(FFN_PARTS)]
    mix = [jnp.concatenate([oa_ref[r, :], ob_ref[r, :]], axis=1) for r in parts]
    a = [_dot(m, wo_ref[...]) for m in mix]
    h1 = [x_ref[r, :] + _rms(ai, g1_ref[...]) for r, ai in zip(parts, a)]
    hn = [_rms(h, g2_ref[...]).astype(BF16) for h in h1]
    gate_up = [(_dot(h, wg_ref[...]), _dot(h, wu_ref[...])) for h in hn]
    act = [(g / (1.0 + jnp.exp(-g)) * u).astype(BF16) for g, u in gate_up]
    ff = [_dot(ac, wd_ref[...]) for ac in act]
    for r, h, f in zip(parts, h1, ff):
        out_ref[r, :] = h + _rms(f, g3_ref[...])


def _ffn(o_a, o_b, x3, wo, g1, g2, wgu, wd, g3):
    nb, rows, _ = x3.shape
    tm = FFN_TM
    const = lambda b, t: (0, 0)
    resident = lambda shape: pl.BlockSpec(shape, const, pipeline_mode=pl.Buffered(1))
    row = lambda w: pl.BlockSpec((None, tm, w), lambda b, t: (b, t, 0))
    return pl.pallas_call(
        _ffn_kernel,
        grid=(nb, rows // tm),
        in_specs=[
            row(SWA_Q_W), row(FOX_W), row(D_MODEL),
            resident((D_MODEL, D_MODEL)), resident((1, D_MODEL)), resident((1, D_MODEL)),
            pl.BlockSpec((D_MODEL, D_FF), lambda b, t: (0, 0), pipeline_mode=pl.Buffered(1)),
            pl.BlockSpec((D_MODEL, D_FF), lambda b, t: (0, 1), pipeline_mode=pl.Buffered(1)),
            resident((D_FF, D_MODEL)),
            resident((1, D_MODEL)),
        ],
        out_specs=row(D_MODEL),
        out_shape=jax.ShapeDtypeStruct((nb, rows, D_MODEL), F32),
        compiler_params=pltpu.CompilerParams(
            dimension_semantics=("arbitrary", "arbitrary"), vmem_limit_bytes=VMEM_LIMIT),
        name="outproj_ffn",
    )(o_a, o_b, x3, wo, g1, g2, wgu, wgu, wd, g3)


def kernel(x, meta_tokens, rel_bias, ln_pre_mix, ln_post_mix, ln_pre_ffn, ln_post_ffn,
           w_in, b_forget, sinks, w_out, w_gate_up, w_down):
    nb, seq, d = x.shape
    assert d == D_MODEL and seq % IN_TM == 0 and seq % FOX_T == 0 and seq % FFN_TM == 0 and seq % SWA_TQ == 0
    assert w_in.shape[0] == 1, "single-layer block"
    assert meta_tokens.shape == (N_META, D_MODEL)
    x = x.astype(F32)

    w_all = w_in[0].astype(BF16)
    pad_gate = GATE_ROWS - FOX_HEADS
    w_ft = jnp.pad(w_in[0, :, OFF_VB:], ((0, 0), (0, pad_gate))).astype(BF16)
    lane_rep = lambda v: jnp.broadcast_to(jnp.pad(v.astype(F32), (0, pad_gate))[:, None], (GATE_ROWS, LANES))
    b_ft = lane_rep(b_forget[0])
    g_pre = ln_pre_mix[0].astype(F32).reshape(1, D_MODEL)

    swa_bias = _bias_tables(rel_bias)

    x_m = jnp.pad(meta_tokens.astype(F32), ((0, META_TM - N_META), (0, 0)))[None]
    pa_m, _, ka_m, vt_m, cum_m, _, kmx_m = _inproj(x_m, g_pre, w_all, w_ft, b_ft,
                                                   jnp.zeros((GATE_ROWS, LANES), F32), META_TM)
    c0 = lane_rep(cum_m[0, :, :, N_META - 1].reshape(FOX_HEADS))
    pa, qt, ka, vt, cum, qn2, kmx = _inproj(x, g_pre, w_all, w_ft, b_ft, c0, IN_TM)

    o_a = _swa(pa, pa_m[0, :N_META], swa_bias, sinks[0].astype(F32))
    o_b, w_out_bf, w_gu_bf, w_down_bf = _fox(qt, ka, vt, ka_m[0, :, :N_META], vt_m[0, :, :, :N_META], cum, qn2, kmx,
                                             kmx_m[0, 0], w_out[0].astype(F32), w_gate_up[0].astype(F32),
                                             w_down[0].astype(F32))

    return _ffn(o_a, o_b, x,
                w_out_bf,
                ln_post_mix[0].astype(F32).reshape(1, D_MODEL),
                ln_pre_ffn[0].astype(F32).reshape(1, D_MODEL),
                w_gu_bf,
                w_down_bf,
                ln_post_ffn[0].astype(F32).reshape(1, D_MODEL))
```

```python
import math

import numpy as np
import jax
import jax.numpy as jnp
from jax import lax
from jax.experimental import pallas as pl
from jax.experimental.pallas import tpu as pltpu

D_MODEL = 1024
N_META = 16
HEAD_DIM = 64
SWA_Q_HEADS = 8
SWA_KV_HEADS = 2
SWA_GROUP = SWA_Q_HEADS // SWA_KV_HEADS
FOX_HEADS = 8
SWA_Q_W = SWA_Q_HEADS * HEAD_DIM
SWA_KV_W = SWA_KV_HEADS * HEAD_DIM
FOX_W = FOX_HEADS * HEAD_DIM
OFF_QA = SWA_Q_W
OFF_KA = OFF_QA + SWA_KV_W
OFF_VA = OFF_KA + SWA_KV_W
OFF_QB = OFF_VA + FOX_W
OFF_KB = OFF_QB + FOX_W
OFF_VB = OFF_KB + FOX_W
WINDOW = 128
BLOCK = 128
N_BUCKETS = 32
MAX_DISTANCE = 128
D_FF = 2816
EPS = 1e-6
NEG_INF = -1e30
SCALE = HEAD_DIM ** -0.5
LOG2E = math.log2(math.e)

LANES = 128
SUBLANES = 8
PAIR_W = 2 * HEAD_DIM
VMEM_LIMIT = 56 * 1024 * 1024

SWA_KEYS = 2 * BLOCK + N_META
SWA_TQ = 2048
SWA_ONES = 16
IN_TM = 512
META_TM = 128
GATE_ROWS = 16
FOX_VROWS = HEAD_DIM + 16
FOX_T = 512
FOX_GROUP = 4
FOX_SAFE_LOG2 = 40.0
FOX_BOUND_MARGIN = 1.001
FFN_TM = 512
FFN_PARTS = 2
SLAB_ALIGN = 16
N_PIECES = 3

F32 = jnp.float32
BF16 = jnp.bfloat16


def _dot(a, b):
    return jnp.dot(a, b, preferred_element_type=F32)


def _dot_nt(a, b):
    return lax.dot_general(a, b, (((1,), (1,)), ((), ())), preferred_element_type=F32)


def _rms(t, g):
    return t * lax.rsqrt(jnp.mean(t * t, axis=-1, keepdims=True) + EPS) * g


def _bf16_pieces(v):
    pieces = []
    for _ in range(N_PIECES - 1):
        p = v.astype(BF16)
        pieces.append(p)
        v = v - p.astype(F32)
    pieces.append(v.astype(BF16))
    return pieces


def _t5_bucket_np(dist):
    n = np.maximum(dist, 0).astype(np.int32)
    max_exact = N_BUCKETS // 2
    nf = np.maximum(n, 1).astype(np.float32)
    large = max_exact + (np.log(nf / np.float32(max_exact)) / np.float32(math.log(MAX_DISTANCE / max_exact))
                         * np.float32(N_BUCKETS - max_exact)).astype(np.int32)
    large = np.minimum(large, N_BUCKETS - 1)
    return np.where(n < max_exact, n, large).astype(np.int32)


def _bias_kernel(tab_ref, bkt_ref, valid_ref, out_ref):
    bkt = bkt_ref[...]
    valid = valid_ref[...] > 0
    for h in range(SWA_Q_HEADS):
        acc = jnp.zeros(bkt.shape, F32)
        for b in range(N_BUCKETS):
            acc = jnp.where(bkt == b, tab_ref[b, h], acc)
        out_ref[:, h * BLOCK:(h + 1) * BLOCK] = jnp.where(valid, acc * LOG2E, NEG_INF)


def _bias_tables(rel_bias):
    ki = np.arange(2 * BLOCK)[:, None]
    qi = np.arange(BLOCK)[None, :]
    d_w = qi + BLOCK - ki
    bkt_w = _t5_bucket_np(d_w)
    in_window = (d_w >= 0) & (d_w < WINDOW)
    mi = np.arange(N_META)[:, None]
    bkt_m0 = _t5_bucket_np(N_META + qi - mi)
    far = _t5_bucket_np(np.asarray([[N_META + BLOCK - (N_META - 1)]]))
    assert far[0, 0] == N_BUCKETS - 1, "meta keys of later blocks must share the last bucket"
    bkt = np.stack([np.concatenate([bkt_w, bkt_m0]), np.concatenate([bkt_w, np.full_like(bkt_m0, far[0, 0])])])
    meta_ok = np.ones((N_META, BLOCK), bool)
    valid = np.stack([np.concatenate([in_window & (ki >= BLOCK), meta_ok]), np.concatenate([in_window, meta_ok])])
    spec = pl.BlockSpec((None, SWA_KEYS, BLOCK), lambda a: (a, 0, 0))
    return pl.pallas_call(
        _bias_kernel,
        grid=(2,),
        in_specs=[pl.BlockSpec(memory_space=pltpu.SMEM), spec, spec],
        out_specs=pl.BlockSpec((None, SWA_KEYS, SWA_Q_HEADS * BLOCK), lambda a: (a, 0, 0)),
        out_shape=jax.ShapeDtypeStruct((2, SWA_KEYS, SWA_Q_HEADS * BLOCK), F32),
        name="bias_tables",
    )(rel_bias.astype(F32), jnp.asarray(bkt.astype(np.int32)), jnp.asarray(valid.astype(np.int32)))


def _fox_aug8(val_row, pieces_first):
    n = val_row.shape[1]
    row = lax.broadcasted_iota(jnp.int32, (SUBLANES, n), 0)
    p0, o0 = (0, N_PIECES) if pieces_first else (N_PIECES, 0)
    aug = jnp.where((row >= o0) & (row < o0 + N_PIECES), 1.0, 0.0)
    for i, piece in enumerate(_bf16_pieces(val_row)):
        aug = jnp.where(row == p0 + i, piece.astype(F32), aug)
    return aug


def _fox_head_t(own_t, aug8, e):
    pad = jnp.zeros((HEAD_DIM - SUBLANES, own_t.shape[1]), F32)
    return jnp.concatenate([own_t, aug8, pad] if e == 0 else [aug8, pad, own_t], axis=0)


def _inproj_kernel(x_ref, g_ref, w_ref, wf_ref, bft_ref, tri_ref, c0_ref,
                   pa_ref, qt_ref, ka_ref, vt_ref, cum_ref, qn2_ref, kmx_ref, carry_ref):
    @pl.when(pl.program_id(1) == 0)
    def _():
        carry_ref[...] = c0_ref[...]

    tm = x_ref.shape[0]
    y = _rms(x_ref[...], g_ref[...]).astype(BF16)

    tile_lanes = lambda a: jnp.concatenate([a] * (tm // LANES), axis=1)
    f_t = lax.dot_general(wf_ref[...], y, (((0,), (1,)), ((), ())), preferred_element_type=F32) + tile_lanes(bft_ref[...])
    acc_fox = _dot(y, w_ref[:, OFF_VA:OFF_VB])
    ls_t = jnp.minimum(f_t, 0.0) - jnp.log1p(jnp.exp(-jnp.abs(f_t)))
    cum_t = tile_lanes(carry_ref[...])
    for piece in _bf16_pieces(ls_t):
        cum_t = cum_t + _dot(piece, tri_ref[...])
    carry_ref[...] = jnp.broadcast_to(cum_t[:, tm - 1:tm], carry_ref.shape)
    for g in range(FOX_HEADS // FOX_GROUP):
        cum_ref[g] = cum_t[g * FOX_GROUP:(g + 1) * FOX_GROUP]
    cb_t = cum_t * (-LOG2E)

    acc_swa = _dot(y, w_ref[:, 0:OFF_VA])
    pa_ref[:, 0:OFF_QA] = (acc_swa[:, 0:OFF_QA] * (SCALE * LOG2E)).astype(BF16)
    pa_ref[:, OFF_QA:OFF_VA] = acc_swa[:, OFF_QA:OFF_VA].astype(BF16)

    ones = jnp.ones((FOX_VROWS - HEAD_DIM, tm), F32)
    for p in range(FOX_HEADS // 2):
        cols = lambda off: slice(off - OFF_VA + p * PAIR_W, off - OFF_VA + (p + 1) * PAIR_W)
        q_t = (acc_fox[:, cols(OFF_VA)] * (SCALE * LOG2E)).astype(BF16).astype(F32).T
        k_t = acc_fox[:, cols(OFF_QB)].astype(BF16).astype(F32).T
        v_t = acc_fox[:, cols(OFF_KB)].astype(BF16).astype(F32).T
        qt_ref[p] = q_t.astype(BF16)
        for e in range(2):
            h = 2 * p + e
            own = slice(e * HEAD_DIM, (e + 1) * HEAD_DIM)
            qn2_ref[h // FOX_GROUP, h % FOX_GROUP:h % FOX_GROUP + 1, :] = jnp.sum(q_t[own] * q_t[own], axis=0, keepdims=True)
            kn2 = jnp.sum(k_t[own] * k_t[own], axis=0, keepdims=True)
            kmx_ref[h:h + 1, :] = jnp.broadcast_to(jnp.max(kn2, axis=1, keepdims=True), (1, LANES))
            ka_ref[h] = _fox_head_t(k_t[own], _fox_aug8(cb_t[h:h + 1], True), e).T.astype(BF16)
            vt_ref[h] = jnp.concatenate([v_t[own], ones], axis=0).astype(BF16)


def _inproj(x3, g, w, wft, bft, c0, tm):
    nb, rows, _ = x3.shape
    nt = rows // tm
    tri = jnp.asarray(np.triu(np.ones((tm, tm), np.float32)), BF16)
    const = lambda b, t: (0, 0)
    npair = FOX_HEADS // 2
    ngroup = FOX_HEADS // FOX_GROUP
    return pl.pallas_call(
        _inproj_kernel,
        grid=(nb, nt),
        in_specs=[
            pl.BlockSpec((None, tm, D_MODEL), lambda b, t: (b, t, 0)),
            pl.BlockSpec((1, D_MODEL), const),
            pl.BlockSpec(w.shape, const),
            pl.BlockSpec((D_MODEL, GATE_ROWS), const),
            pl.BlockSpec((GATE_ROWS, LANES), const),
            pl.BlockSpec((tm, tm), const),
            pl.BlockSpec((GATE_ROWS, LANES), const),
        ],
        out_specs=[
            pl.BlockSpec((None, tm, OFF_VA), lambda b, t: (b, t, 0)),
            pl.BlockSpec((None, npair, PAIR_W, tm), lambda b, t: (b, 0, 0, t)),
            pl.BlockSpec((None, FOX_HEADS, tm, PAIR_W), lambda b, t: (b, 0, t, 0)),
            pl.BlockSpec((None, FOX_HEADS, FOX_VROWS, tm), lambda b, t: (b, 0, 0, t)),
            pl.BlockSpec((None, ngroup, FOX_GROUP, tm), lambda b, t: (b, 0, 0, t)),
            pl.BlockSpec((None, ngroup, FOX_GROUP, tm), lambda b, t: (b, 0, 0, t)),
            pl.BlockSpec((None, None, FOX_HEADS, LANES), lambda b, t: (b, t, 0, 0)),
        ],
        out_shape=[
            jax.ShapeDtypeStruct((nb, rows, OFF_VA), BF16),
            jax.ShapeDtypeStruct((nb, npair, PAIR_W, rows), BF16),
            jax.ShapeDtypeStruct((nb, FOX_HEADS, rows, PAIR_W), BF16),
            jax.ShapeDtypeStruct((nb, FOX_HEADS, FOX_VROWS, rows), BF16),
            jax.ShapeDtypeStruct((nb, ngroup, FOX_GROUP, rows), F32),
            jax.ShapeDtypeStruct((nb, ngroup, FOX_GROUP, rows), F32),
            jax.ShapeDtypeStruct((nb, nt, FOX_HEADS, LANES), F32),
        ],
        scratch_shapes=[pltpu.VMEM((GATE_ROWS, LANES), F32)],
        compiler_params=pltpu.CompilerParams(
            dimension_semantics=("arbitrary", "arbitrary"), vmem_limit_bytes=VMEM_LIMIT),
        name="inproj",
    )(x3, g, w, wft, bft, tri, c0)


def _swa_scores(q, k_all, bias):
    nq = q.shape[0]
    zeros = jnp.zeros((HEAD_DIM, nq), F32)
    blocks = []
    for pair in range(SWA_Q_HEADS // 2):
        q_t = q[:, pair * PAIR_W:(pair + 1) * PAIR_W].astype(F32).T
        for hp in range(2):
            qh = q_t[hp * HEAD_DIM:(hp + 1) * HEAD_DIM]
            g = (2 * pair + hp) // SWA_GROUP
            blocks.append(jnp.concatenate([qh, zeros] if g == 0 else [zeros, qh], axis=0))
    q_bd = jnp.concatenate(blocks, axis=1).astype(BF16)
    return _dot(k_all, q_bd) + bias


def _swa_outputs(s, v_all, sink_ref):
    nq = s.shape[1] // SWA_Q_HEADS
    ps, sink_terms = [], []
    for h in range(SWA_Q_HEADS):
        sh = s[:, h * nq:(h + 1) * nq]
        sink = sink_ref[h] * LOG2E
        m = jnp.maximum(jnp.max(sh, axis=0, keepdims=True), sink)
        ps.append(jnp.exp2(sh - m).astype(BF16))
        sink_terms.append(jnp.exp2(sink - m))
    n = BLOCK
    v_t = jnp.concatenate([v_all[0:n].astype(F32).T, v_all[n:2 * n].astype(F32).T,
                           v_all[2 * n:].astype(F32).T], axis=1)
    ones = jnp.ones((SWA_ONES, v_t.shape[1]), F32)
    outs = []
    for g in range(SWA_KV_HEADS):
        v_aug = jnp.concatenate([v_t[g * HEAD_DIM:(g + 1) * HEAD_DIM], ones], axis=0).astype(BF16)
        p_g = jnp.concatenate(ps[g * SWA_GROUP:(g + 1) * SWA_GROUP], axis=1)
        o_g = _dot(v_aug, p_g)
        for j in range(SWA_GROUP):
            cols = slice(j * nq, (j + 1) * nq)
            l = o_g[HEAD_DIM:HEAD_DIM + 1, cols] + sink_terms[g * SWA_GROUP + j]
            outs.append(o_g[0:HEAD_DIM, cols] / l)
    pairs = [jnp.concatenate(outs[2 * p:2 * p + 2], axis=0).T for p in range(SWA_Q_HEADS // 2)]
    return jnp.concatenate(pairs, axis=1).astype(BF16)


def _swa_kernel(sink_ref, q_ref, kc_ref, kp_ref, vc_ref, vp_ref, km_ref, vm_ref, bias0_ref, bias1_ref, o_ref):
    n = BLOCK
    nsub = q_ref.shape[0] // n
    k_rows = [kp_ref[...]] + [kc_ref[c * n:(c + 1) * n] for c in range(nsub)]
    v_rows = [vp_ref[...]] + [vc_ref[c * n:(c + 1) * n] for c in range(nsub)]
    def block_scores(c):
        k_all = jnp.concatenate([k_rows[c], k_rows[c + 1], km_ref[...]], axis=0)
        bias = bias0_ref[...] if c == 0 else bias1_ref[...]
        return _swa_scores(q_ref[c * n:(c + 1) * n], k_all, bias)

    s_next = block_scores(0)
    for c in range(nsub):
        s_cur = s_next
        if c + 1 < nsub:
            s_next = block_scores(c + 1)
        v_all = jnp.concatenate([v_rows[c], v_rows[c + 1], vm_ref[...]], axis=0)
        o_ref[c * n:(c + 1) * n] = _swa_outputs(s_cur, v_all, sink_ref)


def _swa(proj, projm, bias, sinks):
    nb, rows, _ = proj.shape
    tq = SWA_TQ
    ratio = tq // BLOCK
    kcol, vcol = OFF_QA // SWA_KV_W, OFF_KA // SWA_KV_W
    cur = lambda c: (lambda b, n: (b, n, c))
    prev = lambda c: (lambda b, n: (b, jnp.maximum(n * ratio - 1, 0), c))
    bias_spec = lambda idx: pl.BlockSpec((None, SWA_KEYS, SWA_Q_HEADS * BLOCK), idx)
    return pl.pallas_call(
        _swa_kernel,
        grid=(nb, rows // tq),
        in_specs=[
            pl.BlockSpec(memory_space=pltpu.SMEM),
            pl.BlockSpec((None, tq, SWA_Q_W), lambda b, n: (b, n, 0)),
            pl.BlockSpec((None, tq, SWA_KV_W), cur(kcol)),
            pl.BlockSpec((None, BLOCK, SWA_KV_W), prev(kcol)),
            pl.BlockSpec((None, tq, SWA_KV_W), cur(vcol)),
            pl.BlockSpec((None, BLOCK, SWA_KV_W), prev(vcol)),
            pl.BlockSpec((N_META, SWA_KV_W), lambda b, n: (0, kcol)),
            pl.BlockSpec((N_META, SWA_KV_W), lambda b, n: (0, vcol)),
            bias_spec(lambda b, n: (jnp.minimum(n, 1), 0, 0)),
            bias_spec(lambda b, n: (1, 0, 0)),
        ],
        out_specs=pl.BlockSpec((None, tq, SWA_Q_W), lambda b, n: (b, n, 0)),
        out_shape=jax.ShapeDtypeStruct((nb, rows, SWA_Q_W), BF16),
        compiler_params=pltpu.CompilerParams(
            dimension_semantics=("arbitrary", "arbitrary"), vmem_limit_bytes=VMEM_LIMIT),
        name="swa",
    )(sinks, proj, proj, proj, proj, proj, projm, projm, bias, bias)


def _fox_kernel(qt_ref, ka_ref, vt_ref, kam_ref, vtm_ref, cq_ref, qn2_ref, kmx_ref, kmxm_ref,
                wo_ref, wgu_ref, wd_ref, o_ref, wo_bf_ref, wgu_bf_ref, wd_bf_ref, acc_scr):
    wo_bf_ref[...] = wo_ref[...].astype(BF16)
    wgu_bf_ref[...] = wgu_ref[...].astype(BF16)
    wd_bf_ref[...] = wd_ref[...].astype(BF16)

    i = pl.program_id(2)
    t = o_ref.shape[0]
    nh = acc_scr.shape[0]
    colmax = lambda x: jnp.max(x, axis=0, keepdims=True)
    tile_rows = lambda j: pl.ds(pl.multiple_of(j * t, t), t)

    kmax2 = jnp.maximum(jnp.max(kmx_ref[...], axis=0), kmxm_ref[...])
    r = [jnp.sqrt(qn2_ref[h:h + 1, :] * kmax2[h:h + 1, 0:1]) * FOX_BOUND_MARGIN for h in range(nh)]
    worst = r[0]
    for rh in r[1:]:
        worst = jnp.maximum(worst, rh)
    safe = jnp.max(worst) <= FOX_SAFE_LOG2

    qt = []
    for h in range(nh):
        e = h % 2
        u = cq_ref[h:h + 1, :] * LOG2E - jnp.where(safe, r[h], 0.0)
        own = qt_ref[h // 2, e * HEAD_DIM:(e + 1) * HEAD_DIM, :].astype(F32)
        qt.append(_fox_head_t(own, _fox_aug8(u, False), e).astype(BF16))

    def scores(h, j, diagonal):
        s = _dot(ka_ref[h, tile_rows(j), :], qt[h])
        if diagonal:
            key = lax.broadcasted_iota(jnp.int32, (t, t), 0)
            qry = lax.broadcasted_iota(jnp.int32, (t, t), 1)
            s = jnp.where(key <= qry, s, NEG_INF)
        return s

    @pl.when(safe)
    def _():
        def run(work):
            prev = None
            for score_fn, pv_fn in work:
                s = score_fn()
                if prev is not None:
                    prev[0](prev[1])
                prev = (pv_fn, s)
            prev[0](prev[1])

        def tile_work(j):
            def pv(h):
                def apply(s):
                    acc_scr[h] += _dot(vt_ref[h, :, tile_rows(j)], jnp.exp2(s).astype(BF16))
                return apply
            return [(lambda h=h: scores(h, j, False), pv(h)) for h in range(nh)]

        def first_work():
            def sc(h):
                return jnp.concatenate([scores(h, i, True), _dot(kam_ref[h], qt[h])], axis=0)
            def pv(h):
                def apply(s):
                    v_t = jnp.concatenate([vt_ref[h, :, tile_rows(i)], vtm_ref[h]], axis=1)
                    acc_scr[h] = _dot(v_t, jnp.exp2(s).astype(BF16))
                return apply
            return [(lambda h=h: sc(h), pv(h)) for h in range(nh)]

        run(first_work())

        def tile_pair(p, carry):
            run(tile_work(2 * p) + tile_work(2 * p + 1))
            return carry

        lax.fori_loop(0, i // 2, tile_pair, 0)

        @pl.when(i % 2 == 1)
        def _():
            run(tile_work(i - 1))

    @pl.when(jnp.logical_not(safe))
    def _():
        m0 = []
        for h in range(nh):
            s = _dot(kam_ref[h], qt[h])
            m = colmax(s)
            acc_scr[h] = _dot(vtm_ref[h], jnp.exp2(s - m).astype(BF16))
            m0.append(m)

        def tile(j, ms, diagonal):
            out = []
            for h in range(nh):
                s = scores(h, j, diagonal)
                m_next = jnp.maximum(ms[h], colmax(s))
                p = jnp.exp2(s - m_next).astype(BF16)
                acc_scr[h] = jnp.exp2(ms[h] - m_next) * acc_scr[h] + _dot(vt_ref[h, :, tile_rows(j)], p)
                out.append(m_next)
            return tuple(out)

        ms = lax.fori_loop(0, i, lambda j, ms: tile(j, ms, False), tuple(m0))
        tile(i, ms, True)

    for pp in range(nh // 2):
        o_t = jnp.concatenate([acc_scr[h, 0:HEAD_DIM] / acc_scr[h, HEAD_DIM:HEAD_DIM + 1]
                               for h in (2 * pp, 2 * pp + 1)], axis=0)
        o_ref[:, pp * PAIR_W:(pp + 1) * PAIR_W] = o_t.T.astype(BF16)


def _fox(qt, ka, vt, ka_m, vt_m, cum, qn2, kmx, kmx_m, w_out, w_gate_up, w_down):
    nb, _, rows, _ = ka.shape
    t = FOX_T
    nh = FOX_GROUP
    ngroup = FOX_HEADS // nh
    nt = kmx.shape[1]
    nq = rows // t
    nsteps = nb * ngroup * nq
    slab = lambda w: min(r for r in range(SLAB_ALIGN, w.shape[0] + 1, SLAB_ALIGN)
                         if w.shape[0] % r == 0 and r * nsteps >= w.shape[0])
    step = lambda b, g, i: (b * ngroup + g) * nq + i
    w_spec = lambda w: pl.BlockSpec((slab(w), w.shape[1]),
                                    lambda b, g, i: (jnp.minimum(step(b, g, i), w.shape[0] // slab(w) - 1), 0))
    return pl.pallas_call(
        _fox_kernel,
        grid=(nb, ngroup, nq),
        in_specs=[
            pl.BlockSpec((None, nh // 2, PAIR_W, t), lambda b, g, i: (b, g, 0, i)),
            pl.BlockSpec((None, nh, rows, PAIR_W), lambda b, g, i: (b, g, 0, 0)),
            pl.BlockSpec((None, nh, FOX_VROWS, rows), lambda b, g, i: (b, g, 0, 0)),
            pl.BlockSpec((nh, N_META, PAIR_W), lambda b, g, i: (g, 0, 0)),
            pl.BlockSpec((nh, FOX_VROWS, N_META), lambda b, g, i: (g, 0, 0)),
            pl.BlockSpec((None, None, nh, t), lambda b, g, i: (b, g, 0, i)),
            pl.BlockSpec((None, None, nh, t), lambda b, g, i: (b, g, 0, i)),
            pl.BlockSpec((None, nt, None, nh, LANES), lambda b, g, i: (b, 0, g, 0, 0)),
            pl.BlockSpec((None, nh, LANES), lambda b, g, i: (g, 0, 0)),
            w_spec(w_out), w_spec(w_gate_up), w_spec(w_down),
        ],
        out_specs=[pl.BlockSpec((None, t, nh * HEAD_DIM), lambda b, g, i: (b, i, g)),
                   w_spec(w_out), w_spec(w_gate_up), w_spec(w_down)],
        out_shape=[jax.ShapeDtypeStruct((nb, rows, FOX_W), BF16)]
        + [jax.ShapeDtypeStruct(w.shape, BF16) for w in (w_out, w_gate_up, w_down)],
        scratch_shapes=[pltpu.VMEM((nh, FOX_VROWS, t), F32)],
        compiler_params=pltpu.CompilerParams(
            dimension_semantics=("arbitrary", "arbitrary", "arbitrary"), vmem_limit_bytes=VMEM_LIMIT),
        name="fox",
    )(qt, ka, vt, ka_m, vt_m, cum, qn2,
      kmx.reshape(nb, nt, ngroup, nh, LANES), kmx_m.reshape(ngroup, nh, LANES), w_out, w_gate_up, w_down)


def _ffn_kernel(oa_ref, ob_ref, x_ref, wo_ref, g1_ref, g2_ref, wg_ref, wu_ref, wd_ref, g3_ref, out_ref):
    tm = x_ref.shape[0]
    parts = [slice(c * tm // FFN_PARTS, (c + 1) * tm // FFN_PARTS) for c in range(FFN_PARTS)]
    mix = [jnp.concatenate([oa_ref[r, :], ob_ref[r, :]], axis=1) for r in parts]
    a = [_dot(m, wo_ref[...]) for m in mix]
    h1 = [x_ref[r, :] + _rms(ai, g1_ref[...]) for r, ai in zip(parts, a)]
    hn = [_rms(h, g2_ref[...]).astype(BF16) for h in h1]
    gate_up = [(_dot(h, wg_ref[...]), _dot(h, wu_ref[...])) for h in hn]
    act = [(g / (1.0 + jnp.exp(-g)) * u).astype(BF16) for g, u in gate_up]
    ff = [_dot(ac, wd_ref[...]) for ac in act]
    for r, h, f in zip(parts, h1, ff):
        out_ref[r, :] = h + _rms(f, g3_ref[...])


def _ffn(o_a, o_b, x3, wo, g1, g2, wgu, wd, g3):
    nb, rows, _ = x3.shape
    tm = FFN_TM
    const = lambda b, t: (0, 0)
    resident = lambda shape: pl.BlockSpec(shape, const, pipeline_mode=pl.Buffered(1))
    row = lambda w: pl.BlockSpec((None, tm, w), lambda b, t: (b, t, 0))
    return pl.pallas_call(
        _ffn_kernel,
        grid=(nb, rows // tm),
        in_specs=[
            row(SWA_Q_W), row(FOX_W), row(D_MODEL),
            resident((D_MODEL, D_MODEL)), resident((1, D_MODEL)), resident((1, D_MODEL)),
            pl.BlockSpec((D_MODEL, D_FF), lambda b, t: (0, 0), pipeline_mode=pl.Buffered(1)),
            pl.BlockSpec((D_MODEL, D_FF), lambda b, t: (0, 1), pipeline_mode=pl.Buffered(1)),
            resident((D_FF, D_MODEL)),
            resident((1, D_MODEL)),
        ],
        out_specs=row(D_MODEL),
        out_shape=jax.ShapeDtypeStruct((nb, rows, D_MODEL), F32),
        compiler_params=pltpu.CompilerParams(
            dimension_semantics=("arbitrary", "arbitrary"), vmem_limit_bytes=VMEM_LIMIT),
        name="outproj_ffn",
    )(o_a, o_b, x3, wo, g1, g2, wgu, wgu, wd, g3)


def kernel(x, meta_tokens, rel_bias, ln_pre_mix, ln_post_mix, ln_pre_ffn, ln_post_ffn,
           w_in, b_forget, sinks, w_out, w_gate_up, w_down):
    nb, seq, d = x.shape
    assert d == D_MODEL and seq % IN_TM == 0 and seq % FOX_T == 0 and seq % FFN_TM == 0 and seq % SWA_TQ == 0
    assert w_in.shape[0] == 1, "single-layer block"
    assert meta_tokens.shape == (N_META, D_MODEL)
    x = x.astype(F32)

    w_all = w_in[0].astype(BF16)
    pad_gate = GATE_ROWS - FOX_HEADS
    w_ft = jnp.pad(w_in[0, :, OFF_VB:], ((0, 0), (0, pad_gate))).astype(BF16)
    lane_rep = lambda v: jnp.broadcast_to(jnp.pad(v.astype(F32), (0, pad_gate))[:, None], (GATE_ROWS, LANES))
    b_ft = lane_rep(b_forget[0])
    g_pre = ln_pre_mix[0].astype(F32).reshape(1, D_MODEL)

    swa_bias = _bias_tables(rel_bias)

    x_m = jnp.pad(meta_tokens.astype(F32), ((0, META_TM - N_META), (0, 0)))[None]
    pa_m, _, ka_m, vt_m, cum_m, _, kmx_m = _inproj(x_m, g_pre, w_all, w_ft, b_ft,
                                                   jnp.zeros((GATE_ROWS, LANES), F32), META_TM)
    c0 = lane_rep(cum_m[0, :, :, N_META - 1].reshape(FOX_HEADS))
    pa, qt, ka, vt, cum, qn2, kmx = _inproj(x, g_pre, w_all, w_ft, b_ft, c0, IN_TM)

    o_a = _swa(pa, pa_m[0, :N_META], swa_bias, sinks[0].astype(F32))
    o_b, w_out_bf, w_gu_bf, w_down_bf = _fox(qt, ka, vt, ka_m[0, :, :N_META], vt_m[0, :, :, :N_META], cum, qn2, kmx,
                                             kmx_m[0, 0], w_out[0].astype(F32), w_gate_up[0].astype(F32),
                                             w_down[0].astype(F32))

    return _ffn(o_a, o_b, x,
                w_out_bf,
                ln_post_mix[0].astype(F32).reshape(1, D_MODEL),
                ln_pre_ffn[0].astype(F32).reshape(1, D_MODEL),
                w_gu_bf,
                w_down_bf,
                ln_post_ffn[0].astype(F32).reshape(1, D_MODEL))
```

```python
import math

import numpy as np
import jax
import jax.numpy as jnp
from jax import lax
from jax.experimental import pallas as pl
from jax.experimental.pallas import tpu as pltpu

D_MODEL = 1024
N_META = 16
HEAD_DIM = 64
SWA_Q_HEADS = 8
SWA_KV_HEADS = 2
SWA_GROUP = SWA_Q_HEADS // SWA_KV_HEADS
FOX_HEADS = 8
SWA_Q_W = SWA_Q_HEADS * HEAD_DIM
SWA_KV_W = SWA_KV_HEADS * HEAD_DIM
FOX_W = FOX_HEADS * HEAD_DIM
OFF_QA = SWA_Q_W
OFF_KA = OFF_QA + SWA_KV_W
OFF_VA = OFF_KA + SWA_KV_W
OFF_QB = OFF_VA + FOX_W
OFF_KB = OFF_QB + FOX_W
OFF_VB = OFF_KB + FOX_W
WINDOW = 128
BLOCK = 128
N_BUCKETS = 32
MAX_DISTANCE = 128
D_FF = 2816
EPS = 1e-6
NEG_INF = -1e30
SCALE = HEAD_DIM ** -0.5
LOG2E = math.log2(math.e)

LANES = 128
SUBLANES = 8
PAIR_W = 2 * HEAD_DIM
VMEM_LIMIT = 56 * 1024 * 1024

SWA_KEYS = 2 * BLOCK + N_META
SWA_TQ = 2048
SWA_ONES = 16
IN_TM = 512
META_TM = 128
GATE_ROWS = 16
FOX_VROWS = HEAD_DIM + 16
FOX_T = 512
FOX_GROUP = 4
FOX_SAFE_LOG2 = 40.0
FOX_BOUND_MARGIN = 1.001
FFN_TM = 512
FFN_PARTS = 2
SLAB_ALIGN = 16
N_PIECES = 3

F32 = jnp.float32
BF16 = jnp.bfloat16


def _dot(a, b):
    return jnp.dot(a, b, preferred_element_type=F32)


def _dot_nt(a, b):
    return lax.dot_general(a, b, (((1,), (1,)), ((), ())), preferred_element_type=F32)


def _rms(t, g):
    return t * lax.rsqrt(jnp.mean(t * t, axis=-1, keepdims=True) + EPS) * g


def _bf16_pieces(v):
    pieces = []
    for _ in range(N_PIECES - 1):
        p = v.astype(BF16)
        pieces.append(p)
        v = v - p.astype(F32)
    pieces.append(v.astype(BF16))
    return pieces


def _t5_bucket_np(dist):
    n = np.maximum(dist, 0).astype(np.int32)
    max_exact = N_BUCKETS // 2
    nf = np.maximum(n, 1).astype(np.float32)
    large = max_exact + (np.log(nf / np.float32(max_exact)) / np.float32(math.log(MAX_DISTANCE / max_exact))
                         * np.float32(N_BUCKETS - max_exact)).astype(np.int32)
    large = np.minimum(large, N_BUCKETS - 1)
    return np.where(n < max_exact, n, large).astype(np.int32)


def _bias_kernel(tab_ref, bkt_ref, valid_ref, out_ref):
    bkt = bkt_ref[...]
    valid = valid_ref[...] > 0
    for h in range(SWA_Q_HEADS):
        acc = jnp.zeros(bkt.shape, F32)
        for b in range(N_BUCKETS):
            acc = jnp.where(bkt == b, tab_ref[b, h], acc)
        out_ref[:, h * BLOCK:(h + 1) * BLOCK] = jnp.where(valid, acc * LOG2E, NEG_INF)


def _bias_tables(rel_bias):
    ki = np.arange(2 * BLOCK)[:, None]
    qi = np.arange(BLOCK)[None, :]
    d_w = qi + BLOCK - ki
    bkt_w = _t5_bucket_np(d_w)
    in_window = (d_w >= 0) & (d_w < WINDOW)
    mi = np.arange(N_META)[:, None]
    bkt_m0 = _t5_bucket_np(N_META + qi - mi)
    far = _t5_bucket_np(np.asarray([[N_META + BLOCK - (N_META - 1)]]))
    assert far[0, 0] == N_BUCKETS - 1, "meta keys of later blocks must share the last bucket"
    bkt = np.stack([np.concatenate([bkt_w, bkt_m0]), np.concatenate([bkt_w, np.full_like(bkt_m0, far[0, 0])])])
    meta_ok = np.ones((N_META, BLOCK), bool)
    valid = np.stack([np.concatenate([in_window & (ki >= BLOCK), meta_ok]), np.concatenate([in_window, meta_ok])])
    spec = pl.BlockSpec((None, SWA_KEYS, BLOCK), lambda a: (a, 0, 0))
    return pl.pallas_call(
        _bias_kernel,
        grid=(2,),
        in_specs=[pl.BlockSpec(memory_space=pltpu.SMEM), spec, spec],
        out_specs=pl.BlockSpec((None, SWA_KEYS, SWA_Q_HEADS * BLOCK), lambda a: (a, 0, 0)),
        out_shape=jax.ShapeDtypeStruct((2, SWA_KEYS, SWA_Q_HEADS * BLOCK), F32),
        name="bias_tables",
    )(rel_bias.astype(F32), jnp.asarray(bkt.astype(np.int32)), jnp.asarray(valid.astype(np.int32)))


def _fox_aug8(val_row, pieces_first):
    n = val_row.shape[1]
    row = lax.broadcasted_iota(jnp.int32, (SUBLANES, n), 0)
    p0, o0 = (0, N_PIECES) if pieces_first else (N_PIECES, 0)
    aug = jnp.where((row >= o0) & (row < o0 + N_PIECES), 1.0, 0.0)
    for i, piece in enumerate(_bf16_pieces(val_row)):
        aug = jnp.where(row == p0 + i, piece.astype(F32), aug)
    return aug


def _fox_head_t(own_t, aug8, e):
    pad = jnp.zeros((HEAD_DIM - SUBLANES, own_t.shape[1]), F32)
    return jnp.concatenate([own_t, aug8, pad] if e == 0 else [aug8, pad, own_t], axis=0)


def _inproj_kernel(x_ref, g_ref, w_ref, wf_ref, bft_ref, tri_ref, c0_ref,
                   pa_ref, qt_ref, ka_ref, vt_ref, cum_ref, qn2_ref, kmx_ref, cmeta_ref, carry_ref):
    @pl.when(pl.program_id(1) == 0)
    def _():
        carry_ref[...] = c0_ref[...]

    tm = x_ref.shape[0]
    y = _rms(x_ref[...], g_ref[...]).astype(BF16)

    tile_lanes = lambda a: jnp.concatenate([a] * (tm // LANES), axis=1)
    f_t = lax.dot_general(wf_ref[...], y, (((0,), (1,)), ((), ())), preferred_element_type=F32) + tile_lanes(bft_ref[...])
    acc_fox = _dot(y, w_ref[:, OFF_VA:OFF_VB])
    ls_t = jnp.minimum(f_t, 0.0) - jnp.log1p(jnp.exp(-jnp.abs(f_t)))
    cum_t = tile_lanes(carry_ref[...])
    for piece in _bf16_pieces(ls_t):
        cum_t = cum_t + _dot(piece, tri_ref[...])
    carry_ref[...] = jnp.broadcast_to(cum_t[:, tm - 1:tm], carry_ref.shape)
    cmeta_ref[...] = jnp.broadcast_to(cum_t[:, N_META - 1:N_META], cmeta_ref.shape)
    for g in range(FOX_HEADS // FOX_GROUP):
        cum_ref[g] = cum_t[g * FOX_GROUP:(g + 1) * FOX_GROUP]
    cb_t = cum_t * (-LOG2E)

    acc_swa = _dot(y, w_ref[:, 0:OFF_VA])
    pa_ref[:, 0:OFF_QA] = (acc_swa[:, 0:OFF_QA] * (SCALE * LOG2E)).astype(BF16)
    pa_ref[:, OFF_QA:OFF_VA] = acc_swa[:, OFF_QA:OFF_VA].astype(BF16)

    ones = jnp.ones((FOX_VROWS - HEAD_DIM, tm), F32)
    for p in range(FOX_HEADS // 2):
        cols = lambda off: slice(off - OFF_VA + p * PAIR_W, off - OFF_VA + (p + 1) * PAIR_W)
        q_t = (acc_fox[:, cols(OFF_VA)] * (SCALE * LOG2E)).astype(BF16).astype(F32).T
        k_t = acc_fox[:, cols(OFF_QB)].astype(BF16).astype(F32).T
        v_t = acc_fox[:, cols(OFF_KB)].astype(BF16).astype(F32).T
        qt_ref[p] = q_t.astype(BF16)
        for e in range(2):
            h = 2 * p + e
            own = slice(e * HEAD_DIM, (e + 1) * HEAD_DIM)
            qn2_ref[h // FOX_GROUP, h % FOX_GROUP:h % FOX_GROUP + 1, :] = jnp.sum(q_t[own] * q_t[own], axis=0, keepdims=True)
            kn2 = jnp.sum(k_t[own] * k_t[own], axis=0, keepdims=True)
            kmx_ref[h:h + 1, :] = jnp.broadcast_to(jnp.max(kn2, axis=1, keepdims=True), (1, LANES))
            ka_ref[h] = _fox_head_t(k_t[own], _fox_aug8(cb_t[h:h + 1], True), e).T.astype(BF16)
            vt_ref[h] = jnp.concatenate([v_t[own], ones], axis=0).astype(BF16)


def _inproj(x3, g, w, wft, bft, c0, tm):
    nb, rows, _ = x3.shape
    nt = rows // tm
    tri = jnp.asarray(np.triu(np.ones((tm, tm), np.float32)), BF16)
    const = lambda b, t: (0, 0)
    npair = FOX_HEADS // 2
    ngroup = FOX_HEADS // FOX_GROUP
    return pl.pallas_call(
        _inproj_kernel,
        grid=(nb, nt),
        in_specs=[
            pl.BlockSpec((None, tm, D_MODEL), lambda b, t: (b, t, 0)),
            pl.BlockSpec((1, D_MODEL), const),
            pl.BlockSpec(w.shape, const),
            pl.BlockSpec((D_MODEL, GATE_ROWS), const),
            pl.BlockSpec((GATE_ROWS, LANES), const),
            pl.BlockSpec((tm, tm), const),
            pl.BlockSpec((GATE_ROWS, LANES), const),
        ],
        out_specs=[
            pl.BlockSpec((None, tm, OFF_VA), lambda b, t: (b, t, 0)),
            pl.BlockSpec((None, npair, PAIR_W, tm), lambda b, t: (b, 0, 0, t)),
            pl.BlockSpec((None, FOX_HEADS, tm, PAIR_W), lambda b, t: (b, 0, t, 0)),
            pl.BlockSpec((None, FOX_HEADS, FOX_VROWS, tm), lambda b, t: (b, 0, 0, t)),
            pl.BlockSpec((None, ngroup, FOX_GROUP, tm), lambda b, t: (b, 0, 0, t)),
            pl.BlockSpec((None, ngroup, FOX_GROUP, tm), lambda b, t: (b, 0, 0, t)),
            pl.BlockSpec((None, None, FOX_HEADS, LANES), lambda b, t: (b, t, 0, 0)),
            pl.BlockSpec((None, None, GATE_ROWS, LANES), lambda b, t: (b, t, 0, 0)),
        ],
        out_shape=[
            jax.ShapeDtypeStruct((nb, rows, OFF_VA), BF16),
            jax.ShapeDtypeStruct((nb, npair, PAIR_W, rows), BF16),
            jax.ShapeDtypeStruct((nb, FOX_HEADS, rows, PAIR_W), BF16),
            jax.ShapeDtypeStruct((nb, FOX_HEADS, FOX_VROWS, rows), BF16),
            jax.ShapeDtypeStruct((nb, ngroup, FOX_GROUP, rows), F32),
            jax.ShapeDtypeStruct((nb, ngroup, FOX_GROUP, rows), F32),
            jax.ShapeDtypeStruct((nb, nt, FOX_HEADS, LANES), F32),
            jax.ShapeDtypeStruct((nb, nt, GATE_ROWS, LANES), F32),
        ],
        scratch_shapes=[pltpu.VMEM((GATE_ROWS, LANES), F32)],
        compiler_params=pltpu.CompilerParams(
            dimension_semantics=("arbitrary", "arbitrary"), vmem_limit_bytes=VMEM_LIMIT),
        name="inproj",
    )(x3, g, w, wft, bft, tri, c0)


def _swa_scores(q, k_all, bias):
    nq = q.shape[0]
    zeros = jnp.zeros((HEAD_DIM, nq), F32)
    blocks = []
    for pair in range(SWA_Q_HEADS // 2):
        q_t = q[:, pair * PAIR_W:(pair + 1) * PAIR_W].astype(F32).T
        for hp in range(2):
            qh = q_t[hp * HEAD_DIM:(hp + 1) * HEAD_DIM]
            g = (2 * pair + hp) // SWA_GROUP
            blocks.append(jnp.concatenate([qh, zeros] if g == 0 else [zeros, qh], axis=0))
    q_bd = jnp.concatenate(blocks, axis=1).astype(BF16)
    return _dot(k_all, q_bd) + bias


def _swa_outputs(s, v_all, sink_ref):
    nq = s.shape[1] // SWA_Q_HEADS
    ps, sink_terms = [], []
    for h in range(SWA_Q_HEADS):
        sh = s[:, h * nq:(h + 1) * nq]
        sink = sink_ref[h] * LOG2E
        m = jnp.maximum(jnp.max(sh, axis=0, keepdims=True), sink)
        ps.append(jnp.exp2(sh - m).astype(BF16))
        sink_terms.append(jnp.exp2(sink - m))
    n = BLOCK
    v_t = jnp.concatenate([v_all[0:n].astype(F32).T, v_all[n:2 * n].astype(F32).T,
                           v_all[2 * n:].astype(F32).T], axis=1)
    ones = jnp.ones((SWA_ONES, v_t.shape[1]), F32)
    outs = []
    for g in range(SWA_KV_HEADS):
        v_aug = jnp.concatenate([v_t[g * HEAD_DIM:(g + 1) * HEAD_DIM], ones], axis=0).astype(BF16)
        p_g = jnp.concatenate(ps[g * SWA_GROUP:(g + 1) * SWA_GROUP], axis=1)
        o_g = _dot(v_aug, p_g)
        for j in range(SWA_GROUP):
            cols = slice(j * nq, (j + 1) * nq)
            l = o_g[HEAD_DIM:HEAD_DIM + 1, cols] + sink_terms[g * SWA_GROUP + j]
            outs.append(o_g[0:HEAD_DIM, cols] / l)
    pairs = [jnp.concatenate(outs[2 * p:2 * p + 2], axis=0).T for p in range(SWA_Q_HEADS // 2)]
    return jnp.concatenate(pairs, axis=1).astype(BF16)


def _swa_kernel(sink_ref, q_ref, kc_ref, kp_ref, vc_ref, vp_ref, km_ref, vm_ref, bias0_ref, bias1_ref, o_ref):
    n = BLOCK
    nsub = q_ref.shape[0] // n
    k_rows = [kp_ref[...]] + [kc_ref[c * n:(c + 1) * n] for c in range(nsub)]
    v_rows = [vp_ref[...]] + [vc_ref[c * n:(c + 1) * n] for c in range(nsub)]
    def block_scores(c):
        k_all = jnp.concatenate([k_rows[c], k_rows[c + 1], km_ref[...]], axis=0)
        bias = bias0_ref[...] if c == 0 else bias1_ref[...]
        return _swa_scores(q_ref[c * n:(c + 1) * n], k_all, bias)

    s_next = block_scores(0)
    for c in range(nsub):
        s_cur = s_next
        if c + 1 < nsub:
            s_next = block_scores(c + 1)
        v_all = jnp.concatenate([v_rows[c], v_rows[c + 1], vm_ref[...]], axis=0)
        o_ref[c * n:(c + 1) * n] = _swa_outputs(s_cur, v_all, sink_ref)


def _swa(proj, projm, bias, sinks):
    nb, rows, _ = proj.shape
    tq = SWA_TQ
    ratio = tq // BLOCK
    kcol, vcol = OFF_QA // SWA_KV_W, OFF_KA // SWA_KV_W
    cur = lambda c: (lambda b, n: (b, n, c))
    prev = lambda c: (lambda b, n: (b, jnp.maximum(n * ratio - 1, 0), c))
    bias_spec = lambda idx: pl.BlockSpec((None, SWA_KEYS, SWA_Q_HEADS * BLOCK), idx)
    return pl.pallas_call(
        _swa_kernel,
        grid=(nb, rows // tq),
        in_specs=[
            pl.BlockSpec(memory_space=pltpu.SMEM),
            pl.BlockSpec((None, tq, SWA_Q_W), lambda b, n: (b, n, 0)),
            pl.BlockSpec((None, tq, SWA_KV_W), cur(kcol)),
            pl.BlockSpec((None, BLOCK, SWA_KV_W), prev(kcol)),
            pl.BlockSpec((None, tq, SWA_KV_W), cur(vcol)),
            pl.BlockSpec((None, BLOCK, SWA_KV_W), prev(vcol)),
            pl.BlockSpec((None, N_META, SWA_KV_W), lambda b, n: (0, 0, kcol)),
            pl.BlockSpec((None, N_META, SWA_KV_W), lambda b, n: (0, 0, vcol)),
            bias_spec(lambda b, n: (jnp.minimum(n, 1), 0, 0)),
            bias_spec(lambda b, n: (1, 0, 0)),
        ],
        out_specs=pl.BlockSpec((None, tq, SWA_Q_W), lambda b, n: (b, n, 0)),
        out_shape=jax.ShapeDtypeStruct((nb, rows, SWA_Q_W), BF16),
        compiler_params=pltpu.CompilerParams(
            dimension_semantics=("arbitrary", "arbitrary"), vmem_limit_bytes=VMEM_LIMIT),
        name="swa",
    )(sinks, proj, proj, proj, proj, proj, projm, projm, bias, bias)


def _fox_kernel(qt_ref, ka_ref, vt_ref, kam_ref, vtm_ref, cq_ref, qn2_ref, kmx_ref, kmxm_ref,
                wo_ref, wgu_ref, wd_ref, o_ref, wo_bf_ref, wgu_bf_ref, wd_bf_ref, acc_scr):
    wo_bf_ref[...] = wo_ref[...].astype(BF16)
    wgu_bf_ref[...] = wgu_ref[...].astype(BF16)
    wd_bf_ref[...] = wd_ref[...].astype(BF16)

    i = pl.program_id(2)
    t = o_ref.shape[0]
    nh = acc_scr.shape[0]
    colmax = lambda x: jnp.max(x, axis=0, keepdims=True)
    tile_rows = lambda j: pl.ds(pl.multiple_of(j * t, t), t)

    heads = pl.ds(pl.program_id(1) * nh, nh)
    kmax2 = jnp.maximum(jnp.max(kmx_ref[:, heads, :], axis=0), kmxm_ref[heads, :])
    r = [jnp.sqrt(qn2_ref[h:h + 1, :] * kmax2[h:h + 1, 0:1]) * FOX_BOUND_MARGIN for h in range(nh)]
    worst = r[0]
    for rh in r[1:]:
        worst = jnp.maximum(worst, rh)
    safe = jnp.max(worst) <= FOX_SAFE_LOG2

    qt = []
    for h in range(nh):
        e = h % 2
        u = cq_ref[h:h + 1, :] * LOG2E - jnp.where(safe, r[h], 0.0)
        own = qt_ref[h // 2, e * HEAD_DIM:(e + 1) * HEAD_DIM, :].astype(F32)
        qt.append(_fox_head_t(own, _fox_aug8(u, False), e).astype(BF16))

    def scores(h, j, diagonal):
        s = _dot(ka_ref[h, tile_rows(j), :], qt[h])
        if diagonal:
            key = lax.broadcasted_iota(jnp.int32, (t, t), 0)
            qry = lax.broadcasted_iota(jnp.int32, (t, t), 1)
            s = jnp.where(key <= qry, s, NEG_INF)
        return s

    @pl.when(safe)
    def _():
        def run(work):
            prev = None
            for score_fn, pv_fn in work:
                s = score_fn()
                if prev is not None:
                    prev[0](prev[1])
                prev = (pv_fn, s)
            prev[0](prev[1])

        def tile_work(j):
            def pv(h):
                def apply(s):
                    acc_scr[h] += _dot(vt_ref[h, :, tile_rows(j)], jnp.exp2(s).astype(BF16))
                return apply
            return [(lambda h=h: scores(h, j, False), pv(h)) for h in range(nh)]

        def first_work():
            def sc(h):
                return jnp.concatenate([scores(h, i, True), _dot(kam_ref[h], qt[h])], axis=0)
            def pv(h):
                def apply(s):
                    v_t = jnp.concatenate([vt_ref[h, :, tile_rows(i)], vtm_ref[h, :, 0:N_META]], axis=1)
                    acc_scr[h] = _dot(v_t, jnp.exp2(s).astype(BF16))
                return apply
            return [(lambda h=h: sc(h), pv(h)) for h in range(nh)]

        run(first_work())

        def tile_pair(p, carry):
            run(tile_work(2 * p) + tile_work(2 * p + 1))
            return carry

        lax.fori_loop(0, i // 2, tile_pair, 0)

        @pl.when(i % 2 == 1)
        def _():
            run(tile_work(i - 1))

    @pl.when(jnp.logical_not(safe))
    def _():
        m0 = []
        for h in range(nh):
            s = _dot(kam_ref[h], qt[h])
            m = colmax(s)
            acc_scr[h] = _dot(vtm_ref[h, :, 0:N_META], jnp.exp2(s - m).astype(BF16))
            m0.append(m)

        def tile(j, ms, diagonal):
            out = []
            for h in range(nh):
                s = scores(h, j, diagonal)
                m_next = jnp.maximum(ms[h], colmax(s))
                p = jnp.exp2(s - m_next).astype(BF16)
                acc_scr[h] = jnp.exp2(ms[h] - m_next) * acc_scr[h] + _dot(vt_ref[h, :, tile_rows(j)], p)
                out.append(m_next)
            return tuple(out)

        ms = lax.fori_loop(0, i, lambda j, ms: tile(j, ms, False), tuple(m0))
        tile(i, ms, True)

    for pp in range(nh // 2):
        o_t = jnp.concatenate([acc_scr[h, 0:HEAD_DIM] / acc_scr[h, HEAD_DIM:HEAD_DIM + 1]
                               for h in (2 * pp, 2 * pp + 1)], axis=0)
        o_ref[:, pp * PAIR_W:(pp + 1) * PAIR_W] = o_t.T.astype(BF16)


def _fox(qt, ka, vt, ka_m, vt_m, cum, qn2, kmx, kmx_m, w_out, w_gate_up, w_down):
    nb, _, rows, _ = ka.shape
    t = FOX_T
    nh = FOX_GROUP
    ngroup = FOX_HEADS // nh
    nt = kmx.shape[1]
    nq = rows // t
    nsteps = nb * ngroup * nq
    slab = lambda w: min(r for r in range(SLAB_ALIGN, w.shape[0] + 1, SLAB_ALIGN)
                         if w.shape[0] % r == 0 and r * nsteps >= w.shape[0])
    step = lambda b, g, i: (b * ngroup + g) * nq + i
    w_spec = lambda w: pl.BlockSpec((slab(w), w.shape[1]),
                                    lambda b, g, i: (jnp.minimum(step(b, g, i), w.shape[0] // slab(w) - 1), 0))
    return pl.pallas_call(
        _fox_kernel,
        grid=(nb, ngroup, nq),
        in_specs=[
            pl.BlockSpec((None, nh // 2, PAIR_W, t), lambda b, g, i: (b, g, 0, i)),
            pl.BlockSpec((None, nh, rows, PAIR_W), lambda b, g, i: (b, g, 0, 0)),
            pl.BlockSpec((None, nh, FOX_VROWS, rows), lambda b, g, i: (b, g, 0, 0)),
            pl.BlockSpec((None, nh, N_META, PAIR_W), lambda b, g, i: (0, g, 0, 0)),
            pl.BlockSpec((None, nh, FOX_VROWS, META_TM), lambda b, g, i: (0, g, 0, 0)),
            pl.BlockSpec((None, None, nh, t), lambda b, g, i: (b, g, 0, i)),
            pl.BlockSpec((None, None, nh, t), lambda b, g, i: (b, g, 0, i)),
            pl.BlockSpec((None, nt, FOX_HEADS, LANES), lambda b, g, i: (b, 0, 0, 0)),
            pl.BlockSpec((None, None, FOX_HEADS, LANES), lambda b, g, i: (0, 0, 0, 0)),
            w_spec(w_out), w_spec(w_gate_up), w_spec(w_down),
        ],
        out_specs=[pl.BlockSpec((None, t, nh * HEAD_DIM), lambda b, g, i: (b, i, g)),
                   w_spec(w_out), w_spec(w_gate_up), w_spec(w_down)],
        out_shape=[jax.ShapeDtypeStruct((nb, rows, FOX_W), BF16)]
        + [jax.ShapeDtypeStruct(w.shape, BF16) for w in (w_out, w_gate_up, w_down)],
        scratch_shapes=[pltpu.VMEM((nh, FOX_VROWS, t), F32)],
        compiler_params=pltpu.CompilerParams(
            dimension_semantics=("arbitrary", "arbitrary", "arbitrary"), vmem_limit_bytes=VMEM_LIMIT),
        name="fox",
    )(qt, ka, vt, ka_m, vt_m, cum, qn2,
      kmx, kmx_m, w_out, w_gate_up, w_down)


def _ffn_kernel(oa_ref, ob_ref, x_ref, wo_ref, g1_ref, g2_ref, wg_ref, wu_ref, wd_ref, g3_ref, out_ref):
    tm = x_ref.shape[0]
    parts = [slice(c * tm // FFN_PARTS, (c + 1) * tm // FFN_PARTS) for c in range(FFN_PARTS)]
    mix = [jnp.concatenate([oa_ref[r, :], ob_ref[r, :]], axis=1) for r in parts]
    a = [_dot(m, wo_ref[...]) for m in mix]
    h1 = [x_ref[r, :] + _rms(ai, g1_ref[...]) for r, ai in zip(parts, a)]
    hn = [_rms(h, g2_ref[...]).astype(BF16) for h in h1]
    gate_up = [(_dot(h, wg_ref[...]), _dot(h, wu_ref[...])) for h in hn]
    act = [(g / (1.0 + jnp.exp(-g)) * u).astype(BF16) for g, u in gate_up]
    ff = [_dot(ac, wd_ref[...]) for ac in act]
    for r, h, f in zip(parts, h1, ff):
        out_ref[r, :] = h + _rms(f, g3_ref[...])


def _ffn(o_a, o_b, x3, wo, g1, g2, wgu, wd, g3):
    nb, rows, _ = x3.shape
    tm = FFN_TM
    const = lambda b, t: (0, 0)
    resident = lambda shape: pl.BlockSpec(shape, const, pipeline_mode=pl.Buffered(1))
    row = lambda w: pl.BlockSpec((None, tm, w), lambda b, t: (b, t, 0))
    return pl.pallas_call(
        _ffn_kernel,
        grid=(nb, rows // tm),
        in_specs=[
            row(SWA_Q_W), row(FOX_W), row(D_MODEL),
            resident((D_MODEL, D_MODEL)), resident((1, D_MODEL)), resident((1, D_MODEL)),
            pl.BlockSpec((D_MODEL, D_FF), lambda b, t: (0, 0), pipeline_mode=pl.Buffered(1)),
            pl.BlockSpec((D_MODEL, D_FF), lambda b, t: (0, 1), pipeline_mode=pl.Buffered(1)),
            resident((D_FF, D_MODEL)),
            resident((1, D_MODEL)),
        ],
        out_specs=row(D_MODEL),
        out_shape=jax.ShapeDtypeStruct((nb, rows, D_MODEL), F32),
        compiler_params=pltpu.CompilerParams(
            dimension_semantics=("arbitrary", "arbitrary"), vmem_limit_bytes=VMEM_LIMIT),
        name="outproj_ffn",
    )(o_a, o_b, x3, wo, g1, g2, wgu, wgu, wd, g3)


def kernel(x, meta_tokens, rel_bias, ln_pre_mix, ln_post_mix, ln_pre_ffn, ln_post_ffn,
           w_in, b_forget, sinks, w_out, w_gate_up, w_down):
    nb, seq, d = x.shape
    assert d == D_MODEL and seq % IN_TM == 0 and seq % FOX_T == 0 and seq % FFN_TM == 0 and seq % SWA_TQ == 0
    assert w_in.shape[0] == 1, "single-layer block"
    assert meta_tokens.shape == (N_META, D_MODEL)
    x = x.astype(F32)

    w_all = w_in[0].astype(BF16)
    pad_gate = GATE_ROWS - FOX_HEADS
    w_ft = jnp.pad(w_in[0, :, OFF_VB:], ((0, 0), (0, pad_gate))).astype(BF16)
    lane_rep = lambda v: jnp.broadcast_to(jnp.pad(v.astype(F32), (0, pad_gate))[:, None], (GATE_ROWS, LANES))
    b_ft = lane_rep(b_forget[0])
    g_pre = ln_pre_mix[0].astype(F32).reshape(1, D_MODEL)

    swa_bias = _bias_tables(rel_bias)

    x_m = jnp.pad(meta_tokens.astype(F32), ((0, META_TM - N_META), (0, 0)))[None]
    pa_m, _, ka_m, vt_m, _, _, kmx_m, c_meta = _inproj(x_m, g_pre, w_all, w_ft, b_ft,
                                                       jnp.zeros((GATE_ROWS, LANES), F32), META_TM)
    pa, qt, ka, vt, cum, qn2, kmx, _ = _inproj(x, g_pre, w_all, w_ft, b_ft, c_meta[0, 0], IN_TM)

    o_a = _swa(pa, pa_m, swa_bias, sinks[0].astype(F32))
    o_b, w_out_bf, w_gu_bf, w_down_bf = _fox(qt, ka, vt, ka_m, vt_m, cum, qn2, kmx,
                                             kmx_m, w_out[0].astype(F32), w_gate_up[0].astype(F32),
                                             w_down[0].astype(F32))

    return _ffn(o_a, o_b, x,
                w_out_bf,
                ln_post_mix[0].astype(F32).reshape(1, D_MODEL),
                ln_pre_ffn[0].astype(F32).reshape(1, D_MODEL),
                w_gu_bf,
                w_down_bf,
                ln_post_ffn[0].astype(F32).reshape(1, D_MODEL))
```

```python
import math

import numpy as np
import jax
import jax.numpy as jnp
from jax import lax
from jax.experimental import pallas as pl
from jax.experimental.pallas import tpu as pltpu

D_MODEL = 1024
N_META = 16
HEAD_DIM = 64
SWA_Q_HEADS = 8
SWA_KV_HEADS = 2
SWA_GROUP = SWA_Q_HEADS // SWA_KV_HEADS
FOX_HEADS = 8
SWA_Q_W = SWA_Q_HEADS * HEAD_DIM
SWA_KV_W = SWA_KV_HEADS * HEAD_DIM
FOX_W = FOX_HEADS * HEAD_DIM
OFF_QA = SWA_Q_W
OFF_KA = OFF_QA + SWA_KV_W
OFF_VA = OFF_KA + SWA_KV_W
OFF_QB = OFF_VA + FOX_W
OFF_KB = OFF_QB + FOX_W
OFF_VB = OFF_KB + FOX_W
WINDOW = 128
BLOCK = 128
N_BUCKETS = 32
MAX_DISTANCE = 128
D_FF = 2816
EPS = 1e-6
NEG_INF = -1e30
SCALE = HEAD_DIM ** -0.5
LOG2E = math.log2(math.e)

LANES = 128
SUBLANES = 8
PAIR_W = 2 * HEAD_DIM
VMEM_LIMIT = 56 * 1024 * 1024

SWA_KEYS = 2 * BLOCK + N_META
SWA_TQ = 2048
SWA_ONES = 16
IN_TM = 512
META_TM = 128
GATE_ROWS = 16
FOX_VROWS = HEAD_DIM + 16
FOX_T = 512
FOX_GROUP = 4
FOX_SAFE_LOG2 = -1.0
FOX_BOUND_MARGIN = 1.001
FFN_TM = 512
FFN_PARTS = 2
SLAB_ALIGN = 16
N_PIECES = 3

F32 = jnp.float32
BF16 = jnp.bfloat16


def _dot(a, b):
    return jnp.dot(a, b, preferred_element_type=F32)


def _dot_nt(a, b):
    return lax.dot_general(a, b, (((1,), (1,)), ((), ())), preferred_element_type=F32)


def _rms(t, g):
    return t * lax.rsqrt(jnp.mean(t * t, axis=-1, keepdims=True) + EPS) * g


def _bf16_pieces(v):
    pieces = []
    for _ in range(N_PIECES - 1):
        p = v.astype(BF16)
        pieces.append(p)
        v = v - p.astype(F32)
    pieces.append(v.astype(BF16))
    return pieces


def _t5_bucket_np(dist):
    n = np.maximum(dist, 0).astype(np.int32)
    max_exact = N_BUCKETS // 2
    nf = np.maximum(n, 1).astype(np.float32)
    large = max_exact + (np.log(nf / np.float32(max_exact)) / np.float32(math.log(MAX_DISTANCE / max_exact))
                         * np.float32(N_BUCKETS - max_exact)).astype(np.int32)
    large = np.minimum(large, N_BUCKETS - 1)
    return np.where(n < max_exact, n, large).astype(np.int32)


def _bias_kernel(tab_ref, bkt_ref, valid_ref, out_ref):
    bkt = bkt_ref[...]
    valid = valid_ref[...] > 0
    for h in range(SWA_Q_HEADS):
        acc = jnp.zeros(bkt.shape, F32)
        for b in range(N_BUCKETS):
            acc = jnp.where(bkt == b, tab_ref[b, h], acc)
        out_ref[:, h * BLOCK:(h + 1) * BLOCK] = jnp.where(valid, acc * LOG2E, NEG_INF)


def _bias_tables(rel_bias):
    ki = np.arange(2 * BLOCK)[:, None]
    qi = np.arange(BLOCK)[None, :]
    d_w = qi + BLOCK - ki
    bkt_w = _t5_bucket_np(d_w)
    in_window = (d_w >= 0) & (d_w < WINDOW)
    mi = np.arange(N_META)[:, None]
    bkt_m0 = _t5_bucket_np(N_META + qi - mi)
    far = _t5_bucket_np(np.asarray([[N_META + BLOCK - (N_META - 1)]]))
    assert far[0, 0] == N_BUCKETS - 1, "meta keys of later blocks must share the last bucket"
    bkt = np.stack([np.concatenate([bkt_w, bkt_m0]), np.concatenate([bkt_w, np.full_like(bkt_m0, far[0, 0])])])
    meta_ok = np.ones((N_META, BLOCK), bool)
    valid = np.stack([np.concatenate([in_window & (ki >= BLOCK), meta_ok]), np.concatenate([in_window, meta_ok])])
    spec = pl.BlockSpec((None, SWA_KEYS, BLOCK), lambda a: (a, 0, 0))
    return pl.pallas_call(
        _bias_kernel,
        grid=(2,),
        in_specs=[pl.BlockSpec(memory_space=pltpu.SMEM), spec, spec],
        out_specs=pl.BlockSpec((None, SWA_KEYS, SWA_Q_HEADS * BLOCK), lambda a: (a, 0, 0)),
        out_shape=jax.ShapeDtypeStruct((2, SWA_KEYS, SWA_Q_HEADS * BLOCK), F32),
        name="bias_tables",
    )(rel_bias.astype(F32), jnp.asarray(bkt.astype(np.int32)), jnp.asarray(valid.astype(np.int32)))


def _fox_aug8(val_row, pieces_first):
    n = val_row.shape[1]
    row = lax.broadcasted_iota(jnp.int32, (SUBLANES, n), 0)
    p0, o0 = (0, N_PIECES) if pieces_first else (N_PIECES, 0)
    aug = jnp.where((row >= o0) & (row < o0 + N_PIECES), 1.0, 0.0)
    for i, piece in enumerate(_bf16_pieces(val_row)):
        aug = jnp.where(row == p0 + i, piece.astype(F32), aug)
    return aug


def _fox_head_t(own_t, aug8, e):
    pad = jnp.zeros((HEAD_DIM - SUBLANES, own_t.shape[1]), F32)
    return jnp.concatenate([own_t, aug8, pad] if e == 0 else [aug8, pad, own_t], axis=0)


def _inproj_kernel(x_ref, g_ref, w_ref, wf_ref, bft_ref, tri_ref, c0_ref,
                   pa_ref, qt_ref, ka_ref, vt_ref, cum_ref, qn2_ref, kmx_ref, cmeta_ref, carry_ref):
    @pl.when(pl.program_id(1) == 0)
    def _():
        carry_ref[...] = c0_ref[...]

    tm = x_ref.shape[0]
    y = _rms(x_ref[...], g_ref[...]).astype(BF16)

    tile_lanes = lambda a: jnp.concatenate([a] * (tm // LANES), axis=1)
    f_t = lax.dot_general(wf_ref[...], y, (((0,), (1,)), ((), ())), preferred_element_type=F32) + tile_lanes(bft_ref[...])
    acc_fox = _dot(y, w_ref[:, OFF_VA:OFF_VB])
    ls_t = jnp.minimum(f_t, 0.0) - jnp.log1p(jnp.exp(-jnp.abs(f_t)))
    cum_t = tile_lanes(carry_ref[...])
    for piece in _bf16_pieces(ls_t):
        cum_t = cum_t + _dot(piece, tri_ref[...])
    carry_ref[...] = jnp.broadcast_to(cum_t[:, tm - 1:tm], carry_ref.shape)
    cmeta_ref[...] = jnp.broadcast_to(cum_t[:, N_META - 1:N_META], cmeta_ref.shape)
    for g in range(FOX_HEADS // FOX_GROUP):
        cum_ref[g] = cum_t[g * FOX_GROUP:(g + 1) * FOX_GROUP]
    cb_t = cum_t * (-LOG2E)

    acc_swa = _dot(y, w_ref[:, 0:OFF_VA])
    pa_ref[:, 0:OFF_QA] = (acc_swa[:, 0:OFF_QA] * (SCALE * LOG2E)).astype(BF16)
    pa_ref[:, OFF_QA:OFF_VA] = acc_swa[:, OFF_QA:OFF_VA].astype(BF16)

    ones = jnp.ones((FOX_VROWS - HEAD_DIM, tm), F32)
    for p in range(FOX_HEADS // 2):
        cols = lambda off: slice(off - OFF_VA + p * PAIR_W, off - OFF_VA + (p + 1) * PAIR_W)
        q_t = (acc_fox[:, cols(OFF_VA)] * (SCALE * LOG2E)).astype(BF16).astype(F32).T
        k_t = acc_fox[:, cols(OFF_QB)].astype(BF16).astype(F32).T
        v_t = acc_fox[:, cols(OFF_KB)].astype(BF16).astype(F32).T
        qt_ref[p] = q_t.astype(BF16)
        for e in range(2):
            h = 2 * p + e
            own = slice(e * HEAD_DIM, (e + 1) * HEAD_DIM)
            qn2_ref[h // FOX_GROUP, h % FOX_GROUP:h % FOX_GROUP + 1, :] = jnp.sum(q_t[own] * q_t[own], axis=0, keepdims=True)
            kn2 = jnp.sum(k_t[own] * k_t[own], axis=0, keepdims=True)
            kmx_ref[h:h + 1, :] = jnp.broadcast_to(jnp.max(kn2, axis=1, keepdims=True), (1, LANES))
            ka_ref[h] = _fox_head_t(k_t[own], _fox_aug8(cb_t[h:h + 1], True), e).T.astype(BF16)
            vt_ref[h] = jnp.concatenate([v_t[own], ones], axis=0).astype(BF16)


def _inproj(x3, g, w, wft, bft, c0, tm):
    nb, rows, _ = x3.shape
    nt = rows // tm
    tri = jnp.asarray(np.triu(np.ones((tm, tm), np.float32)), BF16)
    const = lambda b, t: (0, 0)
    npair = FOX_HEADS // 2
    ngroup = FOX_HEADS // FOX_GROUP
    return pl.pallas_call(
        _inproj_kernel,
        grid=(nb, nt),
        in_specs=[
            pl.BlockSpec((None, tm, D_MODEL), lambda b, t: (b, t, 0)),
            pl.BlockSpec((1, D_MODEL), const),
            pl.BlockSpec(w.shape, const),
            pl.BlockSpec((D_MODEL, GATE_ROWS), const),
            pl.BlockSpec((GATE_ROWS, LANES), const),
            pl.BlockSpec((tm, tm), const),
            pl.BlockSpec((GATE_ROWS, LANES), const),
        ],
        out_specs=[
            pl.BlockSpec((None, tm, OFF_VA), lambda b, t: (b, t, 0)),
            pl.BlockSpec((None, npair, PAIR_W, tm), lambda b, t: (b, 0, 0, t)),
            pl.BlockSpec((None, FOX_HEADS, tm, PAIR_W), lambda b, t: (b, 0, t, 0)),
            pl.BlockSpec((None, FOX_HEADS, FOX_VROWS, tm), lambda b, t: (b, 0, 0, t)),
            pl.BlockSpec((None, ngroup, FOX_GROUP, tm), lambda b, t: (b, 0, 0, t)),
            pl.BlockSpec((None, ngroup, FOX_GROUP, tm), lambda b, t: (b, 0, 0, t)),
            pl.BlockSpec((None, None, FOX_HEADS, LANES), lambda b, t: (b, t, 0, 0)),
            pl.BlockSpec((None, None, GATE_ROWS, LANES), lambda b, t: (b, t, 0, 0)),
        ],
        out_shape=[
            jax.ShapeDtypeStruct((nb, rows, OFF_VA), BF16),
            jax.ShapeDtypeStruct((nb, npair, PAIR_W, rows), BF16),
            jax.ShapeDtypeStruct((nb, FOX_HEADS, rows, PAIR_W), BF16),
            jax.ShapeDtypeStruct((nb, FOX_HEADS, FOX_VROWS, rows), BF16),
            jax.ShapeDtypeStruct((nb, ngroup, FOX_GROUP, rows), F32),
            jax.ShapeDtypeStruct((nb, ngroup, FOX_GROUP, rows), F32),
            jax.ShapeDtypeStruct((nb, nt, FOX_HEADS, LANES), F32),
            jax.ShapeDtypeStruct((nb, nt, GATE_ROWS, LANES), F32),
        ],
        scratch_shapes=[pltpu.VMEM((GATE_ROWS, LANES), F32)],
        compiler_params=pltpu.CompilerParams(
            dimension_semantics=("arbitrary", "arbitrary"), vmem_limit_bytes=VMEM_LIMIT),
        name="inproj",
    )(x3, g, w, wft, bft, tri, c0)


def _swa_scores(q, k_all, bias):
    nq = q.shape[0]
    zeros = jnp.zeros((HEAD_DIM, nq), F32)
    blocks = []
    for pair in range(SWA_Q_HEADS // 2):
        q_t = q[:, pair * PAIR_W:(pair + 1) * PAIR_W].astype(F32).T
        for hp in range(2):
            qh = q_t[hp * HEAD_DIM:(hp + 1) * HEAD_DIM]
            g = (2 * pair + hp) // SWA_GROUP
            blocks.append(jnp.concatenate([qh, zeros] if g == 0 else [zeros, qh], axis=0))
    q_bd = jnp.concatenate(blocks, axis=1).astype(BF16)
    return _dot(k_all, q_bd) + bias


def _swa_outputs(s, v_all, sink_ref):
    nq = s.shape[1] // SWA_Q_HEADS
    ps, sink_terms = [], []
    for h in range(SWA_Q_HEADS):
        sh = s[:, h * nq:(h + 1) * nq]
        sink = sink_ref[h] * LOG2E
        m = jnp.maximum(jnp.max(sh, axis=0, keepdims=True), sink)
        ps.append(jnp.exp2(sh - m).astype(BF16))
        sink_terms.append(jnp.exp2(sink - m))
    n = BLOCK
    v_t = jnp.concatenate([v_all[0:n].astype(F32).T, v_all[n:2 * n].astype(F32).T,
                           v_all[2 * n:].astype(F32).T], axis=1)
    ones = jnp.ones((SWA_ONES, v_t.shape[1]), F32)
    outs = []
    for g in range(SWA_KV_HEADS):
        v_aug = jnp.concatenate([v_t[g * HEAD_DIM:(g + 1) * HEAD_DIM], ones], axis=0).astype(BF16)
        p_g = jnp.concatenate(ps[g * SWA_GROUP:(g + 1) * SWA_GROUP], axis=1)
        o_g = _dot(v_aug, p_g)
        for j in range(SWA_GROUP):
            cols = slice(j * nq, (j + 1) * nq)
            l = o_g[HEAD_DIM:HEAD_DIM + 1, cols] + sink_terms[g * SWA_GROUP + j]
            outs.append(o_g[0:HEAD_DIM, cols] / l)
    pairs = [jnp.concatenate(outs[2 * p:2 * p + 2], axis=0).T for p in range(SWA_Q_HEADS // 2)]
    return jnp.concatenate(pairs, axis=1).astype(BF16)


def _swa_kernel(sink_ref, q_ref, kc_ref, kp_ref, vc_ref, vp_ref, km_ref, vm_ref, bias0_ref, bias1_ref, o_ref):
    n = BLOCK
    nsub = q_ref.shape[0] // n
    k_rows = [kp_ref[...]] + [kc_ref[c * n:(c + 1) * n] for c in range(nsub)]
    v_rows = [vp_ref[...]] + [vc_ref[c * n:(c + 1) * n] for c in range(nsub)]
    def block_scores(c):
        k_all = jnp.concatenate([k_rows[c], k_rows[c + 1], km_ref[...]], axis=0)
        bias = bias0_ref[...] if c == 0 else bias1_ref[...]
        return _swa_scores(q_ref[c * n:(c + 1) * n], k_all, bias)

    s_next = block_scores(0)
    for c in range(nsub):
        s_cur = s_next
        if c + 1 < nsub:
            s_next = block_scores(c + 1)
        v_all = jnp.concatenate([v_rows[c], v_rows[c + 1], vm_ref[...]], axis=0)
        o_ref[c * n:(c + 1) * n] = _swa_outputs(s_cur, v_all, sink_ref)


def _swa(proj, projm, bias, sinks):
    nb, rows, _ = proj.shape
    tq = SWA_TQ
    ratio = tq // BLOCK
    kcol, vcol = OFF_QA // SWA_KV_W, OFF_KA // SWA_KV_W
    cur = lambda c: (lambda b, n: (b, n, c))
    prev = lambda c: (lambda b, n: (b, jnp.maximum(n * ratio - 1, 0), c))
    bias_spec = lambda idx: pl.BlockSpec((None, SWA_KEYS, SWA_Q_HEADS * BLOCK), idx)
    return pl.pallas_call(
        _swa_kernel,
        grid=(nb, rows // tq),
        in_specs=[
            pl.BlockSpec(memory_space=pltpu.SMEM),
            pl.BlockSpec((None, tq, SWA_Q_W), lambda b, n: (b, n, 0)),
            pl.BlockSpec((None, tq, SWA_KV_W), cur(kcol)),
            pl.BlockSpec((None, BLOCK, SWA_KV_W), prev(kcol)),
            pl.BlockSpec((None, tq, SWA_KV_W), cur(vcol)),
            pl.BlockSpec((None, BLOCK, SWA_KV_W), prev(vcol)),
            pl.BlockSpec((None, N_META, SWA_KV_W), lambda b, n: (0, 0, kcol)),
            pl.BlockSpec((None, N_META, SWA_KV_W), lambda b, n: (0, 0, vcol)),
            bias_spec(lambda b, n: (jnp.minimum(n, 1), 0, 0)),
            bias_spec(lambda b, n: (1, 0, 0)),
        ],
        out_specs=pl.BlockSpec((None, tq, SWA_Q_W), lambda b, n: (b, n, 0)),
        out_shape=jax.ShapeDtypeStruct((nb, rows, SWA_Q_W), BF16),
        compiler_params=pltpu.CompilerParams(
            dimension_semantics=("arbitrary", "arbitrary"), vmem_limit_bytes=VMEM_LIMIT),
        name="swa",
    )(sinks, proj, proj, proj, proj, proj, projm, projm, bias, bias)


def _fox_kernel(qt_ref, ka_ref, vt_ref, kam_ref, vtm_ref, cq_ref, qn2_ref, kmx_ref, kmxm_ref,
                wo_ref, wgu_ref, wd_ref, o_ref, wo_bf_ref, wgu_bf_ref, wd_bf_ref, acc_scr):
    wo_bf_ref[...] = wo_ref[...].astype(BF16)
    wgu_bf_ref[...] = wgu_ref[...].astype(BF16)
    wd_bf_ref[...] = wd_ref[...].astype(BF16)

    i = pl.program_id(2)
    t = o_ref.shape[0]
    nh = acc_scr.shape[0]
    colmax = lambda x: jnp.max(x, axis=0, keepdims=True)
    tile_rows = lambda j: pl.ds(pl.multiple_of(j * t, t), t)

    heads = pl.ds(pl.program_id(1) * nh, nh)
    kmax2 = jnp.maximum(jnp.max(kmx_ref[:, heads, :], axis=0), kmxm_ref[heads, :])
    r = [jnp.sqrt(qn2_ref[h:h + 1, :] * kmax2[h:h + 1, 0:1]) * FOX_BOUND_MARGIN for h in range(nh)]
    worst = r[0]
    for rh in r[1:]:
        worst = jnp.maximum(worst, rh)
    safe = jnp.max(worst) <= FOX_SAFE_LOG2

    qt = []
    for h in range(nh):
        e = h % 2
        u = cq_ref[h:h + 1, :] * LOG2E - jnp.where(safe, r[h], 0.0)
        own = qt_ref[h // 2, e * HEAD_DIM:(e + 1) * HEAD_DIM, :].astype(F32)
        qt.append(_fox_head_t(own, _fox_aug8(u, False), e).astype(BF16))

    def scores(h, j, diagonal):
        s = _dot(ka_ref[h, tile_rows(j), :], qt[h])
        if diagonal:
            key = lax.broadcasted_iota(jnp.int32, (t, t), 0)
            qry = lax.broadcasted_iota(jnp.int32, (t, t), 1)
            s = jnp.where(key <= qry, s, NEG_INF)
        return s

    @pl.when(safe)
    def _():
        def run(work):
            prev = None
            for score_fn, pv_fn in work:
                s = score_fn()
                if prev is not None:
                    prev[0](prev[1])
                prev = (pv_fn, s)
            prev[0](prev[1])

        def tile_work(j):
            def pv(h):
                def apply(s):
                    acc_scr[h] += _dot(vt_ref[h, :, tile_rows(j)], jnp.exp2(s).astype(BF16))
                return apply
            return [(lambda h=h: scores(h, j, False), pv(h)) for h in range(nh)]

        def first_work():
            def sc(h):
                return jnp.concatenate([scores(h, i, True), _dot(kam_ref[h], qt[h])], axis=0)
            def pv(h):
                def apply(s):
                    v_t = jnp.concatenate([vt_ref[h, :, tile_rows(i)], vtm_ref[h, :, 0:N_META]], axis=1)
                    acc_scr[h] = _dot(v_t, jnp.exp2(s).astype(BF16))
                return apply
            return [(lambda h=h: sc(h), pv(h)) for h in range(nh)]

        run(first_work())

        def tile_pair(p, carry):
            run(tile_work(2 * p) + tile_work(2 * p + 1))
            return carry

        lax.fori_loop(0, i // 2, tile_pair, 0)

        @pl.when(i % 2 == 1)
        def _():
            run(tile_work(i - 1))

    @pl.when(jnp.logical_not(safe))
    def _():
        m0 = []
        for h in range(nh):
            s = _dot(kam_ref[h], qt[h])
            m = colmax(s)
            acc_scr[h] = _dot(vtm_ref[h, :, 0:N_META], jnp.exp2(s - m).astype(BF16))
            m0.append(m)

        def tile(j, ms, diagonal):
            out = []
            for h in range(nh):
                s = scores(h, j, diagonal)
                m_next = jnp.maximum(ms[h], colmax(s))
                p = jnp.exp2(s - m_next).astype(BF16)
                acc_scr[h] = jnp.exp2(ms[h] - m_next) * acc_scr[h] + _dot(vt_ref[h, :, tile_rows(j)], p)
                out.append(m_next)
            return tuple(out)

        ms = lax.fori_loop(0, i, lambda j, ms: tile(j, ms, False), tuple(m0))
        tile(i, ms, True)

    for pp in range(nh // 2):
        o_t = jnp.concatenate([acc_scr[h, 0:HEAD_DIM] / acc_scr[h, HEAD_DIM:HEAD_DIM + 1]
                               for h in (2 * pp, 2 * pp + 1)], axis=0)
        o_ref[:, pp * PAIR_W:(pp + 1) * PAIR_W] = o_t.T.astype(BF16)


def _fox(qt, ka, vt, ka_m, vt_m, cum, qn2, kmx, kmx_m, w_out, w_gate_up, w_down):
    nb, _, rows, _ = ka.shape
    t = FOX_T
    nh = FOX_GROUP
    ngroup = FOX_HEADS // nh
    nt = kmx.shape[1]
    nq = rows // t
    nsteps = nb * ngroup * nq
    slab = lambda w: min(r for r in range(SLAB_ALIGN, w.shape[0] + 1, SLAB_ALIGN)
                         if w.shape[0] % r == 0 and r * nsteps >= w.shape[0])
    step = lambda b, g, i: (b * ngroup + g) * nq + i
    w_spec = lambda w: pl.BlockSpec((slab(w), w.shape[1]),
                                    lambda b, g, i: (jnp.minimum(step(b, g, i), w.shape[0] // slab(w) - 1), 0))
    return pl.pallas_call(
        _fox_kernel,
        grid=(nb, ngroup, nq),
        in_specs=[
            pl.BlockSpec((None, nh // 2, PAIR_W, t), lambda b, g, i: (b, g, 0, i)),
            pl.BlockSpec((None, nh, rows, PAIR_W), lambda b, g, i: (b, g, 0, 0)),
            pl.BlockSpec((None, nh, FOX_VROWS, rows), lambda b, g, i: (b, g, 0, 0)),
            pl.BlockSpec((None, nh, N_META, PAIR_W), lambda b, g, i: (0, g, 0, 0)),
            pl.BlockSpec((None, nh, FOX_VROWS, META_TM), lambda b, g, i: (0, g, 0, 0)),
            pl.BlockSpec((None, None, nh, t), lambda b, g, i: (b, g, 0, i)),
            pl.BlockSpec((None, None, nh, t), lambda b, g, i: (b, g, 0, i)),
            pl.BlockSpec((None, nt, FOX_HEADS, LANES), lambda b, g, i: (b, 0, 0, 0)),
            pl.BlockSpec((None, None, FOX_HEADS, LANES), lambda b, g, i: (0, 0, 0, 0)),
            w_spec(w_out), w_spec(w_gate_up), w_spec(w_down),
        ],
        out_specs=[pl.BlockSpec((None, t, nh * HEAD_DIM), lambda b, g, i: (b, i, g)),
                   w_spec(w_out), w_spec(w_gate_up), w_spec(w_down)],
        out_shape=[jax.ShapeDtypeStruct((nb, rows, FOX_W), BF16)]
        + [jax.ShapeDtypeStruct(w.shape, BF16) for w in (w_out, w_gate_up, w_down)],
        scratch_shapes=[pltpu.VMEM((nh, FOX_VROWS, t), F32)],
        compiler_params=pltpu.CompilerParams(
            dimension_semantics=("arbitrary", "arbitrary", "arbitrary"), vmem_limit_bytes=VMEM_LIMIT),
        name="fox",
    )(qt, ka, vt, ka_m, vt_m, cum, qn2,
      kmx, kmx_m, w_out, w_gate_up, w_down)


def _ffn_kernel(oa_ref, ob_ref, x_ref, wo_ref, g1_ref, g2_ref, wg_ref, wu_ref, wd_ref, g3_ref, out_ref):
    tm = x_ref.shape[0]
    parts = [slice(c * tm // FFN_PARTS, (c + 1) * tm // FFN_PARTS) for c in range(FFN_PARTS)]
    mix = [jnp.concatenate([oa_ref[r, :], ob_ref[r, :]], axis=1) for r in parts]
    a = [_dot(m, wo_ref[...]) for m in mix]
    h1 = [x_ref[r, :] + _rms(ai, g1_ref[...]) for r, ai in zip(parts, a)]
    hn = [_rms(h, g2_ref[...]).astype(BF16) for h in h1]
    gate_up = [(_dot(h, wg_ref[...]), _dot(h, wu_ref[...])) for h in hn]
    act = [(g / (1.0 + jnp.exp(-g)) * u).astype(BF16) for g, u in gate_up]
    ff = [_dot(ac, wd_ref[...]) for ac in act]
    for r, h, f in zip(parts, h1, ff):
        out_ref[r, :] = h + _rms(f, g3_ref[...])


def _ffn(o_a, o_b, x3, wo, g1, g2, wgu, wd, g3):
    nb, rows, _ = x3.shape
    tm = FFN_TM
    const = lambda b, t: (0, 0)
    resident = lambda shape: pl.BlockSpec(shape, const, pipeline_mode=pl.Buffered(1))
    row = lambda w: pl.BlockSpec((None, tm, w), lambda b, t: (b, t, 0))
    return pl.pallas_call(
        _ffn_kernel,
        grid=(nb, rows // tm),
        in_specs=[
            row(SWA_Q_W), row(FOX_W), row(D_MODEL),
            resident((D_MODEL, D_MODEL)), resident((1, D_MODEL)), resident((1, D_MODEL)),
            pl.BlockSpec((D_MODEL, D_FF), lambda b, t: (0, 0), pipeline_mode=pl.Buffered(1)),
            pl.BlockSpec((D_MODEL, D_FF), lambda b, t: (0, 1), pipeline_mode=pl.Buffered(1)),
            resident((D_FF, D_MODEL)),
            resident((1, D_MODEL)),
        ],
        out_specs=row(D_MODEL),
        out_shape=jax.ShapeDtypeStruct((nb, rows, D_MODEL), F32),
        compiler_params=pltpu.CompilerParams(
            dimension_semantics=("arbitrary", "arbitrary"), vmem_limit_bytes=VMEM_LIMIT),
        name="outproj_ffn",
    )(o_a, o_b, x3, wo, g1, g2, wgu, wgu, wd, g3)


def kernel(x, meta_tokens, rel_bias, ln_pre_mix, ln_post_mix, ln_pre_ffn, ln_post_ffn,
           w_in, b_forget, sinks, w_out, w_gate_up, w_down):
    nb, seq, d = x.shape
    assert d == D_MODEL and seq % IN_TM == 0 and seq % FOX_T == 0 and seq % FFN_TM == 0 and seq % SWA_TQ == 0
    assert w_in.shape[0] == 1, "single-layer block"
    assert meta_tokens.shape == (N_META, D_MODEL)
    x = x.astype(F32)

    w_all = w_in[0].astype(BF16)
    pad_gate = GATE_ROWS - FOX_HEADS
    w_ft = jnp.pad(w_in[0, :, OFF_VB:], ((0, 0), (0, pad_gate))).astype(BF16)
    lane_rep = lambda v: jnp.broadcast_to(jnp.pad(v.astype(F32), (0, pad_gate))[:, None], (GATE_ROWS, LANES))
    b_ft = lane_rep(b_forget[0])
    g_pre = ln_pre_mix[0].astype(F32).reshape(1, D_MODEL)

    swa_bias = _bias_tables(rel_bias)

    x_m = jnp.pad(meta_tokens.astype(F32), ((0, META_TM - N_META), (0, 0)))[None]
    pa_m, _, ka_m, vt_m, _, _, kmx_m, c_meta = _inproj(x_m, g_pre, w_all, w_ft, b_ft,
                                                       jnp.zeros((GATE_ROWS, LANES), F32), META_TM)
    pa, qt, ka, vt, cum, qn2, kmx, _ = _inproj(x, g_pre, w_all, w_ft, b_ft, c_meta[0, 0], IN_TM)

    o_a = _swa(pa, pa_m, swa_bias, sinks[0].astype(F32))
    o_b, w_out_bf, w_gu_bf, w_down_bf = _fox(qt, ka, vt, ka_m, vt_m, cum, qn2, kmx,
                                             kmx_m, w_out[0].astype(F32), w_gate_up[0].astype(F32),
                                             w_down[0].astype(F32))

    return _ffn(o_a, o_b, x,
                w_out_bf,
                ln_post_mix[0].astype(F32).reshape(1, D_MODEL),
                ln_pre_ffn[0].astype(F32).reshape(1, D_MODEL),
                w_gu_bf,
                w_down_bf,
                ln_post_ffn[0].astype(F32).reshape(1, D_MODEL))
```

```python
import math

import numpy as np
import jax
import jax.numpy as jnp
from jax import lax
from jax.experimental import pallas as pl
from jax.experimental.pallas import tpu as pltpu

D_MODEL = 1024
N_META = 16
HEAD_DIM = 64
SWA_Q_HEADS = 8
SWA_KV_HEADS = 2
SWA_GROUP = SWA_Q_HEADS // SWA_KV_HEADS
FOX_HEADS = 8
SWA_Q_W = SWA_Q_HEADS * HEAD_DIM
SWA_KV_W = SWA_KV_HEADS * HEAD_DIM
FOX_W = FOX_HEADS * HEAD_DIM
OFF_QA = SWA_Q_W
OFF_KA = OFF_QA + SWA_KV_W
OFF_VA = OFF_KA + SWA_KV_W
OFF_QB = OFF_VA + FOX_W
OFF_KB = OFF_QB + FOX_W
OFF_VB = OFF_KB + FOX_W
WINDOW = 128
BLOCK = 128
N_BUCKETS = 32
MAX_DISTANCE = 128
D_FF = 2816
EPS = 1e-6
NEG_INF = -1e30
SCALE = HEAD_DIM ** -0.5
LOG2E = math.log2(math.e)

LANES = 128
SUBLANES = 8
PAIR_W = 2 * HEAD_DIM
VMEM_LIMIT = 56 * 1024 * 1024

SWA_KEYS = 2 * BLOCK + N_META
SWA_TQ = 2048
SWA_ONES = 16
IN_TM = 512
META_TM = 128
GATE_ROWS = 16
FOX_VROWS = HEAD_DIM + 16
FOX_T = 512
FOX_GROUP = 4
FOX_SAFE_LOG2 = 40.0
FOX_BOUND_MARGIN = 1.001
FFN_TM = 512
FFN_PARTS = 2
SLAB_ALIGN = 16
N_PIECES = 3

F32 = jnp.float32
BF16 = jnp.bfloat16


def _dot(a, b):
    return jnp.dot(a, b, preferred_element_type=F32)


def _dot_nt(a, b):
    return lax.dot_general(a, b, (((1,), (1,)), ((), ())), preferred_element_type=F32)


def _rms(t, g):
    return t * lax.rsqrt(jnp.mean(t * t, axis=-1, keepdims=True) + EPS) * g


def _bf16_pieces(v):
    pieces = []
    for _ in range(N_PIECES - 1):
        p = v.astype(BF16)
        pieces.append(p)
        v = v - p.astype(F32)
    pieces.append(v.astype(BF16))
    return pieces


def _t5_bucket_np(dist):
    n = np.maximum(dist, 0).astype(np.int32)
    max_exact = N_BUCKETS // 2
    nf = np.maximum(n, 1).astype(np.float32)
    large = max_exact + (np.log(nf / np.float32(max_exact)) / np.float32(math.log(MAX_DISTANCE / max_exact))
                         * np.float32(N_BUCKETS - max_exact)).astype(np.int32)
    large = np.minimum(large, N_BUCKETS - 1)
    return np.where(n < max_exact, n, large).astype(np.int32)


def _bias_kernel(tab_ref, bkt_ref, valid_ref, out_ref):
    bkt = bkt_ref[...]
    valid = valid_ref[...] > 0
    for h in range(SWA_Q_HEADS):
        acc = jnp.zeros(bkt.shape, F32)
        for b in range(N_BUCKETS):
            acc = jnp.where(bkt == b, tab_ref[b, h], acc)
        out_ref[:, h * BLOCK:(h + 1) * BLOCK] = jnp.where(valid, acc * LOG2E, NEG_INF)


def _bias_tables(rel_bias):
    ki = np.arange(2 * BLOCK)[:, None]
    qi = np.arange(BLOCK)[None, :]
    d_w = qi + BLOCK - ki
    bkt_w = _t5_bucket_np(d_w)
    in_window = (d_w >= 0) & (d_w < WINDOW)
    mi = np.arange(N_META)[:, None]
    bkt_m0 = _t5_bucket_np(N_META + qi - mi)
    far = _t5_bucket_np(np.asarray([[N_META + BLOCK - (N_META - 1)]]))
    assert far[0, 0] == N_BUCKETS - 1, "meta keys of later blocks must share the last bucket"
    bkt = np.stack([np.concatenate([bkt_w, bkt_m0]), np.concatenate([bkt_w, np.full_like(bkt_m0, far[0, 0])])])
    meta_ok = np.ones((N_META, BLOCK), bool)
    valid = np.stack([np.concatenate([in_window & (ki >= BLOCK), meta_ok]), np.concatenate([in_window, meta_ok])])
    spec = pl.BlockSpec((None, SWA_KEYS, BLOCK), lambda a: (a, 0, 0))
    return pl.pallas_call(
        _bias_kernel,
        grid=(2,),
        in_specs=[pl.BlockSpec(memory_space=pltpu.SMEM), spec, spec],
        out_specs=pl.BlockSpec((None, SWA_KEYS, SWA_Q_HEADS * BLOCK), lambda a: (a, 0, 0)),
        out_shape=jax.ShapeDtypeStruct((2, SWA_KEYS, SWA_Q_HEADS * BLOCK), F32),
        name="bias_tables",
    )(rel_bias.astype(F32), jnp.asarray(bkt.astype(np.int32)), jnp.asarray(valid.astype(np.int32)))


def _fox_aug8(val_row, pieces_first):
    n = val_row.shape[1]
    row = lax.broadcasted_iota(jnp.int32, (SUBLANES, n), 0)
    p0, o0 = (0, N_PIECES) if pieces_first else (N_PIECES, 0)
    aug = jnp.where((row >= o0) & (row < o0 + N_PIECES), 1.0, 0.0)
    for i, piece in enumerate(_bf16_pieces(val_row)):
        aug = jnp.where(row == p0 + i, piece.astype(F32), aug)
    return aug


def _fox_head_t(own_t, aug8, e):
    pad = jnp.zeros((HEAD_DIM - SUBLANES, own_t.shape[1]), F32)
    return jnp.concatenate([own_t, aug8, pad] if e == 0 else [aug8, pad, own_t], axis=0)


def _inproj_kernel(x_ref, g_ref, w_ref, wf_ref, bft_ref, tri_ref, c0_ref,
                   pa_ref, qt_ref, ka_ref, vt_ref, cum_ref, qn2_ref, kmx_ref, cmeta_ref, carry_ref):
    @pl.when(pl.program_id(1) == 0)
    def _():
        carry_ref[...] = c0_ref[...]

    tm = x_ref.shape[0]
    y = _rms(x_ref[...], g_ref[...]).astype(BF16)

    tile_lanes = lambda a: jnp.concatenate([a] * (tm // LANES), axis=1)
    f_t = lax.dot_general(wf_ref[...], y, (((0,), (1,)), ((), ())), preferred_element_type=F32) + tile_lanes(bft_ref[...])
    acc_fox = _dot(y, w_ref[:, OFF_VA:OFF_VB])
    ls_t = jnp.minimum(f_t, 0.0) - jnp.log1p(jnp.exp(-jnp.abs(f_t)))
    cum_t = tile_lanes(carry_ref[...])
    for piece in _bf16_pieces(ls_t):
        cum_t = cum_t + _dot(piece, tri_ref[...])
    carry_ref[...] = jnp.broadcast_to(cum_t[:, tm - 1:tm], carry_ref.shape)
    cmeta_ref[...] = jnp.broadcast_to(cum_t[:, N_META - 1:N_META], cmeta_ref.shape)
    for g in range(FOX_HEADS // FOX_GROUP):
        cum_ref[g] = cum_t[g * FOX_GROUP:(g + 1) * FOX_GROUP]
    cb_t = cum_t * (-LOG2E)

    acc_swa = _dot(y, w_ref[:, 0:OFF_VA])
    pa_ref[:, 0:OFF_QA] = (acc_swa[:, 0:OFF_QA] * (SCALE * LOG2E)).astype(BF16)
    pa_ref[:, OFF_QA:OFF_VA] = acc_swa[:, OFF_QA:OFF_VA].astype(BF16)

    ones = jnp.ones((FOX_VROWS - HEAD_DIM, tm), F32)
    for p in range(FOX_HEADS // 2):
        cols = lambda off: slice(off - OFF_VA + p * PAIR_W, off - OFF_VA + (p + 1) * PAIR_W)
        q_t = (acc_fox[:, cols(OFF_VA)] * (SCALE * LOG2E)).astype(BF16).astype(F32).T
        k_t = acc_fox[:, cols(OFF_QB)].astype(BF16).astype(F32).T
        v_t = acc_fox[:, cols(OFF_KB)].astype(BF16).astype(F32).T
        qt_ref[p] = q_t.astype(BF16)
        for e in range(2):
            h = 2 * p + e
            own = slice(e * HEAD_DIM, (e + 1) * HEAD_DIM)
            qn2_ref[h // FOX_GROUP, h % FOX_GROUP:h % FOX_GROUP + 1, :] = jnp.sum(q_t[own] * q_t[own], axis=0, keepdims=True)
            kn2 = jnp.sum(k_t[own] * k_t[own], axis=0, keepdims=True)
            kmx_ref[h:h + 1, :] = jnp.broadcast_to(jnp.max(kn2, axis=1, keepdims=True), (1, LANES))
            ka_ref[h] = _fox_head_t(k_t[own], _fox_aug8(cb_t[h:h + 1], True), e).T.astype(BF16)
            vt_ref[h] = jnp.concatenate([v_t[own], ones], axis=0).astype(BF16)


def _inproj(x3, g, w, wft, bft, c0, tm):
    nb, rows, _ = x3.shape
    nt = rows // tm
    tri = jnp.asarray(np.triu(np.ones((tm, tm), np.float32)), BF16)
    const = lambda b, t: (0, 0)
    npair = FOX_HEADS // 2
    ngroup = FOX_HEADS // FOX_GROUP
    return pl.pallas_call(
        _inproj_kernel,
        grid=(nb, nt),
        in_specs=[
            pl.BlockSpec((None, tm, D_MODEL), lambda b, t: (b, t, 0)),
            pl.BlockSpec((1, D_MODEL), const),
            pl.BlockSpec(w.shape, const),
            pl.BlockSpec((D_MODEL, GATE_ROWS), const),
            pl.BlockSpec((GATE_ROWS, LANES), const),
            pl.BlockSpec((tm, tm), const),
            pl.BlockSpec((GATE_ROWS, LANES), const),
        ],
        out_specs=[
            pl.BlockSpec((None, tm, OFF_VA), lambda b, t: (b, t, 0)),
            pl.BlockSpec((None, npair, PAIR_W, tm), lambda b, t: (b, 0, 0, t)),
            pl.BlockSpec((None, FOX_HEADS, tm, PAIR_W), lambda b, t: (b, 0, t, 0)),
            pl.BlockSpec((None, FOX_HEADS, FOX_VROWS, tm), lambda b, t: (b, 0, 0, t)),
            pl.BlockSpec((None, ngroup, FOX_GROUP, tm), lambda b, t: (b, 0, 0, t)),
            pl.BlockSpec((None, ngroup, FOX_GROUP, tm), lambda b, t: (b, 0, 0, t)),
            pl.BlockSpec((None, None, FOX_HEADS, LANES), lambda b, t: (b, t, 0, 0)),
            pl.BlockSpec((None, None, GATE_ROWS, LANES), lambda b, t: (b, t, 0, 0)),
        ],
        out_shape=[
            jax.ShapeDtypeStruct((nb, rows, OFF_VA), BF16),
            jax.ShapeDtypeStruct((nb, npair, PAIR_W, rows), BF16),
            jax.ShapeDtypeStruct((nb, FOX_HEADS, rows, PAIR_W), BF16),
            jax.ShapeDtypeStruct((nb, FOX_HEADS, FOX_VROWS, rows), BF16),
            jax.ShapeDtypeStruct((nb, ngroup, FOX_GROUP, rows), F32),
            jax.ShapeDtypeStruct((nb, ngroup, FOX_GROUP, rows), F32),
            jax.ShapeDtypeStruct((nb, nt, FOX_HEADS, LANES), F32),
            jax.ShapeDtypeStruct((nb, nt, GATE_ROWS, LANES), F32),
        ],
        scratch_shapes=[pltpu.VMEM((GATE_ROWS, LANES), F32)],
        compiler_params=pltpu.CompilerParams(
            dimension_semantics=("arbitrary", "arbitrary"), vmem_limit_bytes=VMEM_LIMIT),
        name="inproj",
    )(x3, g, w, wft, bft, tri, c0)


def _swa_scores(q, k_all, bias):
    nq = q.shape[0]
    zeros = jnp.zeros((HEAD_DIM, nq), F32)
    blocks = []
    for pair in range(SWA_Q_HEADS // 2):
        q_t = q[:, pair * PAIR_W:(pair + 1) * PAIR_W].astype(F32).T
        for hp in range(2):
            qh = q_t[hp * HEAD_DIM:(hp + 1) * HEAD_DIM]
            g = (2 * pair + hp) // SWA_GROUP
            blocks.append(jnp.concatenate([qh, zeros] if g == 0 else [zeros, qh], axis=0))
    q_bd = jnp.concatenate(blocks, axis=1).astype(BF16)
    return _dot(k_all, q_bd) + bias


def _swa_outputs(s, v_all, sink_ref):
    nq = s.shape[1] // SWA_Q_HEADS
    ps, sink_terms = [], []
    for h in range(SWA_Q_HEADS):
        sh = s[:, h * nq:(h + 1) * nq]
        sink = sink_ref[h] * LOG2E
        m = jnp.maximum(jnp.max(sh, axis=0, keepdims=True), sink)
        ps.append(jnp.exp2(sh - m).astype(BF16))
        sink_terms.append(jnp.exp2(sink - m))
    n = BLOCK
    v_t = jnp.concatenate([v_all[0:n].astype(F32).T, v_all[n:2 * n].astype(F32).T,
                           v_all[2 * n:].astype(F32).T], axis=1)
    ones = jnp.ones((SWA_ONES, v_t.shape[1]), F32)
    outs = []
    for g in range(SWA_KV_HEADS):
        v_aug = jnp.concatenate([v_t[g * HEAD_DIM:(g + 1) * HEAD_DIM], ones], axis=0).astype(BF16)
        p_g = jnp.concatenate(ps[g * SWA_GROUP:(g + 1) * SWA_GROUP], axis=1)
        o_g = _dot(v_aug, p_g)
        for j in range(SWA_GROUP):
            cols = slice(j * nq, (j + 1) * nq)
            l = o_g[HEAD_DIM:HEAD_DIM + 1, cols] + sink_terms[g * SWA_GROUP + j]
            outs.append(o_g[0:HEAD_DIM, cols] / l)
    pairs = [jnp.concatenate(outs[2 * p:2 * p + 2], axis=0).T for p in range(SWA_Q_HEADS // 2)]
    return jnp.concatenate(pairs, axis=1).astype(BF16)


def _swa_kernel(sink_ref, q_ref, kc_ref, kp_ref, vc_ref, vp_ref, km_ref, vm_ref, bias0_ref, bias1_ref, o_ref):
    n = BLOCK
    nsub = q_ref.shape[0] // n
    k_rows = [kp_ref[...]] + [kc_ref[c * n:(c + 1) * n] for c in range(nsub)]
    v_rows = [vp_ref[...]] + [vc_ref[c * n:(c + 1) * n] for c in range(nsub)]
    def block_scores(c):
        k_all = jnp.concatenate([k_rows[c], k_rows[c + 1], km_ref[...]], axis=0)
        bias = bias0_ref[...] if c == 0 else bias1_ref[...]
        return _swa_scores(q_ref[c * n:(c + 1) * n], k_all, bias)

    s_next = block_scores(0)
    for c in range(nsub):
        s_cur = s_next
        if c + 1 < nsub:
            s_next = block_scores(c + 1)
        v_all = jnp.concatenate([v_rows[c], v_rows[c + 1], vm_ref[...]], axis=0)
        o_ref[c * n:(c + 1) * n] = _swa_outputs(s_cur, v_all, sink_ref)


def _swa(proj, projm, bias, sinks):
    nb, rows, _ = proj.shape
    tq = SWA_TQ
    ratio = tq // BLOCK
    kcol, vcol = OFF_QA // SWA_KV_W, OFF_KA // SWA_KV_W
    cur = lambda c: (lambda b, n: (b, n, c))
    prev = lambda c: (lambda b, n: (b, jnp.maximum(n * ratio - 1, 0), c))
    bias_spec = lambda idx: pl.BlockSpec((None, SWA_KEYS, SWA_Q_HEADS * BLOCK), idx)
    return pl.pallas_call(
        _swa_kernel,
        grid=(nb, rows // tq),
        in_specs=[
            pl.BlockSpec(memory_space=pltpu.SMEM),
            pl.BlockSpec((None, tq, SWA_Q_W), lambda b, n: (b, n, 0)),
            pl.BlockSpec((None, tq, SWA_KV_W), cur(kcol)),
            pl.BlockSpec((None, BLOCK, SWA_KV_W), prev(kcol)),
            pl.BlockSpec((None, tq, SWA_KV_W), cur(vcol)),
            pl.BlockSpec((None, BLOCK, SWA_KV_W), prev(vcol)),
            pl.BlockSpec((None, N_META, SWA_KV_W), lambda b, n: (0, 0, kcol)),
            pl.BlockSpec((None, N_META, SWA_KV_W), lambda b, n: (0, 0, vcol)),
            bias_spec(lambda b, n: (jnp.minimum(n, 1), 0, 0)),
            bias_spec(lambda b, n: (1, 0, 0)),
        ],
        out_specs=pl.BlockSpec((None, tq, SWA_Q_W), lambda b, n: (b, n, 0)),
        out_shape=jax.ShapeDtypeStruct((nb, rows, SWA_Q_W), BF16),
        compiler_params=pltpu.CompilerParams(
            dimension_semantics=("arbitrary", "arbitrary"), vmem_limit_bytes=VMEM_LIMIT),
        name="swa",
    )(sinks, proj, proj, proj, proj, proj, projm, projm, bias, bias)


def _fox_kernel(qt_ref, ka_ref, vt_ref, kam_ref, vtm_ref, cq_ref, qn2_ref, kmx_ref, kmxm_ref,
                wo_ref, wgu_ref, wd_ref, o_ref, wo_bf_ref, wgu_bf_ref, wd_bf_ref, acc_scr):
    wo_bf_ref[...] = wo_ref[...].astype(BF16)
    wgu_bf_ref[...] = wgu_ref[...].astype(BF16)
    wd_bf_ref[...] = wd_ref[...].astype(BF16)

    i = pl.program_id(2)
    t = o_ref.shape[0]
    nh = acc_scr.shape[0]
    colmax = lambda x: jnp.max(x, axis=0, keepdims=True)
    tile_rows = lambda j: pl.ds(pl.multiple_of(j * t, t), t)

    heads = pl.ds(pl.program_id(1) * nh, nh)
    kmax2 = jnp.maximum(jnp.max(kmx_ref[:, heads, :], axis=0), kmxm_ref[heads, :])
    r = [jnp.sqrt(qn2_ref[h:h + 1, :] * kmax2[h:h + 1, 0:1]) * FOX_BOUND_MARGIN for h in range(nh)]
    worst = r[0]
    for rh in r[1:]:
        worst = jnp.maximum(worst, rh)
    safe = jnp.max(worst) <= FOX_SAFE_LOG2

    qt = []
    for h in range(nh):
        e = h % 2
        u = cq_ref[h:h + 1, :] * LOG2E - jnp.where(safe, r[h], 0.0)
        own = qt_ref[h // 2, e * HEAD_DIM:(e + 1) * HEAD_DIM, :].astype(F32)
        qt.append(_fox_head_t(own, _fox_aug8(u, False), e).astype(BF16))

    def scores(h, j, diagonal):
        s = _dot(ka_ref[h, tile_rows(j), :], qt[h])
        if diagonal:
            key = lax.broadcasted_iota(jnp.int32, (t, t), 0)
            qry = lax.broadcasted_iota(jnp.int32, (t, t), 1)
            s = jnp.where(key <= qry, s, NEG_INF)
        return s

    @pl.when(safe)
    def _():
        def run(work):
            prev = None
            for score_fn, pv_fn in work:
                s = score_fn()
                if prev is not None:
                    prev[0](prev[1])
                prev = (pv_fn, s)
            prev[0](prev[1])

        def tile_work(j):
            def pv(h):
                def apply(s):
                    acc_scr[h] += _dot(vt_ref[h, :, tile_rows(j)], jnp.exp2(s).astype(BF16))
                return apply
            return [(lambda h=h: scores(h, j, False), pv(h)) for h in range(nh)]

        def first_work():
            def sc(h):
                return jnp.concatenate([scores(h, i, True), _dot(kam_ref[h], qt[h])], axis=0)
            def pv(h):
                def apply(s):
                    v_t = jnp.concatenate([vt_ref[h, :, tile_rows(i)], vtm_ref[h, :, 0:N_META]], axis=1)
                    acc_scr[h] = _dot(v_t, jnp.exp2(s).astype(BF16))
                return apply
            return [(lambda h=h: sc(h), pv(h)) for h in range(nh)]

        run(first_work())

        def tile_pair(p, carry):
            run(tile_work(2 * p) + tile_work(2 * p + 1))
            return carry

        lax.fori_loop(0, i // 2, tile_pair, 0)

        @pl.when(i % 2 == 1)
        def _():
            run(tile_work(i - 1))

    @pl.when(jnp.logical_not(safe))
    def _():
        m0 = []
        for h in range(nh):
            s = _dot(kam_ref[h], qt[h])
            m = colmax(s)
            acc_scr[h] = _dot(vtm_ref[h, :, 0:N_META], jnp.exp2(s - m).astype(BF16))
            m0.append(m)

        def tile(j, ms, diagonal):
            out = []
            ss = [scores(h, j, diagonal) for h in range(nh)]
            for h in range(nh):
                s = ss[h]
                m_next = jnp.maximum(ms[h], colmax(s))
                p = jnp.exp2(s - m_next).astype(BF16)
                acc_scr[h] = jnp.exp2(ms[h] - m_next) * acc_scr[h] + _dot(vt_ref[h, :, tile_rows(j)], p)
                out.append(m_next)
            return tuple(out)

        ms = lax.fori_loop(0, i, lambda j, ms: tile(j, ms, False), tuple(m0))
        tile(i, ms, True)

    for pp in range(nh // 2):
        o_t = jnp.concatenate([acc_scr[h, 0:HEAD_DIM] / acc_scr[h, HEAD_DIM:HEAD_DIM + 1]
                               for h in (2 * pp, 2 * pp + 1)], axis=0)
        o_ref[:, pp * PAIR_W:(pp + 1) * PAIR_W] = o_t.T.astype(BF16)


def _fox(qt, ka, vt, ka_m, vt_m, cum, qn2, kmx, kmx_m, w_out, w_gate_up, w_down):
    nb, _, rows, _ = ka.shape
    t = FOX_T
    nh = FOX_GROUP
    ngroup = FOX_HEADS // nh
    nt = kmx.shape[1]
    nq = rows // t
    nsteps = nb * ngroup * nq
    slab = lambda w: min(r for r in range(SLAB_ALIGN, w.shape[0] + 1, SLAB_ALIGN)
                         if w.shape[0] % r == 0 and r * nsteps >= w.shape[0])
    step = lambda b, g, i: (b * ngroup + g) * nq + i
    w_spec = lambda w: pl.BlockSpec((slab(w), w.shape[1]),
                                    lambda b, g, i: (jnp.minimum(step(b, g, i), w.shape[0] // slab(w) - 1), 0))
    return pl.pallas_call(
        _fox_kernel,
        grid=(nb, ngroup, nq),
        in_specs=[
            pl.BlockSpec((None, nh // 2, PAIR_W, t), lambda b, g, i: (b, g, 0, i)),
            pl.BlockSpec((None, nh, rows, PAIR_W), lambda b, g, i: (b, g, 0, 0)),
            pl.BlockSpec((None, nh, FOX_VROWS, rows), lambda b, g, i: (b, g, 0, 0)),
            pl.BlockSpec((None, nh, N_META, PAIR_W), lambda b, g, i: (0, g, 0, 0)),
            pl.BlockSpec((None, nh, FOX_VROWS, META_TM), lambda b, g, i: (0, g, 0, 0)),
            pl.BlockSpec((None, None, nh, t), lambda b, g, i: (b, g, 0, i)),
            pl.BlockSpec((None, None, nh, t), lambda b, g, i: (b, g, 0, i)),
            pl.BlockSpec((None, nt, FOX_HEADS, LANES), lambda b, g, i: (b, 0, 0, 0)),
            pl.BlockSpec((None, None, FOX_HEADS, LANES), lambda b, g, i: (0, 0, 0, 0)),
            w_spec(w_out), w_spec(w_gate_up), w_spec(w_down),
        ],
        out_specs=[pl.BlockSpec((None, t, nh * HEAD_DIM), lambda b, g, i: (b, i, g)),
                   w_spec(w_out), w_spec(w_gate_up), w_spec(w_down)],
        out_shape=[jax.ShapeDtypeStruct((nb, rows, FOX_W), BF16)]
        + [jax.ShapeDtypeStruct(w.shape, BF16) for w in (w_out, w_gate_up, w_down)],
        scratch_shapes=[pltpu.VMEM((nh, FOX_VROWS, t), F32)],
        compiler_params=pltpu.CompilerParams(
            dimension_semantics=("arbitrary", "arbitrary", "arbitrary"), vmem_limit_bytes=VMEM_LIMIT),
        name="fox",
    )(qt, ka, vt, ka_m, vt_m, cum, qn2,
      kmx, kmx_m, w_out, w_gate_up, w_down)


def _ffn_kernel(oa_ref, ob_ref, x_ref, wo_ref, g1_ref, g2_ref, wg_ref, wu_ref, wd_ref, g3_ref, out_ref):
    tm = x_ref.shape[0]
    parts = [slice(c * tm // FFN_PARTS, (c + 1) * tm // FFN_PARTS) for c in range(FFN_PARTS)]
    mix = [jnp.concatenate([oa_ref[r, :], ob_ref[r, :]], axis=1) for r in parts]
    a = [_dot(m, wo_ref[...]) for m in mix]
    h1 = [x_ref[r, :] + _rms(ai, g1_ref[...]) for r, ai in zip(parts, a)]
    hn = [_rms(h, g2_ref[...]).astype(BF16) for h in h1]
    gate_up = [(_dot(h, wg_ref[...]), _dot(h, wu_ref[...])) for h in hn]
    act = [(g / (1.0 + jnp.exp(-g)) * u).astype(BF16) for g, u in gate_up]
    ff = [_dot(ac, wd_ref[...]) for ac in act]
    for r, h, f in zip(parts, h1, ff):
        out_ref[r, :] = h + _rms(f, g3_ref[...])


def _ffn(o_a, o_b, x3, wo, g1, g2, wgu, wd, g3):
    nb, rows, _ = x3.shape
    tm = FFN_TM
    const = lambda b, t: (0, 0)
    resident = lambda shape: pl.BlockSpec(shape, const, pipeline_mode=pl.Buffered(1))
    row = lambda w: pl.BlockSpec((None, tm, w), lambda b, t: (b, t, 0))
    return pl.pallas_call(
        _ffn_kernel,
        grid=(nb, rows // tm),
        in_specs=[
            row(SWA_Q_W), row(FOX_W), row(D_MODEL),
            resident((D_MODEL, D_MODEL)), resident((1, D_MODEL)), resident((1, D_MODEL)),
            pl.BlockSpec((D_MODEL, D_FF), lambda b, t: (0, 0), pipeline_mode=pl.Buffered(1)),
            pl.BlockSpec((D_MODEL, D_FF), lambda b, t: (0, 1), pipeline_mode=pl.Buffered(1)),
            resident((D_FF, D_MODEL)),
            resident((1, D_MODEL)),
        ],
        out_specs=row(D_MODEL),
        out_shape=jax.ShapeDtypeStruct((nb, rows, D_MODEL), F32),
        compiler_params=pltpu.CompilerParams(
            dimension_semantics=("arbitrary", "arbitrary"), vmem_limit_bytes=VMEM_LIMIT),
        name="outproj_ffn",
    )(o_a, o_b, x3, wo, g1, g2, wgu, wgu, wd, g3)


def kernel(x, meta_tokens, rel_bias, ln_pre_mix, ln_post_mix, ln_pre_ffn, ln_post_ffn,
           w_in, b_forget, sinks, w_out, w_gate_up, w_down):
    nb, seq, d = x.shape
    assert d == D_MODEL and seq % IN_TM == 0 and seq % FOX_T == 0 and seq % FFN_TM == 0 and seq % SWA_TQ == 0
    assert w_in.shape[0] == 1, "single-layer block"
    assert meta_tokens.shape == (N_META, D_MODEL)
    x = x.astype(F32)

    w_all = w_in[0].astype(BF16)
    pad_gate = GATE_ROWS - FOX_HEADS
    w_ft = jnp.pad(w_in[0, :, OFF_VB:], ((0, 0), (0, pad_gate))).astype(BF16)
    lane_rep = lambda v: jnp.broadcast_to(jnp.pad(v.astype(F32), (0, pad_gate))[:, None], (GATE_ROWS, LANES))
    b_ft = lane_rep(b_forget[0])
    g_pre = ln_pre_mix[0].astype(F32).reshape(1, D_MODEL)

    swa_bias = _bias_tables(rel_bias)

    x_m = jnp.pad(meta_tokens.astype(F32), ((0, META_TM - N_META), (0, 0)))[None]
    pa_m, _, ka_m, vt_m, _, _, kmx_m, c_meta = _inproj(x_m, g_pre, w_all, w_ft, b_ft,
                                                       jnp.zeros((GATE_ROWS, LANES), F32), META_TM)
    pa, qt, ka, vt, cum, qn2, kmx, _ = _inproj(x, g_pre, w_all, w_ft, b_ft, c_meta[0, 0], IN_TM)

    o_a = _swa(pa, pa_m, swa_bias, sinks[0].astype(F32))
    o_b, w_out_bf, w_gu_bf, w_down_bf = _fox(qt, ka, vt, ka_m, vt_m, cum, qn2, kmx,
                                             kmx_m, w_out[0].astype(F32), w_gate_up[0].astype(F32),
                                             w_down[0].astype(F32))

    return _ffn(o_a, o_b, x,
                w_out_bf,
                ln_post_mix[0].astype(F32).reshape(1, D_MODEL),
                ln_pre_ffn[0].astype(F32).reshape(1, D_MODEL),
                w_gu_bf,
                w_down_bf,
                ln_post_ffn[0].astype(F32).reshape(1, D_MODEL))
```

```python
import math

import numpy as np
import jax
import jax.numpy as jnp
from jax import lax
from jax.experimental import pallas as pl
from jax.experimental.pallas import tpu as pltpu

D_MODEL = 1024
N_META = 16
HEAD_DIM = 64
SWA_Q_HEADS = 8
SWA_KV_HEADS = 2
SWA_GROUP = SWA_Q_HEADS // SWA_KV_HEADS
FOX_HEADS = 8
SWA_Q_W = SWA_Q_HEADS * HEAD_DIM
SWA_KV_W = SWA_KV_HEADS * HEAD_DIM
FOX_W = FOX_HEADS * HEAD_DIM
OFF_QA = SWA_Q_W
OFF_KA = OFF_QA + SWA_KV_W
OFF_VA = OFF_KA + SWA_KV_W
OFF_QB = OFF_VA + FOX_W
OFF_KB = OFF_QB + FOX_W
OFF_VB = OFF_KB + FOX_W
WINDOW = 128
BLOCK = 128
N_BUCKETS = 32
MAX_DISTANCE = 128
D_FF = 2816
EPS = 1e-6
NEG_INF = -1e30
SCALE = HEAD_DIM ** -0.5
LOG2E = math.log2(math.e)

LANES = 128
SUBLANES = 8
PAIR_W = 2 * HEAD_DIM
VMEM_LIMIT = 56 * 1024 * 1024

SWA_KEYS = 2 * BLOCK + N_META
SWA_TQ = 2048
SWA_ONES = 16
IN_TM = 512
META_TM = 128
GATE_ROWS = 16
FOX_VROWS = HEAD_DIM + 16
FOX_T = 512
FOX_GROUP = 4
FOX_SAFE_LOG2 = 40.0
FOX_BOUND_MARGIN = 1.001
FFN_TM = 512
FFN_PARTS = 2
SLAB_ALIGN = 16
N_PIECES = 3

F32 = jnp.float32
BF16 = jnp.bfloat16


def _dot(a, b):
    return jnp.dot(a, b, preferred_element_type=F32)


def _dot_nt(a, b):
    return lax.dot_general(a, b, (((1,), (1,)), ((), ())), preferred_element_type=F32)


def _rms(t, g):
    return t * lax.rsqrt(jnp.mean(t * t, axis=-1, keepdims=True) + EPS) * g


def _bf16_pieces(v):
    pieces = []
    for _ in range(N_PIECES - 1):
        p = v.astype(BF16)
        pieces.append(p)
        v = v - p.astype(F32)
    pieces.append(v.astype(BF16))
    return pieces


def _t5_bucket_np(dist):
    n = np.maximum(dist, 0).astype(np.int32)
    max_exact = N_BUCKETS // 2
    nf = np.maximum(n, 1).astype(np.float32)
    large = max_exact + (np.log(nf / np.float32(max_exact)) / np.float32(math.log(MAX_DISTANCE / max_exact))
                         * np.float32(N_BUCKETS - max_exact)).astype(np.int32)
    large = np.minimum(large, N_BUCKETS - 1)
    return np.where(n < max_exact, n, large).astype(np.int32)


def _bias_kernel(tab_ref, bkt_ref, valid_ref, out_ref):
    bkt = bkt_ref[...]
    valid = valid_ref[...] > 0
    for h in range(SWA_Q_HEADS):
        acc = jnp.zeros(bkt.shape, F32)
        for b in range(N_BUCKETS):
            acc = jnp.where(bkt == b, tab_ref[b, h], acc)
        out_ref[:, h * BLOCK:(h + 1) * BLOCK] = jnp.where(valid, acc * LOG2E, NEG_INF)


def _bias_tables(rel_bias):
    ki = np.arange(2 * BLOCK)[:, None]
    qi = np.arange(BLOCK)[None, :]
    d_w = qi + BLOCK - ki
    bkt_w = _t5_bucket_np(d_w)
    in_window = (d_w >= 0) & (d_w < WINDOW)
    mi = np.arange(N_META)[:, None]
    bkt_m0 = _t5_bucket_np(N_META + qi - mi)
    far = _t5_bucket_np(np.asarray([[N_META + BLOCK - (N_META - 1)]]))
    assert far[0, 0] == N_BUCKETS - 1, "meta keys of later blocks must share the last bucket"
    bkt = np.stack([np.concatenate([bkt_w, bkt_m0]), np.concatenate([bkt_w, np.full_like(bkt_m0, far[0, 0])])])
    meta_ok = np.ones((N_META, BLOCK), bool)
    valid = np.stack([np.concatenate([in_window & (ki >= BLOCK), meta_ok]), np.concatenate([in_window, meta_ok])])
    spec = pl.BlockSpec((None, SWA_KEYS, BLOCK), lambda a: (a, 0, 0))
    return pl.pallas_call(
        _bias_kernel,
        grid=(2,),
        in_specs=[pl.BlockSpec(memory_space=pltpu.SMEM), spec, spec],
        out_specs=pl.BlockSpec((None, SWA_KEYS, SWA_Q_HEADS * BLOCK), lambda a: (a, 0, 0)),
        out_shape=jax.ShapeDtypeStruct((2, SWA_KEYS, SWA_Q_HEADS * BLOCK), F32),
        name="bias_tables",
    )(rel_bias.astype(F32), jnp.asarray(bkt.astype(np.int32)), jnp.asarray(valid.astype(np.int32)))


def _fox_aug8(val_row, pieces_first):
    n = val_row.shape[1]
    row = lax.broadcasted_iota(jnp.int32, (SUBLANES, n), 0)
    p0, o0 = (0, N_PIECES) if pieces_first else (N_PIECES, 0)
    aug = jnp.where((row >= o0) & (row < o0 + N_PIECES), 1.0, 0.0)
    for i, piece in enumerate(_bf16_pieces(val_row)):
        aug = jnp.where(row == p0 + i, piece.astype(F32), aug)
    return aug


def _fox_head_t(own_t, aug8, e):
    pad = jnp.zeros((HEAD_DIM - SUBLANES, own_t.shape[1]), F32)
    return jnp.concatenate([own_t, aug8, pad] if e == 0 else [aug8, pad, own_t], axis=0)


def _inproj_kernel(x_ref, g_ref, w_ref, wf_ref, bft_ref, tri_ref, c0_ref,
                   pa_ref, qt_ref, ka_ref, vt_ref, cum_ref, qn2_ref, kmx_ref, cmeta_ref, carry_ref):
    @pl.when(pl.program_id(1) == 0)
    def _():
        carry_ref[...] = c0_ref[...]

    tm = x_ref.shape[0]
    half = tm // 2
    y_top = _rms(x_ref[0:half, :], g_ref[...]).astype(BF16)
    fox_top = _dot(y_top, w_ref[:, OFF_VA:OFF_VB])
    y = jnp.concatenate([y_top, _rms(x_ref[half:tm, :], g_ref[...]).astype(BF16)], axis=0)
    tile_lanes = lambda a: jnp.concatenate([a] * (tm // LANES), axis=1)
    f_t = lax.dot_general(wf_ref[...], y, (((0,), (1,)), ((), ())), preferred_element_type=F32) + tile_lanes(bft_ref[...])
    acc_fox = jnp.concatenate([fox_top, _dot(y[half:tm], w_ref[:, OFF_VA:OFF_VB])], axis=0)
    ls_t = jnp.minimum(f_t, 0.0) - jnp.log1p(jnp.exp(-jnp.abs(f_t)))
    cum_t = tile_lanes(carry_ref[...])
    for piece in _bf16_pieces(ls_t):
        cum_t = cum_t + _dot(piece, tri_ref[...])
    carry_ref[...] = jnp.broadcast_to(cum_t[:, tm - 1:tm], carry_ref.shape)
    cmeta_ref[...] = jnp.broadcast_to(cum_t[:, N_META - 1:N_META], cmeta_ref.shape)
    for g in range(FOX_HEADS // FOX_GROUP):
        cum_ref[g] = cum_t[g * FOX_GROUP:(g + 1) * FOX_GROUP]
    cb_t = cum_t * (-LOG2E)

    acc_swa = _dot(y, w_ref[:, 0:OFF_VA])
    pa_ref[:, 0:OFF_QA] = (acc_swa[:, 0:OFF_QA] * (SCALE * LOG2E)).astype(BF16)
    pa_ref[:, OFF_QA:OFF_VA] = acc_swa[:, OFF_QA:OFF_VA].astype(BF16)

    ones = jnp.ones((FOX_VROWS - HEAD_DIM, tm), F32)
    for p in range(FOX_HEADS // 2):
        cols = lambda off: slice(off - OFF_VA + p * PAIR_W, off - OFF_VA + (p + 1) * PAIR_W)
        q_t = (acc_fox[:, cols(OFF_VA)] * (SCALE * LOG2E)).astype(BF16).astype(F32).T
        k_t = acc_fox[:, cols(OFF_QB)].astype(BF16).astype(F32).T
        v_t = acc_fox[:, cols(OFF_KB)].astype(BF16).astype(F32).T
        qt_ref[p] = q_t.astype(BF16)
        for e in range(2):
            h = 2 * p + e
            own = slice(e * HEAD_DIM, (e + 1) * HEAD_DIM)
            qn2_ref[h // FOX_GROUP, h % FOX_GROUP:h % FOX_GROUP + 1, :] = jnp.sum(q_t[own] * q_t[own], axis=0, keepdims=True)
            kn2 = jnp.sum(k_t[own] * k_t[own], axis=0, keepdims=True)
            kmx_ref[h:h + 1, :] = jnp.broadcast_to(jnp.max(kn2, axis=1, keepdims=True), (1, LANES))
            ka_ref[h] = _fox_head_t(k_t[own], _fox_aug8(cb_t[h:h + 1], True), e).T.astype(BF16)
            vt_ref[h] = jnp.concatenate([v_t[own], ones], axis=0).astype(BF16)


def _inproj(x3, g, w, wft, bft, c0, tm):
    nb, rows, _ = x3.shape
    nt = rows // tm
    tri = jnp.asarray(np.triu(np.ones((tm, tm), np.float32)), BF16)
    const = lambda b, t: (0, 0)
    npair = FOX_HEADS // 2
    ngroup = FOX_HEADS // FOX_GROUP
    return pl.pallas_call(
        _inproj_kernel,
        grid=(nb, nt),
        in_specs=[
            pl.BlockSpec((None, tm, D_MODEL), lambda b, t: (b, t, 0)),
            pl.BlockSpec((1, D_MODEL), const),
            pl.BlockSpec(w.shape, const),
            pl.BlockSpec((D_MODEL, GATE_ROWS), const),
            pl.BlockSpec((GATE_ROWS, LANES), const),
            pl.BlockSpec((tm, tm), const),
            pl.BlockSpec((GATE_ROWS, LANES), const),
        ],
        out_specs=[
            pl.BlockSpec((None, tm, OFF_VA), lambda b, t: (b, t, 0)),
            pl.BlockSpec((None, npair, PAIR_W, tm), lambda b, t: (b, 0, 0, t)),
            pl.BlockSpec((None, FOX_HEADS, tm, PAIR_W), lambda b, t: (b, 0, t, 0)),
            pl.BlockSpec((None, FOX_HEADS, FOX_VROWS, tm), lambda b, t: (b, 0, 0, t)),
            pl.BlockSpec((None, ngroup, FOX_GROUP, tm), lambda b, t: (b, 0, 0, t)),
            pl.BlockSpec((None, ngroup, FOX_GROUP, tm), lambda b, t: (b, 0, 0, t)),
            pl.BlockSpec((None, None, FOX_HEADS, LANES), lambda b, t: (b, t, 0, 0)),
            pl.BlockSpec((None, None, GATE_ROWS, LANES), lambda b, t: (b, t, 0, 0)),
        ],
        out_shape=[
            jax.ShapeDtypeStruct((nb, rows, OFF_VA), BF16),
            jax.ShapeDtypeStruct((nb, npair, PAIR_W, rows), BF16),
            jax.ShapeDtypeStruct((nb, FOX_HEADS, rows, PAIR_W), BF16),
            jax.ShapeDtypeStruct((nb, FOX_HEADS, FOX_VROWS, rows), BF16),
            jax.ShapeDtypeStruct((nb, ngroup, FOX_GROUP, rows), F32),
            jax.ShapeDtypeStruct((nb, ngroup, FOX_GROUP, rows), F32),
            jax.ShapeDtypeStruct((nb, nt, FOX_HEADS, LANES), F32),
            jax.ShapeDtypeStruct((nb, nt, GATE_ROWS, LANES), F32),
        ],
        scratch_shapes=[pltpu.VMEM((GATE_ROWS, LANES), F32)],
        compiler_params=pltpu.CompilerParams(
            dimension_semantics=("arbitrary", "arbitrary"), vmem_limit_bytes=VMEM_LIMIT),
        name="inproj",
    )(x3, g, w, wft, bft, tri, c0)


def _swa_scores(q, k_all, bias):
    nq = q.shape[0]
    zeros = jnp.zeros((HEAD_DIM, nq), F32)
    blocks = []
    for pair in range(SWA_Q_HEADS // 2):
        q_t = q[:, pair * PAIR_W:(pair + 1) * PAIR_W].astype(F32).T
        for hp in range(2):
            qh = q_t[hp * HEAD_DIM:(hp + 1) * HEAD_DIM]
            g = (2 * pair + hp) // SWA_GROUP
            blocks.append(jnp.concatenate([qh, zeros] if g == 0 else [zeros, qh], axis=0))
    q_bd = jnp.concatenate(blocks, axis=1).astype(BF16)
    return _dot(k_all, q_bd) + bias


def _swa_outputs(s, v_all, sink_ref):
    nq = s.shape[1] // SWA_Q_HEADS
    ps, sink_terms = [], []
    for h in range(SWA_Q_HEADS):
        sh = s[:, h * nq:(h + 1) * nq]
        sink = sink_ref[h] * LOG2E
        m = jnp.maximum(jnp.max(sh, axis=0, keepdims=True), sink)
        ps.append(jnp.exp2(sh - m).astype(BF16))
        sink_terms.append(jnp.exp2(sink - m))
    n = BLOCK
    v_t = jnp.concatenate([v_all[0:n].astype(F32).T, v_all[n:2 * n].astype(F32).T,
                           v_all[2 * n:].astype(F32).T], axis=1)
    ones = jnp.ones((SWA_ONES, v_t.shape[1]), F32)
    outs = []
    for g in range(SWA_KV_HEADS):
        v_aug = jnp.concatenate([v_t[g * HEAD_DIM:(g + 1) * HEAD_DIM], ones], axis=0).astype(BF16)
        p_g = jnp.concatenate(ps[g * SWA_GROUP:(g + 1) * SWA_GROUP], axis=1)
        o_g = _dot(v_aug, p_g)
        for j in range(SWA_GROUP):
            cols = slice(j * nq, (j + 1) * nq)
            l = o_g[HEAD_DIM:HEAD_DIM + 1, cols] + sink_terms[g * SWA_GROUP + j]
            outs.append(o_g[0:HEAD_DIM, cols] / l)
    pairs = [jnp.concatenate(outs[2 * p:2 * p + 2], axis=0).T for p in range(SWA_Q_HEADS // 2)]
    return jnp.concatenate(pairs, axis=1).astype(BF16)


def _swa_kernel(sink_ref, q_ref, kc_ref, kp_ref, vc_ref, vp_ref, km_ref, vm_ref, bias0_ref, bias1_ref, o_ref):
    n = BLOCK
    nsub = q_ref.shape[0] // n
    k_rows = [kp_ref[...]] + [kc_ref[c * n:(c + 1) * n] for c in range(nsub)]
    v_rows = [vp_ref[...]] + [vc_ref[c * n:(c + 1) * n] for c in range(nsub)]
    def block_scores(c):
        k_all = jnp.concatenate([k_rows[c], k_rows[c + 1], km_ref[...]], axis=0)
        bias = bias0_ref[...] if c == 0 else bias1_ref[...]
        return _swa_scores(q_ref[c * n:(c + 1) * n], k_all, bias)

    s_next = block_scores(0)
    for c in range(nsub):
        s_cur = s_next
        if c + 1 < nsub:
            s_next = block_scores(c + 1)
        v_all = jnp.concatenate([v_rows[c], v_rows[c + 1], vm_ref[...]], axis=0)
        o_ref[c * n:(c + 1) * n] = _swa_outputs(s_cur, v_all, sink_ref)


def _swa(proj, projm, bias, sinks):
    nb, rows, _ = proj.shape
    tq = SWA_TQ
    ratio = tq // BLOCK
    kcol, vcol = OFF_QA // SWA_KV_W, OFF_KA // SWA_KV_W
    cur = lambda c: (lambda b, n: (b, n, c))
    prev = lambda c: (lambda b, n: (b, jnp.maximum(n * ratio - 1, 0), c))
    bias_spec = lambda idx: pl.BlockSpec((None, SWA_KEYS, SWA_Q_HEADS * BLOCK), idx)
    return pl.pallas_call(
        _swa_kernel,
        grid=(nb, rows // tq),
        in_specs=[
            pl.BlockSpec(memory_space=pltpu.SMEM),
            pl.BlockSpec((None, tq, SWA_Q_W), lambda b, n: (b, n, 0)),
            pl.BlockSpec((None, tq, SWA_KV_W), cur(kcol)),
            pl.BlockSpec((None, BLOCK, SWA_KV_W), prev(kcol)),
            pl.BlockSpec((None, tq, SWA_KV_W), cur(vcol)),
            pl.BlockSpec((None, BLOCK, SWA_KV_W), prev(vcol)),
            pl.BlockSpec((None, N_META, SWA_KV_W), lambda b, n: (0, 0, kcol)),
            pl.BlockSpec((None, N_META, SWA_KV_W), lambda b, n: (0, 0, vcol)),
            bias_spec(lambda b, n: (jnp.minimum(n, 1), 0, 0)),
            bias_spec(lambda b, n: (1, 0, 0)),
        ],
        out_specs=pl.BlockSpec((None, tq, SWA_Q_W), lambda b, n: (b, n, 0)),
        out_shape=jax.ShapeDtypeStruct((nb, rows, SWA_Q_W), BF16),
        compiler_params=pltpu.CompilerParams(
            dimension_semantics=("arbitrary", "arbitrary"), vmem_limit_bytes=VMEM_LIMIT),
        name="swa",
    )(sinks, proj, proj, proj, proj, proj, projm, projm, bias, bias)


def _fox_kernel(qt_ref, ka_ref, vt_ref, kam_ref, vtm_ref, cq_ref, qn2_ref, kmx_ref, kmxm_ref,
                wo_ref, wgu_ref, wd_ref, o_ref, wo_bf_ref, wgu_bf_ref, wd_bf_ref, acc_scr):
    wo_bf_ref[...] = wo_ref[...].astype(BF16)
    wgu_bf_ref[...] = wgu_ref[...].astype(BF16)
    wd_bf_ref[...] = wd_ref[...].astype(BF16)

    i = pl.program_id(2)
    t = o_ref.shape[0]
    nh = acc_scr.shape[0]
    colmax = lambda x: jnp.max(x, axis=0, keepdims=True)
    tile_rows = lambda j: pl.ds(pl.multiple_of(j * t, t), t)

    heads = pl.ds(pl.program_id(1) * nh, nh)
    kmax2 = jnp.maximum(jnp.max(kmx_ref[:, heads, :], axis=0), kmxm_ref[heads, :])
    r = [jnp.sqrt(qn2_ref[h:h + 1, :] * kmax2[h:h + 1, 0:1]) * FOX_BOUND_MARGIN for h in range(nh)]
    worst = r[0]
    for rh in r[1:]:
        worst = jnp.maximum(worst, rh)
    safe = jnp.max(worst) <= FOX_SAFE_LOG2

    qt = []
    for h in range(nh):
        e = h % 2
        u = cq_ref[h:h + 1, :] * LOG2E - jnp.where(safe, r[h], 0.0)
        own = qt_ref[h // 2, e * HEAD_DIM:(e + 1) * HEAD_DIM, :].astype(F32)
        qt.append(_fox_head_t(own, _fox_aug8(u, False), e).astype(BF16))

    def scores(h, j, diagonal):
        s = _dot(ka_ref[h, tile_rows(j), :], qt[h])
        if diagonal:
            key = lax.broadcasted_iota(jnp.int32, (t, t), 0)
            qry = lax.broadcasted_iota(jnp.int32, (t, t), 1)
            s = jnp.where(key <= qry, s, NEG_INF)
        return s

    @pl.when(safe)
    def _():
        def run(work):
            prev = None
            for score_fn, pv_fn in work:
                s = score_fn()
                if prev is not None:
                    prev[0](prev[1])
                prev = (pv_fn, s)
            prev[0](prev[1])

        def tile_work(j):
            def pv(h):
                def apply(s):
                    acc_scr[h] += _dot(vt_ref[h, :, tile_rows(j)], jnp.exp2(s).astype(BF16))
                return apply
            return [(lambda h=h: scores(h, j, False), pv(h)) for h in range(nh)]

        def first_work():
            def sc(h):
                return jnp.concatenate([scores(h, i, True), _dot(kam_ref[h], qt[h])], axis=0)
            def pv(h):
                def apply(s):
                    v_t = jnp.concatenate([vt_ref[h, :, tile_rows(i)], vtm_ref[h, :, 0:N_META]], axis=1)
                    acc_scr[h] = _dot(v_t, jnp.exp2(s).astype(BF16))
                return apply
            return [(lambda h=h: sc(h), pv(h)) for h in range(nh)]

        run(first_work())

        def tile_pair(p, carry):
            run(tile_work(2 * p) + tile_work(2 * p + 1))
            return carry

        lax.fori_loop(0, i // 2, tile_pair, 0)

        @pl.when(i % 2 == 1)
        def _():
            run(tile_work(i - 1))

    @pl.when(jnp.logical_not(safe))
    def _():
        m0 = []
        for h in range(nh):
            s = _dot(kam_ref[h], qt[h])
            m = colmax(s)
            acc_scr[h] = _dot(vtm_ref[h, :, 0:N_META], jnp.exp2(s - m).astype(BF16))
            m0.append(m)

        def tile(j, ms, diagonal):
            out = []
            ss = [scores(h, j, diagonal) for h in range(nh)]
            for h in range(nh):
                s = ss[h]
                m_next = jnp.maximum(ms[h], colmax(s))
                p = jnp.exp2(s - m_next).astype(BF16)
                acc_scr[h] = jnp.exp2(ms[h] - m_next) * acc_scr[h] + _dot(vt_ref[h, :, tile_rows(j)], p)
                out.append(m_next)
            return tuple(out)

        ms = lax.fori_loop(0, i, lambda j, ms: tile(j, ms, False), tuple(m0))
        tile(i, ms, True)

    for pp in range(nh // 2):
        o_t = jnp.concatenate([acc_scr[h, 0:HEAD_DIM] / acc_scr[h, HEAD_DIM:HEAD_DIM + 1]
                               for h in (2 * pp, 2 * pp + 1)], axis=0)
        o_ref[:, pp * PAIR_W:(pp + 1) * PAIR_W] = o_t.T.astype(BF16)


def _fox(qt, ka, vt, ka_m, vt_m, cum, qn2, kmx, kmx_m, w_out, w_gate_up, w_down):
    nb, _, rows, _ = ka.shape
    t = FOX_T
    nh = FOX_GROUP
    ngroup = FOX_HEADS // nh
    nt = kmx.shape[1]
    nq = rows // t
    nsteps = nb * ngroup * nq
    slab = lambda w: min(r for r in range(SLAB_ALIGN, w.shape[0] + 1, SLAB_ALIGN)
                         if w.shape[0] % r == 0 and r * nsteps >= w.shape[0])
    step = lambda b, g, i: (b * ngroup + g) * nq + i
    w_spec = lambda w: pl.BlockSpec((slab(w), w.shape[1]),
                                    lambda b, g, i: (jnp.minimum(step(b, g, i), w.shape[0] // slab(w) - 1), 0))
    return pl.pallas_call(
        _fox_kernel,
        grid=(nb, ngroup, nq),
        in_specs=[
            pl.BlockSpec((None, nh // 2, PAIR_W, t), lambda b, g, i: (b, g, 0, i)),
            pl.BlockSpec((None, nh, rows, PAIR_W), lambda b, g, i: (b, g, 0, 0)),
            pl.BlockSpec((None, nh, FOX_VROWS, rows), lambda b, g, i: (b, g, 0, 0)),
            pl.BlockSpec((None, nh, N_META, PAIR_W), lambda b, g, i: (0, g, 0, 0)),
            pl.BlockSpec((None, nh, FOX_VROWS, META_TM), lambda b, g, i: (0, g, 0, 0)),
            pl.BlockSpec((None, None, nh, t), lambda b, g, i: (b, g, 0, i)),
            pl.BlockSpec((None, None, nh, t), lambda b, g, i: (b, g, 0, i)),
            pl.BlockSpec((None, nt, FOX_HEADS, LANES), lambda b, g, i: (b, 0, 0, 0)),
            pl.BlockSpec((None, None, FOX_HEADS, LANES), lambda b, g, i: (0, 0, 0, 0)),
            w_spec(w_out), w_spec(w_gate_up), w_spec(w_down),
        ],
        out_specs=[pl.BlockSpec((None, t, nh * HEAD_DIM), lambda b, g, i: (b, i, g)),
                   w_spec(w_out), w_spec(w_gate_up), w_spec(w_down)],
        out_shape=[jax.ShapeDtypeStruct((nb, rows, FOX_W), BF16)]
        + [jax.ShapeDtypeStruct(w.shape, BF16) for w in (w_out, w_gate_up, w_down)],
        scratch_shapes=[pltpu.VMEM((nh, FOX_VROWS, t), F32)],
        compiler_params=pltpu.CompilerParams(
            dimension_semantics=("arbitrary", "arbitrary", "arbitrary"), vmem_limit_bytes=VMEM_LIMIT),
        name="fox",
    )(qt, ka, vt, ka_m, vt_m, cum, qn2,
      kmx, kmx_m, w_out, w_gate_up, w_down)


def _ffn_kernel(oa_ref, ob_ref, x_ref, wo_ref, g1_ref, g2_ref, wg_ref, wu_ref, wd_ref, g3_ref, out_ref):
    tm = x_ref.shape[0]
    parts = [slice(c * tm // FFN_PARTS, (c + 1) * tm // FFN_PARTS) for c in range(FFN_PARTS)]
    mix = [jnp.concatenate([oa_ref[r, :], ob_ref[r, :]], axis=1) for r in parts]
    a = [_dot(m, wo_ref[...]) for m in mix]
    h1 = [x_ref[r, :] + _rms(ai, g1_ref[...]) for r, ai in zip(parts, a)]
    hn = [_rms(h, g2_ref[...]).astype(BF16) for h in h1]
    gate_up = [(_dot(h, wg_ref[...]), _dot(h, wu_ref[...])) for h in hn]
    act = [(g / (1.0 + jnp.exp(-g)) * u).astype(BF16) for g, u in gate_up]
    ff = [_dot(ac, wd_ref[...]) for ac in act]
    for r, h, f in zip(parts, h1, ff):
        out_ref[r, :] = h + _rms(f, g3_ref[...])


def _ffn(o_a, o_b, x3, wo, g1, g2, wgu, wd, g3):
    nb, rows, _ = x3.shape
    tm = FFN_TM
    const = lambda b, t: (0, 0)
    resident = lambda shape: pl.BlockSpec(shape, const, pipeline_mode=pl.Buffered(1))
    row = lambda w: pl.BlockSpec((None, tm, w), lambda b, t: (b, t, 0))
    return pl.pallas_call(
        _ffn_kernel,
        grid=(nb, rows // tm),
        in_specs=[
            row(SWA_Q_W), row(FOX_W), row(D_MODEL),
            resident((D_MODEL, D_MODEL)), resident((1, D_MODEL)), resident((1, D_MODEL)),
            pl.BlockSpec((D_MODEL, D_FF), lambda b, t: (0, 0), pipeline_mode=pl.Buffered(1)),
            pl.BlockSpec((D_MODEL, D_FF), lambda b, t: (0, 1), pipeline_mode=pl.Buffered(1)),
            resident((D_FF, D_MODEL)),
            resident((1, D_MODEL)),
        ],
        out_specs=row(D_MODEL),
        out_shape=jax.ShapeDtypeStruct((nb, rows, D_MODEL), F32),
        compiler_params=pltpu.CompilerParams(
            dimension_semantics=("arbitrary", "arbitrary"), vmem_limit_bytes=VMEM_LIMIT),
        name="outproj_ffn",
    )(o_a, o_b, x3, wo, g1, g2, wgu, wgu, wd, g3)


def kernel(x, meta_tokens, rel_bias, ln_pre_mix, ln_post_mix, ln_pre_ffn, ln_post_ffn,
           w_in, b_forget, sinks, w_out, w_gate_up, w_down):
    nb, seq, d = x.shape
    assert d == D_MODEL and seq % IN_TM == 0 and seq % FOX_T == 0 and seq % FFN_TM == 0 and seq % SWA_TQ == 0
    assert w_in.shape[0] == 1, "single-layer block"
    assert meta_tokens.shape == (N_META, D_MODEL)
    x = x.astype(F32)

    w_all = w_in[0].astype(BF16)
    pad_gate = GATE_ROWS - FOX_HEADS
    w_ft = jnp.pad(w_in[0, :, OFF_VB:], ((0, 0), (0, pad_gate))).astype(BF16)
    lane_rep = lambda v: jnp.broadcast_to(jnp.pad(v.astype(F32), (0, pad_gate))[:, None], (GATE_ROWS, LANES))
    b_ft = lane_rep(b_forget[0])
    g_pre = ln_pre_mix[0].astype(F32).reshape(1, D_MODEL)

    swa_bias = _bias_tables(rel_bias)

    x_m = jnp.pad(meta_tokens.astype(F32), ((0, META_TM - N_META), (0, 0)))[None]
    pa_m, _, ka_m, vt_m, _, _, kmx_m, c_meta = _inproj(x_m, g_pre, w_all, w_ft, b_ft,
                                                       jnp.zeros((GATE_ROWS, LANES), F32), META_TM)
    pa, qt, ka, vt, cum, qn2, kmx, _ = _inproj(x, g_pre, w_all, w_ft, b_ft, c_meta[0, 0], IN_TM)

    o_a = _swa(pa, pa_m, swa_bias, sinks[0].astype(F32))
    o_b, w_out_bf, w_gu_bf, w_down_bf = _fox(qt, ka, vt, ka_m, vt_m, cum, qn2, kmx,
                                             kmx_m, w_out[0].astype(F32), w_gate_up[0].astype(F32),
                                             w_down[0].astype(F32))

    return _ffn(o_a, o_b, x,
                w_out_bf,
                ln_post_mix[0].astype(F32).reshape(1, D_MODEL),
                ln_pre_ffn[0].astype(F32).reshape(1, D_MODEL),
                w_gu_bf,
                w_down_bf,
                ln_post_ffn[0].astype(F32).reshape(1, D_MODEL))
```

```python
import math

import numpy as np
import jax
import jax.numpy as jnp
from jax import lax
from jax.experimental import pallas as pl
from jax.experimental.pallas import tpu as pltpu

D_MODEL = 1024
N_META = 16
HEAD_DIM = 64
SWA_Q_HEADS = 8
SWA_KV_HEADS = 2
SWA_GROUP = SWA_Q_HEADS // SWA_KV_HEADS
FOX_HEADS = 8
SWA_Q_W = SWA_Q_HEADS * HEAD_DIM
SWA_KV_W = SWA_KV_HEADS * HEAD_DIM
FOX_W = FOX_HEADS * HEAD_DIM
OFF_QA = SWA_Q_W
OFF_KA = OFF_QA + SWA_KV_W
OFF_VA = OFF_KA + SWA_KV_W
OFF_QB = OFF_VA + FOX_W
OFF_KB = OFF_QB + FOX_W
OFF_VB = OFF_KB + FOX_W
WINDOW = 128
BLOCK = 128
N_BUCKETS = 32
MAX_DISTANCE = 128
D_FF = 2816
EPS = 1e-6
NEG_INF = -1e30
SCALE = HEAD_DIM ** -0.5
LOG2E = math.log2(math.e)

LANES = 128
SUBLANES = 8
PAIR_W = 2 * HEAD_DIM
VMEM_LIMIT = 56 * 1024 * 1024

SWA_KEYS = 2 * BLOCK + N_META
SWA_TQ = 2048
SWA_ONES = 16
IN_TM = 512
META_TM = 128
GATE_ROWS = 16
FOX_VROWS = HEAD_DIM + 16
FOX_T = 512
FOX_GROUP = 4
FOX_SAFE_LOG2 = 40.0
FOX_BOUND_MARGIN = 1.001
FFN_TM = 512
FFN_PARTS = 2
SLAB_ALIGN = 16
N_PIECES = 3

F32 = jnp.float32
BF16 = jnp.bfloat16


def _dot(a, b):
    return jnp.dot(a, b, preferred_element_type=F32)


def _dot_nt(a, b):
    return lax.dot_general(a, b, (((1,), (1,)), ((), ())), preferred_element_type=F32)


def _rms(t, g):
    return t * lax.rsqrt(jnp.mean(t * t, axis=-1, keepdims=True) + EPS) * g


def _bf16_pieces(v):
    pieces = []
    for _ in range(N_PIECES - 1):
        p = v.astype(BF16)
        pieces.append(p)
        v = v - p.astype(F32)
    pieces.append(v.astype(BF16))
    return pieces


def _t5_bucket_np(dist):
    n = np.maximum(dist, 0).astype(np.int32)
    max_exact = N_BUCKETS // 2
    nf = np.maximum(n, 1).astype(np.float32)
    large = max_exact + (np.log(nf / np.float32(max_exact)) / np.float32(math.log(MAX_DISTANCE / max_exact))
                         * np.float32(N_BUCKETS - max_exact)).astype(np.int32)
    large = np.minimum(large, N_BUCKETS - 1)
    return np.where(n < max_exact, n, large).astype(np.int32)


def _bias_kernel(tab_ref, bkt_ref, valid_ref, out_ref):
    bkt = bkt_ref[...]
    valid = valid_ref[...] > 0
    for h in range(SWA_Q_HEADS):
        acc = jnp.zeros(bkt.shape, F32)
        for b in range(N_BUCKETS):
            acc = jnp.where(bkt == b, tab_ref[b, h], acc)
        out_ref[:, h * BLOCK:(h + 1) * BLOCK] = jnp.where(valid, acc * LOG2E, NEG_INF)


def _bias_tables(rel_bias):
    ki = np.arange(2 * BLOCK)[:, None]
    qi = np.arange(BLOCK)[None, :]
    d_w = qi + BLOCK - ki
    bkt_w = _t5_bucket_np(d_w)
    in_window = (d_w >= 0) & (d_w < WINDOW)
    mi = np.arange(N_META)[:, None]
    bkt_m0 = _t5_bucket_np(N_META + qi - mi)
    far = _t5_bucket_np(np.asarray([[N_META + BLOCK - (N_META - 1)]]))
    assert far[0, 0] == N_BUCKETS - 1, "meta keys of later blocks must share the last bucket"
    bkt = np.stack([np.concatenate([bkt_w, bkt_m0]), np.concatenate([bkt_w, np.full_like(bkt_m0, far[0, 0])])])
    meta_ok = np.ones((N_META, BLOCK), bool)
    valid = np.stack([np.concatenate([in_window & (ki >= BLOCK), meta_ok]), np.concatenate([in_window, meta_ok])])
    spec = pl.BlockSpec((None, SWA_KEYS, BLOCK), lambda a: (a, 0, 0))
    return pl.pallas_call(
        _bias_kernel,
        grid=(2,),
        in_specs=[pl.BlockSpec(memory_space=pltpu.SMEM), spec, spec],
        out_specs=pl.BlockSpec((None, SWA_KEYS, SWA_Q_HEADS * BLOCK), lambda a: (a, 0, 0)),
        out_shape=jax.ShapeDtypeStruct((2, SWA_KEYS, SWA_Q_HEADS * BLOCK), F32),
        name="bias_tables",
    )(rel_bias.astype(F32), jnp.asarray(bkt.astype(np.int32)), jnp.asarray(valid.astype(np.int32)))


def _fox_aug8(val_row, pieces_first):
    n = val_row.shape[1]
    row = lax.broadcasted_iota(jnp.int32, (SUBLANES, n), 0)
    p0, o0 = (0, N_PIECES) if pieces_first else (N_PIECES, 0)
    aug = jnp.where((row >= o0) & (row < o0 + N_PIECES), 1.0, 0.0)
    for i, piece in enumerate(_bf16_pieces(val_row)):
        aug = jnp.where(row == p0 + i, piece.astype(F32), aug)
    return aug


def _fox_head_t(own_t, aug8, e):
    pad = jnp.zeros((HEAD_DIM - SUBLANES, own_t.shape[1]), F32)
    return jnp.concatenate([own_t, aug8, pad] if e == 0 else [aug8, pad, own_t], axis=0)


def _inproj_kernel(x_ref, g_ref, w_ref, wf_ref, bft_ref, tri_ref, c0_ref,
                   pa_ref, qt_ref, ka_ref, vt_ref, cum_ref, qn2_ref, kmx_ref, cmeta_ref, carry_ref):
    @pl.when(pl.program_id(1) == 0)
    def _():
        carry_ref[...] = c0_ref[...]

    tm = x_ref.shape[0]
    half = tm // 2
    y_top = _rms(x_ref[0:half, :], g_ref[...]).astype(BF16)
    fox_top = _dot(y_top, w_ref[:, OFF_VA:OFF_VB])
    y = jnp.concatenate([y_top, _rms(x_ref[half:tm, :], g_ref[...]).astype(BF16)], axis=0)
    tile_lanes = lambda a: jnp.concatenate([a] * (tm // LANES), axis=1)
    f_t = lax.dot_general(wf_ref[...], y, (((0,), (1,)), ((), ())), preferred_element_type=F32) + tile_lanes(bft_ref[...])
    acc_fox = jnp.concatenate([fox_top, _dot(y[half:tm], w_ref[:, OFF_VA:OFF_VB])], axis=0)
    ls_t = jnp.minimum(f_t, 0.0) - jnp.log1p(jnp.exp(-jnp.abs(f_t)))
    cum_t = tile_lanes(carry_ref[...])
    for piece in _bf16_pieces(ls_t):
        cum_t = cum_t + _dot(piece, tri_ref[...])
    carry_ref[...] = jnp.broadcast_to(cum_t[:, tm - 1:tm], carry_ref.shape)
    cmeta_ref[...] = jnp.broadcast_to(cum_t[:, N_META - 1:N_META], cmeta_ref.shape)
    for g in range(FOX_HEADS // FOX_GROUP):
        cum_ref[g] = cum_t[g * FOX_GROUP:(g + 1) * FOX_GROUP]
    cb_t = cum_t * (-LOG2E)

    acc_swa = _dot(y, w_ref[:, 0:OFF_VA])
    pa_ref[:, 0:OFF_QA] = (acc_swa[:, 0:OFF_QA] * (SCALE * LOG2E)).astype(BF16)
    pa_ref[:, OFF_QA:OFF_VA] = acc_swa[:, OFF_QA:OFF_VA].astype(BF16)

    ones = jnp.ones((FOX_VROWS - HEAD_DIM, tm), F32)
    for p in range(FOX_HEADS // 2):
        cols = lambda off: slice(off - OFF_VA + p * PAIR_W, off - OFF_VA + (p + 1) * PAIR_W)
        q_t = (acc_fox[:, cols(OFF_VA)] * (SCALE * LOG2E)).astype(BF16).astype(F32).T
        k_t = acc_fox[:, cols(OFF_QB)].astype(BF16).astype(F32).T
        v_t = acc_fox[:, cols(OFF_KB)].astype(BF16).astype(F32).T
        qt_ref[p] = q_t.astype(BF16)
        for e in range(2):
            h = 2 * p + e
            own = slice(e * HEAD_DIM, (e + 1) * HEAD_DIM)
            qn2_ref[h // FOX_GROUP, h % FOX_GROUP:h % FOX_GROUP + 1, :] = jnp.sum(q_t[own] * q_t[own], axis=0, keepdims=True)
            kn2 = jnp.sum(k_t[own] * k_t[own], axis=0, keepdims=True)
            kmx_ref[h:h + 1, :] = jnp.broadcast_to(jnp.max(kn2, axis=1, keepdims=True), (1, LANES))
            ka_ref[h] = _fox_head_t(k_t[own], _fox_aug8(cb_t[h:h + 1], True), e).T.astype(BF16)
            vt_ref[h] = jnp.concatenate([v_t[own], ones], axis=0).astype(BF16)


def _inproj(x3, g, w, wft, bft, c0, tm):
    nb, rows, _ = x3.shape
    nt = rows // tm
    tri = jnp.asarray(np.triu(np.ones((tm, tm), np.float32)), BF16)
    const = lambda b, t: (0, 0)
    npair = FOX_HEADS // 2
    ngroup = FOX_HEADS // FOX_GROUP
    return pl.pallas_call(
        _inproj_kernel,
        grid=(nb, nt),
        in_specs=[
            pl.BlockSpec((None, tm, D_MODEL), lambda b, t: (b, t, 0)),
            pl.BlockSpec((1, D_MODEL), const),
            pl.BlockSpec(w.shape, const),
            pl.BlockSpec((D_MODEL, GATE_ROWS), const),
            pl.BlockSpec((GATE_ROWS, LANES), const),
            pl.BlockSpec((tm, tm), const),
            pl.BlockSpec((GATE_ROWS, LANES), const),
        ],
        out_specs=[
            pl.BlockSpec((None, tm, OFF_VA), lambda b, t: (b, t, 0)),
            pl.BlockSpec((None, npair, PAIR_W, tm), lambda b, t: (b, 0, 0, t)),
            pl.BlockSpec((None, FOX_HEADS, tm, PAIR_W), lambda b, t: (b, 0, t, 0)),
            pl.BlockSpec((None, FOX_HEADS, FOX_VROWS, tm), lambda b, t: (b, 0, 0, t)),
            pl.BlockSpec((None, ngroup, FOX_GROUP, tm), lambda b, t: (b, 0, 0, t)),
            pl.BlockSpec((None, ngroup, FOX_GROUP, tm), lambda b, t: (b, 0, 0, t)),
            pl.BlockSpec((None, None, FOX_HEADS, LANES), lambda b, t: (b, t, 0, 0)),
            pl.BlockSpec((None, None, GATE_ROWS, LANES), lambda b, t: (b, t, 0, 0)),
        ],
        out_shape=[
            jax.ShapeDtypeStruct((nb, rows, OFF_VA), BF16),
            jax.ShapeDtypeStruct((nb, npair, PAIR_W, rows), BF16),
            jax.ShapeDtypeStruct((nb, FOX_HEADS, rows, PAIR_W), BF16),
            jax.ShapeDtypeStruct((nb, FOX_HEADS, FOX_VROWS, rows), BF16),
            jax.ShapeDtypeStruct((nb, ngroup, FOX_GROUP, rows), F32),
            jax.ShapeDtypeStruct((nb, ngroup, FOX_GROUP, rows), F32),
            jax.ShapeDtypeStruct((nb, nt, FOX_HEADS, LANES), F32),
            jax.ShapeDtypeStruct((nb, nt, GATE_ROWS, LANES), F32),
        ],
        scratch_shapes=[pltpu.VMEM((GATE_ROWS, LANES), F32)],
        compiler_params=pltpu.CompilerParams(
            dimension_semantics=("arbitrary", "arbitrary"), vmem_limit_bytes=VMEM_LIMIT),
        name="inproj",
    )(x3, g, w, wft, bft, tri, c0)


def _swa_scores(q, k_all, bias):
    nq = q.shape[0]
    zeros = jnp.zeros((HEAD_DIM, nq), F32)
    blocks = []
    for pair in range(SWA_Q_HEADS // 2):
        q_t = q[:, pair * PAIR_W:(pair + 1) * PAIR_W].astype(F32).T
        for hp in range(2):
            qh = q_t[hp * HEAD_DIM:(hp + 1) * HEAD_DIM]
            g = (2 * pair + hp) // SWA_GROUP
            blocks.append(jnp.concatenate([qh, zeros] if g == 0 else [zeros, qh], axis=0))
    q_bd = jnp.concatenate(blocks, axis=1).astype(BF16)
    return _dot(k_all, q_bd) + bias


def _swa_outputs(s, v_all, sink_ref):
    nq = s.shape[1] // SWA_Q_HEADS
    ps, sink_terms = [], []
    for h in range(SWA_Q_HEADS):
        sh = s[:, h * nq:(h + 1) * nq]
        sink = sink_ref[h] * LOG2E
        m = jnp.maximum(jnp.max(sh, axis=0, keepdims=True), sink)
        ps.append(jnp.exp2(sh - m).astype(BF16))
        sink_terms.append(jnp.exp2(sink - m))
    n = BLOCK
    v_t = jnp.concatenate([v_all[0:n].astype(F32).T, v_all[n:2 * n].astype(F32).T,
                           v_all[2 * n:].astype(F32).T], axis=1)
    ones = jnp.ones((SWA_ONES, v_t.shape[1]), F32)
    outs = []
    for g in range(SWA_KV_HEADS):
        v_aug = jnp.concatenate([v_t[g * HEAD_DIM:(g + 1) * HEAD_DIM], ones], axis=0).astype(BF16)
        p_g = jnp.concatenate(ps[g * SWA_GROUP:(g + 1) * SWA_GROUP], axis=1)
        o_g = _dot(v_aug, p_g)
        for j in range(SWA_GROUP):
            cols = slice(j * nq, (j + 1) * nq)
            l = o_g[HEAD_DIM:HEAD_DIM + 1, cols] + sink_terms[g * SWA_GROUP + j]
            outs.append(o_g[0:HEAD_DIM, cols] / l)
    pairs = [jnp.concatenate(outs[2 * p:2 * p + 2], axis=0).T for p in range(SWA_Q_HEADS // 2)]
    return jnp.concatenate(pairs, axis=1).astype(BF16)


def _swa_kernel(sink_ref, q_ref, kc_ref, kp_ref, vc_ref, vp_ref, km_ref, vm_ref, bias0_ref, bias1_ref, o_ref):
    n = BLOCK
    nsub = q_ref.shape[0] // n
    k_rows = [kp_ref[...]] + [kc_ref[c * n:(c + 1) * n] for c in range(nsub)]
    v_rows = [vp_ref[...]] + [vc_ref[c * n:(c + 1) * n] for c in range(nsub)]
    def block_scores(c):
        k_all = jnp.concatenate([k_rows[c], k_rows[c + 1], km_ref[...]], axis=0)
        bias = bias0_ref[...] if c == 0 else bias1_ref[...]
        return _swa_scores(q_ref[c * n:(c + 1) * n], k_all, bias)

    s_next = block_scores(0)
    for c in range(nsub):
        s_cur = s_next
        if c + 1 < nsub:
            s_next = block_scores(c + 1)
        v_all = jnp.concatenate([v_rows[c], v_rows[c + 1], vm_ref[...]], axis=0)
        o_ref[c * n:(c + 1) * n] = _swa_outputs(s_cur, v_all, sink_ref)


def _swa(proj, projm, bias, sinks):
    nb, rows, _ = proj.shape
    tq = SWA_TQ
    ratio = tq // BLOCK
    kcol, vcol = OFF_QA // SWA_KV_W, OFF_KA // SWA_KV_W
    cur = lambda c: (lambda b, n: (b, n, c))
    prev = lambda c: (lambda b, n: (b, jnp.maximum(n * ratio - 1, 0), c))
    bias_spec = lambda idx: pl.BlockSpec((None, SWA_KEYS, SWA_Q_HEADS * BLOCK), idx)
    return pl.pallas_call(
        _swa_kernel,
        grid=(nb, rows // tq),
        in_specs=[
            pl.BlockSpec(memory_space=pltpu.SMEM),
            pl.BlockSpec((None, tq, SWA_Q_W), lambda b, n: (b, n, 0)),
            pl.BlockSpec((None, tq, SWA_KV_W), cur(kcol)),
            pl.BlockSpec((None, BLOCK, SWA_KV_W), prev(kcol)),
            pl.BlockSpec((None, tq, SWA_KV_W), cur(vcol)),
            pl.BlockSpec((None, BLOCK, SWA_KV_W), prev(vcol)),
            pl.BlockSpec((None, N_META, SWA_KV_W), lambda b, n: (0, 0, kcol)),
            pl.BlockSpec((None, N_META, SWA_KV_W), lambda b, n: (0, 0, vcol)),
            bias_spec(lambda b, n: (jnp.minimum(n, 1), 0, 0)),
            bias_spec(lambda b, n: (1, 0, 0)),
        ],
        out_specs=pl.BlockSpec((None, tq, SWA_Q_W), lambda b, n: (b, n, 0)),
        out_shape=jax.ShapeDtypeStruct((nb, rows, SWA_Q_W), BF16),
        compiler_params=pltpu.CompilerParams(
            dimension_semantics=("arbitrary", "arbitrary"), vmem_limit_bytes=VMEM_LIMIT),
        name="swa",
    )(sinks, proj, proj, proj, proj, proj, projm, projm, bias, bias)


def _fox_kernel(qt_ref, ka_ref, vt_ref, kam_ref, vtm_ref, cq_ref, qn2_ref, kmx_ref, kmxm_ref,
                wo_ref, wgu_ref, wd_ref, acc_ref, wo_bf_ref, wgu_bf_ref, wd_bf_ref):
    wo_bf_ref[...] = wo_ref[...].astype(BF16)
    wgu_bf_ref[...] = wgu_ref[...].astype(BF16)
    wd_bf_ref[...] = wd_ref[...].astype(BF16)

    i = pl.program_id(2)
    t = acc_ref.shape[2]
    nh = acc_ref.shape[0]
    colmax = lambda x: jnp.max(x, axis=0, keepdims=True)
    tile_rows = lambda j: pl.ds(pl.multiple_of(j * t, t), t)

    heads = pl.ds(pl.program_id(1) * nh, nh)
    kmax2 = jnp.maximum(jnp.max(kmx_ref[:, heads, :], axis=0), kmxm_ref[heads, :])
    r = [jnp.sqrt(qn2_ref[h:h + 1, :] * kmax2[h:h + 1, 0:1]) * FOX_BOUND_MARGIN for h in range(nh)]
    worst = r[0]
    for rh in r[1:]:
        worst = jnp.maximum(worst, rh)
    safe = jnp.max(worst) <= FOX_SAFE_LOG2

    qt = []
    for h in range(nh):
        e = h % 2
        u = cq_ref[h:h + 1, :] * LOG2E - jnp.where(safe, r[h], 0.0)
        own = qt_ref[h // 2, e * HEAD_DIM:(e + 1) * HEAD_DIM, :].astype(F32)
        qt.append(_fox_head_t(own, _fox_aug8(u, False), e).astype(BF16))

    def scores(h, j, diagonal):
        s = _dot(ka_ref[h, tile_rows(j), :], qt[h])
        if diagonal:
            key = lax.broadcasted_iota(jnp.int32, (t, t), 0)
            qry = lax.broadcasted_iota(jnp.int32, (t, t), 1)
            s = jnp.where(key <= qry, s, NEG_INF)
        return s

    @pl.when(safe)
    def _():
        def run(work):
            prev = None
            for score_fn, pv_fn in work:
                s = score_fn()
                if prev is not None:
                    prev[0](prev[1])
                prev = (pv_fn, s)
            prev[0](prev[1])

        def tile_work(j):
            def pv(h):
                def apply(s):
                    acc_ref[h] += _dot(vt_ref[h, :, tile_rows(j)], jnp.exp2(s).astype(BF16))
                return apply
            return [(lambda h=h: scores(h, j, False), pv(h)) for h in range(nh)]

        def first_work():
            def sc(h):
                return jnp.concatenate([scores(h, i, True), _dot(kam_ref[h], qt[h])], axis=0)
            def pv(h):
                def apply(s):
                    v_t = jnp.concatenate([vt_ref[h, :, tile_rows(i)], vtm_ref[h, :, 0:N_META]], axis=1)
                    acc_ref[h] = _dot(v_t, jnp.exp2(s).astype(BF16))
                return apply
            return [(lambda h=h: sc(h), pv(h)) for h in range(nh)]

        run(first_work())

        def tile_pair(p, carry):
            run(tile_work(2 * p) + tile_work(2 * p + 1))
            return carry

        lax.fori_loop(0, i // 2, tile_pair, 0)

        @pl.when(i % 2 == 1)
        def _():
            run(tile_work(i - 1))

    @pl.when(jnp.logical_not(safe))
    def _():
        m0 = []
        for h in range(nh):
            s = _dot(kam_ref[h], qt[h])
            m = colmax(s)
            acc_ref[h] = _dot(vtm_ref[h, :, 0:N_META], jnp.exp2(s - m).astype(BF16))
            m0.append(m)

        def tile(j, ms, diagonal):
            out = []
            ss = [scores(h, j, diagonal) for h in range(nh)]
            for h in range(nh):
                s = ss[h]
                m_next = jnp.maximum(ms[h], colmax(s))
                p = jnp.exp2(s - m_next).astype(BF16)
                acc_ref[h] = jnp.exp2(ms[h] - m_next) * acc_ref[h] + _dot(vt_ref[h, :, tile_rows(j)], p)
                out.append(m_next)
            return tuple(out)

        ms = lax.fori_loop(0, i, lambda j, ms: tile(j, ms, False), tuple(m0))
        tile(i, ms, True)


def _fox(qt, ka, vt, ka_m, vt_m, cum, qn2, kmx, kmx_m, w_out, w_gate_up, w_down):
    nb, _, rows, _ = ka.shape
    t = FOX_T
    nh = FOX_GROUP
    ngroup = FOX_HEADS // nh
    nt = kmx.shape[1]
    nq = rows // t
    nsteps = nb * ngroup * nq
    slab = lambda w: min(r for r in range(SLAB_ALIGN, w.shape[0] + 1, SLAB_ALIGN)
                         if w.shape[0] % r == 0 and r * nsteps >= w.shape[0])
    step = lambda b, g, i: (b * ngroup + g) * nq + i
    w_spec = lambda w: pl.BlockSpec((slab(w), w.shape[1]),
                                    lambda b, g, i: (jnp.minimum(step(b, g, i), w.shape[0] // slab(w) - 1), 0))
    return pl.pallas_call(
        _fox_kernel,
        grid=(nb, ngroup, nq),
        in_specs=[
            pl.BlockSpec((None, nh // 2, PAIR_W, t), lambda b, g, i: (b, g, 0, i)),
            pl.BlockSpec((None, nh, rows, PAIR_W), lambda b, g, i: (b, g, 0, 0)),
            pl.BlockSpec((None, nh, FOX_VROWS, rows), lambda b, g, i: (b, g, 0, 0)),
            pl.BlockSpec((None, nh, N_META, PAIR_W), lambda b, g, i: (0, g, 0, 0)),
            pl.BlockSpec((None, nh, FOX_VROWS, META_TM), lambda b, g, i: (0, g, 0, 0)),
            pl.BlockSpec((None, None, nh, t), lambda b, g, i: (b, g, 0, i)),
            pl.BlockSpec((None, None, nh, t), lambda b, g, i: (b, g, 0, i)),
            pl.BlockSpec((None, nt, FOX_HEADS, LANES), lambda b, g, i: (b, 0, 0, 0)),
            pl.BlockSpec((None, None, FOX_HEADS, LANES), lambda b, g, i: (0, 0, 0, 0)),
            w_spec(w_out), w_spec(w_gate_up), w_spec(w_down),
        ],
        out_specs=[pl.BlockSpec((None, nh, FOX_VROWS, t), lambda b, g, i: (b, g, 0, i)),
                   w_spec(w_out), w_spec(w_gate_up), w_spec(w_down)],
        out_shape=[jax.ShapeDtypeStruct((nb, FOX_HEADS, FOX_VROWS, rows), F32)]
        + [jax.ShapeDtypeStruct(w.shape, BF16) for w in (w_out, w_gate_up, w_down)],
        compiler_params=pltpu.CompilerParams(
            dimension_semantics=("arbitrary", "arbitrary", "arbitrary"), vmem_limit_bytes=VMEM_LIMIT),
        name="fox",
    )(qt, ka, vt, ka_m, vt_m, cum, qn2,
      kmx, kmx_m, w_out, w_gate_up, w_down)


def _ffn_kernel(oa_ref, fox_ref, x_ref, wo_ref, g1_ref, g2_ref, wg_ref, wu_ref, wd_ref, g3_ref, out_ref):
    tm = x_ref.shape[0]
    parts = [slice(c * tm // FFN_PARTS, (c + 1) * tm // FFN_PARTS) for c in range(FFN_PARTS)]
    def fox_rows(r):
        pairs = []
        for p in range(FOX_HEADS // 2):
            o_t = jnp.concatenate([fox_ref[h, 0:HEAD_DIM, r] / fox_ref[h, HEAD_DIM:HEAD_DIM + 1, r]
                                   for h in (2 * p, 2 * p + 1)], axis=0)
            pairs.append(o_t.T.astype(BF16))
        return jnp.concatenate(pairs, axis=1)

    mix = [jnp.concatenate([oa_ref[r, :], fox_rows(r)], axis=1) for r in parts]
    a = [_dot(m, wo_ref[...]) for m in mix]
    h1 = [x_ref[r, :] + _rms(ai, g1_ref[...]) for r, ai in zip(parts, a)]
    hn = [_rms(h, g2_ref[...]).astype(BF16) for h in h1]
    gate_up = [(_dot(h, wg_ref[...]), _dot(h, wu_ref[...])) for h in hn]
    act = [(g / (1.0 + jnp.exp(-g)) * u).astype(BF16) for g, u in gate_up]
    ff = [_dot(ac, wd_ref[...]) for ac in act]
    for r, h, f in zip(parts, h1, ff):
        out_ref[r, :] = h + _rms(f, g3_ref[...])


def _ffn(o_a, o_b, x3, wo, g1, g2, wgu, wd, g3):
    nb, rows, _ = x3.shape
    tm = FFN_TM
    const = lambda b, t: (0, 0)
    resident = lambda shape: pl.BlockSpec(shape, const, pipeline_mode=pl.Buffered(1))
    row = lambda w: pl.BlockSpec((None, tm, w), lambda b, t: (b, t, 0))
    return pl.pallas_call(
        _ffn_kernel,
        grid=(nb, rows // tm),
        in_specs=[
            row(SWA_Q_W), pl.BlockSpec((None, FOX_HEADS, FOX_VROWS, tm), lambda b, t: (b, 0, 0, t)), row(D_MODEL),
            resident((D_MODEL, D_MODEL)), resident((1, D_MODEL)), resident((1, D_MODEL)),
            pl.BlockSpec((D_MODEL, D_FF), lambda b, t: (0, 0), pipeline_mode=pl.Buffered(1)),
            pl.BlockSpec((D_MODEL, D_FF), lambda b, t: (0, 1), pipeline_mode=pl.Buffered(1)),
            resident((D_FF, D_MODEL)),
            resident((1, D_MODEL)),
        ],
        out_specs=row(D_MODEL),
        out_shape=jax.ShapeDtypeStruct((nb, rows, D_MODEL), F32),
        compiler_params=pltpu.CompilerParams(
            dimension_semantics=("arbitrary", "arbitrary"), vmem_limit_bytes=VMEM_LIMIT),
        name="outproj_ffn",
    )(o_a, o_b, x3, wo, g1, g2, wgu, wgu, wd, g3)


def kernel(x, meta_tokens, rel_bias, ln_pre_mix, ln_post_mix, ln_pre_ffn, ln_post_ffn,
           w_in, b_forget, sinks, w_out, w_gate_up, w_down):
    nb, seq, d = x.shape
    assert d == D_MODEL and seq % IN_TM == 0 and seq % FOX_T == 0 and seq % FFN_TM == 0 and seq % SWA_TQ == 0
    assert w_in.shape[0] == 1, "single-layer block"
    assert meta_tokens.shape == (N_META, D_MODEL)
    x = x.astype(F32)

    w_all = w_in[0].astype(BF16)
    pad_gate = GATE_ROWS - FOX_HEADS
    w_ft = jnp.pad(w_in[0, :, OFF_VB:], ((0, 0), (0, pad_gate))).astype(BF16)
    lane_rep = lambda v: jnp.broadcast_to(jnp.pad(v.astype(F32), (0, pad_gate))[:, None], (GATE_ROWS, LANES))
    b_ft = lane_rep(b_forget[0])
    g_pre = ln_pre_mix[0].astype(F32).reshape(1, D_MODEL)

    swa_bias = _bias_tables(rel_bias)

    x_m = jnp.pad(meta_tokens.astype(F32), ((0, META_TM - N_META), (0, 0)))[None]
    pa_m, _, ka_m, vt_m, _, _, kmx_m, c_meta = _inproj(x_m, g_pre, w_all, w_ft, b_ft,
                                                       jnp.zeros((GATE_ROWS, LANES), F32), META_TM)
    pa, qt, ka, vt, cum, qn2, kmx, _ = _inproj(x, g_pre, w_all, w_ft, b_ft, c_meta[0, 0], IN_TM)

    o_a = _swa(pa, pa_m, swa_bias, sinks[0].astype(F32))
    o_b, w_out_bf, w_gu_bf, w_down_bf = _fox(qt, ka, vt, ka_m, vt_m, cum, qn2, kmx,
                                             kmx_m, w_out[0].astype(F32), w_gate_up[0].astype(F32),
                                             w_down[0].astype(F32))

    return _ffn(o_a, o_b, x,
                w_out_bf,
                ln_post_mix[0].astype(F32).reshape(1, D_MODEL),
                ln_pre_ffn[0].astype(F32).reshape(1, D_MODEL),
                w_gu_bf,
                w_down_bf,
                ln_post_ffn[0].astype(F32).reshape(1, D_MODEL))
```

```python
import math

import numpy as np
import jax
import jax.numpy as jnp
from jax import lax
from jax.experimental import pallas as pl
from jax.experimental.pallas import tpu as pltpu

D_MODEL = 1024
N_META = 16
HEAD_DIM = 64
SWA_Q_HEADS = 8
SWA_KV_HEADS = 2
SWA_GROUP = SWA_Q_HEADS // SWA_KV_HEADS
FOX_HEADS = 8
SWA_Q_W = SWA_Q_HEADS * HEAD_DIM
SWA_KV_W = SWA_KV_HEADS * HEAD_DIM
FOX_W = FOX_HEADS * HEAD_DIM
OFF_QA = SWA_Q_W
OFF_KA = OFF_QA + SWA_KV_W
OFF_VA = OFF_KA + SWA_KV_W
OFF_QB = OFF_VA + FOX_W
OFF_KB = OFF_QB + FOX_W
OFF_VB = OFF_KB + FOX_W
WINDOW = 128
BLOCK = 128
N_BUCKETS = 32
MAX_DISTANCE = 128
D_FF = 2816
EPS = 1e-6
NEG_INF = -1e30
SCALE = HEAD_DIM ** -0.5
LOG2E = math.log2(math.e)

LANES = 128
SUBLANES = 8
PAIR_W = 2 * HEAD_DIM
VMEM_LIMIT = 56 * 1024 * 1024

SWA_KEYS = 2 * BLOCK + N_META
SWA_TQ = 2048
SWA_ONES = 16
IN_TM = 512
META_TM = 128
GATE_ROWS = 16
FOX_VROWS = HEAD_DIM + 16
FOX_T = 512
FOX_GROUP = 4
FOX_SAFE_LOG2 = 40.0
FOX_BOUND_MARGIN = 1.001
FFN_TM = 512
FFN_PARTS = 2
SLAB_ALIGN = 16
N_PIECES = 3
AUG_ROWS = 16

F32 = jnp.float32
BF16 = jnp.bfloat16


def _dot(a, b):
    return jnp.dot(a, b, preferred_element_type=F32)


def _rms(t, g):
    return t * lax.rsqrt(jnp.mean(t * t, axis=-1, keepdims=True) + EPS) * g


def _bf16_pieces(v):
    pieces = []
    for _ in range(N_PIECES - 1):
        p = v.astype(BF16)
        pieces.append(p)
        v = v - p.astype(F32)
    pieces.append(v.astype(BF16))
    return pieces


def _t5_bucket_np(dist):
    n = np.maximum(dist, 0).astype(np.int32)
    max_exact = N_BUCKETS // 2
    nf = np.maximum(n, 1).astype(np.float32)
    large = max_exact + (np.log(nf / np.float32(max_exact)) / np.float32(math.log(MAX_DISTANCE / max_exact))
                         * np.float32(N_BUCKETS - max_exact)).astype(np.int32)
    large = np.minimum(large, N_BUCKETS - 1)
    return np.where(n < max_exact, n, large).astype(np.int32)


def _bias_kernel(tab_ref, bkt_ref, valid_ref, out_ref):
    bkt = bkt_ref[...]
    valid = valid_ref[...] > 0
    for h in range(SWA_Q_HEADS):
        acc = jnp.zeros(bkt.shape, F32)
        for b in range(N_BUCKETS):
            acc = jnp.where(bkt == b, tab_ref[b, h], acc)
        out_ref[:, h * BLOCK:(h + 1) * BLOCK] = jnp.where(valid, acc * LOG2E, NEG_INF)


def _bias_tables(rel_bias):
    ki = np.arange(2 * BLOCK)[:, None]
    qi = np.arange(BLOCK)[None, :]
    d_w = qi + BLOCK - ki
    bkt_w = _t5_bucket_np(d_w)
    in_window = (d_w >= 0) & (d_w < WINDOW)
    mi = np.arange(N_META)[:, None]
    bkt_m0 = _t5_bucket_np(N_META + qi - mi)
    far = _t5_bucket_np(np.asarray([[N_META + BLOCK - (N_META - 1)]]))
    assert far[0, 0] == N_BUCKETS - 1, "meta keys of later blocks must share the last bucket"
    bkt = np.stack([np.concatenate([bkt_w, bkt_m0]), np.concatenate([bkt_w, np.full_like(bkt_m0, far[0, 0])])])
    meta_ok = np.ones((N_META, BLOCK), bool)
    valid = np.stack([np.concatenate([in_window & (ki >= BLOCK), meta_ok]), np.concatenate([in_window, meta_ok])])
    spec = pl.BlockSpec((None, SWA_KEYS, BLOCK), lambda a: (a, 0, 0))
    return pl.pallas_call(
        _bias_kernel,
        grid=(2,),
        in_specs=[pl.BlockSpec(memory_space=pltpu.SMEM), spec, spec],
        out_specs=pl.BlockSpec((None, SWA_KEYS, SWA_Q_HEADS * BLOCK), lambda a: (a, 0, 0)),
        out_shape=jax.ShapeDtypeStruct((2, SWA_KEYS, SWA_Q_HEADS * BLOCK), F32),
        name="bias_tables",
    )(rel_bias.astype(F32), jnp.asarray(bkt.astype(np.int32)), jnp.asarray(valid.astype(np.int32)))


def _fox_aug_rows(val_rows, key_side):
    n = val_rows[0].shape[1]
    row = lax.broadcasted_iota(jnp.int32, (AUG_ROWS, n), 0)
    ones = (row >= N_PIECES) & (row < 3 * N_PIECES) if key_side else (row < N_PIECES)
    aug = jnp.where(ones, 1.0, 0.0)
    first = 0 if key_side else N_PIECES
    for g, val in enumerate(val_rows):
        for i, piece in enumerate(_bf16_pieces(val)):
            aug = jnp.where(row == first + g * N_PIECES + i, piece.astype(F32), aug)
    return aug


def _fox_head_t(own_t, aug, e):
    pad = jnp.zeros((HEAD_DIM - AUG_ROWS, own_t.shape[1]), F32)
    return jnp.concatenate([own_t, aug, pad] if e == 0 else [aug, pad, own_t], axis=0)


def _inproj_kernel(x_ref, g_ref, w_ref, wf_ref, bft_ref, tri_ref, c0_ref, k0_ref,
                   pa_ref, qt_ref, ka_ref, vt_ref, kpre_ref, rmax_ref, cmeta_ref, carry_ref, kpre_scr):
    @pl.when(pl.program_id(1) == 0)
    def _():
        carry_ref[...] = c0_ref[...]
        kpre_scr[...] = k0_ref[...]

    tm = x_ref.shape[0]
    half = tm // 2
    y_top = _rms(x_ref[0:half, :], g_ref[...]).astype(BF16)
    fox_top = _dot(y_top, w_ref[:, OFF_VA:OFF_VB])
    y = jnp.concatenate([y_top, _rms(x_ref[half:tm, :], g_ref[...]).astype(BF16)], axis=0)
    tile_lanes = lambda a: jnp.concatenate([a] * (tm // LANES), axis=1)
    f_t = lax.dot_general(wf_ref[...], y, (((0,), (1,)), ((), ())), preferred_element_type=F32) + tile_lanes(bft_ref[...])
    acc_fox = jnp.concatenate([fox_top, _dot(y[half:tm], w_ref[:, OFF_VA:OFF_VB])], axis=0)
    ls_t = jnp.minimum(f_t, 0.0) - jnp.log1p(jnp.exp(-jnp.abs(f_t)))
    cum_t = tile_lanes(carry_ref[...])
    for piece in _bf16_pieces(ls_t):
        cum_t = cum_t + _dot(piece, tri_ref[...])
    carry_ref[...] = jnp.broadcast_to(cum_t[:, tm - 1:tm], carry_ref.shape)
    cmeta_ref[...] = jnp.broadcast_to(cum_t[:, N_META - 1:N_META], cmeta_ref.shape)
    cb_t = cum_t * (-LOG2E)

    acc_swa = _dot(y, w_ref[:, 0:OFF_VA])
    pa_ref[:, 0:OFF_QA] = (acc_swa[:, 0:OFF_QA] * (SCALE * LOG2E)).astype(BF16)
    pa_ref[:, OFF_QA:OFF_VA] = acc_swa[:, OFF_QA:OFF_VA].astype(BF16)

    ones = jnp.ones((FOX_VROWS - HEAD_DIM, tm), F32)
    for p in range(FOX_HEADS // 2):
        cols = lambda off: slice(off - OFF_VA + p * PAIR_W, off - OFF_VA + (p + 1) * PAIR_W)
        q_t = (acc_fox[:, cols(OFF_VA)] * (SCALE * LOG2E)).astype(BF16).astype(F32).T
        k_t = acc_fox[:, cols(OFF_QB)].astype(BF16).astype(F32).T
        v_t = acc_fox[:, cols(OFF_KB)].astype(BF16).astype(F32).T
        for e in range(2):
            h = 2 * p + e
            own = slice(e * HEAD_DIM, (e + 1) * HEAD_DIM)
            qn2 = jnp.sum(q_t[own] * q_t[own], axis=0, keepdims=True)
            kn2 = jnp.max(jnp.sum(k_t[own] * k_t[own], axis=0, keepdims=True), axis=1, keepdims=True)
            kpre = jnp.maximum(kpre_scr[h:h + 1, :], kn2)
            kpre_scr[h:h + 1, :] = kpre
            kpre_ref[h:h + 1, :] = kpre
            r = jnp.sqrt(qn2 * kpre[:, 0:1]) * FOX_BOUND_MARGIN
            rmax_ref[h:h + 1, :] = jnp.broadcast_to(jnp.max(r, axis=1, keepdims=True), (1, LANES))
            ka_ref[h] = _fox_head_t(k_t[own], _fox_aug_rows([cb_t[h:h + 1]], True), e).T.astype(BF16)
            qt_ref[h] = _fox_head_t(q_t[own], _fox_aug_rows([-cb_t[h:h + 1], -r], False), e).astype(BF16)
            vt_ref[h] = jnp.concatenate([v_t[own], ones], axis=0).astype(BF16)


def _inproj(x3, g, w, wft, bft, c0, k0, tm):
    nb, rows, _ = x3.shape
    nt = rows // tm
    tri = jnp.asarray(np.triu(np.ones((tm, tm), np.float32)), BF16)
    const = lambda b, t: (0, 0)
    return pl.pallas_call(
        _inproj_kernel,
        grid=(nb, nt),
        in_specs=[
            pl.BlockSpec((None, tm, D_MODEL), lambda b, t: (b, t, 0)),
            pl.BlockSpec((1, D_MODEL), const),
            pl.BlockSpec(w.shape, const),
            pl.BlockSpec((D_MODEL, GATE_ROWS), const),
            pl.BlockSpec((GATE_ROWS, LANES), const),
            pl.BlockSpec((tm, tm), const),
            pl.BlockSpec((GATE_ROWS, LANES), const),
            pl.BlockSpec((FOX_HEADS, LANES), const),
        ],
        out_specs=[
            pl.BlockSpec((None, tm, OFF_VA), lambda b, t: (b, t, 0)),
            pl.BlockSpec((None, FOX_HEADS, PAIR_W, tm), lambda b, t: (b, 0, 0, t)),
            pl.BlockSpec((None, FOX_HEADS, tm, PAIR_W), lambda b, t: (b, 0, t, 0)),
            pl.BlockSpec((None, FOX_HEADS, FOX_VROWS, tm), lambda b, t: (b, 0, 0, t)),
            pl.BlockSpec((None, None, FOX_HEADS, LANES), lambda b, t: (b, t, 0, 0)),
            pl.BlockSpec((None, None, FOX_HEADS, LANES), lambda b, t: (b, t, 0, 0)),
            pl.BlockSpec((None, None, GATE_ROWS, LANES), lambda b, t: (b, t, 0, 0)),
        ],
        out_shape=[
            jax.ShapeDtypeStruct((nb, rows, OFF_VA), BF16),
            jax.ShapeDtypeStruct((nb, FOX_HEADS, PAIR_W, rows), BF16),
            jax.ShapeDtypeStruct((nb, FOX_HEADS, rows, PAIR_W), BF16),
            jax.ShapeDtypeStruct((nb, FOX_HEADS, FOX_VROWS, rows), BF16),
            jax.ShapeDtypeStruct((nb, nt, FOX_HEADS, LANES), F32),
            jax.ShapeDtypeStruct((nb, nt, FOX_HEADS, LANES), F32),
            jax.ShapeDtypeStruct((nb, nt, GATE_ROWS, LANES), F32),
        ],
        scratch_shapes=[pltpu.VMEM((GATE_ROWS, LANES), F32), pltpu.VMEM((FOX_HEADS, LANES), F32)],
        compiler_params=pltpu.CompilerParams(
            dimension_semantics=("arbitrary", "arbitrary"), vmem_limit_bytes=VMEM_LIMIT),
        name="inproj",
    )(x3, g, w, wft, bft, tri, c0, k0)


def _swa_scores(q, k_all, bias):
    nq = q.shape[0]
    zeros = jnp.zeros((HEAD_DIM, nq), F32)
    blocks = []
    for pair in range(SWA_Q_HEADS // 2):
        q_t = q[:, pair * PAIR_W:(pair + 1) * PAIR_W].astype(F32).T
        for hp in range(2):
            qh = q_t[hp * HEAD_DIM:(hp + 1) * HEAD_DIM]
            g = (2 * pair + hp) // SWA_GROUP
            blocks.append(jnp.concatenate([qh, zeros] if g == 0 else [zeros, qh], axis=0))
    q_bd = jnp.concatenate(blocks, axis=1).astype(BF16)
    return _dot(k_all, q_bd) + bias


def _swa_outputs(s, v_all, sink_ref):
    nq = s.shape[1] // SWA_Q_HEADS
    ps, sink_terms = [], []
    for h in range(SWA_Q_HEADS):
        sh = s[:, h * nq:(h + 1) * nq]
        sink = sink_ref[h] * LOG2E
        m = jnp.maximum(jnp.max(sh, axis=0, keepdims=True), sink)
        ps.append(jnp.exp2(sh - m).astype(BF16))
        sink_terms.append(jnp.exp2(sink - m))
    n = BLOCK
    v_t = jnp.concatenate([v_all[0:n].astype(F32).T, v_all[n:2 * n].astype(F32).T,
                           v_all[2 * n:].astype(F32).T], axis=1)
    ones = jnp.ones((SWA_ONES, v_t.shape[1]), F32)
    outs = []
    for g in range(SWA_KV_HEADS):
        v_aug = jnp.concatenate([v_t[g * HEAD_DIM:(g + 1) * HEAD_DIM], ones], axis=0).astype(BF16)
        p_g = jnp.concatenate(ps[g * SWA_GROUP:(g + 1) * SWA_GROUP], axis=1)
        o_g = _dot(v_aug, p_g)
        for j in range(SWA_GROUP):
            cols = slice(j * nq, (j + 1) * nq)
            l = o_g[HEAD_DIM:HEAD_DIM + 1, cols] + sink_terms[g * SWA_GROUP + j]
            outs.append(o_g[0:HEAD_DIM, cols] / l)
    pairs = [jnp.concatenate(outs[2 * p:2 * p + 2], axis=0).T for p in range(SWA_Q_HEADS // 2)]
    return jnp.concatenate(pairs, axis=1).astype(BF16)


def _swa_kernel(sink_ref, q_ref, kc_ref, kp_ref, vc_ref, vp_ref, km_ref, vm_ref, bias0_ref, bias1_ref, o_ref):
    n = BLOCK
    nsub = q_ref.shape[0] // n
    k_rows = [kp_ref[...]] + [kc_ref[c * n:(c + 1) * n] for c in range(nsub)]
    v_rows = [vp_ref[...]] + [vc_ref[c * n:(c + 1) * n] for c in range(nsub)]
    def block_scores(c):
        k_all = jnp.concatenate([k_rows[c], k_rows[c + 1], km_ref[...]], axis=0)
        bias = bias0_ref[...] if c == 0 else bias1_ref[...]
        return _swa_scores(q_ref[c * n:(c + 1) * n], k_all, bias)

    s_next = block_scores(0)
    for c in range(nsub):
        s_cur = s_next
        if c + 1 < nsub:
            s_next = block_scores(c + 1)
        v_all = jnp.concatenate([v_rows[c], v_rows[c + 1], vm_ref[...]], axis=0)
        o_ref[c * n:(c + 1) * n] = _swa_outputs(s_cur, v_all, sink_ref)


def _swa(proj, projm, bias, sinks):
    nb, rows, _ = proj.shape
    tq = SWA_TQ
    ratio = tq // BLOCK
    kcol, vcol = OFF_QA // SWA_KV_W, OFF_KA // SWA_KV_W
    cur = lambda c: (lambda b, n: (b, n, c))
    prev = lambda c: (lambda b, n: (b, jnp.maximum(n * ratio - 1, 0), c))
    bias_spec = lambda idx: pl.BlockSpec((None, SWA_KEYS, SWA_Q_HEADS * BLOCK), idx)
    return pl.pallas_call(
        _swa_kernel,
        grid=(nb, rows // tq),
        in_specs=[
            pl.BlockSpec(memory_space=pltpu.SMEM),
            pl.BlockSpec((None, tq, SWA_Q_W), lambda b, n: (b, n, 0)),
            pl.BlockSpec((None, tq, SWA_KV_W), cur(kcol)),
            pl.BlockSpec((None, BLOCK, SWA_KV_W), prev(kcol)),
            pl.BlockSpec((None, tq, SWA_KV_W), cur(vcol)),
            pl.BlockSpec((None, BLOCK, SWA_KV_W), prev(vcol)),
            pl.BlockSpec((None, N_META, SWA_KV_W), lambda b, n: (0, 0, kcol)),
            pl.BlockSpec((None, N_META, SWA_KV_W), lambda b, n: (0, 0, vcol)),
            bias_spec(lambda b, n: (jnp.minimum(n, 1), 0, 0)),
            bias_spec(lambda b, n: (1, 0, 0)),
        ],
        out_specs=pl.BlockSpec((None, tq, SWA_Q_W), lambda b, n: (b, n, 0)),
        out_shape=jax.ShapeDtypeStruct((nb, rows, SWA_Q_W), BF16),
        compiler_params=pltpu.CompilerParams(
            dimension_semantics=("arbitrary", "arbitrary"), vmem_limit_bytes=VMEM_LIMIT),
        name="swa",
    )(sinks, proj, proj, proj, proj, proj, projm, projm, bias, bias)


def _fox_kernel(flags_ref, qt_ref, ka_ref, vt_ref, kam_ref, vtm_ref,
                wo_ref, wgu_ref, wd_ref, acc_ref, wo_bf_ref, wgu_bf_ref, wd_bf_ref):
    wo_bf_ref[...] = wo_ref[...].astype(BF16)
    wgu_bf_ref[...] = wgu_ref[...].astype(BF16)
    wd_bf_ref[...] = wd_ref[...].astype(BF16)

    b, g, i = pl.program_id(0), pl.program_id(1), pl.program_id(2)
    t = acc_ref.shape[2]
    nh = acc_ref.shape[0]
    colmax = lambda x: jnp.max(x, axis=0, keepdims=True)
    tile_rows = lambda j: pl.ds(pl.multiple_of(j * t, t), t)

    safe = flags_ref[(b * pl.num_programs(2) + i) * pl.num_programs(1) + g] > 0
    qt = [qt_ref[h] for h in range(nh)]

    def scores(h, j, diagonal):
        s = _dot(ka_ref[h, tile_rows(j), :], qt[h])
        if diagonal:
            key = lax.broadcasted_iota(jnp.int32, (t, t), 0)
            qry = lax.broadcasted_iota(jnp.int32, (t, t), 1)
            s = jnp.where(key <= qry, s, NEG_INF)
        return s

    @pl.when(safe)
    def _():
        def run(work):
            prev = None
            for score_fn, pv_fn in work:
                s = score_fn()
                if prev is not None:
                    prev[0](prev[1])
                prev = (pv_fn, s)
            prev[0](prev[1])

        def tile_work(j):
            def pv(h):
                def apply(s):
                    acc_ref[h] += _dot(vt_ref[h, :, tile_rows(j)], jnp.exp2(s).astype(BF16))
                return apply
            return [(lambda h=h: scores(h, j, False), pv(h)) for h in range(nh)]

        def first_work():
            def sc(h):
                return jnp.concatenate([scores(h, i, True), _dot(kam_ref[h], qt[h])], axis=0)
            def pv(h):
                def apply(s):
                    v_t = jnp.concatenate([vt_ref[h, :, tile_rows(i)], vtm_ref[h, :, 0:N_META]], axis=1)
                    acc_ref[h] = _dot(v_t, jnp.exp2(s).astype(BF16))
                return apply
            return [(lambda h=h: sc(h), pv(h)) for h in range(nh)]

        run(first_work())

        def tile_pair(p, carry):
            run(tile_work(2 * p) + tile_work(2 * p + 1))
            return carry

        lax.fori_loop(0, i // 2, tile_pair, 0)

        @pl.when(i % 2 == 1)
        def _():
            run(tile_work(i - 1))

    @pl.when(jnp.logical_not(safe))
    def _():
        m0 = []
        for h in range(nh):
            s = _dot(kam_ref[h], qt[h])
            m = colmax(s)
            acc_ref[h] = _dot(vtm_ref[h, :, 0:N_META], jnp.exp2(s - m).astype(BF16))
            m0.append(m)

        def tile(j, ms, diagonal):
            out = []
            ss = [scores(h, j, diagonal) for h in range(nh)]
            for h in range(nh):
                s = ss[h]
                m_next = jnp.maximum(ms[h], colmax(s))
                p = jnp.exp2(s - m_next).astype(BF16)
                acc_ref[h] = jnp.exp2(ms[h] - m_next) * acc_ref[h] + _dot(vt_ref[h, :, tile_rows(j)], p)
                out.append(m_next)
            return tuple(out)

        ms = lax.fori_loop(0, i, lambda j, ms: tile(j, ms, False), tuple(m0))
        tile(i, ms, True)


def _fox(flags, qt, ka, vt, ka_m, vt_m, w_out, w_gate_up, w_down):
    nb, _, rows, _ = ka.shape
    t = FOX_T
    nh = FOX_GROUP
    ngroup = FOX_HEADS // nh
    nq = rows // t
    nsteps = nb * ngroup * nq
    slab = lambda w: min(r for r in range(SLAB_ALIGN, w.shape[0] + 1, SLAB_ALIGN)
                         if w.shape[0] % r == 0 and r * nsteps >= w.shape[0])
    step = lambda b, g, i: (b * ngroup + g) * nq + i
    w_spec = lambda w: pl.BlockSpec((slab(w), w.shape[1]),
                                    lambda b, g, i, f: (jnp.minimum(step(b, g, i), w.shape[0] // slab(w) - 1), 0))
    return pl.pallas_call(
        _fox_kernel,
        grid_spec=pltpu.PrefetchScalarGridSpec(
            num_scalar_prefetch=1,
            grid=(nb, ngroup, nq),
            in_specs=[
                pl.BlockSpec((None, nh, PAIR_W, t), lambda b, g, i, f: (b, g, 0, i)),
                pl.BlockSpec((None, nh, rows, PAIR_W), lambda b, g, i, f: (b, g, 0, 0)),
                pl.BlockSpec((None, nh, FOX_VROWS, rows), lambda b, g, i, f: (b, g, 0, 0)),
                pl.BlockSpec((None, nh, N_META, PAIR_W), lambda b, g, i, f: (0, g, 0, 0)),
                pl.BlockSpec((None, nh, FOX_VROWS, META_TM), lambda b, g, i, f: (0, g, 0, 0)),
                w_spec(w_out), w_spec(w_gate_up), w_spec(w_down),
            ],
            out_specs=[pl.BlockSpec((None, nh, FOX_VROWS, t), lambda b, g, i, f: (b, g, 0, i)),
                       w_spec(w_out), w_spec(w_gate_up), w_spec(w_down)],
        ),
        out_shape=[jax.ShapeDtypeStruct((nb, FOX_HEADS, FOX_VROWS, rows), F32)]
        + [jax.ShapeDtypeStruct(w.shape, BF16) for w in (w_out, w_gate_up, w_down)],
        compiler_params=pltpu.CompilerParams(
            dimension_semantics=("arbitrary", "arbitrary", "arbitrary"), vmem_limit_bytes=VMEM_LIMIT),
        name="fox",
    )(flags, qt, ka, vt, ka_m, vt_m, w_out, w_gate_up, w_down)


def _ffn_kernel(oa_ref, fox_ref, x_ref, wo_ref, g1_ref, g2_ref, wg_ref, wu_ref, wd_ref, g3_ref, out_ref):
    tm = x_ref.shape[0]
    parts = [slice(c * tm // FFN_PARTS, (c + 1) * tm // FFN_PARTS) for c in range(FFN_PARTS)]
    def fox_rows(r):
        pairs = []
        for p in range(FOX_HEADS // 2):
            o_t = jnp.concatenate([fox_ref[h, 0:HEAD_DIM, r] / fox_ref[h, HEAD_DIM:HEAD_DIM + 1, r]
                                   for h in (2 * p, 2 * p + 1)], axis=0)
            pairs.append(o_t.T.astype(BF16))
        return jnp.concatenate(pairs, axis=1)

    mix = [jnp.concatenate([oa_ref[r, :], fox_rows(r)], axis=1) for r in parts]
    a = [_dot(m, wo_ref[...]) for m in mix]
    h1 = [x_ref[r, :] + _rms(ai, g1_ref[...]) for r, ai in zip(parts, a)]
    hn = [_rms(h, g2_ref[...]).astype(BF16) for h in h1]
    gate_up = [(_dot(h, wg_ref[...]), _dot(h, wu_ref[...])) for h in hn]
    act = [(g / (1.0 + jnp.exp(-g)) * u).astype(BF16) for g, u in gate_up]
    ff = [_dot(ac, wd_ref[...]) for ac in act]
    for r, h, f in zip(parts, h1, ff):
        out_ref[r, :] = h + _rms(f, g3_ref[...])


def _ffn(o_a, o_b, x3, wo, g1, g2, wgu, wd, g3):
    nb, rows, _ = x3.shape
    tm = FFN_TM
    const = lambda b, t: (0, 0)
    resident = lambda shape: pl.BlockSpec(shape, const, pipeline_mode=pl.Buffered(1))
    row = lambda w: pl.BlockSpec((None, tm, w), lambda b, t: (b, t, 0))
    return pl.pallas_call(
        _ffn_kernel,
        grid=(nb, rows // tm),
        in_specs=[
            row(SWA_Q_W), pl.BlockSpec((None, FOX_HEADS, FOX_VROWS, tm), lambda b, t: (b, 0, 0, t)), row(D_MODEL),
            resident((D_MODEL, D_MODEL)), resident((1, D_MODEL)), resident((1, D_MODEL)),
            pl.BlockSpec((D_MODEL, D_FF), lambda b, t: (0, 0), pipeline_mode=pl.Buffered(1)),
            pl.BlockSpec((D_MODEL, D_FF), lambda b, t: (0, 1), pipeline_mode=pl.Buffered(1)),
            resident((D_FF, D_MODEL)),
            resident((1, D_MODEL)),
        ],
        out_specs=row(D_MODEL),
        out_shape=jax.ShapeDtypeStruct((nb, rows, D_MODEL), F32),
        compiler_params=pltpu.CompilerParams(
            dimension_semantics=("arbitrary", "arbitrary"), vmem_limit_bytes=VMEM_LIMIT),
        name="outproj_ffn",
    )(o_a, o_b, x3, wo, g1, g2, wgu, wgu, wd, g3)


def kernel(x, meta_tokens, rel_bias, ln_pre_mix, ln_post_mix, ln_pre_ffn, ln_post_ffn,
           w_in, b_forget, sinks, w_out, w_gate_up, w_down):
    nb, seq, d = x.shape
    assert IN_TM == FOX_T, "the in-proj tile's running key-norm maximum is the fox query tile's bound"
    assert d == D_MODEL and seq % IN_TM == 0 and seq % FOX_T == 0 and seq % FFN_TM == 0 and seq % SWA_TQ == 0
    assert w_in.shape[0] == 1, "single-layer block"
    assert meta_tokens.shape == (N_META, D_MODEL)
    x = x.astype(F32)

    w_all = w_in[0].astype(BF16)
    pad_gate = GATE_ROWS - FOX_HEADS
    w_ft = jnp.pad(w_in[0, :, OFF_VB:], ((0, 0), (0, pad_gate))).astype(BF16)
    lane_rep = lambda v: jnp.broadcast_to(jnp.pad(v.astype(F32), (0, pad_gate))[:, None], (GATE_ROWS, LANES))
    b_ft = lane_rep(b_forget[0])
    g_pre = ln_pre_mix[0].astype(F32).reshape(1, D_MODEL)

    swa_bias = _bias_tables(rel_bias)

    x_m = jnp.pad(meta_tokens.astype(F32), ((0, META_TM - N_META), (0, 0)))[None]
    pa_m, _, ka_m, vt_m, kpre_m, _, c_meta = _inproj(x_m, g_pre, w_all, w_ft, b_ft, jnp.zeros((GATE_ROWS, LANES), F32),
                                                     jnp.zeros((FOX_HEADS, LANES), F32), META_TM)
    pa, qt, ka, vt, _, rmax, _ = _inproj(x, g_pre, w_all, w_ft, b_ft, c_meta[0, 0], kpre_m[0, 0], IN_TM)
    shift = rmax[..., 0].reshape(nb, seq // IN_TM, FOX_HEADS // FOX_GROUP, FOX_GROUP)
    flags = (jnp.max(shift, axis=-1) <= FOX_SAFE_LOG2).astype(jnp.int32).reshape(-1)

    o_a = _swa(pa, pa_m, swa_bias, sinks[0].astype(F32))
    o_b, w_out_bf, w_gu_bf, w_down_bf = _fox(flags, qt, ka, vt, ka_m, vt_m,
                                             w_out[0].astype(F32), w_gate_up[0].astype(F32),
                                             w_down[0].astype(F32))

    return _ffn(o_a, o_b, x,
                w_out_bf,
                ln_post_mix[0].astype(F32).reshape(1, D_MODEL),
                ln_pre_ffn[0].astype(F32).reshape(1, D_MODEL),
                w_gu_bf,
                w_down_bf,
                ln_post_ffn[0].astype(F32).reshape(1, D_MODEL))
```

```python
import math

import numpy as np
import jax
import jax.numpy as jnp
from jax import lax
from jax.experimental import pallas as pl
from jax.experimental.pallas import tpu as pltpu

D_MODEL = 1024
N_META = 16
HEAD_DIM = 64
SWA_Q_HEADS = 8
SWA_KV_HEADS = 2
SWA_GROUP = SWA_Q_HEADS // SWA_KV_HEADS
FOX_HEADS = 8
SWA_Q_W = SWA_Q_HEADS * HEAD_DIM
SWA_KV_W = SWA_KV_HEADS * HEAD_DIM
FOX_W = FOX_HEADS * HEAD_DIM
OFF_QA = SWA_Q_W
OFF_KA = OFF_QA + SWA_KV_W
OFF_VA = OFF_KA + SWA_KV_W
OFF_QB = OFF_VA + FOX_W
OFF_KB = OFF_QB + FOX_W
OFF_VB = OFF_KB + FOX_W
WINDOW = 128
BLOCK = 128
N_BUCKETS = 32
MAX_DISTANCE = 128
D_FF = 2816
EPS = 1e-6
NEG_INF = -1e30
SCALE = HEAD_DIM ** -0.5
LOG2E = math.log2(math.e)

LANES = 128
SUBLANES = 8
PAIR_W = 2 * HEAD_DIM
VMEM_LIMIT = 56 * 1024 * 1024

SWA_KEYS = 2 * BLOCK + N_META
SWA_TQ = 2048
SWA_ONES = 16
IN_TM = 512
META_TM = 128
GATE_ROWS = 16
FOX_VROWS = HEAD_DIM + 16
FOX_T = 512
FOX_GROUP = 8
FOX_SAFE_LOG2 = 40.0
FOX_BOUND_MARGIN = 1.001
FFN_TM = 512
FFN_PARTS = 2
SLAB_ALIGN = 16
N_PIECES = 3
AUG_ROWS = 16

F32 = jnp.float32
BF16 = jnp.bfloat16


def _dot(a, b):
    return jnp.dot(a, b, preferred_element_type=F32)


def _rms(t, g):
    return t * lax.rsqrt(jnp.mean(t * t, axis=-1, keepdims=True) + EPS) * g


def _bf16_pieces(v):
    pieces = []
    for _ in range(N_PIECES - 1):
        p = v.astype(BF16)
        pieces.append(p)
        v = v - p.astype(F32)
    pieces.append(v.astype(BF16))
    return pieces


def _t5_bucket_np(dist):
    n = np.maximum(dist, 0).astype(np.int32)
    max_exact = N_BUCKETS // 2
    nf = np.maximum(n, 1).astype(np.float32)
    large = max_exact + (np.log(nf / np.float32(max_exact)) / np.float32(math.log(MAX_DISTANCE / max_exact))
                         * np.float32(N_BUCKETS - max_exact)).astype(np.int32)
    large = np.minimum(large, N_BUCKETS - 1)
    return np.where(n < max_exact, n, large).astype(np.int32)


def _bias_kernel(tab_ref, bkt_ref, valid_ref, out_ref):
    bkt = bkt_ref[...]
    valid = valid_ref[...] > 0
    for h in range(SWA_Q_HEADS):
        acc = jnp.zeros(bkt.shape, F32)
        for b in range(N_BUCKETS):
            acc = jnp.where(bkt == b, tab_ref[b, h], acc)
        out_ref[:, h * BLOCK:(h + 1) * BLOCK] = jnp.where(valid, acc * LOG2E, NEG_INF)


def _bias_tables(rel_bias):
    ki = np.arange(2 * BLOCK)[:, None]
    qi = np.arange(BLOCK)[None, :]
    d_w = qi + BLOCK - ki
    bkt_w = _t5_bucket_np(d_w)
    in_window = (d_w >= 0) & (d_w < WINDOW)
    mi = np.arange(N_META)[:, None]
    bkt_m0 = _t5_bucket_np(N_META + qi - mi)
    far = _t5_bucket_np(np.asarray([[N_META + BLOCK - (N_META - 1)]]))
    assert far[0, 0] == N_BUCKETS - 1, "meta keys of later blocks must share the last bucket"
    bkt = np.stack([np.concatenate([bkt_w, bkt_m0]), np.concatenate([bkt_w, np.full_like(bkt_m0, far[0, 0])])])
    meta_ok = np.ones((N_META, BLOCK), bool)
    valid = np.stack([np.concatenate([in_window & (ki >= BLOCK), meta_ok]), np.concatenate([in_window, meta_ok])])
    spec = pl.BlockSpec((None, SWA_KEYS, BLOCK), lambda a: (a, 0, 0))
    return pl.pallas_call(
        _bias_kernel,
        grid=(2,),
        in_specs=[pl.BlockSpec(memory_space=pltpu.SMEM), spec, spec],
        out_specs=pl.BlockSpec((None, SWA_KEYS, SWA_Q_HEADS * BLOCK), lambda a: (a, 0, 0)),
        out_shape=jax.ShapeDtypeStruct((2, SWA_KEYS, SWA_Q_HEADS * BLOCK), F32),
        name="bias_tables",
    )(rel_bias.astype(F32), jnp.asarray(bkt.astype(np.int32)), jnp.asarray(valid.astype(np.int32)))


def _fox_aug_rows(val_rows, key_side):
    n = val_rows[0].shape[1]
    row = lax.broadcasted_iota(jnp.int32, (AUG_ROWS, n), 0)
    ones = (row >= N_PIECES) & (row < 3 * N_PIECES) if key_side else (row < N_PIECES)
    aug = jnp.where(ones, 1.0, 0.0)
    first = 0 if key_side else N_PIECES
    for g, val in enumerate(val_rows):
        for i, piece in enumerate(_bf16_pieces(val)):
            aug = jnp.where(row == first + g * N_PIECES + i, piece.astype(F32), aug)
    return aug


def _fox_head_t(own_t, aug, e):
    pad = jnp.zeros((HEAD_DIM - AUG_ROWS, own_t.shape[1]), F32)
    return jnp.concatenate([own_t, aug, pad] if e == 0 else [aug, pad, own_t], axis=0)


def _inproj_kernel(x_ref, g_ref, w_ref, wf_ref, bft_ref, tri_ref, c0_ref, k0_ref,
                   pa_ref, qt_ref, ka_ref, vt_ref, kpre_ref, rmax_ref, cmeta_ref, carry_ref, kpre_scr):
    @pl.when(pl.program_id(1) == 0)
    def _():
        carry_ref[...] = c0_ref[...]
        kpre_scr[...] = k0_ref[...]

    tm = x_ref.shape[0]
    half = tm // 2
    y_top = _rms(x_ref[0:half, :], g_ref[...]).astype(BF16)
    fox_top = _dot(y_top, w_ref[:, OFF_VA:OFF_VB])
    y = jnp.concatenate([y_top, _rms(x_ref[half:tm, :], g_ref[...]).astype(BF16)], axis=0)
    tile_lanes = lambda a: jnp.concatenate([a] * (tm // LANES), axis=1)
    f_t = lax.dot_general(wf_ref[...], y, (((0,), (1,)), ((), ())), preferred_element_type=F32) + tile_lanes(bft_ref[...])
    acc_fox = jnp.concatenate([fox_top, _dot(y[half:tm], w_ref[:, OFF_VA:OFF_VB])], axis=0)
    ls_t = jnp.minimum(f_t, 0.0) - jnp.log1p(jnp.exp(-jnp.abs(f_t)))
    cum_t = tile_lanes(carry_ref[...])
    for piece in _bf16_pieces(ls_t):
        cum_t = cum_t + _dot(piece, tri_ref[...])
    carry_ref[...] = jnp.broadcast_to(cum_t[:, tm - 1:tm], carry_ref.shape)
    cmeta_ref[...] = jnp.broadcast_to(cum_t[:, N_META - 1:N_META], cmeta_ref.shape)
    cb_t = cum_t * (-LOG2E)

    acc_swa = _dot(y, w_ref[:, 0:OFF_VA])
    pa_ref[:, 0:OFF_QA] = (acc_swa[:, 0:OFF_QA] * (SCALE * LOG2E)).astype(BF16)
    pa_ref[:, OFF_QA:OFF_VA] = acc_swa[:, OFF_QA:OFF_VA].astype(BF16)

    ones = jnp.ones((FOX_VROWS - HEAD_DIM, tm), F32)
    for p in range(FOX_HEADS // 2):
        cols = lambda off: slice(off - OFF_VA + p * PAIR_W, off - OFF_VA + (p + 1) * PAIR_W)
        q_t = (acc_fox[:, cols(OFF_VA)] * (SCALE * LOG2E)).astype(BF16).astype(F32).T
        k_t = acc_fox[:, cols(OFF_QB)].astype(BF16).astype(F32).T
        v_t = acc_fox[:, cols(OFF_KB)].astype(BF16).astype(F32).T
        for e in range(2):
            h = 2 * p + e
            own = slice(e * HEAD_DIM, (e + 1) * HEAD_DIM)
            qn2 = jnp.sum(q_t[own] * q_t[own], axis=0, keepdims=True)
            kn2 = jnp.max(jnp.sum(k_t[own] * k_t[own], axis=0, keepdims=True), axis=1, keepdims=True)
            kpre = jnp.maximum(kpre_scr[h:h + 1, :], kn2)
            kpre_scr[h:h + 1, :] = kpre
            kpre_ref[h:h + 1, :] = kpre
            r = jnp.sqrt(qn2 * kpre[:, 0:1]) * FOX_BOUND_MARGIN
            rmax_ref[h:h + 1, :] = jnp.broadcast_to(jnp.max(r, axis=1, keepdims=True), (1, LANES))
            ka_ref[h] = _fox_head_t(k_t[own], _fox_aug_rows([cb_t[h:h + 1]], True), e).T.astype(BF16)
            qt_ref[h] = _fox_head_t(q_t[own], _fox_aug_rows([-cb_t[h:h + 1], -r], False), e).astype(BF16)
            vt_ref[h] = jnp.concatenate([v_t[own], ones], axis=0).astype(BF16)


def _inproj(x3, g, w, wft, bft, c0, k0, tm):
    nb, rows, _ = x3.shape
    nt = rows // tm
    tri = jnp.asarray(np.triu(np.ones((tm, tm), np.float32)), BF16)
    const = lambda b, t: (0, 0)
    return pl.pallas_call(
        _inproj_kernel,
        grid=(nb, nt),
        in_specs=[
            pl.BlockSpec((None, tm, D_MODEL), lambda b, t: (b, t, 0)),
            pl.BlockSpec((1, D_MODEL), const),
            pl.BlockSpec(w.shape, const),
            pl.BlockSpec((D_MODEL, GATE_ROWS), const),
            pl.BlockSpec((GATE_ROWS, LANES), const),
            pl.BlockSpec((tm, tm), const),
            pl.BlockSpec((GATE_ROWS, LANES), const),
            pl.BlockSpec((FOX_HEADS, LANES), const),
        ],
        out_specs=[
            pl.BlockSpec((None, tm, OFF_VA), lambda b, t: (b, t, 0)),
            pl.BlockSpec((None, FOX_HEADS, PAIR_W, tm), lambda b, t: (b, 0, 0, t)),
            pl.BlockSpec((None, FOX_HEADS, tm, PAIR_W), lambda b, t: (b, 0, t, 0)),
            pl.BlockSpec((None, FOX_HEADS, FOX_VROWS, tm), lambda b, t: (b, 0, 0, t)),
            pl.BlockSpec((None, None, FOX_HEADS, LANES), lambda b, t: (b, t, 0, 0)),
            pl.BlockSpec((None, None, FOX_HEADS, LANES), lambda b, t: (b, t, 0, 0)),
            pl.BlockSpec((None, None, GATE_ROWS, LANES), lambda b, t: (b, t, 0, 0)),
        ],
        out_shape=[
            jax.ShapeDtypeStruct((nb, rows, OFF_VA), BF16),
            jax.ShapeDtypeStruct((nb, FOX_HEADS, PAIR_W, rows), BF16),
            jax.ShapeDtypeStruct((nb, FOX_HEADS, rows, PAIR_W), BF16),
            jax.ShapeDtypeStruct((nb, FOX_HEADS, FOX_VROWS, rows), BF16),
            jax.ShapeDtypeStruct((nb, nt, FOX_HEADS, LANES), F32),
            jax.ShapeDtypeStruct((nb, nt, FOX_HEADS, LANES), F32),
            jax.ShapeDtypeStruct((nb, nt, GATE_ROWS, LANES), F32),
        ],
        scratch_shapes=[pltpu.VMEM((GATE_ROWS, LANES), F32), pltpu.VMEM((FOX_HEADS, LANES), F32)],
        compiler_params=pltpu.CompilerParams(
            dimension_semantics=("arbitrary", "arbitrary"), vmem_limit_bytes=VMEM_LIMIT),
        name="inproj",
    )(x3, g, w, wft, bft, tri, c0, k0)


def _swa_scores(q, k_all, bias):
    nq = q.shape[0]
    zeros = jnp.zeros((HEAD_DIM, nq), F32)
    blocks = []
    for pair in range(SWA_Q_HEADS // 2):
        q_t = q[:, pair * PAIR_W:(pair + 1) * PAIR_W].astype(F32).T
        for hp in range(2):
            qh = q_t[hp * HEAD_DIM:(hp + 1) * HEAD_DIM]
            g = (2 * pair + hp) // SWA_GROUP
            blocks.append(jnp.concatenate([qh, zeros] if g == 0 else [zeros, qh], axis=0))
    q_bd = jnp.concatenate(blocks, axis=1).astype(BF16)
    return _dot(k_all, q_bd) + bias


def _swa_outputs(s, v_all, sink_ref):
    nq = s.shape[1] // SWA_Q_HEADS
    ps, sink_terms = [], []
    for h in range(SWA_Q_HEADS):
        sh = s[:, h * nq:(h + 1) * nq]
        sink = sink_ref[h] * LOG2E
        m = jnp.maximum(jnp.max(sh, axis=0, keepdims=True), sink)
        ps.append(jnp.exp2(sh - m).astype(BF16))
        sink_terms.append(jnp.exp2(sink - m))
    n = BLOCK
    v_t = jnp.concatenate([v_all[0:n].astype(F32).T, v_all[n:2 * n].astype(F32).T,
                           v_all[2 * n:].astype(F32).T], axis=1)
    ones = jnp.ones((SWA_ONES, v_t.shape[1]), F32)
    outs = []
    for g in range(SWA_KV_HEADS):
        v_aug = jnp.concatenate([v_t[g * HEAD_DIM:(g + 1) * HEAD_DIM], ones], axis=0).astype(BF16)
        p_g = jnp.concatenate(ps[g * SWA_GROUP:(g + 1) * SWA_GROUP], axis=1)
        o_g = _dot(v_aug, p_g)
        for j in range(SWA_GROUP):
            cols = slice(j * nq, (j + 1) * nq)
            l = o_g[HEAD_DIM:HEAD_DIM + 1, cols] + sink_terms[g * SWA_GROUP + j]
            outs.append(o_g[0:HEAD_DIM, cols] / l)
    pairs = [jnp.concatenate(outs[2 * p:2 * p + 2], axis=0).T for p in range(SWA_Q_HEADS // 2)]
    return jnp.concatenate(pairs, axis=1).astype(BF16)


def _swa_kernel(sink_ref, q_ref, kc_ref, kp_ref, vc_ref, vp_ref, km_ref, vm_ref, bias0_ref, bias1_ref, o_ref):
    n = BLOCK
    nsub = q_ref.shape[0] // n
    k_rows = [kp_ref[...]] + [kc_ref[c * n:(c + 1) * n] for c in range(nsub)]
    v_rows = [vp_ref[...]] + [vc_ref[c * n:(c + 1) * n] for c in range(nsub)]
    def block_scores(c):
        k_all = jnp.concatenate([k_rows[c], k_rows[c + 1], km_ref[...]], axis=0)
        bias = bias0_ref[...] if c == 0 else bias1_ref[...]
        return _swa_scores(q_ref[c * n:(c + 1) * n], k_all, bias)

    s_next = block_scores(0)
    for c in range(nsub):
        s_cur = s_next
        if c + 1 < nsub:
            s_next = block_scores(c + 1)
        v_all = jnp.concatenate([v_rows[c], v_rows[c + 1], vm_ref[...]], axis=0)
        o_ref[c * n:(c + 1) * n] = _swa_outputs(s_cur, v_all, sink_ref)


def _swa(proj, projm, bias, sinks):
    nb, rows, _ = proj.shape
    tq = SWA_TQ
    ratio = tq // BLOCK
    kcol, vcol = OFF_QA // SWA_KV_W, OFF_KA // SWA_KV_W
    cur = lambda c: (lambda b, n: (b, n, c))
    prev = lambda c: (lambda b, n: (b, jnp.maximum(n * ratio - 1, 0), c))
    bias_spec = lambda idx: pl.BlockSpec((None, SWA_KEYS, SWA_Q_HEADS * BLOCK), idx)
    return pl.pallas_call(
        _swa_kernel,
        grid=(nb, rows // tq),
        in_specs=[
            pl.BlockSpec(memory_space=pltpu.SMEM),
            pl.BlockSpec((None, tq, SWA_Q_W), lambda b, n: (b, n, 0)),
            pl.BlockSpec((None, tq, SWA_KV_W), cur(kcol)),
            pl.BlockSpec((None, BLOCK, SWA_KV_W), prev(kcol)),
            pl.BlockSpec((None, tq, SWA_KV_W), cur(vcol)),
            pl.BlockSpec((None, BLOCK, SWA_KV_W), prev(vcol)),
            pl.BlockSpec((None, N_META, SWA_KV_W), lambda b, n: (0, 0, kcol)),
            pl.BlockSpec((None, N_META, SWA_KV_W), lambda b, n: (0, 0, vcol)),
            bias_spec(lambda b, n: (jnp.minimum(n, 1), 0, 0)),
            bias_spec(lambda b, n: (1, 0, 0)),
        ],
        out_specs=pl.BlockSpec((None, tq, SWA_Q_W), lambda b, n: (b, n, 0)),
        out_shape=jax.ShapeDtypeStruct((nb, rows, SWA_Q_W), BF16),
        compiler_params=pltpu.CompilerParams(
            dimension_semantics=("arbitrary", "arbitrary"), vmem_limit_bytes=VMEM_LIMIT),
        name="swa",
    )(sinks, proj, proj, proj, proj, proj, projm, projm, bias, bias)


def _fox_kernel(flags_ref, qt_ref, ka_ref, vt_ref, kam_ref, vtm_ref,
                wo_ref, wgu_ref, wd_ref, acc_ref, wo_bf_ref, wgu_bf_ref, wd_bf_ref):
    wo_bf_ref[...] = wo_ref[...].astype(BF16)
    wgu_bf_ref[...] = wgu_ref[...].astype(BF16)
    wd_bf_ref[...] = wd_ref[...].astype(BF16)

    b, g, i = pl.program_id(0), pl.program_id(1), pl.program_id(2)
    t = acc_ref.shape[2]
    nh = acc_ref.shape[0]
    colmax = lambda x: jnp.max(x, axis=0, keepdims=True)
    tile_rows = lambda j: pl.ds(pl.multiple_of(j * t, t), t)

    safe = flags_ref[(b * pl.num_programs(2) + i) * pl.num_programs(1) + g] > 0
    qt = [qt_ref[h] for h in range(nh)]

    def scores(h, j, diagonal):
        s = _dot(ka_ref[h, tile_rows(j), :], qt[h])
        if diagonal:
            key = lax.broadcasted_iota(jnp.int32, (t, t), 0)
            qry = lax.broadcasted_iota(jnp.int32, (t, t), 1)
            s = jnp.where(key <= qry, s, NEG_INF)
        return s

    @pl.when(safe)
    def _():
        def run(work):
            prev = None
            for score_fn, pv_fn in work:
                s = score_fn()
                if prev is not None:
                    prev[0](prev[1])
                prev = (pv_fn, s)
            prev[0](prev[1])

        def tile_work(j):
            def pv(h):
                def apply(s):
                    acc_ref[h] += _dot(vt_ref[h, :, tile_rows(j)], jnp.exp2(s).astype(BF16))
                return apply
            return [(lambda h=h: scores(h, j, False), pv(h)) for h in range(nh)]

        def first_work():
            def sc(h):
                return jnp.concatenate([scores(h, i, True), _dot(kam_ref[h], qt[h])], axis=0)
            def pv(h):
                def apply(s):
                    v_t = jnp.concatenate([vt_ref[h, :, tile_rows(i)], vtm_ref[h, :, 0:N_META]], axis=1)
                    acc_ref[h] = _dot(v_t, jnp.exp2(s).astype(BF16))
                return apply
            return [(lambda h=h: sc(h), pv(h)) for h in range(nh)]

        run(first_work())

        def tile_pair(p, carry):
            run(tile_work(2 * p) + tile_work(2 * p + 1))
            return carry

        lax.fori_loop(0, i // 2, tile_pair, 0)

        @pl.when(i % 2 == 1)
        def _():
            run(tile_work(i - 1))

    @pl.when(jnp.logical_not(safe))
    def _():
        m0 = []
        for h in range(nh):
            s = _dot(kam_ref[h], qt[h])
            m = colmax(s)
            acc_ref[h] = _dot(vtm_ref[h, :, 0:N_META], jnp.exp2(s - m).astype(BF16))
            m0.append(m)

        def tile(j, ms, diagonal):
            out = []
            ss = [scores(h, j, diagonal) for h in range(nh)]
            for h in range(nh):
                s = ss[h]
                m_next = jnp.maximum(ms[h], colmax(s))
                p = jnp.exp2(s - m_next).astype(BF16)
                acc_ref[h] = jnp.exp2(ms[h] - m_next) * acc_ref[h] + _dot(vt_ref[h, :, tile_rows(j)], p)
                out.append(m_next)
            return tuple(out)

        ms = lax.fori_loop(0, i, lambda j, ms: tile(j, ms, False), tuple(m0))
        tile(i, ms, True)


def _fox(flags, qt, ka, vt, ka_m, vt_m, w_out, w_gate_up, w_down):
    nb, _, rows, _ = ka.shape
    t = FOX_T
    nh = FOX_GROUP
    ngroup = FOX_HEADS // nh
    nq = rows // t
    nsteps = nb * ngroup * nq
    slab = lambda w: min(r for r in range(SLAB_ALIGN, w.shape[0] + 1, SLAB_ALIGN)
                         if w.shape[0] % r == 0 and r * nsteps >= w.shape[0])
    step = lambda b, g, i: (b * ngroup + g) * nq + i
    w_spec = lambda w: pl.BlockSpec((slab(w), w.shape[1]),
                                    lambda b, g, i, f: (jnp.minimum(step(b, g, i), w.shape[0] // slab(w) - 1), 0))
    return pl.pallas_call(
        _fox_kernel,
        grid_spec=pltpu.PrefetchScalarGridSpec(
            num_scalar_prefetch=1,
            grid=(nb, ngroup, nq),
            in_specs=[
                pl.BlockSpec((None, nh, PAIR_W, t), lambda b, g, i, f: (b, g, 0, i)),
                pl.BlockSpec((None, nh, rows, PAIR_W), lambda b, g, i, f: (b, g, 0, 0)),
                pl.BlockSpec((None, nh, FOX_VROWS, rows), lambda b, g, i, f: (b, g, 0, 0)),
                pl.BlockSpec((None, nh, N_META, PAIR_W), lambda b, g, i, f: (0, g, 0, 0)),
                pl.BlockSpec((None, nh, FOX_VROWS, META_TM), lambda b, g, i, f: (0, g, 0, 0)),
                w_spec(w_out), w_spec(w_gate_up), w_spec(w_down),
            ],
            out_specs=[pl.BlockSpec((None, nh, FOX_VROWS, t), lambda b, g, i, f: (b, g, 0, i)),
                       w_spec(w_out), w_spec(w_gate_up), w_spec(w_down)],
        ),
        out_shape=[jax.ShapeDtypeStruct((nb, FOX_HEADS, FOX_VROWS, rows), F32)]
        + [jax.ShapeDtypeStruct(w.shape, BF16) for w in (w_out, w_gate_up, w_down)],
        compiler_params=pltpu.CompilerParams(
            dimension_semantics=("arbitrary", "arbitrary", "arbitrary"), vmem_limit_bytes=VMEM_LIMIT),
        name="fox",
    )(flags, qt, ka, vt, ka_m, vt_m, w_out, w_gate_up, w_down)


def _ffn_kernel(oa_ref, fox_ref, x_ref, wo_ref, g1_ref, g2_ref, wg_ref, wu_ref, wd_ref, g3_ref, out_ref):
    tm = x_ref.shape[0]
    parts = [slice(c * tm // FFN_PARTS, (c + 1) * tm // FFN_PARTS) for c in range(FFN_PARTS)]
    def fox_rows(r):
        pairs = []
        for p in range(FOX_HEADS // 2):
            o_t = jnp.concatenate([fox_ref[h, 0:HEAD_DIM, r] / fox_ref[h, HEAD_DIM:HEAD_DIM + 1, r]
                                   for h in (2 * p, 2 * p + 1)], axis=0)
            pairs.append(o_t.T.astype(BF16))
        return jnp.concatenate(pairs, axis=1)

    mix = [jnp.concatenate([oa_ref[r, :], fox_rows(r)], axis=1) for r in parts]
    a = [_dot(m, wo_ref[...]) for m in mix]
    h1 = [x_ref[r, :] + _rms(ai, g1_ref[...]) for r, ai in zip(parts, a)]
    hn = [_rms(h, g2_ref[...]).astype(BF16) for h in h1]
    gate_up = [(_dot(h, wg_ref[...]), _dot(h, wu_ref[...])) for h in hn]
    act = [(g / (1.0 + jnp.exp(-g)) * u).astype(BF16) for g, u in gate_up]
    ff = [_dot(ac, wd_ref[...]) for ac in act]
    for r, h, f in zip(parts, h1, ff):
        out_ref[r, :] = h + _rms(f, g3_ref[...])


def _ffn(o_a, o_b, x3, wo, g1, g2, wgu, wd, g3):
    nb, rows, _ = x3.shape
    tm = FFN_TM
    const = lambda b, t: (0, 0)
    resident = lambda shape: pl.BlockSpec(shape, const, pipeline_mode=pl.Buffered(1))
    row = lambda w: pl.BlockSpec((None, tm, w), lambda b, t: (b, t, 0))
    return pl.pallas_call(
        _ffn_kernel,
        grid=(nb, rows // tm),
        in_specs=[
            row(SWA_Q_W), pl.BlockSpec((None, FOX_HEADS, FOX_VROWS, tm), lambda b, t: (b, 0, 0, t)), row(D_MODEL),
            resident((D_MODEL, D_MODEL)), resident((1, D_MODEL)), resident((1, D_MODEL)),
            pl.BlockSpec((D_MODEL, D_FF), lambda b, t: (0, 0), pipeline_mode=pl.Buffered(1)),
            pl.BlockSpec((D_MODEL, D_FF), lambda b, t: (0, 1), pipeline_mode=pl.Buffered(1)),
            resident((D_FF, D_MODEL)),
            resident((1, D_MODEL)),
        ],
        out_specs=row(D_MODEL),
        out_shape=jax.ShapeDtypeStruct((nb, rows, D_MODEL), F32),
        compiler_params=pltpu.CompilerParams(
            dimension_semantics=("arbitrary", "arbitrary"), vmem_limit_bytes=VMEM_LIMIT),
        name="outproj_ffn",
    )(o_a, o_b, x3, wo, g1, g2, wgu, wgu, wd, g3)


def kernel(x, meta_tokens, rel_bias, ln_pre_mix, ln_post_mix, ln_pre_ffn, ln_post_ffn,
           w_in, b_forget, sinks, w_out, w_gate_up, w_down):
    nb, seq, d = x.shape
    assert IN_TM == FOX_T, "the in-proj tile's running key-norm maximum is the fox query tile's bound"
    assert d == D_MODEL and seq % IN_TM == 0 and seq % FOX_T == 0 and seq % FFN_TM == 0 and seq % SWA_TQ == 0
    assert w_in.shape[0] == 1, "single-layer block"
    assert meta_tokens.shape == (N_META, D_MODEL)
    x = x.astype(F32)

    w_all = w_in[0].astype(BF16)
    pad_gate = GATE_ROWS - FOX_HEADS
    w_ft = jnp.pad(w_in[0, :, OFF_VB:], ((0, 0), (0, pad_gate))).astype(BF16)
    lane_rep = lambda v: jnp.broadcast_to(jnp.pad(v.astype(F32), (0, pad_gate))[:, None], (GATE_ROWS, LANES))
    b_ft = lane_rep(b_forget[0])
    g_pre = ln_pre_mix[0].astype(F32).reshape(1, D_MODEL)

    swa_bias = _bias_tables(rel_bias)

    x_m = jnp.pad(meta_tokens.astype(F32), ((0, META_TM - N_META), (0, 0)))[None]
    pa_m, _, ka_m, vt_m, kpre_m, _, c_meta = _inproj(x_m, g_pre, w_all, w_ft, b_ft, jnp.zeros((GATE_ROWS, LANES), F32),
                                                     jnp.zeros((FOX_HEADS, LANES), F32), META_TM)
    pa, qt, ka, vt, _, rmax, _ = _inproj(x, g_pre, w_all, w_ft, b_ft, c_meta[0, 0], kpre_m[0, 0], IN_TM)
    shift = rmax[..., 0].reshape(nb, seq // IN_TM, FOX_HEADS // FOX_GROUP, FOX_GROUP)
    flags = (jnp.max(shift, axis=-1) <= FOX_SAFE_LOG2).astype(jnp.int32).reshape(-1)

    o_a = _swa(pa, pa_m, swa_bias, sinks[0].astype(F32))
    o_b, w_out_bf, w_gu_bf, w_down_bf = _fox(flags, qt, ka, vt, ka_m, vt_m,
                                             w_out[0].astype(F32), w_gate_up[0].astype(F32),
                                             w_down[0].astype(F32))

    return _ffn(o_a, o_b, x,
                w_out_bf,
                ln_post_mix[0].astype(F32).reshape(1, D_MODEL),
                ln_pre_ffn[0].astype(F32).reshape(1, D_MODEL),
                w_gu_bf,
                w_down_bf,
                ln_post_ffn[0].astype(F32).reshape(1, D_MODEL))
```

```python
import math

import numpy as np
import jax
import jax.numpy as jnp
from jax import lax
from jax.experimental import pallas as pl
from jax.experimental.pallas import tpu as pltpu

D_MODEL = 1024
N_META = 16
HEAD_DIM = 64
SWA_Q_HEADS = 8
SWA_KV_HEADS = 2
SWA_GROUP = SWA_Q_HEADS // SWA_KV_HEADS
FOX_HEADS = 8
SWA_Q_W = SWA_Q_HEADS * HEAD_DIM
SWA_KV_W = SWA_KV_HEADS * HEAD_DIM
FOX_W = FOX_HEADS * HEAD_DIM
OFF_QA = SWA_Q_W
OFF_KA = OFF_QA + SWA_KV_W
OFF_VA = OFF_KA + SWA_KV_W
OFF_QB = OFF_VA + FOX_W
OFF_KB = OFF_QB + FOX_W
OFF_VB = OFF_KB + FOX_W
WINDOW = 128
BLOCK = 128
N_BUCKETS = 32
MAX_DISTANCE = 128
D_FF = 2816
EPS = 1e-6
NEG_INF = -1e30
SCALE = HEAD_DIM ** -0.5
LOG2E = math.log2(math.e)

LANES = 128
PAIR_W = 2 * HEAD_DIM
VMEM_LIMIT = 56 * 1024 * 1024

SWA_KEYS = 2 * BLOCK + N_META
SWA_TQ = 4096
SWA_ONES = 16
IN_TM = 512
META_TM = 128
GATE_ROWS = 16
FOX_VROWS = HEAD_DIM + 16
FOX_T = 512
FOX_GROUP = 8
FOX_SAFE_LOG2 = 40.0
FOX_BOUND_MARGIN = 1.001
FFN_TM = 512
FFN_PARTS = 2
SLAB_ALIGN = 16
N_PIECES = 3
AUG_ROWS = 16

F32 = jnp.float32
BF16 = jnp.bfloat16


def _dot(a, b):
    return jnp.dot(a, b, preferred_element_type=F32)


def _rms(t, g):
    return t * lax.rsqrt(jnp.mean(t * t, axis=-1, keepdims=True) + EPS) * g


def _bf16_pieces(v):
    pieces = []
    for _ in range(N_PIECES - 1):
        p = v.astype(BF16)
        pieces.append(p)
        v = v - p.astype(F32)
    pieces.append(v.astype(BF16))
    return pieces


def _t5_bucket_np(dist):
    n = np.maximum(dist, 0).astype(np.int32)
    max_exact = N_BUCKETS // 2
    nf = np.maximum(n, 1).astype(np.float32)
    large = max_exact + (np.log(nf / np.float32(max_exact)) / np.float32(math.log(MAX_DISTANCE / max_exact))
                         * np.float32(N_BUCKETS - max_exact)).astype(np.int32)
    large = np.minimum(large, N_BUCKETS - 1)
    return np.where(n < max_exact, n, large).astype(np.int32)


def _bias_kernel(tab_ref, bkt_ref, valid_ref, out_ref):
    bkt = bkt_ref[...]
    valid = valid_ref[...] > 0
    for h in range(SWA_Q_HEADS):
        acc = jnp.zeros(bkt.shape, F32)
        for b in range(N_BUCKETS):
            acc = jnp.where(bkt == b, tab_ref[b, h], acc)
        out_ref[:, h * BLOCK:(h + 1) * BLOCK] = jnp.where(valid, acc * LOG2E, NEG_INF)


def _bias_tables(rel_bias):
    ki = np.arange(2 * BLOCK)[:, None]
    qi = np.arange(BLOCK)[None, :]
    d_w = qi + BLOCK - ki
    bkt_w = _t5_bucket_np(d_w)
    in_window = (d_w >= 0) & (d_w < WINDOW)
    mi = np.arange(N_META)[:, None]
    bkt_m0 = _t5_bucket_np(N_META + qi - mi)
    far = _t5_bucket_np(np.asarray([[N_META + BLOCK - (N_META - 1)]]))
    assert far[0, 0] == N_BUCKETS - 1, "meta keys of later blocks must share the last bucket"
    bkt = np.stack([np.concatenate([bkt_w, bkt_m0]), np.concatenate([bkt_w, np.full_like(bkt_m0, far[0, 0])])])
    meta_ok = np.ones((N_META, BLOCK), bool)
    valid = np.stack([np.concatenate([in_window & (ki >= BLOCK), meta_ok]), np.concatenate([in_window, meta_ok])])
    spec = pl.BlockSpec((None, SWA_KEYS, BLOCK), lambda a: (a, 0, 0))
    return pl.pallas_call(
        _bias_kernel,
        grid=(2,),
        in_specs=[pl.BlockSpec(memory_space=pltpu.SMEM), spec, spec],
        out_specs=pl.BlockSpec((None, SWA_KEYS, SWA_Q_HEADS * BLOCK), lambda a: (a, 0, 0)),
        out_shape=jax.ShapeDtypeStruct((2, SWA_KEYS, SWA_Q_HEADS * BLOCK), F32),
        name="bias_tables",
    )(rel_bias.astype(F32), jnp.asarray(bkt.astype(np.int32)), jnp.asarray(valid.astype(np.int32)))


def _fox_aug_rows(val_rows, key_side):
    n = val_rows[0].shape[1]
    row = lax.broadcasted_iota(jnp.int32, (AUG_ROWS, n), 0)
    ones = (row >= N_PIECES) & (row < 3 * N_PIECES) if key_side else (row < N_PIECES)
    aug = jnp.where(ones, 1.0, 0.0)
    first = 0 if key_side else N_PIECES
    for g, val in enumerate(val_rows):
        for i, piece in enumerate(_bf16_pieces(val)):
            aug = jnp.where(row == first + g * N_PIECES + i, piece.astype(F32), aug)
    return aug


def _fox_head_t(own_t, aug, e):
    pad = jnp.zeros((HEAD_DIM - AUG_ROWS, own_t.shape[1]), F32)
    return jnp.concatenate([own_t, aug, pad] if e == 0 else [aug, pad, own_t], axis=0)


def _inproj_kernel(x_ref, g_ref, w_ref, wf_ref, bft_ref, tri_ref, c0_ref, k0_ref,
                   pa_ref, qt_ref, ka_ref, vt_ref, kpre_ref, rmax_ref, cmeta_ref, carry_ref, kpre_scr):
    @pl.when(pl.program_id(1) == 0)
    def _():
        carry_ref[...] = c0_ref[...]
        kpre_scr[...] = k0_ref[...]

    tm = x_ref.shape[0]
    half = tm // 2
    y_top = _rms(x_ref[0:half, :], g_ref[...]).astype(BF16)
    fox_top = _dot(y_top, w_ref[:, OFF_VA:OFF_VB])
    y = jnp.concatenate([y_top, _rms(x_ref[half:tm, :], g_ref[...]).astype(BF16)], axis=0)
    tile_lanes = lambda a: jnp.concatenate([a] * (tm // LANES), axis=1)
    f_t = lax.dot_general(wf_ref[...], y, (((0,), (1,)), ((), ())), preferred_element_type=F32) + tile_lanes(bft_ref[...])
    acc_fox = jnp.concatenate([fox_top, _dot(y[half:tm], w_ref[:, OFF_VA:OFF_VB])], axis=0)
    ls_t = jnp.minimum(f_t, 0.0) - jnp.log1p(jnp.exp(-jnp.abs(f_t)))
    cum_t = tile_lanes(carry_ref[...])
    for piece in _bf16_pieces(ls_t):
        cum_t = cum_t + _dot(piece, tri_ref[...])
    carry_ref[...] = jnp.broadcast_to(cum_t[:, tm - 1:tm], carry_ref.shape)
    cmeta_ref[...] = jnp.broadcast_to(cum_t[:, N_META - 1:N_META], cmeta_ref.shape)
    cb_t = cum_t * (-LOG2E)

    acc_swa = _dot(y, w_ref[:, 0:OFF_VA])
    pa_ref[:, 0:OFF_QA] = (acc_swa[:, 0:OFF_QA] * (SCALE * LOG2E)).astype(BF16)
    pa_ref[:, OFF_QA:OFF_VA] = acc_swa[:, OFF_QA:OFF_VA].astype(BF16)

    ones = jnp.ones((FOX_VROWS - HEAD_DIM, tm), F32)
    for p in range(FOX_HEADS // 2):
        cols = lambda off: slice(off - OFF_VA + p * PAIR_W, off - OFF_VA + (p + 1) * PAIR_W)
        q_t = (acc_fox[:, cols(OFF_VA)] * (SCALE * LOG2E)).astype(BF16).astype(F32).T
        k_t = acc_fox[:, cols(OFF_QB)].astype(BF16).astype(F32).T
        v_t = acc_fox[:, cols(OFF_KB)].astype(BF16).astype(F32).T
        for e in range(2):
            h = 2 * p + e
            own = slice(e * HEAD_DIM, (e + 1) * HEAD_DIM)
            qn2 = jnp.sum(q_t[own] * q_t[own], axis=0, keepdims=True)
            kn2 = jnp.max(jnp.sum(k_t[own] * k_t[own], axis=0, keepdims=True), axis=1, keepdims=True)
            kpre = jnp.maximum(kpre_scr[h:h + 1, :], kn2)
            kpre_scr[h:h + 1, :] = kpre
            kpre_ref[h:h + 1, :] = kpre
            r = jnp.sqrt(qn2 * kpre[:, 0:1]) * FOX_BOUND_MARGIN
            rmax_ref[h:h + 1, :] = jnp.broadcast_to(jnp.max(r, axis=1, keepdims=True), (1, LANES))
            ka_ref[h] = _fox_head_t(k_t[own], _fox_aug_rows([cb_t[h:h + 1]], True), e).T.astype(BF16)
            qt_ref[h] = _fox_head_t(q_t[own], _fox_aug_rows([-cb_t[h:h + 1], -r], False), e).astype(BF16)
            vt_ref[h] = jnp.concatenate([v_t[own], ones], axis=0).astype(BF16)


def _inproj(x3, g, w, wft, bft, c0, k0, tm):
    nb, rows, _ = x3.shape
    nt = rows // tm
    tri = jnp.asarray(np.triu(np.ones((tm, tm), np.float32)), BF16)
    const = lambda b, t: (0, 0)
    return pl.pallas_call(
        _inproj_kernel,
        grid=(nb, nt),
        in_specs=[
            pl.BlockSpec((None, tm, D_MODEL), lambda b, t: (b, t, 0)),
            pl.BlockSpec((1, D_MODEL), const),
            pl.BlockSpec(w.shape, const),
            pl.BlockSpec((D_MODEL, GATE_ROWS), const),
            pl.BlockSpec((GATE_ROWS, LANES), const),
            pl.BlockSpec((tm, tm), const),
            pl.BlockSpec((GATE_ROWS, LANES), const),
            pl.BlockSpec((FOX_HEADS, LANES), const),
        ],
        out_specs=[
            pl.BlockSpec((None, tm, OFF_VA), lambda b, t: (b, t, 0)),
            pl.BlockSpec((None, FOX_HEADS, PAIR_W, tm), lambda b, t: (b, 0, 0, t)),
            pl.BlockSpec((None, FOX_HEADS, tm, PAIR_W), lambda b, t: (b, 0, t, 0)),
            pl.BlockSpec((None, FOX_HEADS, FOX_VROWS, tm), lambda b, t: (b, 0, 0, t)),
            pl.BlockSpec((None, None, FOX_HEADS, LANES), lambda b, t: (b, t, 0, 0)),
            pl.BlockSpec((None, None, FOX_HEADS, LANES), lambda b, t: (b, t, 0, 0)),
            pl.BlockSpec((None, None, GATE_ROWS, LANES), lambda b, t: (b, t, 0, 0)),
        ],
        out_shape=[
            jax.ShapeDtypeStruct((nb, rows, OFF_VA), BF16),
            jax.ShapeDtypeStruct((nb, FOX_HEADS, PAIR_W, rows), BF16),
            jax.ShapeDtypeStruct((nb, FOX_HEADS, rows, PAIR_W), BF16),
            jax.ShapeDtypeStruct((nb, FOX_HEADS, FOX_VROWS, rows), BF16),
            jax.ShapeDtypeStruct((nb, nt, FOX_HEADS, LANES), F32),
            jax.ShapeDtypeStruct((nb, nt, FOX_HEADS, LANES), F32),
            jax.ShapeDtypeStruct((nb, nt, GATE_ROWS, LANES), F32),
        ],
        scratch_shapes=[pltpu.VMEM((GATE_ROWS, LANES), F32), pltpu.VMEM((FOX_HEADS, LANES), F32)],
        compiler_params=pltpu.CompilerParams(
            dimension_semantics=("arbitrary", "arbitrary"), vmem_limit_bytes=VMEM_LIMIT),
        name="inproj",
    )(x3, g, w, wft, bft, tri, c0, k0)


def _swa_scores(q, k_all, bias):
    nq = q.shape[0]
    zeros = jnp.zeros((HEAD_DIM, nq), F32)
    blocks = []
    for pair in range(SWA_Q_HEADS // 2):
        q_t = q[:, pair * PAIR_W:(pair + 1) * PAIR_W].astype(F32).T
        for hp in range(2):
            qh = q_t[hp * HEAD_DIM:(hp + 1) * HEAD_DIM]
            g = (2 * pair + hp) // SWA_GROUP
            blocks.append(jnp.concatenate([qh, zeros] if g == 0 else [zeros, qh], axis=0))
    q_bd = jnp.concatenate(blocks, axis=1).astype(BF16)
    return _dot(k_all, q_bd) + bias


def _swa_outputs(s, v_all, sink_ref):
    nq = s.shape[1] // SWA_Q_HEADS
    ps, sink_terms = [], []
    for h in range(SWA_Q_HEADS):
        sh = s[:, h * nq:(h + 1) * nq]
        sink = sink_ref[h] * LOG2E
        m = jnp.maximum(jnp.max(sh, axis=0, keepdims=True), sink)
        ps.append(jnp.exp2(sh - m).astype(BF16))
        sink_terms.append(jnp.exp2(sink - m))
    n = BLOCK
    v_t = jnp.concatenate([v_all[0:n].astype(F32).T, v_all[n:2 * n].astype(F32).T,
                           v_all[2 * n:].astype(F32).T], axis=1)
    ones = jnp.ones((SWA_ONES, v_t.shape[1]), F32)
    outs = []
    for g in range(SWA_KV_HEADS):
        v_aug = jnp.concatenate([v_t[g * HEAD_DIM:(g + 1) * HEAD_DIM], ones], axis=0).astype(BF16)
        p_g = jnp.concatenate(ps[g * SWA_GROUP:(g + 1) * SWA_GROUP], axis=1)
        o_g = _dot(v_aug, p_g)
        for j in range(SWA_GROUP):
            cols = slice(j * nq, (j + 1) * nq)
            l = o_g[HEAD_DIM:HEAD_DIM + 1, cols] + sink_terms[g * SWA_GROUP + j]
            outs.append(o_g[0:HEAD_DIM, cols] / l)
    pairs = [jnp.concatenate(outs[2 * p:2 * p + 2], axis=0).T for p in range(SWA_Q_HEADS // 2)]
    return jnp.concatenate(pairs, axis=1).astype(BF16)


def _swa_kernel(sink_ref, q_ref, kc_ref, kp_ref, vc_ref, vp_ref, km_ref, vm_ref, bias0_ref, bias1_ref, o_ref):
    n = BLOCK
    nsub = q_ref.shape[0] // n
    k_rows = [kp_ref[...]] + [kc_ref[c * n:(c + 1) * n] for c in range(nsub)]
    v_rows = [vp_ref[...]] + [vc_ref[c * n:(c + 1) * n] for c in range(nsub)]
    def block_scores(c):
        k_all = jnp.concatenate([k_rows[c], k_rows[c + 1], km_ref[...]], axis=0)
        bias = bias0_ref[...] if c == 0 else bias1_ref[...]
        return _swa_scores(q_ref[c * n:(c + 1) * n], k_all, bias)

    s_next = block_scores(0)
    for c in range(nsub):
        s_cur = s_next
        if c + 1 < nsub:
            s_next = block_scores(c + 1)
        v_all = jnp.concatenate([v_rows[c], v_rows[c + 1], vm_ref[...]], axis=0)
        o_ref[c * n:(c + 1) * n] = _swa_outputs(s_cur, v_all, sink_ref)


def _swa(proj, projm, bias, sinks):
    nb, rows, _ = proj.shape
    tq = SWA_TQ
    ratio = tq // BLOCK
    kcol, vcol = OFF_QA // SWA_KV_W, OFF_KA // SWA_KV_W
    cur = lambda c: (lambda b, n: (b, n, c))
    prev = lambda c: (lambda b, n: (b, jnp.maximum(n * ratio - 1, 0), c))
    bias_spec = lambda idx: pl.BlockSpec((None, SWA_KEYS, SWA_Q_HEADS * BLOCK), idx)
    return pl.pallas_call(
        _swa_kernel,
        grid=(nb, rows // tq),
        in_specs=[
            pl.BlockSpec(memory_space=pltpu.SMEM),
            pl.BlockSpec((None, tq, SWA_Q_W), lambda b, n: (b, n, 0)),
            pl.BlockSpec((None, tq, SWA_KV_W), cur(kcol)),
            pl.BlockSpec((None, BLOCK, SWA_KV_W), prev(kcol)),
            pl.BlockSpec((None, tq, SWA_KV_W), cur(vcol)),
            pl.BlockSpec((None, BLOCK, SWA_KV_W), prev(vcol)),
            pl.BlockSpec((None, N_META, SWA_KV_W), lambda b, n: (0, 0, kcol)),
            pl.BlockSpec((None, N_META, SWA_KV_W), lambda b, n: (0, 0, vcol)),
            bias_spec(lambda b, n: (jnp.minimum(n, 1), 0, 0)),
            bias_spec(lambda b, n: (1, 0, 0)),
        ],
        out_specs=pl.BlockSpec((None, tq, SWA_Q_W), lambda b, n: (b, n, 0)),
        out_shape=jax.ShapeDtypeStruct((nb, rows, SWA_Q_W), BF16),
        compiler_params=pltpu.CompilerParams(
            dimension_semantics=("arbitrary", "arbitrary"), vmem_limit_bytes=VMEM_LIMIT),
        name="swa",
    )(sinks, proj, proj, proj, proj, proj, projm, projm, bias, bias)


def _fox_kernel(flags_ref, qt_ref, ka_ref, vt_ref, kam_ref, vtm_ref,
                wo_ref, wgu_ref, wd_ref, acc_ref, wo_bf_ref, wgu_bf_ref, wd_bf_ref):
    wo_bf_ref[...] = wo_ref[...].astype(BF16)
    wgu_bf_ref[...] = wgu_ref[...].astype(BF16)
    wd_bf_ref[...] = wd_ref[...].astype(BF16)

    b, g, i = pl.program_id(0), pl.program_id(1), pl.program_id(2)
    t = acc_ref.shape[2]
    nh = acc_ref.shape[0]
    colmax = lambda x: jnp.max(x, axis=0, keepdims=True)
    tile_rows = lambda j: pl.ds(pl.multiple_of(j * t, t), t)

    safe = flags_ref[(b * pl.num_programs(2) + i) * pl.num_programs(1) + g] > 0
    qt = [qt_ref[h] for h in range(nh)]

    def scores(h, j, diagonal):
        s = _dot(ka_ref[h, tile_rows(j), :], qt[h])
        if diagonal:
            key = lax.broadcasted_iota(jnp.int32, (t, t), 0)
            qry = lax.broadcasted_iota(jnp.int32, (t, t), 1)
            s = jnp.where(key <= qry, s, NEG_INF)
        return s

    @pl.when(safe)
    def _():
        def run(work):
            prev = None
            for score_fn, pv_fn in work:
                s = score_fn()
                if prev is not None:
                    prev[0](prev[1])
                prev = (pv_fn, s)
            prev[0](prev[1])

        def tile_work(j):
            def pv(h):
                def apply(s):
                    acc_ref[h] += _dot(vt_ref[h, :, tile_rows(j)], jnp.exp2(s).astype(BF16))
                return apply
            return [(lambda h=h: scores(h, j, False), pv(h)) for h in range(nh)]

        def first_work():
            def sc(h):
                return jnp.concatenate([scores(h, i, True), _dot(kam_ref[h], qt[h])], axis=0)
            def pv(h):
                def apply(s):
                    v_t = jnp.concatenate([vt_ref[h, :, tile_rows(i)], vtm_ref[h, :, 0:N_META]], axis=1)
                    acc_ref[h] = _dot(v_t, jnp.exp2(s).astype(BF16))
                return apply
            return [(lambda h=h: sc(h), pv(h)) for h in range(nh)]

        run(first_work())

        def tile_pair(p, carry):
            run(tile_work(2 * p) + tile_work(2 * p + 1))
            return carry

        lax.fori_loop(0, i // 2, tile_pair, 0)

        @pl.when(i % 2 == 1)
        def _():
            run(tile_work(i - 1))

    @pl.when(jnp.logical_not(safe))
    def _():
        m0 = []
        for h in range(nh):
            s = _dot(kam_ref[h], qt[h])
            m = colmax(s)
            acc_ref[h] = _dot(vtm_ref[h, :, 0:N_META], jnp.exp2(s - m).astype(BF16))
            m0.append(m)

        def tile(j, ms, diagonal):
            out = []
            ss = [scores(h, j, diagonal) for h in range(nh)]
            for h in range(nh):
                s = ss[h]
                m_next = jnp.maximum(ms[h], colmax(s))
                p = jnp.exp2(s - m_next).astype(BF16)
                acc_ref[h] = jnp.exp2(ms[h] - m_next) * acc_ref[h] + _dot(vt_ref[h, :, tile_rows(j)], p)
                out.append(m_next)
            return tuple(out)

        ms = lax.fori_loop(0, i, lambda j, ms: tile(j, ms, False), tuple(m0))
        tile(i, ms, True)


def _fox(flags, qt, ka, vt, ka_m, vt_m, w_out, w_gate_up, w_down):
    nb, _, rows, _ = ka.shape
    t = FOX_T
    nh = FOX_GROUP
    ngroup = FOX_HEADS // nh
    nq = rows // t
    nsteps = nb * ngroup * nq
    slab = lambda w: min(r for r in range(SLAB_ALIGN, w.shape[0] + 1, SLAB_ALIGN)
                         if w.shape[0] % r == 0 and r * nsteps >= w.shape[0])
    step = lambda b, g, i: (b * ngroup + g) * nq + i
    w_spec = lambda w: pl.BlockSpec((slab(w), w.shape[1]),
                                    lambda b, g, i, f: (jnp.minimum(step(b, g, i), w.shape[0] // slab(w) - 1), 0))
    return pl.pallas_call(
        _fox_kernel,
        grid_spec=pltpu.PrefetchScalarGridSpec(
            num_scalar_prefetch=1,
            grid=(nb, ngroup, nq),
            in_specs=[
                pl.BlockSpec((None, nh, PAIR_W, t), lambda b, g, i, f: (b, g, 0, i)),
                pl.BlockSpec((None, nh, rows, PAIR_W), lambda b, g, i, f: (b, g, 0, 0)),
                pl.BlockSpec((None, nh, FOX_VROWS, rows), lambda b, g, i, f: (b, g, 0, 0)),
                pl.BlockSpec((None, nh, N_META, PAIR_W), lambda b, g, i, f: (0, g, 0, 0)),
                pl.BlockSpec((None, nh, FOX_VROWS, META_TM), lambda b, g, i, f: (0, g, 0, 0)),
                w_spec(w_out), w_spec(w_gate_up), w_spec(w_down),
            ],
            out_specs=[pl.BlockSpec((None, nh, FOX_VROWS, t), lambda b, g, i, f: (b, g, 0, i)),
                       w_spec(w_out), w_spec(w_gate_up), w_spec(w_down)],
        ),
        out_shape=[jax.ShapeDtypeStruct((nb, FOX_HEADS, FOX_VROWS, rows), F32)]
        + [jax.ShapeDtypeStruct(w.shape, BF16) for w in (w_out, w_gate_up, w_down)],
        compiler_params=pltpu.CompilerParams(
            dimension_semantics=("arbitrary", "arbitrary", "arbitrary"), vmem_limit_bytes=VMEM_LIMIT),
        name="fox",
    )(flags, qt, ka, vt, ka_m, vt_m, w_out, w_gate_up, w_down)


def _ffn_kernel(oa_ref, fox_ref, x_ref, wo_ref, g1_ref, g2_ref, wg_ref, wu_ref, wd_ref, g3_ref, out_ref):
    tm = x_ref.shape[0]
    parts = [slice(c * tm // FFN_PARTS, (c + 1) * tm // FFN_PARTS) for c in range(FFN_PARTS)]
    def fox_rows(r):
        pairs = []
        for p in range(FOX_HEADS // 2):
            o_t = jnp.concatenate([fox_ref[h, 0:HEAD_DIM, r] / fox_ref[h, HEAD_DIM:HEAD_DIM + 1, r]
                                   for h in (2 * p, 2 * p + 1)], axis=0)
            pairs.append(o_t.T.astype(BF16))
        return jnp.concatenate(pairs, axis=1)

    mix = [jnp.concatenate([oa_ref[r, :], fox_rows(r)], axis=1) for r in parts]
    a = [_dot(m, wo_ref[...]) for m in mix]
    h1 = [x_ref[r, :] + _rms(ai, g1_ref[...]) for r, ai in zip(parts, a)]
    hn = [_rms(h, g2_ref[...]).astype(BF16) for h in h1]
    gate_up = [(_dot(h, wg_ref[...]), _dot(h, wu_ref[...])) for h in hn]
    act = [(g / (1.0 + jnp.exp(-g)) * u).astype(BF16) for g, u in gate_up]
    ff = [_dot(ac, wd_ref[...]) for ac in act]
    for r, h, f in zip(parts, h1, ff):
        out_ref[r, :] = h + _rms(f, g3_ref[...])


def _ffn(o_a, o_b, x3, wo, g1, g2, wgu, wd, g3):
    nb, rows, _ = x3.shape
    tm = FFN_TM
    const = lambda b, t: (0, 0)
    resident = lambda shape: pl.BlockSpec(shape, const, pipeline_mode=pl.Buffered(1))
    row = lambda w: pl.BlockSpec((None, tm, w), lambda b, t: (b, t, 0))
    return pl.pallas_call(
        _ffn_kernel,
        grid=(nb, rows // tm),
        in_specs=[
            row(SWA_Q_W), pl.BlockSpec((None, FOX_HEADS, FOX_VROWS, tm), lambda b, t: (b, 0, 0, t)), row(D_MODEL),
            resident((D_MODEL, D_MODEL)), resident((1, D_MODEL)), resident((1, D_MODEL)),
            pl.BlockSpec((D_MODEL, D_FF), lambda b, t: (0, 0), pipeline_mode=pl.Buffered(1)),
            pl.BlockSpec((D_MODEL, D_FF), lambda b, t: (0, 1), pipeline_mode=pl.Buffered(1)),
            resident((D_FF, D_MODEL)),
            resident((1, D_MODEL)),
        ],
        out_specs=row(D_MODEL),
        out_shape=jax.ShapeDtypeStruct((nb, rows, D_MODEL), F32),
        compiler_params=pltpu.CompilerParams(
            dimension_semantics=("arbitrary", "arbitrary"), vmem_limit_bytes=VMEM_LIMIT),
        name="outproj_ffn",
    )(o_a, o_b, x3, wo, g1, g2, wgu, wgu, wd, g3)


def kernel(x, meta_tokens, rel_bias, ln_pre_mix, ln_post_mix, ln_pre_ffn, ln_post_ffn,
           w_in, b_forget, sinks, w_out, w_gate_up, w_down):
    nb, seq, d = x.shape
    assert IN_TM == FOX_T, "the in-proj tile's running key-norm maximum is the fox query tile's bound"
    assert d == D_MODEL and seq % IN_TM == 0 and seq % FOX_T == 0 and seq % FFN_TM == 0 and seq % SWA_TQ == 0
    assert w_in.shape[0] == 1, "single-layer block"
    assert meta_tokens.shape == (N_META, D_MODEL)
    x = x.astype(F32)

    w_all = w_in[0].astype(BF16)
    pad_gate = GATE_ROWS - FOX_HEADS
    w_ft = jnp.pad(w_in[0, :, OFF_VB:], ((0, 0), (0, pad_gate))).astype(BF16)
    lane_rep = lambda v: jnp.broadcast_to(jnp.pad(v.astype(F32), (0, pad_gate))[:, None], (GATE_ROWS, LANES))
    b_ft = lane_rep(b_forget[0])
    g_pre = ln_pre_mix[0].astype(F32).reshape(1, D_MODEL)

    swa_bias = _bias_tables(rel_bias)

    x_m = jnp.pad(meta_tokens.astype(F32), ((0, META_TM - N_META), (0, 0)))[None]
    pa_m, _, ka_m, vt_m, kpre_m, _, c_meta = _inproj(x_m, g_pre, w_all, w_ft, b_ft, jnp.zeros((GATE_ROWS, LANES), F32),
                                                     jnp.zeros((FOX_HEADS, LANES), F32), META_TM)
    pa, qt, ka, vt, _, rmax, _ = _inproj(x, g_pre, w_all, w_ft, b_ft, c_meta[0, 0], kpre_m[0, 0], IN_TM)
    shift = rmax[..., 0].reshape(nb, seq // IN_TM, FOX_HEADS // FOX_GROUP, FOX_GROUP)
    flags = (jnp.max(shift, axis=-1) <= FOX_SAFE_LOG2).astype(jnp.int32).reshape(-1)

    o_a = _swa(pa, pa_m, swa_bias, sinks[0].astype(F32))
    o_b, w_out_bf, w_gu_bf, w_down_bf = _fox(flags, qt, ka, vt, ka_m, vt_m,
                                             w_out[0].astype(F32), w_gate_up[0].astype(F32),
                                             w_down[0].astype(F32))

    return _ffn(o_a, o_b, x,
                w_out_bf,
                ln_post_mix[0].astype(F32).reshape(1, D_MODEL),
                ln_pre_ffn[0].astype(F32).reshape(1, D_MODEL),
                w_gu_bf,
                w_down_bf,
                ln_post_ffn[0].astype(F32).reshape(1, D_MODEL))
```

```python
import math

import numpy as np
import jax
import jax.numpy as jnp
from jax import lax
from jax.experimental import pallas as pl
from jax.experimental.pallas import tpu as pltpu

D_MODEL = 1024
N_META = 16
HEAD_DIM = 64
SWA_Q_HEADS = 8
SWA_KV_HEADS = 2
SWA_GROUP = SWA_Q_HEADS // SWA_KV_HEADS
FOX_HEADS = 8
SWA_Q_W = SWA_Q_HEADS * HEAD_DIM
SWA_KV_W = SWA_KV_HEADS * HEAD_DIM
FOX_W = FOX_HEADS * HEAD_DIM
OFF_QA = SWA_Q_W
OFF_KA = OFF_QA + SWA_KV_W
OFF_VA = OFF_KA + SWA_KV_W
OFF_QB = OFF_VA + FOX_W
OFF_KB = OFF_QB + FOX_W
OFF_VB = OFF_KB + FOX_W
WINDOW = 128
BLOCK = 128
N_BUCKETS = 32
MAX_DISTANCE = 128
D_FF = 2816
EPS = 1e-6
NEG_INF = -1e30
SCALE = HEAD_DIM ** -0.5
LOG2E = math.log2(math.e)

LANES = 128
SUBLANES = 8
PAIR_W = 2 * HEAD_DIM
VMEM_LIMIT = 56 * 1024 * 1024

SWA_KEYS = 2 * BLOCK + N_META
SWA_TQ = 2048
SWA_ONES = 16
IN_TM = 512
META_TM = 128
GATE_ROWS = 16
FOX_VROWS = HEAD_DIM + 16
FOX_T = 512
FOX_GROUP = 8
FOX_SAFE_LOG2 = 40.0
FOX_BOUND_MARGIN = 1.001
FFN_TM = 512
FFN_PARTS = 2
SLAB_ALIGN = 16
N_PIECES = 3
AUG_ROWS = 16

F32 = jnp.float32
BF16 = jnp.bfloat16


def _dot(a, b):
    return jnp.dot(a, b, preferred_element_type=F32)


def _rms(t, g):
    return t * lax.rsqrt(jnp.mean(t * t, axis=-1, keepdims=True) + EPS) * g


def _bf16_pieces(v):
    pieces = []
    for _ in range(N_PIECES - 1):
        p = v.astype(BF16)
        pieces.append(p)
        v = v - p.astype(F32)
    pieces.append(v.astype(BF16))
    return pieces


def _t5_bucket_np(dist):
    n = np.maximum(dist, 0).astype(np.int32)
    max_exact = N_BUCKETS // 2
    nf = np.maximum(n, 1).astype(np.float32)
    large = max_exact + (np.log(nf / np.float32(max_exact)) / np.float32(math.log(MAX_DISTANCE / max_exact))
                         * np.float32(N_BUCKETS - max_exact)).astype(np.int32)
    large = np.minimum(large, N_BUCKETS - 1)
    return np.where(n < max_exact, n, large).astype(np.int32)


def _bias_kernel(tab_ref, bkt_ref, valid_ref, out_ref):
    bkt = bkt_ref[...]
    valid = valid_ref[...] > 0
    for h in range(SWA_Q_HEADS):
        acc = jnp.zeros(bkt.shape, F32)
        for b in range(N_BUCKETS):
            acc = jnp.where(bkt == b, tab_ref[b, h], acc)
        out_ref[:, h * BLOCK:(h + 1) * BLOCK] = jnp.where(valid, acc * LOG2E, NEG_INF)


def _bias_tables(rel_bias):
    ki = np.arange(2 * BLOCK)[:, None]
    qi = np.arange(BLOCK)[None, :]
    d_w = qi + BLOCK - ki
    bkt_w = _t5_bucket_np(d_w)
    in_window = (d_w >= 0) & (d_w < WINDOW)
    mi = np.arange(N_META)[:, None]
    bkt_m0 = _t5_bucket_np(N_META + qi - mi)
    far = _t5_bucket_np(np.asarray([[N_META + BLOCK - (N_META - 1)]]))
    assert far[0, 0] == N_BUCKETS - 1, "meta keys of later blocks must share the last bucket"
    bkt = np.stack([np.concatenate([bkt_w, bkt_m0]), np.concatenate([bkt_w, np.full_like(bkt_m0, far[0, 0])])])
    meta_ok = np.ones((N_META, BLOCK), bool)
    valid = np.stack([np.concatenate([in_window & (ki >= BLOCK), meta_ok]), np.concatenate([in_window, meta_ok])])
    spec = pl.BlockSpec((None, SWA_KEYS, BLOCK), lambda a: (a, 0, 0))
    return pl.pallas_call(
        _bias_kernel,
        grid=(2,),
        in_specs=[pl.BlockSpec(memory_space=pltpu.SMEM), spec, spec],
        out_specs=pl.BlockSpec((None, SWA_KEYS, SWA_Q_HEADS * BLOCK), lambda a: (a, 0, 0)),
        out_shape=jax.ShapeDtypeStruct((2, SWA_KEYS, SWA_Q_HEADS * BLOCK), F32),
        name="bias_tables",
    )(rel_bias.astype(F32), jnp.asarray(bkt.astype(np.int32)), jnp.asarray(valid.astype(np.int32)))


def _fox_aug_rows(val_rows, key_side):
    n = val_rows[0].shape[1]
    row = lax.broadcasted_iota(jnp.int32, (AUG_ROWS, n), 0)
    ones = (row >= N_PIECES) & (row < 3 * N_PIECES) if key_side else (row < N_PIECES)
    aug = jnp.where(ones, 1.0, 0.0)
    first = 0 if key_side else N_PIECES
    for g, val in enumerate(val_rows):
        for i, piece in enumerate(_bf16_pieces(val)):
            aug = jnp.where(row == first + g * N_PIECES + i, piece.astype(F32), aug)
    return aug


def _fox_head_t(own_t, aug, e):
    pad = jnp.zeros((HEAD_DIM - AUG_ROWS, own_t.shape[1]), F32)
    return jnp.concatenate([own_t, aug, pad] if e == 0 else [aug, pad, own_t], axis=0)


def _inproj_kernel(x_ref, g_ref, w_ref, wf_ref, bft_ref, tri_ref, c0_ref, k0_ref,
                   pa_ref, qt_ref, ka_ref, vt_ref, kpre_ref, rmax_ref, cmeta_ref, carry_ref, kpre_scr):
    @pl.when(pl.program_id(1) == 0)
    def _():
        carry_ref[...] = c0_ref[...]
        kpre_scr[...] = k0_ref[...]

    tm = x_ref.shape[0]
    half = tm // 2
    y_top = _rms(x_ref[0:half, :], g_ref[...]).astype(BF16)
    fox_top = _dot(y_top, w_ref[:, OFF_VA:OFF_VB])
    y = jnp.concatenate([y_top, _rms(x_ref[half:tm, :], g_ref[...]).astype(BF16)], axis=0)
    tile_lanes = lambda a: jnp.concatenate([a] * (tm // LANES), axis=1)
    f_t = lax.dot_general(wf_ref[...], y, (((0,), (1,)), ((), ())), preferred_element_type=F32) + tile_lanes(bft_ref[...])
    acc_fox = jnp.concatenate([fox_top, _dot(y[half:tm], w_ref[:, OFF_VA:OFF_VB])], axis=0)
    ls_t = jnp.minimum(f_t, 0.0) - jnp.log1p(jnp.exp(-jnp.abs(f_t)))
    cum_t = tile_lanes(carry_ref[...])
    for piece in _bf16_pieces(ls_t):
        cum_t = cum_t + _dot(piece, tri_ref[...])
    carry_ref[...] = jnp.broadcast_to(cum_t[:, tm - 1:tm], carry_ref.shape)
    cmeta_ref[...] = jnp.broadcast_to(cum_t[:, N_META - 1:N_META], cmeta_ref.shape)
    cb_t = cum_t * (-LOG2E)

    acc_swa = _dot(y, w_ref[:, 0:OFF_VA])
    pa_ref[:, 0:OFF_QA] = (acc_swa[:, 0:OFF_QA] * (SCALE * LOG2E)).astype(BF16)
    pa_ref[:, OFF_QA:OFF_VA] = acc_swa[:, OFF_QA:OFF_VA].astype(BF16)

    ones = jnp.ones((FOX_VROWS - HEAD_DIM, tm), F32)
    for p in range(FOX_HEADS // 2):
        cols = lambda off: slice(off - OFF_VA + p * PAIR_W, off - OFF_VA + (p + 1) * PAIR_W)
        q_t = (acc_fox[:, cols(OFF_VA)] * (SCALE * LOG2E)).astype(BF16).astype(F32).T
        k_t = acc_fox[:, cols(OFF_QB)].astype(BF16).astype(F32).T
        v_t = acc_fox[:, cols(OFF_KB)].astype(BF16).astype(F32).T
        for e in range(2):
            h = 2 * p + e
            own = slice(e * HEAD_DIM, (e + 1) * HEAD_DIM)
            qn2 = jnp.sum(q_t[own] * q_t[own], axis=0, keepdims=True)
            kn2 = jnp.max(jnp.sum(k_t[own] * k_t[own], axis=0, keepdims=True), axis=1, keepdims=True)
            kpre = jnp.maximum(kpre_scr[h:h + 1, :], kn2)
            kpre_scr[h:h + 1, :] = kpre
            kpre_ref[h:h + 1, :] = kpre
            r = jnp.sqrt(qn2 * kpre[:, 0:1]) * FOX_BOUND_MARGIN
            rmax_ref[h:h + 1, :] = jnp.broadcast_to(jnp.max(r, axis=1, keepdims=True), (1, LANES))
            ka_ref[h] = _fox_head_t(k_t[own], _fox_aug_rows([cb_t[h:h + 1]], True), e).T.astype(BF16)
            qt_ref[h] = _fox_head_t(q_t[own], _fox_aug_rows([-cb_t[h:h + 1], -r], False), e).astype(BF16)
            vt_ref[h] = jnp.concatenate([v_t[own], ones], axis=0).astype(BF16)


def _inproj(x3, g, w, wft, bft, c0, k0, tm):
    nb, rows, _ = x3.shape
    nt = rows // tm
    tri = jnp.asarray(np.triu(np.ones((tm, tm), np.float32)), BF16)
    const = lambda b, t: (0, 0)
    return pl.pallas_call(
        _inproj_kernel,
        grid=(nb, nt),
        in_specs=[
            pl.BlockSpec((None, tm, D_MODEL), lambda b, t: (b, t, 0)),
            pl.BlockSpec((1, D_MODEL), const),
            pl.BlockSpec(w.shape, const),
            pl.BlockSpec((D_MODEL, GATE_ROWS), const),
            pl.BlockSpec((GATE_ROWS, LANES), const),
            pl.BlockSpec((tm, tm), const),
            pl.BlockSpec((GATE_ROWS, LANES), const),
            pl.BlockSpec((FOX_HEADS, LANES), const),
        ],
        out_specs=[
            pl.BlockSpec((None, tm, OFF_VA), lambda b, t: (b, t, 0)),
            pl.BlockSpec((None, None, FOX_HEADS, PAIR_W, tm), lambda b, t: (b, t, 0, 0, 0)),
            pl.BlockSpec((None, FOX_HEADS, tm, PAIR_W), lambda b, t: (b, 0, t, 0)),
            pl.BlockSpec((None, None, FOX_HEADS, FOX_VROWS, tm), lambda b, t: (b, t, 0, 0, 0)),
            pl.BlockSpec((None, None, FOX_HEADS, LANES), lambda b, t: (b, t, 0, 0)),
            pl.BlockSpec((None, None, FOX_HEADS, LANES), lambda b, t: (b, t, 0, 0)),
            pl.BlockSpec((None, None, GATE_ROWS, LANES), lambda b, t: (b, t, 0, 0)),
        ],
        out_shape=[
            jax.ShapeDtypeStruct((nb, rows, OFF_VA), BF16),
            jax.ShapeDtypeStruct((nb, nt, FOX_HEADS, PAIR_W, tm), BF16),
            jax.ShapeDtypeStruct((nb, FOX_HEADS, rows, PAIR_W), BF16),
            jax.ShapeDtypeStruct((nb, nt, FOX_HEADS, FOX_VROWS, tm), BF16),
            jax.ShapeDtypeStruct((nb, nt, FOX_HEADS, LANES), F32),
            jax.ShapeDtypeStruct((nb, nt, FOX_HEADS, LANES), F32),
            jax.ShapeDtypeStruct((nb, nt, GATE_ROWS, LANES), F32),
        ],
        scratch_shapes=[pltpu.VMEM((GATE_ROWS, LANES), F32), pltpu.VMEM((FOX_HEADS, LANES), F32)],
        compiler_params=pltpu.CompilerParams(
            dimension_semantics=("arbitrary", "arbitrary"), vmem_limit_bytes=VMEM_LIMIT),
        name="inproj",
    )(x3, g, w, wft, bft, tri, c0, k0)


def _swa_scores(q, k_all, bias):
    nq = q.shape[0]
    zeros = jnp.zeros((HEAD_DIM, nq), F32)
    blocks = []
    for pair in range(SWA_Q_HEADS // 2):
        q_t = q[:, pair * PAIR_W:(pair + 1) * PAIR_W].astype(F32).T
        for hp in range(2):
            qh = q_t[hp * HEAD_DIM:(hp + 1) * HEAD_DIM]
            g = (2 * pair + hp) // SWA_GROUP
            blocks.append(jnp.concatenate([qh, zeros] if g == 0 else [zeros, qh], axis=0))
    q_bd = jnp.concatenate(blocks, axis=1).astype(BF16)
    return _dot(k_all, q_bd) + bias


def _swa_outputs(s, v_all, sink_ref):
    nq = s.shape[1] // SWA_Q_HEADS
    ps, sink_terms = [], []
    for h in range(SWA_Q_HEADS):
        sh = s[:, h * nq:(h + 1) * nq]
        sink = sink_ref[h] * LOG2E
        m = jnp.maximum(jnp.max(sh, axis=0, keepdims=True), sink)
        ps.append(jnp.exp2(sh - m).astype(BF16))
        sink_terms.append(jnp.exp2(sink - m))
    n = BLOCK
    v_t = jnp.concatenate([v_all[0:n].astype(F32).T, v_all[n:2 * n].astype(F32).T,
                           v_all[2 * n:].astype(F32).T], axis=1)
    ones = jnp.ones((SWA_ONES, v_t.shape[1]), F32)
    outs = []
    for g in range(SWA_KV_HEADS):
        v_aug = jnp.concatenate([v_t[g * HEAD_DIM:(g + 1) * HEAD_DIM], ones], axis=0).astype(BF16)
        p_g = jnp.concatenate(ps[g * SWA_GROUP:(g + 1) * SWA_GROUP], axis=1)
        o_g = _dot(v_aug, p_g)
        for j in range(SWA_GROUP):
            cols = slice(j * nq, (j + 1) * nq)
            l = o_g[HEAD_DIM:HEAD_DIM + 1, cols] + sink_terms[g * SWA_GROUP + j]
            outs.append(o_g[0:HEAD_DIM, cols] / l)
    pairs = [jnp.concatenate(outs[2 * p:2 * p + 2], axis=0).T for p in range(SWA_Q_HEADS // 2)]
    return jnp.concatenate(pairs, axis=1).astype(BF16)


def _swa_kernel(sink_ref, q_ref, kc_ref, kp_ref, vc_ref, vp_ref, km_ref, vm_ref, bias0_ref, bias1_ref, o_ref):
    n = BLOCK
    nsub = q_ref.shape[0] // n
    k_rows = [kp_ref[...]] + [kc_ref[c * n:(c + 1) * n] for c in range(nsub)]
    v_rows = [vp_ref[...]] + [vc_ref[c * n:(c + 1) * n] for c in range(nsub)]
    def block_scores(c):
        k_all = jnp.concatenate([k_rows[c], k_rows[c + 1], km_ref[...]], axis=0)
        bias = bias0_ref[...] if c == 0 else bias1_ref[...]
        return _swa_scores(q_ref[c * n:(c + 1) * n], k_all, bias)

    s_next = block_scores(0)
    for c in range(nsub):
        s_cur = s_next
        if c + 1 < nsub:
            s_next = block_scores(c + 1)
        v_all = jnp.concatenate([v_rows[c], v_rows[c + 1], vm_ref[...]], axis=0)
        o_ref[c * n:(c + 1) * n] = _swa_outputs(s_cur, v_all, sink_ref)


def _swa(proj, projm, bias, sinks):
    nb, rows, _ = proj.shape
    tq = SWA_TQ
    ratio = tq // BLOCK
    kcol, vcol = OFF_QA // SWA_KV_W, OFF_KA // SWA_KV_W
    cur = lambda c: (lambda b, n: (b, n, c))
    prev = lambda c: (lambda b, n: (b, jnp.maximum(n * ratio - 1, 0), c))
    bias_spec = lambda idx: pl.BlockSpec((None, SWA_KEYS, SWA_Q_HEADS * BLOCK), idx)
    return pl.pallas_call(
        _swa_kernel,
        grid=(nb, rows // tq),
        in_specs=[
            pl.BlockSpec(memory_space=pltpu.SMEM),
            pl.BlockSpec((None, tq, SWA_Q_W), lambda b, n: (b, n, 0)),
            pl.BlockSpec((None, tq, SWA_KV_W), cur(kcol)),
            pl.BlockSpec((None, BLOCK, SWA_KV_W), prev(kcol)),
            pl.BlockSpec((None, tq, SWA_KV_W), cur(vcol)),
            pl.BlockSpec((None, BLOCK, SWA_KV_W), prev(vcol)),
            pl.BlockSpec((None, N_META, SWA_KV_W), lambda b, n: (0, 0, kcol)),
            pl.BlockSpec((None, N_META, SWA_KV_W), lambda b, n: (0, 0, vcol)),
            bias_spec(lambda b, n: (jnp.minimum(n, 1), 0, 0)),
            bias_spec(lambda b, n: (1, 0, 0)),
        ],
        out_specs=pl.BlockSpec((None, tq, SWA_Q_W), lambda b, n: (b, n, 0)),
        out_shape=jax.ShapeDtypeStruct((nb, rows, SWA_Q_W), BF16),
        compiler_params=pltpu.CompilerParams(
            dimension_semantics=("arbitrary", "arbitrary"), vmem_limit_bytes=VMEM_LIMIT),
        name="swa",
    )(sinks, proj, proj, proj, proj, proj, projm, projm, bias, bias)


def _fox_kernel(flags_ref, qt_ref, ka_ref, vt_ref, kam_ref, vtm_ref,
                wo_ref, wgu_ref, wd_ref, acc_ref, wo_bf_ref, wgu_bf_ref, wd_bf_ref):
    wo_bf_ref[...] = wo_ref[...].astype(BF16)
    wgu_bf_ref[...] = wgu_ref[...].astype(BF16)
    wd_bf_ref[...] = wd_ref[...].astype(BF16)

    b, g, i = pl.program_id(0), pl.program_id(1), pl.program_id(2)
    t = acc_ref.shape[2]
    nh = acc_ref.shape[0]
    colmax = lambda x: jnp.max(x, axis=0, keepdims=True)
    tile_rows = lambda j: pl.ds(pl.multiple_of(j * t, t), t)

    safe = flags_ref[(b * pl.num_programs(2) + i) * pl.num_programs(1) + g] > 0
    qt = [qt_ref[h] for h in range(nh)]

    def scores(h, j, diagonal):
        s = _dot(ka_ref[h, tile_rows(j), :], qt[h])
        if diagonal:
            key = lax.broadcasted_iota(jnp.int32, (t, t), 0)
            qry = lax.broadcasted_iota(jnp.int32, (t, t), 1)
            s = jnp.where(key <= qry, s, NEG_INF)
        return s

    @pl.when(safe)
    def _():
        def run(work):
            prev = None
            for score_fn, pv_fn in work:
                s = score_fn()
                if prev is not None:
                    prev[0](prev[1])
                prev = (pv_fn, s)
            prev[0](prev[1])

        def tile_work(j):
            def pv(h):
                def apply(s):
                    acc_ref[h] += _dot(vt_ref[j, h], jnp.exp2(s).astype(BF16))
                return apply
            return [(lambda h=h: scores(h, j, False), pv(h)) for h in range(nh)]

        def first_work():
            def sc(h):
                return jnp.concatenate([scores(h, i, True), _dot(kam_ref[h], qt[h])], axis=0)
            def pv(h):
                def apply(s):
                    v_t = jnp.concatenate([vt_ref[i, h], vtm_ref[h, :, 0:N_META]], axis=1)
                    acc_ref[h] = _dot(v_t, jnp.exp2(s).astype(BF16))
                return apply
            return [(lambda h=h: sc(h), pv(h)) for h in range(nh)]

        run(first_work())

        def tile_pair(p, carry):
            run(tile_work(2 * p) + tile_work(2 * p + 1))
            return carry

        lax.fori_loop(0, i // 2, tile_pair, 0)

        @pl.when(i % 2 == 1)
        def _():
            run(tile_work(i - 1))

    @pl.when(jnp.logical_not(safe))
    def _():
        m0 = []
        for h in range(nh):
            s = _dot(kam_ref[h], qt[h])
            m = colmax(s)
            acc_ref[h] = _dot(vtm_ref[h, :, 0:N_META], jnp.exp2(s - m).astype(BF16))
            m0.append(m)

        def tile(j, ms, diagonal):
            out = []
            ss = [scores(h, j, diagonal) for h in range(nh)]
            for h in range(nh):
                s = ss[h]
                m_next = jnp.maximum(ms[h], colmax(s))
                p = jnp.exp2(s - m_next).astype(BF16)
                acc_ref[h] = jnp.exp2(ms[h] - m_next) * acc_ref[h] + _dot(vt_ref[j, h], p)
                out.append(m_next)
            return tuple(out)

        ms = lax.fori_loop(0, i, lambda j, ms: tile(j, ms, False), tuple(m0))
        tile(i, ms, True)


def _fox(flags, qt, ka, vt, ka_m, vt_m, w_out, w_gate_up, w_down):
    nb, _, rows, _ = ka.shape
    t = FOX_T
    nh = FOX_GROUP
    ngroup = FOX_HEADS // nh
    nq = rows // t
    nsteps = nb * ngroup * nq
    slab = lambda w: min(r for r in range(SLAB_ALIGN, w.shape[0] + 1, SLAB_ALIGN)
                         if w.shape[0] % r == 0 and r * nsteps >= w.shape[0])
    step = lambda b, g, i: (b * ngroup + g) * nq + i
    w_spec = lambda w: pl.BlockSpec((slab(w), w.shape[1]),
                                    lambda b, g, i, f: (jnp.minimum(step(b, g, i), w.shape[0] // slab(w) - 1), 0))
    return pl.pallas_call(
        _fox_kernel,
        grid_spec=pltpu.PrefetchScalarGridSpec(
            num_scalar_prefetch=1,
            grid=(nb, ngroup, nq),
            in_specs=[
                pl.BlockSpec((None, None, nh, PAIR_W, t), lambda b, g, i, f: (b, i, g, 0, 0)),
                pl.BlockSpec((None, nh, rows, PAIR_W), lambda b, g, i, f: (b, g, 0, 0)),
                pl.BlockSpec((None, nq, nh, FOX_VROWS, t), lambda b, g, i, f: (b, 0, g, 0, 0)),
                pl.BlockSpec((None, nh, N_META, PAIR_W), lambda b, g, i, f: (0, g, 0, 0)),
                pl.BlockSpec((None, None, nh, FOX_VROWS, META_TM), lambda b, g, i, f: (0, 0, g, 0, 0)),
                w_spec(w_out), w_spec(w_gate_up), w_spec(w_down),
            ],
            out_specs=[pl.BlockSpec((None, nh, FOX_VROWS, t), lambda b, g, i, f: (b, g, 0, i)),
                       w_spec(w_out), w_spec(w_gate_up), w_spec(w_down)],
        ),
        out_shape=[jax.ShapeDtypeStruct((nb, FOX_HEADS, FOX_VROWS, rows), F32)]
        + [jax.ShapeDtypeStruct(w.shape, BF16) for w in (w_out, w_gate_up, w_down)],
        compiler_params=pltpu.CompilerParams(
            dimension_semantics=("arbitrary", "arbitrary", "arbitrary"), vmem_limit_bytes=VMEM_LIMIT),
        name="fox",
    )(flags, qt, ka, vt, ka_m, vt_m, w_out, w_gate_up, w_down)


def _ffn_kernel(oa_ref, fox_ref, x_ref, wo_ref, g1_ref, g2_ref, wg_ref, wu_ref, wd_ref, g3_ref, out_ref):
    tm = x_ref.shape[0]
    parts = [slice(c * tm // FFN_PARTS, (c + 1) * tm // FFN_PARTS) for c in range(FFN_PARTS)]
    def fox_rows(r):
        pairs = []
        for p in range(FOX_HEADS // 2):
            o_t = jnp.concatenate([fox_ref[h, 0:HEAD_DIM, r] / fox_ref[h, HEAD_DIM:HEAD_DIM + 1, r]
                                   for h in (2 * p, 2 * p + 1)], axis=0)
            pairs.append(o_t.T.astype(BF16))
        return jnp.concatenate(pairs, axis=1)

    mix = [jnp.concatenate([oa_ref[r, :], fox_rows(r)], axis=1) for r in parts]
    a = [_dot(m, wo_ref[...]) for m in mix]
    h1 = [x_ref[r, :] + _rms(ai, g1_ref[...]) for r, ai in zip(parts, a)]
    hn = [_rms(h, g2_ref[...]).astype(BF16) for h in h1]
    gate_up = [(_dot(h, wg_ref[...]), _dot(h, wu_ref[...])) for h in hn]
    act = [(g / (1.0 + jnp.exp(-g)) * u).astype(BF16) for g, u in gate_up]
    ff = [_dot(ac, wd_ref[...]) for ac in act]
    for r, h, f in zip(parts, h1, ff):
        out_ref[r, :] = h + _rms(f, g3_ref[...])


def _ffn(o_a, o_b, x3, wo, g1, g2, wgu, wd, g3):
    nb, rows, _ = x3.shape
    tm = FFN_TM
    const = lambda b, t: (0, 0)
    resident = lambda shape: pl.BlockSpec(shape, const, pipeline_mode=pl.Buffered(1))
    row = lambda w: pl.BlockSpec((None, tm, w), lambda b, t: (b, t, 0))
    return pl.pallas_call(
        _ffn_kernel,
        grid=(nb, rows // tm),
        in_specs=[
            row(SWA_Q_W), pl.BlockSpec((None, FOX_HEADS, FOX_VROWS, tm), lambda b, t: (b, 0, 0, t)), row(D_MODEL),
            resident((D_MODEL, D_MODEL)), resident((1, D_MODEL)), resident((1, D_MODEL)),
            pl.BlockSpec((D_MODEL, D_FF), lambda b, t: (0, 0), pipeline_mode=pl.Buffered(1)),
            pl.BlockSpec((D_MODEL, D_FF), lambda b, t: (0, 1), pipeline_mode=pl.Buffered(1)),
            resident((D_FF, D_MODEL)),
            resident((1, D_MODEL)),
        ],
        out_specs=row(D_MODEL),
        out_shape=jax.ShapeDtypeStruct((nb, rows, D_MODEL), F32),
        compiler_params=pltpu.CompilerParams(
            dimension_semantics=("arbitrary", "arbitrary"), vmem_limit_bytes=VMEM_LIMIT),
        name="outproj_ffn",
    )(o_a, o_b, x3, wo, g1, g2, wgu, wgu, wd, g3)


def kernel(x, meta_tokens, rel_bias, ln_pre_mix, ln_post_mix, ln_pre_ffn, ln_post_ffn,
           w_in, b_forget, sinks, w_out, w_gate_up, w_down):
    nb, seq, d = x.shape
    assert IN_TM == FOX_T, "the in-proj tile's running key-norm maximum is the fox query tile's bound"
    assert d == D_MODEL and seq % IN_TM == 0 and seq % FOX_T == 0 and seq % FFN_TM == 0 and seq % SWA_TQ == 0
    assert w_in.shape[0] == 1, "single-layer block"
    assert meta_tokens.shape == (N_META, D_MODEL)
    x = x.astype(F32)

    w_all = w_in[0].astype(BF16)
    pad_gate = GATE_ROWS - FOX_HEADS
    w_ft = jnp.pad(w_in[0, :, OFF_VB:], ((0, 0), (0, pad_gate))).astype(BF16)
    lane_rep = lambda v: jnp.broadcast_to(jnp.pad(v.astype(F32), (0, pad_gate))[:, None], (GATE_ROWS, LANES))
    b_ft = lane_rep(b_forget[0])
    g_pre = ln_pre_mix[0].astype(F32).reshape(1, D_MODEL)

    swa_bias = _bias_tables(rel_bias)

    x_m = jnp.pad(meta_tokens.astype(F32), ((0, META_TM - N_META), (0, 0)))[None]
    pa_m, _, ka_m, vt_m, kpre_m, _, c_meta = _inproj(x_m, g_pre, w_all, w_ft, b_ft, jnp.zeros((GATE_ROWS, LANES), F32),
                                                     jnp.zeros((FOX_HEADS, LANES), F32), META_TM)
    pa, qt, ka, vt, _, rmax, _ = _inproj(x, g_pre, w_all, w_ft, b_ft, c_meta[0, 0], kpre_m[0, 0], IN_TM)
    shift = rmax[..., 0].reshape(nb, seq // IN_TM, FOX_HEADS // FOX_GROUP, FOX_GROUP)
    flags = (jnp.max(shift, axis=-1) <= FOX_SAFE_LOG2).astype(jnp.int32).reshape(-1)

    o_a = _swa(pa, pa_m, swa_bias, sinks[0].astype(F32))
    o_b, w_out_bf, w_gu_bf, w_down_bf = _fox(flags, qt, ka, vt, ka_m, vt_m,
                                             w_out[0].astype(F32), w_gate_up[0].astype(F32),
                                             w_down[0].astype(F32))

    return _ffn(o_a, o_b, x,
                w_out_bf,
                ln_post_mix[0].astype(F32).reshape(1, D_MODEL),
                ln_pre_ffn[0].astype(F32).reshape(1, D_MODEL),
                w_gu_bf,
                w_down_bf,
                ln_post_ffn[0].astype(F32).reshape(1, D_MODEL))
```

```python
import math

import numpy as np
import jax
import jax.numpy as jnp
from jax import lax
from jax.experimental import pallas as pl
from jax.experimental.pallas import tpu as pltpu

D_MODEL = 1024
N_META = 16
HEAD_DIM = 64
SWA_Q_HEADS = 8
SWA_KV_HEADS = 2
SWA_GROUP = SWA_Q_HEADS // SWA_KV_HEADS
FOX_HEADS = 8
SWA_Q_W = SWA_Q_HEADS * HEAD_DIM
SWA_KV_W = SWA_KV_HEADS * HEAD_DIM
FOX_W = FOX_HEADS * HEAD_DIM
OFF_QA = SWA_Q_W
OFF_KA = OFF_QA + SWA_KV_W
OFF_VA = OFF_KA + SWA_KV_W
OFF_QB = OFF_VA + FOX_W
OFF_KB = OFF_QB + FOX_W
OFF_VB = OFF_KB + FOX_W
WINDOW = 128
BLOCK = 128
N_BUCKETS = 32
MAX_DISTANCE = 128
D_FF = 2816
EPS = 1e-6
NEG_INF = -1e30
SCALE = HEAD_DIM ** -0.5
LOG2E = math.log2(math.e)

LANES = 128
PAIR_W = 2 * HEAD_DIM
VMEM_LIMIT = 56 * 1024 * 1024

SWA_KEYS = 2 * BLOCK + N_META
SWA_TQ = 2048
SWA_ONES = 16
IN_TM = 512
META_TM = 128
GATE_ROWS = 16
FOX_VROWS = HEAD_DIM + 16
FOX_T = 512
FOX_GROUP = 8
FOX_SAFE_LOG2 = 40.0
FOX_BOUND_MARGIN = 1.001
FFN_TM = 512
FFN_PARTS = 2
SLAB_ALIGN = 16
N_PIECES = 3
AUG_ROWS = 16

F32 = jnp.float32
BF16 = jnp.bfloat16


def _dot(a, b):
    return jnp.dot(a, b, preferred_element_type=F32)


def _rms(t, g):
    return t * lax.rsqrt(jnp.mean(t * t, axis=-1, keepdims=True) + EPS) * g


def _bf16_pieces(v):
    pieces = []
    for _ in range(N_PIECES - 1):
        p = v.astype(BF16)
        pieces.append(p)
        v = v - p.astype(F32)
    pieces.append(v.astype(BF16))
    return pieces


def _t5_bucket_np(dist):
    n = np.maximum(dist, 0).astype(np.int32)
    max_exact = N_BUCKETS // 2
    nf = np.maximum(n, 1).astype(np.float32)
    large = max_exact + (np.log(nf / np.float32(max_exact)) / np.float32(math.log(MAX_DISTANCE / max_exact))
                         * np.float32(N_BUCKETS - max_exact)).astype(np.int32)
    large = np.minimum(large, N_BUCKETS - 1)
    return np.where(n < max_exact, n, large).astype(np.int32)


def _bias_kernel(tab_ref, bkt_ref, valid_ref, out_ref):
    bkt = bkt_ref[...]
    valid = valid_ref[...] > 0
    for h in range(SWA_Q_HEADS):
        acc = jnp.zeros(bkt.shape, F32)
        for b in range(N_BUCKETS):
            acc = jnp.where(bkt == b, tab_ref[b, h], acc)
        out_ref[:, h * BLOCK:(h + 1) * BLOCK] = jnp.where(valid, acc * LOG2E, NEG_INF)


def _bias_tables(rel_bias):
    ki = np.arange(2 * BLOCK)[:, None]
    qi = np.arange(BLOCK)[None, :]
    d_w = qi + BLOCK - ki
    bkt_w = _t5_bucket_np(d_w)
    in_window = (d_w >= 0) & (d_w < WINDOW)
    mi = np.arange(N_META)[:, None]
    bkt_m0 = _t5_bucket_np(N_META + qi - mi)
    far = _t5_bucket_np(np.asarray([[N_META + BLOCK - (N_META - 1)]]))
    assert far[0, 0] == N_BUCKETS - 1, "meta keys of later blocks must share the last bucket"
    bkt = np.stack([np.concatenate([bkt_w, bkt_m0]), np.concatenate([bkt_w, np.full_like(bkt_m0, far[0, 0])])])
    meta_ok = np.ones((N_META, BLOCK), bool)
    valid = np.stack([np.concatenate([in_window & (ki >= BLOCK), meta_ok]), np.concatenate([in_window, meta_ok])])
    spec = pl.BlockSpec((None, SWA_KEYS, BLOCK), lambda a: (a, 0, 0))
    return pl.pallas_call(
        _bias_kernel,
        grid=(2,),
        in_specs=[pl.BlockSpec(memory_space=pltpu.SMEM), spec, spec],
        out_specs=pl.BlockSpec((None, SWA_KEYS, SWA_Q_HEADS * BLOCK), lambda a: (a, 0, 0)),
        out_shape=jax.ShapeDtypeStruct((2, SWA_KEYS, SWA_Q_HEADS * BLOCK), F32),
        name="bias_tables",
    )(rel_bias.astype(F32), jnp.asarray(bkt.astype(np.int32)), jnp.asarray(valid.astype(np.int32)))


def _fox_aug_rows(val_rows, key_side):
    n = val_rows[0].shape[1]
    row = lax.broadcasted_iota(jnp.int32, (AUG_ROWS, n), 0)
    ones = (row >= N_PIECES) & (row < 3 * N_PIECES) if key_side else (row < N_PIECES)
    aug = jnp.where(ones, 1.0, 0.0)
    first = 0 if key_side else N_PIECES
    for g, val in enumerate(val_rows):
        for i, piece in enumerate(_bf16_pieces(val)):
            aug = jnp.where(row == first + g * N_PIECES + i, piece.astype(F32), aug)
    return aug


def _fox_head_t(own_t, aug, e):
    pad = jnp.zeros((HEAD_DIM - AUG_ROWS, own_t.shape[1]), F32)
    return jnp.concatenate([own_t, aug, pad] if e == 0 else [aug, pad, own_t], axis=0)


def _inproj_kernel(x_ref, g_ref, w_ref, wf_ref, bft_ref, tri_ref, c0_ref, k0_ref,
                   pa_ref, qt_ref, ka_ref, vt_ref, kpre_ref, rmax_ref, cmeta_ref, carry_ref, kpre_scr):
    @pl.when(pl.program_id(1) == 0)
    def _():
        carry_ref[...] = c0_ref[...]
        kpre_scr[...] = k0_ref[...]

    tm = x_ref.shape[0]
    half = tm // 2
    y_top = _rms(x_ref[0:half, :], g_ref[...]).astype(BF16)
    fox_top = _dot(y_top, w_ref[:, OFF_VA:OFF_VB])
    y = jnp.concatenate([y_top, _rms(x_ref[half:tm, :], g_ref[...]).astype(BF16)], axis=0)
    tile_lanes = lambda a: jnp.concatenate([a] * (tm // LANES), axis=1)
    f_t = lax.dot_general(wf_ref[...], y, (((0,), (1,)), ((), ())), preferred_element_type=F32) + tile_lanes(bft_ref[...])
    acc_fox = jnp.concatenate([fox_top, _dot(y[half:tm], w_ref[:, OFF_VA:OFF_VB])], axis=0)
    ls_t = jnp.minimum(f_t, 0.0) - jnp.log1p(jnp.exp(-jnp.abs(f_t)))
    cum_t = tile_lanes(carry_ref[...])
    for piece in _bf16_pieces(ls_t):
        cum_t = cum_t + _dot(piece, tri_ref[...])
    carry_ref[...] = jnp.broadcast_to(cum_t[:, tm - 1:tm], carry_ref.shape)
    cmeta_ref[...] = jnp.broadcast_to(cum_t[:, N_META - 1:N_META], cmeta_ref.shape)
    cb_t = cum_t * (-LOG2E)

    acc_swa = _dot(y, w_ref[:, 0:OFF_VA])
    pa_ref[:, 0:OFF_QA] = (acc_swa[:, 0:OFF_QA] * (SCALE * LOG2E)).astype(BF16)
    pa_ref[:, OFF_QA:OFF_VA] = acc_swa[:, OFF_QA:OFF_VA].astype(BF16)

    ones = jnp.ones((FOX_VROWS - HEAD_DIM, tm), F32)
    for p in range(FOX_HEADS // 2):
        cols = lambda off: slice(off - OFF_VA + p * PAIR_W, off - OFF_VA + (p + 1) * PAIR_W)
        q_t = (acc_fox[:, cols(OFF_VA)] * (SCALE * LOG2E)).astype(BF16).astype(F32).T
        k_t = acc_fox[:, cols(OFF_QB)].astype(BF16).astype(F32).T
        v_t = acc_fox[:, cols(OFF_KB)].astype(BF16).astype(F32).T
        for e in range(2):
            h = 2 * p + e
            own = slice(e * HEAD_DIM, (e + 1) * HEAD_DIM)
            qn2 = jnp.sum(q_t[own] * q_t[own], axis=0, keepdims=True)
            kn2 = jnp.max(jnp.sum(k_t[own] * k_t[own], axis=0, keepdims=True), axis=1, keepdims=True)
            kpre = jnp.maximum(kpre_scr[h:h + 1, :], kn2)
            kpre_scr[h:h + 1, :] = kpre
            kpre_ref[h:h + 1, :] = kpre
            r = jnp.sqrt(qn2 * kpre[:, 0:1]) * FOX_BOUND_MARGIN
            rmax_ref[h:h + 1, :] = jnp.broadcast_to(jnp.max(r, axis=1, keepdims=True), (1, LANES))
            ka_ref[h] = _fox_head_t(k_t[own], _fox_aug_rows([cb_t[h:h + 1]], True), e).T.astype(BF16)
            qt_ref[h] = _fox_head_t(q_t[own], _fox_aug_rows([-cb_t[h:h + 1], -r], False), e).astype(BF16)
            vt_ref[h] = jnp.concatenate([v_t[own], ones], axis=0).astype(BF16)


def _inproj(x3, g, w, wft, bft, c0, k0, tm):
    nb, rows, _ = x3.shape
    nt = rows // tm
    tri = jnp.asarray(np.triu(np.ones((tm, tm), np.float32)), BF16)
    const = lambda b, t: (0, 0)
    return pl.pallas_call(
        _inproj_kernel,
        grid=(nb, nt),
        in_specs=[
            pl.BlockSpec((None, tm, D_MODEL), lambda b, t: (b, t, 0)),
            pl.BlockSpec((1, D_MODEL), const),
            pl.BlockSpec(w.shape, const),
            pl.BlockSpec((D_MODEL, GATE_ROWS), const),
            pl.BlockSpec((GATE_ROWS, LANES), const),
            pl.BlockSpec((tm, tm), const),
            pl.BlockSpec((GATE_ROWS, LANES), const),
            pl.BlockSpec((FOX_HEADS, LANES), const),
        ],
        out_specs=[
            pl.BlockSpec((None, tm, OFF_VA), lambda b, t: (b, t, 0)),
            pl.BlockSpec((None, None, FOX_HEADS, PAIR_W, tm), lambda b, t: (b, t, 0, 0, 0)),
            pl.BlockSpec((None, FOX_HEADS, tm, PAIR_W), lambda b, t: (b, 0, t, 0)),
            pl.BlockSpec((None, None, FOX_HEADS, FOX_VROWS, tm), lambda b, t: (b, t, 0, 0, 0)),
            pl.BlockSpec((None, None, FOX_HEADS, LANES), lambda b, t: (b, t, 0, 0)),
            pl.BlockSpec((None, None, FOX_HEADS, LANES), lambda b, t: (b, t, 0, 0)),
            pl.BlockSpec((None, None, GATE_ROWS, LANES), lambda b, t: (b, t, 0, 0)),
        ],
        out_shape=[
            jax.ShapeDtypeStruct((nb, rows, OFF_VA), BF16),
            jax.ShapeDtypeStruct((nb, nt, FOX_HEADS, PAIR_W, tm), BF16),
            jax.ShapeDtypeStruct((nb, FOX_HEADS, rows, PAIR_W), BF16),
            jax.ShapeDtypeStruct((nb, nt, FOX_HEADS, FOX_VROWS, tm), BF16),
            jax.ShapeDtypeStruct((nb, nt, FOX_HEADS, LANES), F32),
            jax.ShapeDtypeStruct((nb, nt, FOX_HEADS, LANES), F32),
            jax.ShapeDtypeStruct((nb, nt, GATE_ROWS, LANES), F32),
        ],
        scratch_shapes=[pltpu.VMEM((GATE_ROWS, LANES), F32), pltpu.VMEM((FOX_HEADS, LANES), F32)],
        compiler_params=pltpu.CompilerParams(
            dimension_semantics=("arbitrary", "arbitrary"), vmem_limit_bytes=VMEM_LIMIT),
        name="inproj",
    )(x3, g, w, wft, bft, tri, c0, k0)


def _swa_scores(q, k_all, bias):
    nq = q.shape[0]
    zeros = jnp.zeros((HEAD_DIM, nq), F32)
    blocks = []
    for pair in range(SWA_Q_HEADS // 2):
        q_t = q[:, pair * PAIR_W:(pair + 1) * PAIR_W].astype(F32).T
        for hp in range(2):
            qh = q_t[hp * HEAD_DIM:(hp + 1) * HEAD_DIM]
            g = (2 * pair + hp) // SWA_GROUP
            blocks.append(jnp.concatenate([qh, zeros] if g == 0 else [zeros, qh], axis=0))
    q_bd = jnp.concatenate(blocks, axis=1).astype(BF16)
    return _dot(k_all, q_bd) + bias


def _swa_outputs(s, v_all, sink_ref):
    nq = s.shape[1] // SWA_Q_HEADS
    ps, sink_terms = [], []
    for h in range(SWA_Q_HEADS):
        sh = s[:, h * nq:(h + 1) * nq]
        sink = sink_ref[h] * LOG2E
        m = jnp.maximum(jnp.max(sh, axis=0, keepdims=True), sink)
        ps.append(jnp.exp2(sh - m).astype(BF16))
        sink_terms.append(jnp.exp2(sink - m))
    n = BLOCK
    v_t = jnp.concatenate([v_all[0:n].astype(F32).T, v_all[n:2 * n].astype(F32).T,
                           v_all[2 * n:].astype(F32).T], axis=1)
    ones = jnp.ones((SWA_ONES, v_t.shape[1]), F32)
    outs = []
    for g in range(SWA_KV_HEADS):
        v_aug = jnp.concatenate([v_t[g * HEAD_DIM:(g + 1) * HEAD_DIM], ones], axis=0).astype(BF16)
        p_g = jnp.concatenate(ps[g * SWA_GROUP:(g + 1) * SWA_GROUP], axis=1)
        o_g = _dot(v_aug, p_g)
        for j in range(SWA_GROUP):
            cols = slice(j * nq, (j + 1) * nq)
            l = o_g[HEAD_DIM:HEAD_DIM + 1, cols] + sink_terms[g * SWA_GROUP + j]
            outs.append(o_g[0:HEAD_DIM, cols] / l)
    pairs = [jnp.concatenate(outs[2 * p:2 * p + 2], axis=0).T for p in range(SWA_Q_HEADS // 2)]
    return jnp.concatenate(pairs, axis=1).astype(BF16)


def _swa_kernel(sink_ref, q_ref, kc_ref, kp_ref, vc_ref, vp_ref, km_ref, vm_ref, bias0_ref, bias1_ref, o_ref):
    n = BLOCK
    nsub = q_ref.shape[0] // n
    k_rows = [kp_ref[...]] + [kc_ref[c * n:(c + 1) * n] for c in range(nsub)]
    v_rows = [vp_ref[...]] + [vc_ref[c * n:(c + 1) * n] for c in range(nsub)]
    def block_scores(c):
        k_all = jnp.concatenate([k_rows[c], k_rows[c + 1], km_ref[...]], axis=0)
        bias = bias0_ref[...] if c == 0 else bias1_ref[...]
        return _swa_scores(q_ref[c * n:(c + 1) * n], k_all, bias)

    s_next = block_scores(0)
    for c in range(nsub):
        s_cur = s_next
        if c + 1 < nsub:
            s_next = block_scores(c + 1)
        v_all = jnp.concatenate([v_rows[c], v_rows[c + 1], vm_ref[...]], axis=0)
        o_ref[c * n:(c + 1) * n] = _swa_outputs(s_cur, v_all, sink_ref)


def _swa(proj, projm, bias, sinks):
    nb, rows, _ = proj.shape
    tq = SWA_TQ
    ratio = tq // BLOCK
    kcol, vcol = OFF_QA // SWA_KV_W, OFF_KA // SWA_KV_W
    cur = lambda c: (lambda b, n: (b, n, c))
    prev = lambda c: (lambda b, n: (b, jnp.maximum(n * ratio - 1, 0), c))
    bias_spec = lambda idx: pl.BlockSpec((None, SWA_KEYS, SWA_Q_HEADS * BLOCK), idx)
    return pl.pallas_call(
        _swa_kernel,
        grid=(nb, rows // tq),
        in_specs=[
            pl.BlockSpec(memory_space=pltpu.SMEM),
            pl.BlockSpec((None, tq, SWA_Q_W), lambda b, n: (b, n, 0)),
            pl.BlockSpec((None, tq, SWA_KV_W), cur(kcol)),
            pl.BlockSpec((None, BLOCK, SWA_KV_W), prev(kcol)),
            pl.BlockSpec((None, tq, SWA_KV_W), cur(vcol)),
            pl.BlockSpec((None, BLOCK, SWA_KV_W), prev(vcol)),
            pl.BlockSpec((None, N_META, SWA_KV_W), lambda b, n: (0, 0, kcol)),
            pl.BlockSpec((None, N_META, SWA_KV_W), lambda b, n: (0, 0, vcol)),
            bias_spec(lambda b, n: (jnp.minimum(n, 1), 0, 0)),
            bias_spec(lambda b, n: (1, 0, 0)),
        ],
        out_specs=pl.BlockSpec((None, tq, SWA_Q_W), lambda b, n: (b, n, 0)),
        out_shape=jax.ShapeDtypeStruct((nb, rows, SWA_Q_W), BF16),
        compiler_params=pltpu.CompilerParams(
            dimension_semantics=("arbitrary", "arbitrary"), vmem_limit_bytes=VMEM_LIMIT),
        name="swa",
    )(sinks, proj, proj, proj, proj, proj, projm, projm, bias, bias)


def _fox_kernel(flags_ref, qt_ref, ka_ref, vt_ref, kam_ref, vtm_ref,
                wo_ref, wgu_ref, wd_ref, acc_ref, wo_bf_ref, wgu_bf_ref, wd_bf_ref):
    wo_bf_ref[...] = wo_ref[...].astype(BF16)
    wgu_bf_ref[...] = wgu_ref[...].astype(BF16)
    wd_bf_ref[...] = wd_ref[...].astype(BF16)

    b, g, i = pl.program_id(0), pl.program_id(1), pl.program_id(2)
    t = acc_ref.shape[2]
    nh = acc_ref.shape[0]
    colmax = lambda x: jnp.max(x, axis=0, keepdims=True)
    tile_rows = lambda j: pl.ds(pl.multiple_of(j * t, t), t)

    safe = flags_ref[(b * pl.num_programs(2) + i) * pl.num_programs(1) + g] > 0
    qt = [qt_ref[h] for h in range(nh)]

    def scores(h, j, diagonal):
        s = _dot(ka_ref[h, tile_rows(j), :], qt[h])
        if diagonal:
            key = lax.broadcasted_iota(jnp.int32, (t, t), 0)
            qry = lax.broadcasted_iota(jnp.int32, (t, t), 1)
            s = jnp.where(key <= qry, s, NEG_INF)
        return s

    @pl.when(safe)
    def _():
        def run(work):
            prev = None
            for score_fn, pv_fn in work:
                s = score_fn()
                if prev is not None:
                    prev[0](prev[1])
                prev = (pv_fn, s)
            prev[0](prev[1])

        def tile_work(j):
            def pv(h):
                def apply(s):
                    acc_ref[h] += _dot(vt_ref[j, h], jnp.exp2(s).astype(BF16))
                return apply
            return [(lambda h=h: scores(h, j, False), pv(h)) for h in range(nh)]

        def first_work():
            def sc(h):
                return jnp.concatenate([scores(h, i, True), _dot(kam_ref[h], qt[h])], axis=0)
            def pv(h):
                def apply(s):
                    v_t = jnp.concatenate([vt_ref[i, h], vtm_ref[h, :, 0:N_META]], axis=1)
                    acc_ref[h] = _dot(v_t, jnp.exp2(s).astype(BF16))
                return apply
            return [(lambda h=h: sc(h), pv(h)) for h in range(nh)]

        run(first_work())

        def tile_pair(p, carry):
            run(tile_work(2 * p) + tile_work(2 * p + 1))
            return carry

        lax.fori_loop(0, i // 2, tile_pair, 0)

        @pl.when(i % 2 == 1)
        def _():
            run(tile_work(i - 1))

    @pl.when(jnp.logical_not(safe))
    def _():
        m0 = []
        for h in range(nh):
            s = _dot(kam_ref[h], qt[h])
            m = colmax(s)
            acc_ref[h] = _dot(vtm_ref[h, :, 0:N_META], jnp.exp2(s - m).astype(BF16))
            m0.append(m)

        def tile(j, ms, diagonal):
            out = []
            ss = [scores(h, j, diagonal) for h in range(nh)]
            for h in range(nh):
                s = ss[h]
                m_next = jnp.maximum(ms[h], colmax(s))
                p = jnp.exp2(s - m_next).astype(BF16)
                acc_ref[h] = jnp.exp2(ms[h] - m_next) * acc_ref[h] + _dot(vt_ref[j, h], p)
                out.append(m_next)
            return tuple(out)

        ms = lax.fori_loop(0, i, lambda j, ms: tile(j, ms, False), tuple(m0))
        tile(i, ms, True)


def _fox(flags, qt, ka, vt, ka_m, vt_m, w_out, w_gate_up, w_down):
    nb, _, rows, _ = ka.shape
    t = FOX_T
    nh = FOX_GROUP
    ngroup = FOX_HEADS // nh
    nq = rows // t
    nsteps = nb * ngroup * nq
    slab = lambda w: min(r for r in range(SLAB_ALIGN, w.shape[0] + 1, SLAB_ALIGN)
                         if w.shape[0] % r == 0 and r * nsteps >= w.shape[0])
    step = lambda b, g, i: (b * ngroup + g) * nq + i
    w_spec = lambda w: pl.BlockSpec((slab(w), w.shape[1]),
                                    lambda b, g, i, f: (jnp.minimum(step(b, g, i), w.shape[0] // slab(w) - 1), 0))
    return pl.pallas_call(
        _fox_kernel,
        grid_spec=pltpu.PrefetchScalarGridSpec(
            num_scalar_prefetch=1,
            grid=(nb, ngroup, nq),
            in_specs=[
                pl.BlockSpec((None, None, nh, PAIR_W, t), lambda b, g, i, f: (b, i, g, 0, 0)),
                pl.BlockSpec((None, nh, rows, PAIR_W), lambda b, g, i, f: (b, g, 0, 0)),
                pl.BlockSpec((None, nq, nh, FOX_VROWS, t), lambda b, g, i, f: (b, 0, g, 0, 0)),
                pl.BlockSpec((None, nh, N_META, PAIR_W), lambda b, g, i, f: (0, g, 0, 0)),
                pl.BlockSpec((None, None, nh, FOX_VROWS, META_TM), lambda b, g, i, f: (0, 0, g, 0, 0)),
                w_spec(w_out), w_spec(w_gate_up), w_spec(w_down),
            ],
            out_specs=[pl.BlockSpec((None, nh, FOX_VROWS, t), lambda b, g, i, f: (b, g, 0, i)),
                       w_spec(w_out), w_spec(w_gate_up), w_spec(w_down)],
        ),
        out_shape=[jax.ShapeDtypeStruct((nb, FOX_HEADS, FOX_VROWS, rows), F32)]
        + [jax.ShapeDtypeStruct(w.shape, BF16) for w in (w_out, w_gate_up, w_down)],
        compiler_params=pltpu.CompilerParams(
            dimension_semantics=("arbitrary", "arbitrary", "arbitrary"), vmem_limit_bytes=VMEM_LIMIT),
        name="fox",
    )(flags, qt, ka, vt, ka_m, vt_m, w_out, w_gate_up, w_down)


def _ffn_kernel(oa_ref, fox_ref, x_ref, wo_ref, g1_ref, g2_ref, wg_ref, wu_ref, wd_ref, g3_ref, out_ref):
    tm = x_ref.shape[0]
    parts = [slice(c * tm // FFN_PARTS, (c + 1) * tm // FFN_PARTS) for c in range(FFN_PARTS)]
    def fox_rows(r):
        pairs = []
        for p in range(FOX_HEADS // 2):
            o_t = jnp.concatenate([fox_ref[h, 0:HEAD_DIM, r] / fox_ref[h, HEAD_DIM:HEAD_DIM + 1, r]
                                   for h in (2 * p, 2 * p + 1)], axis=0)
            pairs.append(o_t.T.astype(BF16))
        return jnp.concatenate(pairs, axis=1)

    mix = [jnp.concatenate([oa_ref[r, :], fox_rows(r)], axis=1) for r in parts]
    a = [_dot(m, wo_ref[...]) for m in mix]
    h1 = [x_ref[r, :] + _rms(ai, g1_ref[...]) for r, ai in zip(parts, a)]
    hn = [_rms(h, g2_ref[...]).astype(BF16) for h in h1]
    gate_up = [(_dot(h, wg_ref[...]), _dot(h, wu_ref[...])) for h in hn]
    act = [(g / (1.0 + jnp.exp(-g)) * u).astype(BF16) for g, u in gate_up]
    ff = [_dot(ac, wd_ref[...]) for ac in act]
    for r, h, f in zip(parts, h1, ff):
        out_ref[r, :] = h + _rms(f, g3_ref[...])


def _ffn(o_a, o_b, x3, wo, g1, g2, wgu, wd, g3):
    nb, rows, _ = x3.shape
    tm = FFN_TM
    const = lambda b, t: (0, 0)
    resident = lambda shape: pl.BlockSpec(shape, const, pipeline_mode=pl.Buffered(1))
    row = lambda w: pl.BlockSpec((None, tm, w), lambda b, t: (b, t, 0))
    return pl.pallas_call(
        _ffn_kernel,
        grid=(nb, rows // tm),
        in_specs=[
            row(SWA_Q_W), pl.BlockSpec((None, FOX_HEADS, FOX_VROWS, tm), lambda b, t: (b, 0, 0, t)), row(D_MODEL),
            resident((D_MODEL, D_MODEL)), resident((1, D_MODEL)), resident((1, D_MODEL)),
            pl.BlockSpec((D_MODEL, D_FF), lambda b, t: (0, 0), pipeline_mode=pl.Buffered(1)),
            pl.BlockSpec((D_MODEL, D_FF), lambda b, t: (0, 1), pipeline_mode=pl.Buffered(1)),
            resident((D_FF, D_MODEL)),
            resident((1, D_MODEL)),
        ],
        out_specs=row(D_MODEL),
        out_shape=jax.ShapeDtypeStruct((nb, rows, D_MODEL), F32),
        compiler_params=pltpu.CompilerParams(
            dimension_semantics=("arbitrary", "arbitrary"), vmem_limit_bytes=VMEM_LIMIT),
        name="outproj_ffn",
    )(o_a, o_b, x3, wo, g1, g2, wgu, wgu, wd, g3)


def kernel(x, meta_tokens, rel_bias, ln_pre_mix, ln_post_mix, ln_pre_ffn, ln_post_ffn,
           w_in, b_forget, sinks, w_out, w_gate_up, w_down):
    nb, seq, d = x.shape
    assert IN_TM == FOX_T, "the in-proj tile's running key-norm maximum is the fox query tile's bound"
    assert d == D_MODEL and seq % IN_TM == 0 and seq % FOX_T == 0 and seq % FFN_TM == 0 and seq % SWA_TQ == 0
    assert w_in.shape[0] == 1, "single-layer block"
    assert meta_tokens.shape == (N_META, D_MODEL)
    x = x.astype(F32)

    w_all = w_in[0].astype(BF16)
    pad_gate = GATE_ROWS - FOX_HEADS
    w_ft = jnp.pad(w_in[0, :, OFF_VB:], ((0, 0), (0, pad_gate))).astype(BF16)
    lane_rep = lambda v: jnp.broadcast_to(jnp.pad(v.astype(F32), (0, pad_gate))[:, None], (GATE_ROWS, LANES))
    b_ft = lane_rep(b_forget[0])
    g_pre = ln_pre_mix[0].astype(F32).reshape(1, D_MODEL)

    swa_bias = _bias_tables(rel_bias)

    x_m = jnp.pad(meta_tokens.astype(F32), ((0, META_TM - N_META), (0, 0)))[None]
    pa_m, _, ka_m, vt_m, kpre_m, _, c_meta = _inproj(x_m, g_pre, w_all, w_ft, b_ft, jnp.zeros((GATE_ROWS, LANES), F32),
                                                     jnp.zeros((FOX_HEADS, LANES), F32), META_TM)
    pa, qt, ka, vt, _, rmax, _ = _inproj(x, g_pre, w_all, w_ft, b_ft, c_meta[0, 0], kpre_m[0, 0], IN_TM)
    shift = rmax[..., 0].reshape(nb, seq // IN_TM, FOX_HEADS // FOX_GROUP, FOX_GROUP)
    flags = (jnp.max(shift, axis=-1) <= FOX_SAFE_LOG2).astype(jnp.int32).reshape(-1)

    o_a = _swa(pa, pa_m, swa_bias, sinks[0].astype(F32))
    o_b, w_out_bf, w_gu_bf, w_down_bf = _fox(flags, qt, ka, vt, ka_m, vt_m,
                                             w_out[0].astype(F32), w_gate_up[0].astype(F32),
                                             w_down[0].astype(F32))

    return _ffn(o_a, o_b, x,
                w_out_bf,
                ln_post_mix[0].astype(F32).reshape(1, D_MODEL),
                ln_pre_ffn[0].astype(F32).reshape(1, D_MODEL),
                w_gu_bf,
                w_down_bf,
                ln_post_ffn[0].astype(F32).reshape(1, D_MODEL))
```

```python
import math

import numpy as np
import jax
import jax.numpy as jnp
from jax import lax
from jax.experimental import pallas as pl
from jax.experimental.pallas import tpu as pltpu

D_MODEL = 1024
N_META = 16
HEAD_DIM = 64
SWA_Q_HEADS = 8
SWA_KV_HEADS = 2
SWA_GROUP = SWA_Q_HEADS // SWA_KV_HEADS
FOX_HEADS = 8
SWA_Q_W = SWA_Q_HEADS * HEAD_DIM
SWA_KV_W = SWA_KV_HEADS * HEAD_DIM
FOX_W = FOX_HEADS * HEAD_DIM
OFF_QA = SWA_Q_W
OFF_KA = OFF_QA + SWA_KV_W
OFF_VA = OFF_KA + SWA_KV_W
OFF_QB = OFF_VA + FOX_W
OFF_KB = OFF_QB + FOX_W
OFF_VB = OFF_KB + FOX_W
WINDOW = 128
BLOCK = 128
N_BUCKETS = 32
MAX_DISTANCE = 128
D_FF = 2816
EPS = 1e-6
NEG_INF = -1e30
SCALE = HEAD_DIM ** -0.5
LOG2E = math.log2(math.e)

LANES = 128
PAIR_W = 2 * HEAD_DIM
VMEM_LIMIT = 56 * 1024 * 1024

SWA_KEYS = 2 * BLOCK + N_META
SWA_TQ = 2048
SWA_ONES = 16
IN_TM = 512
META_TM = 128
GATE_ROWS = 16
FOX_VROWS = HEAD_DIM + 16
FOX_T = 512
FOX_GROUP = 8
FOX_SAFE_LOG2 = 40.0
FOX_BOUND_MARGIN = 1.001
FFN_TM = 512
FFN_PARTS = 2
SLAB_ALIGN = 16
N_PIECES = 3
FOX_K = HEAD_DIM + 16
AUG_ROWS = 16

F32 = jnp.float32
BF16 = jnp.bfloat16


def _dot(a, b):
    return jnp.dot(a, b, preferred_element_type=F32)


def _rms(t, g):
    return t * lax.rsqrt(jnp.mean(t * t, axis=-1, keepdims=True) + EPS) * g


def _bf16_pieces(v):
    pieces = []
    for _ in range(N_PIECES - 1):
        p = v.astype(BF16)
        pieces.append(p)
        v = v - p.astype(F32)
    pieces.append(v.astype(BF16))
    return pieces


def _t5_bucket_np(dist):
    n = np.maximum(dist, 0).astype(np.int32)
    max_exact = N_BUCKETS // 2
    nf = np.maximum(n, 1).astype(np.float32)
    large = max_exact + (np.log(nf / np.float32(max_exact)) / np.float32(math.log(MAX_DISTANCE / max_exact))
                         * np.float32(N_BUCKETS - max_exact)).astype(np.int32)
    large = np.minimum(large, N_BUCKETS - 1)
    return np.where(n < max_exact, n, large).astype(np.int32)


def _bias_kernel(tab_ref, bkt_ref, valid_ref, out_ref):
    bkt = bkt_ref[...]
    valid = valid_ref[...] > 0
    for h in range(SWA_Q_HEADS):
        acc = jnp.zeros(bkt.shape, F32)
        for b in range(N_BUCKETS):
            acc = jnp.where(bkt == b, tab_ref[b, h], acc)
        out_ref[:, h * BLOCK:(h + 1) * BLOCK] = jnp.where(valid, acc * LOG2E, NEG_INF)


def _bias_tables(rel_bias):
    ki = np.arange(2 * BLOCK)[:, None]
    qi = np.arange(BLOCK)[None, :]
    d_w = qi + BLOCK - ki
    bkt_w = _t5_bucket_np(d_w)
    in_window = (d_w >= 0) & (d_w < WINDOW)
    mi = np.arange(N_META)[:, None]
    bkt_m0 = _t5_bucket_np(N_META + qi - mi)
    far = _t5_bucket_np(np.asarray([[N_META + BLOCK - (N_META - 1)]]))
    assert far[0, 0] == N_BUCKETS - 1, "meta keys of later blocks must share the last bucket"
    bkt = np.stack([np.concatenate([bkt_w, bkt_m0]), np.concatenate([bkt_w, np.full_like(bkt_m0, far[0, 0])])])
    meta_ok = np.ones((N_META, BLOCK), bool)
    valid = np.stack([np.concatenate([in_window & (ki >= BLOCK), meta_ok]), np.concatenate([in_window, meta_ok])])
    spec = pl.BlockSpec((None, SWA_KEYS, BLOCK), lambda a: (a, 0, 0))
    return pl.pallas_call(
        _bias_kernel,
        grid=(2,),
        in_specs=[pl.BlockSpec(memory_space=pltpu.SMEM), spec, spec],
        out_specs=pl.BlockSpec((None, SWA_KEYS, SWA_Q_HEADS * BLOCK), lambda a: (a, 0, 0)),
        out_shape=jax.ShapeDtypeStruct((2, SWA_KEYS, SWA_Q_HEADS * BLOCK), F32),
        name="bias_tables",
    )(rel_bias.astype(F32), jnp.asarray(bkt.astype(np.int32)), jnp.asarray(valid.astype(np.int32)))


def _fox_aug_rows(val_rows, key_side):
    n = val_rows[0].shape[1]
    row = lax.broadcasted_iota(jnp.int32, (AUG_ROWS, n), 0)
    ones = (row >= N_PIECES) & (row < 3 * N_PIECES) if key_side else (row < N_PIECES)
    aug = jnp.where(ones, 1.0, 0.0)
    first = 0 if key_side else N_PIECES
    for g, val in enumerate(val_rows):
        for i, piece in enumerate(_bf16_pieces(val)):
            aug = jnp.where(row == first + g * N_PIECES + i, piece.astype(F32), aug)
    return aug


def _fox_head_t(own_t, aug, pad_to=None):
    parts = [own_t, aug]
    if pad_to:
        parts.append(jnp.zeros((pad_to - FOX_K, own_t.shape[1]), F32))
    return jnp.concatenate(parts, axis=0)


def _inproj_kernel(x_ref, g_ref, w_ref, wf_ref, bft_ref, tri_ref, c0_ref, k0_ref,
                   pa_ref, qt_ref, ka_ref, vt_ref, kpre_ref, rmax_ref, cmeta_ref, carry_ref, kpre_scr):
    @pl.when(pl.program_id(1) == 0)
    def _():
        carry_ref[...] = c0_ref[...]
        kpre_scr[...] = k0_ref[...]

    tm = x_ref.shape[0]
    half = tm // 2
    y_top = _rms(x_ref[0:half, :], g_ref[...]).astype(BF16)
    fox_top = _dot(y_top, w_ref[:, OFF_VA:OFF_VB])
    y = jnp.concatenate([y_top, _rms(x_ref[half:tm, :], g_ref[...]).astype(BF16)], axis=0)
    tile_lanes = lambda a: jnp.concatenate([a] * (tm // LANES), axis=1)
    f_t = lax.dot_general(wf_ref[...], y, (((0,), (1,)), ((), ())), preferred_element_type=F32) + tile_lanes(bft_ref[...])
    acc_fox = jnp.concatenate([fox_top, _dot(y[half:tm], w_ref[:, OFF_VA:OFF_VB])], axis=0)
    ls_t = jnp.minimum(f_t, 0.0) - jnp.log1p(jnp.exp(-jnp.abs(f_t)))
    cum_t = tile_lanes(carry_ref[...])
    for piece in _bf16_pieces(ls_t):
        cum_t = cum_t + _dot(piece, tri_ref[...])
    carry_ref[...] = jnp.broadcast_to(cum_t[:, tm - 1:tm], carry_ref.shape)
    cmeta_ref[...] = jnp.broadcast_to(cum_t[:, N_META - 1:N_META], cmeta_ref.shape)
    cb_t = cum_t * (-LOG2E)

    acc_swa = _dot(y, w_ref[:, 0:OFF_VA])
    pa_ref[:, 0:OFF_QA] = (acc_swa[:, 0:OFF_QA] * (SCALE * LOG2E)).astype(BF16)
    pa_ref[:, OFF_QA:OFF_VA] = acc_swa[:, OFF_QA:OFF_VA].astype(BF16)

    ones = jnp.ones((FOX_VROWS - HEAD_DIM, tm), F32)
    for p in range(FOX_HEADS // 2):
        cols = lambda off: slice(off - OFF_VA + p * PAIR_W, off - OFF_VA + (p + 1) * PAIR_W)
        q_t = (acc_fox[:, cols(OFF_VA)] * (SCALE * LOG2E)).astype(BF16).astype(F32).T
        k_t = acc_fox[:, cols(OFF_QB)].astype(BF16).astype(F32).T
        v_t = acc_fox[:, cols(OFF_KB)].astype(BF16).astype(F32).T
        for e in range(2):
            h = 2 * p + e
            own = slice(e * HEAD_DIM, (e + 1) * HEAD_DIM)
            qn2 = jnp.sum(q_t[own] * q_t[own], axis=0, keepdims=True)
            kn2 = jnp.max(jnp.sum(k_t[own] * k_t[own], axis=0, keepdims=True), axis=1, keepdims=True)
            kpre = jnp.maximum(kpre_scr[h:h + 1, :], kn2)
            kpre_scr[h:h + 1, :] = kpre
            kpre_ref[h:h + 1, :] = kpre
            r = jnp.sqrt(qn2 * kpre[:, 0:1]) * FOX_BOUND_MARGIN
            rmax_ref[h:h + 1, :] = jnp.broadcast_to(jnp.max(r, axis=1, keepdims=True), (1, LANES))
            ka_ref[h] = _fox_head_t(k_t[own], _fox_aug_rows([cb_t[h:h + 1]], True), PAIR_W).T.astype(BF16)
            qt_ref[h] = _fox_head_t(q_t[own], _fox_aug_rows([-cb_t[h:h + 1], -r], False)).astype(BF16)
            vt_ref[h] = jnp.concatenate([v_t[own], ones], axis=0).astype(BF16)


def _inproj(x3, g, w, wft, bft, c0, k0, tm):
    nb, rows, _ = x3.shape
    nt = rows // tm
    tri = jnp.asarray(np.triu(np.ones((tm, tm), np.float32)), BF16)
    const = lambda b, t: (0, 0)
    return pl.pallas_call(
        _inproj_kernel,
        grid=(nb, nt),
        in_specs=[
            pl.BlockSpec((None, tm, D_MODEL), lambda b, t: (b, t, 0)),
            pl.BlockSpec((1, D_MODEL), const),
            pl.BlockSpec(w.shape, const),
            pl.BlockSpec((D_MODEL, GATE_ROWS), const),
            pl.BlockSpec((GATE_ROWS, LANES), const),
            pl.BlockSpec((tm, tm), const),
            pl.BlockSpec((GATE_ROWS, LANES), const),
            pl.BlockSpec((FOX_HEADS, LANES), const),
        ],
        out_specs=[
            pl.BlockSpec((None, tm, OFF_VA), lambda b, t: (b, t, 0)),
            pl.BlockSpec((None, None, FOX_HEADS, FOX_K, tm), lambda b, t: (b, t, 0, 0, 0)),
            pl.BlockSpec((None, FOX_HEADS, tm, PAIR_W), lambda b, t: (b, 0, t, 0)),
            pl.BlockSpec((None, None, FOX_HEADS, FOX_VROWS, tm), lambda b, t: (b, t, 0, 0, 0)),
            pl.BlockSpec((None, None, FOX_HEADS, LANES), lambda b, t: (b, t, 0, 0)),
            pl.BlockSpec((None, None, FOX_HEADS, LANES), lambda b, t: (b, t, 0, 0)),
            pl.BlockSpec((None, None, GATE_ROWS, LANES), lambda b, t: (b, t, 0, 0)),
        ],
        out_shape=[
            jax.ShapeDtypeStruct((nb, rows, OFF_VA), BF16),
            jax.ShapeDtypeStruct((nb, nt, FOX_HEADS, FOX_K, tm), BF16),
            jax.ShapeDtypeStruct((nb, FOX_HEADS, rows, PAIR_W), BF16),
            jax.ShapeDtypeStruct((nb, nt, FOX_HEADS, FOX_VROWS, tm), BF16),
            jax.ShapeDtypeStruct((nb, nt, FOX_HEADS, LANES), F32),
            jax.ShapeDtypeStruct((nb, nt, FOX_HEADS, LANES), F32),
            jax.ShapeDtypeStruct((nb, nt, GATE_ROWS, LANES), F32),
        ],
        scratch_shapes=[pltpu.VMEM((GATE_ROWS, LANES), F32), pltpu.VMEM((FOX_HEADS, LANES), F32)],
        compiler_params=pltpu.CompilerParams(
            dimension_semantics=("arbitrary", "arbitrary"), vmem_limit_bytes=VMEM_LIMIT),
        name="inproj",
    )(x3, g, w, wft, bft, tri, c0, k0)


def _swa_scores(q, k_all, bias):
    nq = q.shape[0]
    zeros = jnp.zeros((HEAD_DIM, nq), F32)
    blocks = []
    for pair in range(SWA_Q_HEADS // 2):
        q_t = q[:, pair * PAIR_W:(pair + 1) * PAIR_W].astype(F32).T
        for hp in range(2):
            qh = q_t[hp * HEAD_DIM:(hp + 1) * HEAD_DIM]
            g = (2 * pair + hp) // SWA_GROUP
            blocks.append(jnp.concatenate([qh, zeros] if g == 0 else [zeros, qh], axis=0))
    q_bd = jnp.concatenate(blocks, axis=1).astype(BF16)
    return _dot(k_all, q_bd) + bias


def _swa_outputs(s, v_all, sink_ref):
    nq = s.shape[1] // SWA_Q_HEADS
    ps, sink_terms = [], []
    for h in range(SWA_Q_HEADS):
        sh = s[:, h * nq:(h + 1) * nq]
        sink = sink_ref[h] * LOG2E
        m = jnp.maximum(jnp.max(sh, axis=0, keepdims=True), sink)
        ps.append(jnp.exp2(sh - m).astype(BF16))
        sink_terms.append(jnp.exp2(sink - m))
    n = BLOCK
    v_t = jnp.concatenate([v_all[0:n].astype(F32).T, v_all[n:2 * n].astype(F32).T,
                           v_all[2 * n:].astype(F32).T], axis=1)
    ones = jnp.ones((SWA_ONES, v_t.shape[1]), F32)
    outs = []
    for g in range(SWA_KV_HEADS):
        v_aug = jnp.concatenate([v_t[g * HEAD_DIM:(g + 1) * HEAD_DIM], ones], axis=0).astype(BF16)
        p_g = jnp.concatenate(ps[g * SWA_GROUP:(g + 1) * SWA_GROUP], axis=1)
        o_g = _dot(v_aug, p_g)
        for j in range(SWA_GROUP):
            cols = slice(j * nq, (j + 1) * nq)
            l = o_g[HEAD_DIM:HEAD_DIM + 1, cols] + sink_terms[g * SWA_GROUP + j]
            outs.append(o_g[0:HEAD_DIM, cols] / l)
    pairs = [jnp.concatenate(outs[2 * p:2 * p + 2], axis=0).T for p in range(SWA_Q_HEADS // 2)]
    return jnp.concatenate(pairs, axis=1).astype(BF16)


def _swa_kernel(sink_ref, q_ref, kc_ref, kp_ref, vc_ref, vp_ref, km_ref, vm_ref, bias0_ref, bias1_ref, o_ref):
    n = BLOCK
    nsub = q_ref.shape[0] // n
    k_rows = [kp_ref[...]] + [kc_ref[c * n:(c + 1) * n] for c in range(nsub)]
    v_rows = [vp_ref[...]] + [vc_ref[c * n:(c + 1) * n] for c in range(nsub)]
    def block_scores(c):
        k_all = jnp.concatenate([k_rows[c], k_rows[c + 1], km_ref[...]], axis=0)
        bias = bias0_ref[...] if c == 0 else bias1_ref[...]
        return _swa_scores(q_ref[c * n:(c + 1) * n], k_all, bias)

    s_next = block_scores(0)
    for c in range(nsub):
        s_cur = s_next
        if c + 1 < nsub:
            s_next = block_scores(c + 1)
        v_all = jnp.concatenate([v_rows[c], v_rows[c + 1], vm_ref[...]], axis=0)
        o_ref[c * n:(c + 1) * n] = _swa_outputs(s_cur, v_all, sink_ref)


def _swa(proj, projm, bias, sinks):
    nb, rows, _ = proj.shape
    tq = SWA_TQ
    ratio = tq // BLOCK
    kcol, vcol = OFF_QA // SWA_KV_W, OFF_KA // SWA_KV_W
    cur = lambda c: (lambda b, n: (b, n, c))
    prev = lambda c: (lambda b, n: (b, jnp.maximum(n * ratio - 1, 0), c))
    bias_spec = lambda idx: pl.BlockSpec((None, SWA_KEYS, SWA_Q_HEADS * BLOCK), idx)
    return pl.pallas_call(
        _swa_kernel,
        grid=(nb, rows // tq),
        in_specs=[
            pl.BlockSpec(memory_space=pltpu.SMEM),
            pl.BlockSpec((None, tq, SWA_Q_W), lambda b, n: (b, n, 0)),
            pl.BlockSpec((None, tq, SWA_KV_W), cur(kcol)),
            pl.BlockSpec((None, BLOCK, SWA_KV_W), prev(kcol)),
            pl.BlockSpec((None, tq, SWA_KV_W), cur(vcol)),
            pl.BlockSpec((None, BLOCK, SWA_KV_W), prev(vcol)),
            pl.BlockSpec((None, N_META, SWA_KV_W), lambda b, n: (0, 0, kcol)),
            pl.BlockSpec((None, N_META, SWA_KV_W), lambda b, n: (0, 0, vcol)),
            bias_spec(lambda b, n: (jnp.minimum(n, 1), 0, 0)),
            bias_spec(lambda b, n: (1, 0, 0)),
        ],
        out_specs=pl.BlockSpec((None, tq, SWA_Q_W), lambda b, n: (b, n, 0)),
        out_shape=jax.ShapeDtypeStruct((nb, rows, SWA_Q_W), BF16),
        compiler_params=pltpu.CompilerParams(
            dimension_semantics=("arbitrary", "arbitrary"), vmem_limit_bytes=VMEM_LIMIT),
        name="swa",
    )(sinks, proj, proj, proj, proj, proj, projm, projm, bias, bias)


def _fox_kernel(flags_ref, qt_ref, ka_ref, vt_ref, kam_ref, vtm_ref,
                wo_ref, wgu_ref, wd_ref, acc_ref, wo_bf_ref, wgu_bf_ref, wd_bf_ref):
    wo_bf_ref[...] = wo_ref[...].astype(BF16)
    wgu_bf_ref[...] = wgu_ref[...].astype(BF16)
    wd_bf_ref[...] = wd_ref[...].astype(BF16)

    b, g, i = pl.program_id(0), pl.program_id(1), pl.program_id(2)
    t = acc_ref.shape[2]
    nh = acc_ref.shape[0]
    colmax = lambda x: jnp.max(x, axis=0, keepdims=True)
    tile_rows = lambda j: pl.ds(pl.multiple_of(j * t, t), t)

    safe = flags_ref[(b * pl.num_programs(2) + i) * pl.num_programs(1) + g] > 0
    qt = [qt_ref[h] for h in range(nh)]

    def scores(h, j, diagonal):
        s = _dot(ka_ref[h, tile_rows(j), 0:FOX_K], qt[h])
        if diagonal:
            key = lax.broadcasted_iota(jnp.int32, (t, t), 0)
            qry = lax.broadcasted_iota(jnp.int32, (t, t), 1)
            s = jnp.where(key <= qry, s, NEG_INF)
        return s

    @pl.when(safe)
    def _():
        def run(work):
            prev = None
            for score_fn, pv_fn in work:
                s = score_fn()
                if prev is not None:
                    prev[0](prev[1])
                prev = (pv_fn, s)
            prev[0](prev[1])

        def tile_work(j):
            def pv(h):
                def apply(s):
                    acc_ref[h] += _dot(vt_ref[j, h], jnp.exp2(s).astype(BF16))
                return apply
            return [(lambda h=h: scores(h, j, False), pv(h)) for h in range(nh)]

        def first_work():
            def sc(h):
                return jnp.concatenate([scores(h, i, True), _dot(kam_ref[h, :, 0:FOX_K], qt[h])], axis=0)
            def pv(h):
                def apply(s):
                    v_t = jnp.concatenate([vt_ref[i, h], vtm_ref[h, :, 0:N_META]], axis=1)
                    acc_ref[h] = _dot(v_t, jnp.exp2(s).astype(BF16))
                return apply
            return [(lambda h=h: sc(h), pv(h)) for h in range(nh)]

        run(first_work())

        def tile_pair(p, carry):
            run(tile_work(2 * p) + tile_work(2 * p + 1))
            return carry

        lax.fori_loop(0, i // 2, tile_pair, 0)

        @pl.when(i % 2 == 1)
        def _():
            run(tile_work(i - 1))

    @pl.when(jnp.logical_not(safe))
    def _():
        m0 = []
        for h in range(nh):
            s = _dot(kam_ref[h, :, 0:FOX_K], qt[h])
            m = colmax(s)
            acc_ref[h] = _dot(vtm_ref[h, :, 0:N_META], jnp.exp2(s - m).astype(BF16))
            m0.append(m)

        def tile(j, ms, diagonal):
            out = []
            ss = [scores(h, j, diagonal) for h in range(nh)]
            for h in range(nh):
                s = ss[h]
                m_next = jnp.maximum(ms[h], colmax(s))
                p = jnp.exp2(s - m_next).astype(BF16)
                acc_ref[h] = jnp.exp2(ms[h] - m_next) * acc_ref[h] + _dot(vt_ref[j, h], p)
                out.append(m_next)
            return tuple(out)

        ms = lax.fori_loop(0, i, lambda j, ms: tile(j, ms, False), tuple(m0))
        tile(i, ms, True)


def _fox(flags, qt, ka, vt, ka_m, vt_m, w_out, w_gate_up, w_down):
    nb, _, rows, _ = ka.shape
    t = FOX_T
    nh = FOX_GROUP
    ngroup = FOX_HEADS // nh
    nq = rows // t
    nsteps = nb * ngroup * nq
    slab = lambda w: min(r for r in range(SLAB_ALIGN, w.shape[0] + 1, SLAB_ALIGN)
                         if w.shape[0] % r == 0 and r * nsteps >= w.shape[0])
    step = lambda b, g, i: (b * ngroup + g) * nq + i
    w_spec = lambda w: pl.BlockSpec((slab(w), w.shape[1]),
                                    lambda b, g, i, f: (jnp.minimum(step(b, g, i), w.shape[0] // slab(w) - 1), 0))
    return pl.pallas_call(
        _fox_kernel,
        grid_spec=pltpu.PrefetchScalarGridSpec(
            num_scalar_prefetch=1,
            grid=(nb, ngroup, nq),
            in_specs=[
                pl.BlockSpec((None, None, nh, FOX_K, t), lambda b, g, i, f: (b, i, g, 0, 0)),
                pl.BlockSpec((None, nh, rows, PAIR_W), lambda b, g, i, f: (b, g, 0, 0)),
                pl.BlockSpec((None, nq, nh, FOX_VROWS, t), lambda b, g, i, f: (b, 0, g, 0, 0)),
                pl.BlockSpec((None, nh, N_META, PAIR_W), lambda b, g, i, f: (0, g, 0, 0)),
                pl.BlockSpec((None, None, nh, FOX_VROWS, META_TM), lambda b, g, i, f: (0, 0, g, 0, 0)),
                w_spec(w_out), w_spec(w_gate_up), w_spec(w_down),
            ],
            out_specs=[pl.BlockSpec((None, nh, FOX_VROWS, t), lambda b, g, i, f: (b, g, 0, i)),
                       w_spec(w_out), w_spec(w_gate_up), w_spec(w_down)],
        ),
        out_shape=[jax.ShapeDtypeStruct((nb, FOX_HEADS, FOX_VROWS, rows), F32)]
        + [jax.ShapeDtypeStruct(w.shape, BF16) for w in (w_out, w_gate_up, w_down)],
        compiler_params=pltpu.CompilerParams(
            dimension_semantics=("arbitrary", "arbitrary", "arbitrary"), vmem_limit_bytes=VMEM_LIMIT),
        name="fox",
    )(flags, qt, ka, vt, ka_m, vt_m, w_out, w_gate_up, w_down)


def _ffn_kernel(oa_ref, fox_ref, x_ref, wo_ref, g1_ref, g2_ref, wg_ref, wu_ref, wd_ref, g3_ref, out_ref):
    tm = x_ref.shape[0]
    parts = [slice(c * tm // FFN_PARTS, (c + 1) * tm // FFN_PARTS) for c in range(FFN_PARTS)]
    def fox_rows(r):
        pairs = []
        for p in range(FOX_HEADS // 2):
            o_t = jnp.concatenate([fox_ref[h, 0:HEAD_DIM, r] / fox_ref[h, HEAD_DIM:HEAD_DIM + 1, r]
                                   for h in (2 * p, 2 * p + 1)], axis=0)
            pairs.append(o_t.T.astype(BF16))
        return jnp.concatenate(pairs, axis=1)

    mix = [jnp.concatenate([oa_ref[r, :], fox_rows(r)], axis=1) for r in parts]
    a = [_dot(m, wo_ref[...]) for m in mix]
    h1 = [x_ref[r, :] + _rms(ai, g1_ref[...]) for r, ai in zip(parts, a)]
    hn = [_rms(h, g2_ref[...]).astype(BF16) for h in h1]
    gate_up = [(_dot(h, wg_ref[...]), _dot(h, wu_ref[...])) for h in hn]
    act = [(g / (1.0 + jnp.exp(-g)) * u).astype(BF16) for g, u in gate_up]
    ff = [_dot(ac, wd_ref[...]) for ac in act]
    for r, h, f in zip(parts, h1, ff):
        out_ref[r, :] = h + _rms(f, g3_ref[...])


def _ffn(o_a, o_b, x3, wo, g1, g2, wgu, wd, g3):
    nb, rows, _ = x3.shape
    tm = FFN_TM
    const = lambda b, t: (0, 0)
    resident = lambda shape: pl.BlockSpec(shape, const, pipeline_mode=pl.Buffered(1))
    row = lambda w: pl.BlockSpec((None, tm, w), lambda b, t: (b, t, 0))
    return pl.pallas_call(
        _ffn_kernel,
        grid=(nb, rows // tm),
        in_specs=[
            row(SWA_Q_W), pl.BlockSpec((None, FOX_HEADS, FOX_VROWS, tm), lambda b, t: (b, 0, 0, t)), row(D_MODEL),
            resident((D_MODEL, D_MODEL)), resident((1, D_MODEL)), resident((1, D_MODEL)),
            pl.BlockSpec((D_MODEL, D_FF), lambda b, t: (0, 0), pipeline_mode=pl.Buffered(1)),
            pl.BlockSpec((D_MODEL, D_FF), lambda b, t: (0, 1), pipeline_mode=pl.Buffered(1)),
            resident((D_FF, D_MODEL)),
            resident((1, D_MODEL)),
        ],
        out_specs=row(D_MODEL),
        out_shape=jax.ShapeDtypeStruct((nb, rows, D_MODEL), F32),
        compiler_params=pltpu.CompilerParams(
            dimension_semantics=("arbitrary", "arbitrary"), vmem_limit_bytes=VMEM_LIMIT),
        name="outproj_ffn",
    )(o_a, o_b, x3, wo, g1, g2, wgu, wgu, wd, g3)


def kernel(x, meta_tokens, rel_bias, ln_pre_mix, ln_post_mix, ln_pre_ffn, ln_post_ffn,
           w_in, b_forget, sinks, w_out, w_gate_up, w_down):
    nb, seq, d = x.shape
    assert IN_TM == FOX_T, "the in-proj tile's running key-norm maximum is the fox query tile's bound"
    assert d == D_MODEL and seq % IN_TM == 0 and seq % FOX_T == 0 and seq % FFN_TM == 0 and seq % SWA_TQ == 0
    assert w_in.shape[0] == 1, "single-layer block"
    assert meta_tokens.shape == (N_META, D_MODEL)
    x = x.astype(F32)

    w_all = w_in[0].astype(BF16)
    pad_gate = GATE_ROWS - FOX_HEADS
    w_ft = jnp.pad(w_in[0, :, OFF_VB:], ((0, 0), (0, pad_gate))).astype(BF16)
    lane_rep = lambda v: jnp.broadcast_to(jnp.pad(v.astype(F32), (0, pad_gate))[:, None], (GATE_ROWS, LANES))
    b_ft = lane_rep(b_forget[0])
    g_pre = ln_pre_mix[0].astype(F32).reshape(1, D_MODEL)

    swa_bias = _bias_tables(rel_bias)

    x_m = jnp.pad(meta_tokens.astype(F32), ((0, META_TM - N_META), (0, 0)))[None]
    pa_m, _, ka_m, vt_m, kpre_m, _, c_meta = _inproj(x_m, g_pre, w_all, w_ft, b_ft, jnp.zeros((GATE_ROWS, LANES), F32),
                                                     jnp.zeros((FOX_HEADS, LANES), F32), META_TM)
    pa, qt, ka, vt, _, rmax, _ = _inproj(x, g_pre, w_all, w_ft, b_ft, c_meta[0, 0], kpre_m[0, 0], IN_TM)
    shift = rmax[..., 0].reshape(nb, seq // IN_TM, FOX_HEADS // FOX_GROUP, FOX_GROUP)
    flags = (jnp.max(shift, axis=-1) <= FOX_SAFE_LOG2).astype(jnp.int32).reshape(-1)

    o_a = _swa(pa, pa_m, swa_bias, sinks[0].astype(F32))
    o_b, w_out_bf, w_gu_bf, w_down_bf = _fox(flags, qt, ka, vt, ka_m, vt_m,
                                             w_out[0].astype(F32), w_gate_up[0].astype(F32),
                                             w_down[0].astype(F32))

    return _ffn(o_a, o_b, x,
                w_out_bf,
                ln_post_mix[0].astype(F32).reshape(1, D_MODEL),
                ln_pre_ffn[0].astype(F32).reshape(1, D_MODEL),
                w_gu_bf,
                w_down_bf,
                ln_post_ffn[0].astype(F32).reshape(1, D_MODEL))
```

```python
import math

import numpy as np
import jax
import jax.numpy as jnp
from jax import lax
from jax.experimental import pallas as pl
from jax.experimental.pallas import tpu as pltpu

D_MODEL = 1024
N_META = 16
HEAD_DIM = 64
SWA_Q_HEADS = 8
SWA_KV_HEADS = 2
SWA_GROUP = SWA_Q_HEADS // SWA_KV_HEADS
FOX_HEADS = 8
SWA_Q_W = SWA_Q_HEADS * HEAD_DIM
SWA_KV_W = SWA_KV_HEADS * HEAD_DIM
FOX_W = FOX_HEADS * HEAD_DIM
OFF_QA = SWA_Q_W
OFF_KA = OFF_QA + SWA_KV_W
OFF_VA = OFF_KA + SWA_KV_W
OFF_QB = OFF_VA + FOX_W
OFF_KB = OFF_QB + FOX_W
OFF_VB = OFF_KB + FOX_W
WINDOW = 128
BLOCK = 128
N_BUCKETS = 32
MAX_DISTANCE = 128
D_FF = 2816
EPS = 1e-6
NEG_INF = -1e30
SCALE = HEAD_DIM ** -0.5
LOG2E = math.log2(math.e)

LANES = 128
PAIR_W = 2 * HEAD_DIM
VMEM_LIMIT = 56 * 1024 * 1024

SWA_KEYS = 2 * BLOCK + N_META
SWA_TQ = 2048
SWA_ONES = 16
IN_TM = 512
META_TM = 128
GATE_ROWS = 16
FOX_VROWS = HEAD_DIM + 16
FOX_T = 512
FOX_GROUP = 8
FOX_SAFE_LOG2 = 40.0
FOX_BOUND_MARGIN = 1.001
FFN_TM = 512
FFN_PARTS = 2
SLAB_ALIGN = 16
N_PIECES = 3
FOX_K = HEAD_DIM + 16
AUG_ROWS = 16

F32 = jnp.float32
BF16 = jnp.bfloat16


def _dot(a, b):
    return jnp.dot(a, b, preferred_element_type=F32)


def _rms(t, g):
    return t * lax.rsqrt(jnp.mean(t * t, axis=-1, keepdims=True) + EPS) * g


def _bf16_pieces(v):
    pieces = []
    for _ in range(N_PIECES - 1):
        p = v.astype(BF16)
        pieces.append(p)
        v = v - p.astype(F32)
    pieces.append(v.astype(BF16))
    return pieces


def _t5_bucket_np(dist):
    n = np.maximum(dist, 0).astype(np.int32)
    max_exact = N_BUCKETS // 2
    nf = np.maximum(n, 1).astype(np.float32)
    large = max_exact + (np.log(nf / np.float32(max_exact)) / np.float32(math.log(MAX_DISTANCE / max_exact))
                         * np.float32(N_BUCKETS - max_exact)).astype(np.int32)
    large = np.minimum(large, N_BUCKETS - 1)
    return np.where(n < max_exact, n, large).astype(np.int32)


def _bias_kernel(tab_ref, bkt_w_ref, bkt_m_ref, valid_ref, out_ref):
    def lookup(bkt, h):
        acc = jnp.zeros(bkt.shape, F32)
        for b in range(N_BUCKETS):
            acc = jnp.where(bkt == b, tab_ref[b, h], acc)
        return acc * LOG2E

    n_window = 2 * BLOCK
    for h in range(SWA_Q_HEADS):
        cols = slice(h * BLOCK, (h + 1) * BLOCK)
        window = lookup(bkt_w_ref[...], h)
        for a in range(2):
            out_ref[a, 0:n_window, cols] = jnp.where(valid_ref[a] > 0, window, NEG_INF)
            out_ref[a, n_window:SWA_KEYS, cols] = lookup(bkt_m_ref[a], h)


def _bias_tables(rel_bias):
    ki = np.arange(2 * BLOCK)[:, None]
    qi = np.arange(BLOCK)[None, :]
    d_w = qi + BLOCK - ki
    bkt_w = _t5_bucket_np(d_w)
    in_window = (d_w >= 0) & (d_w < WINDOW)
    mi = np.arange(N_META)[:, None]
    bkt_m0 = _t5_bucket_np(N_META + qi - mi)
    far = _t5_bucket_np(np.asarray([[N_META + BLOCK - (N_META - 1)]]))
    assert far[0, 0] == N_BUCKETS - 1, "meta keys of later blocks must share the last bucket"
    bkt_m = np.stack([bkt_m0, np.full_like(bkt_m0, far[0, 0])])
    valid = np.stack([in_window & (ki >= BLOCK), in_window])
    vmem = pl.BlockSpec(memory_space=pltpu.VMEM)
    return pl.pallas_call(
        _bias_kernel,
        in_specs=[pl.BlockSpec(memory_space=pltpu.SMEM), vmem, vmem, vmem],
        out_specs=vmem,
        out_shape=jax.ShapeDtypeStruct((2, SWA_KEYS, SWA_Q_HEADS * BLOCK), F32),
        name="bias_tables",
    )(rel_bias.astype(F32), jnp.asarray(bkt_w.astype(np.int32)), jnp.asarray(bkt_m.astype(np.int32)),
      jnp.asarray(valid.astype(np.int32)))


def _fox_aug_rows(val_rows, key_side):
    n = val_rows[0].shape[1]
    row = lax.broadcasted_iota(jnp.int32, (AUG_ROWS, n), 0)
    ones = (row >= N_PIECES) & (row < 3 * N_PIECES) if key_side else (row < N_PIECES)
    aug = jnp.where(ones, 1.0, 0.0)
    first = 0 if key_side else N_PIECES
    for g, val in enumerate(val_rows):
        for i, piece in enumerate(_bf16_pieces(val)):
            aug = jnp.where(row == first + g * N_PIECES + i, piece.astype(F32), aug)
    return aug


def _fox_head_t(own_t, aug, pad_to=None):
    parts = [own_t, aug]
    if pad_to:
        parts.append(jnp.zeros((pad_to - FOX_K, own_t.shape[1]), F32))
    return jnp.concatenate(parts, axis=0)


def _inproj_kernel(x_ref, g_ref, w_ref, wf_ref, bft_ref, tri_ref, c0_ref, k0_ref,
                   pa_ref, qt_ref, ka_ref, vt_ref, kpre_ref, rmax_ref, cmeta_ref, carry_ref, kpre_scr):
    @pl.when(pl.program_id(1) == 0)
    def _():
        carry_ref[...] = c0_ref[...]
        kpre_scr[...] = k0_ref[...]

    tm = x_ref.shape[0]
    half = tm // 2
    y_top = _rms(x_ref[0:half, :], g_ref[...]).astype(BF16)
    fox_top = _dot(y_top, w_ref[:, OFF_VA:OFF_VB])
    y = jnp.concatenate([y_top, _rms(x_ref[half:tm, :], g_ref[...]).astype(BF16)], axis=0)
    tile_lanes = lambda a: jnp.concatenate([a] * (tm // LANES), axis=1)
    f_t = lax.dot_general(wf_ref[...], y, (((0,), (1,)), ((), ())), preferred_element_type=F32) + tile_lanes(bft_ref[...])
    acc_fox = jnp.concatenate([fox_top, _dot(y[half:tm], w_ref[:, OFF_VA:OFF_VB])], axis=0)
    ls_t = jnp.minimum(f_t, 0.0) - jnp.log1p(jnp.exp(-jnp.abs(f_t)))
    cum_t = tile_lanes(carry_ref[...])
    for piece in _bf16_pieces(ls_t):
        cum_t = cum_t + _dot(piece, tri_ref[...])
    carry_ref[...] = jnp.broadcast_to(cum_t[:, tm - 1:tm], carry_ref.shape)
    cmeta_ref[...] = jnp.broadcast_to(cum_t[:, N_META - 1:N_META], cmeta_ref.shape)
    cb_t = cum_t * (-LOG2E)

    acc_swa = _dot(y, w_ref[:, 0:OFF_VA])
    pa_ref[:, 0:OFF_QA] = (acc_swa[:, 0:OFF_QA] * (SCALE * LOG2E)).astype(BF16)
    pa_ref[:, OFF_QA:OFF_VA] = acc_swa[:, OFF_QA:OFF_VA].astype(BF16)

    ones = jnp.ones((FOX_VROWS - HEAD_DIM, tm), F32)
    for p in range(FOX_HEADS // 2):
        cols = lambda off: slice(off - OFF_VA + p * PAIR_W, off - OFF_VA + (p + 1) * PAIR_W)
        q_t = (acc_fox[:, cols(OFF_VA)] * (SCALE * LOG2E)).astype(BF16).astype(F32).T
        k_t = acc_fox[:, cols(OFF_QB)].astype(BF16).astype(F32).T
        v_t = acc_fox[:, cols(OFF_KB)].astype(BF16).astype(F32).T
        for e in range(2):
            h = 2 * p + e
            own = slice(e * HEAD_DIM, (e + 1) * HEAD_DIM)
            qn2 = jnp.sum(q_t[own] * q_t[own], axis=0, keepdims=True)
            kn2 = jnp.max(jnp.sum(k_t[own] * k_t[own], axis=0, keepdims=True), axis=1, keepdims=True)
            kpre = jnp.maximum(kpre_scr[h:h + 1, :], kn2)
            kpre_scr[h:h + 1, :] = kpre
            kpre_ref[h:h + 1, :] = kpre
            r = jnp.sqrt(qn2 * kpre[:, 0:1]) * FOX_BOUND_MARGIN
            rmax_ref[h:h + 1, :] = jnp.broadcast_to(jnp.max(r, axis=1, keepdims=True), (1, LANES))
            ka_ref[h] = _fox_head_t(k_t[own], _fox_aug_rows([cb_t[h:h + 1]], True), PAIR_W).T.astype(BF16)
            qt_ref[h] = _fox_head_t(q_t[own], _fox_aug_rows([-cb_t[h:h + 1], -r], False)).astype(BF16)
            vt_ref[h] = jnp.concatenate([v_t[own], ones], axis=0).astype(BF16)


def _inproj(x3, g, w, wft, bft, c0, k0, tm):
    nb, rows, _ = x3.shape
    nt = rows // tm
    tri = jnp.asarray(np.triu(np.ones((tm, tm), np.float32)), BF16)
    const = lambda b, t: (0, 0)
    return pl.pallas_call(
        _inproj_kernel,
        grid=(nb, nt),
        in_specs=[
            pl.BlockSpec((None, tm, D_MODEL), lambda b, t: (b, t, 0)),
            pl.BlockSpec((1, D_MODEL), const),
            pl.BlockSpec(w.shape, const),
            pl.BlockSpec((D_MODEL, GATE_ROWS), const),
            pl.BlockSpec((GATE_ROWS, LANES), const),
            pl.BlockSpec((tm, tm), const),
            pl.BlockSpec((GATE_ROWS, LANES), const),
            pl.BlockSpec((FOX_HEADS, LANES), const),
        ],
        out_specs=[
            pl.BlockSpec((None, tm, OFF_VA), lambda b, t: (b, t, 0)),
            pl.BlockSpec((None, None, FOX_HEADS, FOX_K, tm), lambda b, t: (b, t, 0, 0, 0)),
            pl.BlockSpec((None, FOX_HEADS, tm, PAIR_W), lambda b, t: (b, 0, t, 0)),
            pl.BlockSpec((None, None, FOX_HEADS, FOX_VROWS, tm), lambda b, t: (b, t, 0, 0, 0)),
            pl.BlockSpec((None, None, FOX_HEADS, LANES), lambda b, t: (b, t, 0, 0)),
            pl.BlockSpec((None, None, FOX_HEADS, LANES), lambda b, t: (b, t, 0, 0)),
            pl.BlockSpec((None, None, GATE_ROWS, LANES), lambda b, t: (b, t, 0, 0)),
        ],
        out_shape=[
            jax.ShapeDtypeStruct((nb, rows, OFF_VA), BF16),
            jax.ShapeDtypeStruct((nb, nt, FOX_HEADS, FOX_K, tm), BF16),
            jax.ShapeDtypeStruct((nb, FOX_HEADS, rows, PAIR_W), BF16),
            jax.ShapeDtypeStruct((nb, nt, FOX_HEADS, FOX_VROWS, tm), BF16),
            jax.ShapeDtypeStruct((nb, nt, FOX_HEADS, LANES), F32),
            jax.ShapeDtypeStruct((nb, nt, FOX_HEADS, LANES), F32),
            jax.ShapeDtypeStruct((nb, nt, GATE_ROWS, LANES), F32),
        ],
        scratch_shapes=[pltpu.VMEM((GATE_ROWS, LANES), F32), pltpu.VMEM((FOX_HEADS, LANES), F32)],
        compiler_params=pltpu.CompilerParams(
            dimension_semantics=("arbitrary", "arbitrary"), vmem_limit_bytes=VMEM_LIMIT),
        name="inproj",
    )(x3, g, w, wft, bft, tri, c0, k0)


def _swa_scores(q, k_all, bias):
    nq = q.shape[0]
    zeros = jnp.zeros((HEAD_DIM, nq), F32)
    blocks = []
    for pair in range(SWA_Q_HEADS // 2):
        q_t = q[:, pair * PAIR_W:(pair + 1) * PAIR_W].astype(F32).T
        for hp in range(2):
            qh = q_t[hp * HEAD_DIM:(hp + 1) * HEAD_DIM]
            g = (2 * pair + hp) // SWA_GROUP
            blocks.append(jnp.concatenate([qh, zeros] if g == 0 else [zeros, qh], axis=0))
    q_bd = jnp.concatenate(blocks, axis=1).astype(BF16)
    return _dot(k_all, q_bd) + bias


def _swa_outputs(s, v_all, sink_ref):
    nq = s.shape[1] // SWA_Q_HEADS
    ps, sink_terms = [], []
    for h in range(SWA_Q_HEADS):
        sh = s[:, h * nq:(h + 1) * nq]
        sink = sink_ref[h] * LOG2E
        m = jnp.maximum(jnp.max(sh, axis=0, keepdims=True), sink)
        ps.append(jnp.exp2(sh - m).astype(BF16))
        sink_terms.append(jnp.exp2(sink - m))
    n = BLOCK
    v_t = jnp.concatenate([v_all[0:n].astype(F32).T, v_all[n:2 * n].astype(F32).T,
                           v_all[2 * n:].astype(F32).T], axis=1)
    ones = jnp.ones((SWA_ONES, v_t.shape[1]), F32)
    outs = []
    for g in range(SWA_KV_HEADS):
        v_aug = jnp.concatenate([v_t[g * HEAD_DIM:(g + 1) * HEAD_DIM], ones], axis=0).astype(BF16)
        p_g = jnp.concatenate(ps[g * SWA_GROUP:(g + 1) * SWA_GROUP], axis=1)
        o_g = _dot(v_aug, p_g)
        for j in range(SWA_GROUP):
            cols = slice(j * nq, (j + 1) * nq)
            l = o_g[HEAD_DIM:HEAD_DIM + 1, cols] + sink_terms[g * SWA_GROUP + j]
            outs.append(o_g[0:HEAD_DIM, cols] / l)
    pairs = [jnp.concatenate(outs[2 * p:2 * p + 2], axis=0).T for p in range(SWA_Q_HEADS // 2)]
    return jnp.concatenate(pairs, axis=1).astype(BF16)


def _swa_kernel(sink_ref, q_ref, kc_ref, kp_ref, vc_ref, vp_ref, km_ref, vm_ref, bias0_ref, bias1_ref, o_ref):
    n = BLOCK
    nsub = q_ref.shape[0] // n
    k_rows = [kp_ref[...]] + [kc_ref[c * n:(c + 1) * n] for c in range(nsub)]
    v_rows = [vp_ref[...]] + [vc_ref[c * n:(c + 1) * n] for c in range(nsub)]
    def block_scores(c):
        k_all = jnp.concatenate([k_rows[c], k_rows[c + 1], km_ref[...]], axis=0)
        bias = bias0_ref[...] if c == 0 else bias1_ref[...]
        return _swa_scores(q_ref[c * n:(c + 1) * n], k_all, bias)

    s_next = block_scores(0)
    for c in range(nsub):
        s_cur = s_next
        if c + 1 < nsub:
            s_next = block_scores(c + 1)
        v_all = jnp.concatenate([v_rows[c], v_rows[c + 1], vm_ref[...]], axis=0)
        o_ref[c * n:(c + 1) * n] = _swa_outputs(s_cur, v_all, sink_ref)


def _swa(proj, projm, bias, sinks):
    nb, rows, _ = proj.shape
    tq = SWA_TQ
    ratio = tq // BLOCK
    kcol, vcol = OFF_QA // SWA_KV_W, OFF_KA // SWA_KV_W
    cur = lambda c: (lambda b, n: (b, n, c))
    prev = lambda c: (lambda b, n: (b, jnp.maximum(n * ratio - 1, 0), c))
    bias_spec = lambda idx: pl.BlockSpec((None, SWA_KEYS, SWA_Q_HEADS * BLOCK), idx)
    return pl.pallas_call(
        _swa_kernel,
        grid=(nb, rows // tq),
        in_specs=[
            pl.BlockSpec(memory_space=pltpu.SMEM),
            pl.BlockSpec((None, tq, SWA_Q_W), lambda b, n: (b, n, 0)),
            pl.BlockSpec((None, tq, SWA_KV_W), cur(kcol)),
            pl.BlockSpec((None, BLOCK, SWA_KV_W), prev(kcol)),
            pl.BlockSpec((None, tq, SWA_KV_W), cur(vcol)),
            pl.BlockSpec((None, BLOCK, SWA_KV_W), prev(vcol)),
            pl.BlockSpec((None, N_META, SWA_KV_W), lambda b, n: (0, 0, kcol)),
            pl.BlockSpec((None, N_META, SWA_KV_W), lambda b, n: (0, 0, vcol)),
            bias_spec(lambda b, n: (jnp.minimum(n, 1), 0, 0)),
            bias_spec(lambda b, n: (1, 0, 0)),
        ],
        out_specs=pl.BlockSpec((None, tq, SWA_Q_W), lambda b, n: (b, n, 0)),
        out_shape=jax.ShapeDtypeStruct((nb, rows, SWA_Q_W), BF16),
        compiler_params=pltpu.CompilerParams(
            dimension_semantics=("arbitrary", "arbitrary"), vmem_limit_bytes=VMEM_LIMIT),
        name="swa",
    )(sinks, proj, proj, proj, proj, proj, projm, projm, bias, bias)


def _fox_kernel(flags_ref, qt_ref, ka_ref, vt_ref, kam_ref, vtm_ref,
                wo_ref, wgu_ref, wd_ref, acc_ref, wo_bf_ref, wgu_bf_ref, wd_bf_ref):
    wo_bf_ref[...] = wo_ref[...].astype(BF16)
    wgu_bf_ref[...] = wgu_ref[...].astype(BF16)
    wd_bf_ref[...] = wd_ref[...].astype(BF16)

    b, g, i = pl.program_id(0), pl.program_id(1), pl.program_id(2)
    t = acc_ref.shape[2]
    nh = acc_ref.shape[0]
    colmax = lambda x: jnp.max(x, axis=0, keepdims=True)
    tile_rows = lambda j: pl.ds(pl.multiple_of(j * t, t), t)

    safe = flags_ref[(b * pl.num_programs(2) + i) * pl.num_programs(1) + g] > 0
    qt = [qt_ref[h] for h in range(nh)]

    def scores(h, j, diagonal):
        s = _dot(ka_ref[h, tile_rows(j), 0:FOX_K], qt[h])
        if diagonal:
            key = lax.broadcasted_iota(jnp.int32, (t, t), 0)
            qry = lax.broadcasted_iota(jnp.int32, (t, t), 1)
            s = jnp.where(key <= qry, s, NEG_INF)
        return s

    @pl.when(safe)
    def _():
        def run(work):
            prev = None
            for score_fn, pv_fn in work:
                s = score_fn()
                if prev is not None:
                    prev[0](prev[1])
                prev = (pv_fn, s)
            prev[0](prev[1])

        def tile_work(j):
            def pv(h):
                def apply(s):
                    acc_ref[h] += _dot(vt_ref[j, h], jnp.exp2(s).astype(BF16))
                return apply
            return [(lambda h=h: scores(h, j, False), pv(h)) for h in range(nh)]

        def first_work():
            def sc(h):
                return jnp.concatenate([scores(h, i, True), _dot(kam_ref[h, :, 0:FOX_K], qt[h])], axis=0)
            def pv(h):
                def apply(s):
                    v_t = jnp.concatenate([vt_ref[i, h], vtm_ref[h, :, 0:N_META]], axis=1)
                    acc_ref[h] = _dot(v_t, jnp.exp2(s).astype(BF16))
                return apply
            return [(lambda h=h: sc(h), pv(h)) for h in range(nh)]

        run(first_work())

        def tile_pair(p, carry):
            run(tile_work(2 * p) + tile_work(2 * p + 1))
            return carry

        lax.fori_loop(0, i // 2, tile_pair, 0)

        @pl.when(i % 2 == 1)
        def _():
            run(tile_work(i - 1))

    @pl.when(jnp.logical_not(safe))
    def _():
        m0 = []
        for h in range(nh):
            s = _dot(kam_ref[h, :, 0:FOX_K], qt[h])
            m = colmax(s)
            acc_ref[h] = _dot(vtm_ref[h, :, 0:N_META], jnp.exp2(s - m).astype(BF16))
            m0.append(m)

        def tile(j, ms, diagonal):
            out = []
            ss = [scores(h, j, diagonal) for h in range(nh)]
            for h in range(nh):
                s = ss[h]
                m_next = jnp.maximum(ms[h], colmax(s))
                p = jnp.exp2(s - m_next).astype(BF16)
                acc_ref[h] = jnp.exp2(ms[h] - m_next) * acc_ref[h] + _dot(vt_ref[j, h], p)
                out.append(m_next)
            return tuple(out)

        ms = lax.fori_loop(0, i, lambda j, ms: tile(j, ms, False), tuple(m0))
        tile(i, ms, True)


def _fox(flags, qt, ka, vt, ka_m, vt_m, w_out, w_gate_up, w_down):
    nb, _, rows, _ = ka.shape
    t = FOX_T
    nh = FOX_GROUP
    ngroup = FOX_HEADS // nh
    nq = rows // t
    nsteps = nb * ngroup * nq
    slab = lambda w: min(r for r in range(SLAB_ALIGN, w.shape[0] + 1, SLAB_ALIGN)
                         if w.shape[0] % r == 0 and r * nsteps >= w.shape[0])
    step = lambda b, g, i: (b * ngroup + g) * nq + i
    w_spec = lambda w: pl.BlockSpec((slab(w), w.shape[1]),
                                    lambda b, g, i, f: (jnp.minimum(step(b, g, i), w.shape[0] // slab(w) - 1), 0))
    return pl.pallas_call(
        _fox_kernel,
        grid_spec=pltpu.PrefetchScalarGridSpec(
            num_scalar_prefetch=1,
            grid=(nb, ngroup, nq),
            in_specs=[
                pl.BlockSpec((None, None, nh, FOX_K, t), lambda b, g, i, f: (b, i, g, 0, 0)),
                pl.BlockSpec((None, nh, rows, PAIR_W), lambda b, g, i, f: (b, g, 0, 0)),
                pl.BlockSpec((None, nq, nh, FOX_VROWS, t), lambda b, g, i, f: (b, 0, g, 0, 0)),
                pl.BlockSpec((None, nh, N_META, PAIR_W), lambda b, g, i, f: (0, g, 0, 0)),
                pl.BlockSpec((None, None, nh, FOX_VROWS, META_TM), lambda b, g, i, f: (0, 0, g, 0, 0)),
                w_spec(w_out), w_spec(w_gate_up), w_spec(w_down),
            ],
            out_specs=[pl.BlockSpec((None, nh, FOX_VROWS, t), lambda b, g, i, f: (b, g, 0, i)),
                       w_spec(w_out), w_spec(w_gate_up), w_spec(w_down)],
        ),
        out_shape=[jax.ShapeDtypeStruct((nb, FOX_HEADS, FOX_VROWS, rows), F32)]
        + [jax.ShapeDtypeStruct(w.shape, BF16) for w in (w_out, w_gate_up, w_down)],
        compiler_params=pltpu.CompilerParams(
            dimension_semantics=("arbitrary", "arbitrary", "arbitrary"), vmem_limit_bytes=VMEM_LIMIT),
        name="fox",
    )(flags, qt, ka, vt, ka_m, vt_m, w_out, w_gate_up, w_down)


def _ffn_kernel(oa_ref, fox_ref, x_ref, wo_ref, g1_ref, g2_ref, wg_ref, wu_ref, wd_ref, g3_ref, out_ref):
    tm = x_ref.shape[0]
    parts = [slice(c * tm // FFN_PARTS, (c + 1) * tm // FFN_PARTS) for c in range(FFN_PARTS)]
    def fox_rows(r):
        pairs = []
        for p in range(FOX_HEADS // 2):
            o_t = jnp.concatenate([fox_ref[h, 0:HEAD_DIM, r] / fox_ref[h, HEAD_DIM:HEAD_DIM + 1, r]
                                   for h in (2 * p, 2 * p + 1)], axis=0)
            pairs.append(o_t.T.astype(BF16))
        return jnp.concatenate(pairs, axis=1)

    mix = [jnp.concatenate([oa_ref[r, :], fox_rows(r)], axis=1) for r in parts]
    a = [_dot(m, wo_ref[...]) for m in mix]
    h1 = [x_ref[r, :] + _rms(ai, g1_ref[...]) for r, ai in zip(parts, a)]
    hn = [_rms(h, g2_ref[...]).astype(BF16) for h in h1]
    gate_up = [(_dot(h, wg_ref[...]), _dot(h, wu_ref[...])) for h in hn]
    act = [(g / (1.0 + jnp.exp(-g)) * u).astype(BF16) for g, u in gate_up]
    ff = [_dot(ac, wd_ref[...]) for ac in act]
    for r, h, f in zip(parts, h1, ff):
        out_ref[r, :] = h + _rms(f, g3_ref[...])


def _ffn(o_a, o_b, x3, wo, g1, g2, wgu, wd, g3):
    nb, rows, _ = x3.shape
    tm = FFN_TM
    const = lambda b, t: (0, 0)
    resident = lambda shape: pl.BlockSpec(shape, const, pipeline_mode=pl.Buffered(1))
    row = lambda w: pl.BlockSpec((None, tm, w), lambda b, t: (b, t, 0))
    return pl.pallas_call(
        _ffn_kernel,
        grid=(nb, rows // tm),
        in_specs=[
            row(SWA_Q_W), pl.BlockSpec((None, FOX_HEADS, FOX_VROWS, tm), lambda b, t: (b, 0, 0, t)), row(D_MODEL),
            resident((D_MODEL, D_MODEL)), resident((1, D_MODEL)), resident((1, D_MODEL)),
            pl.BlockSpec((D_MODEL, D_FF), lambda b, t: (0, 0), pipeline_mode=pl.Buffered(1)),
            pl.BlockSpec((D_MODEL, D_FF), lambda b, t: (0, 1), pipeline_mode=pl.Buffered(1)),
            resident((D_FF, D_MODEL)),
            resident((1, D_MODEL)),
        ],
        out_specs=row(D_MODEL),
        out_shape=jax.ShapeDtypeStruct((nb, rows, D_MODEL), F32),
        compiler_params=pltpu.CompilerParams(
            dimension_semantics=("arbitrary", "arbitrary"), vmem_limit_bytes=VMEM_LIMIT),
        name="outproj_ffn",
    )(o_a, o_b, x3, wo, g1, g2, wgu, wgu, wd, g3)


def kernel(x, meta_tokens, rel_bias, ln_pre_mix, ln_post_mix, ln_pre_ffn, ln_post_ffn,
           w_in, b_forget, sinks, w_out, w_gate_up, w_down):
    nb, seq, d = x.shape
    assert IN_TM == FOX_T, "the in-proj tile's running key-norm maximum is the fox query tile's bound"
    assert d == D_MODEL and seq % IN_TM == 0 and seq % FOX_T == 0 and seq % FFN_TM == 0 and seq % SWA_TQ == 0
    assert w_in.shape[0] == 1, "single-layer block"
    assert meta_tokens.shape == (N_META, D_MODEL)
    x = x.astype(F32)

    w_all = w_in[0].astype(BF16)
    pad_gate = GATE_ROWS - FOX_HEADS
    w_ft = jnp.pad(w_in[0, :, OFF_VB:], ((0, 0), (0, pad_gate))).astype(BF16)
    lane_rep = lambda v: jnp.broadcast_to(jnp.pad(v.astype(F32), (0, pad_gate))[:, None], (GATE_ROWS, LANES))
    b_ft = lane_rep(b_forget[0])
    g_pre = ln_pre_mix[0].astype(F32).reshape(1, D_MODEL)

    swa_bias = _bias_tables(rel_bias)

    x_m = jnp.pad(meta_tokens.astype(F32), ((0, META_TM - N_META), (0, 0)))[None]
    pa_m, _, ka_m, vt_m, kpre_m, _, c_meta = _inproj(x_m, g_pre, w_all, w_ft, b_ft, jnp.zeros((GATE_ROWS, LANES), F32),
                                                     jnp.zeros((FOX_HEADS, LANES), F32), META_TM)
    pa, qt, ka, vt, _, rmax, _ = _inproj(x, g_pre, w_all, w_ft, b_ft, c_meta[0, 0], kpre_m[0, 0], IN_TM)
    shift = rmax[..., 0].reshape(nb, seq // IN_TM, FOX_HEADS // FOX_GROUP, FOX_GROUP)
    flags = (jnp.max(shift, axis=-1) <= FOX_SAFE_LOG2).astype(jnp.int32).reshape(-1)

    o_a = _swa(pa, pa_m, swa_bias, sinks[0].astype(F32))
    o_b, w_out_bf, w_gu_bf, w_down_bf = _fox(flags, qt, ka, vt, ka_m, vt_m,
                                             w_out[0].astype(F32), w_gate_up[0].astype(F32),
                                             w_down[0].astype(F32))

    return _ffn(o_a, o_b, x,
                w_out_bf,
                ln_post_mix[0].astype(F32).reshape(1, D_MODEL),
                ln_pre_ffn[0].astype(F32).reshape(1, D_MODEL),
                w_gu_bf,
                w_down_bf,
                ln_post_ffn[0].astype(F32).reshape(1, D_MODEL))
```

```python
import math

import numpy as np
import jax
import jax.numpy as jnp
from jax import lax
from jax.experimental import pallas as pl
from jax.experimental.pallas import tpu as pltpu

D_MODEL = 1024
N_META = 16
HEAD_DIM = 64
SWA_Q_HEADS = 8
SWA_KV_HEADS = 2
SWA_GROUP = SWA_Q_HEADS // SWA_KV_HEADS
FOX_HEADS = 8
SWA_Q_W = SWA_Q_HEADS * HEAD_DIM
SWA_KV_W = SWA_KV_HEADS * HEAD_DIM
FOX_W = FOX_HEADS * HEAD_DIM
OFF_QA = SWA_Q_W
OFF_KA = OFF_QA + SWA_KV_W
OFF_VA = OFF_KA + SWA_KV_W
OFF_QB = OFF_VA + FOX_W
OFF_KB = OFF_QB + FOX_W
OFF_VB = OFF_KB + FOX_W
WINDOW = 128
BLOCK = 128
N_BUCKETS = 32
MAX_DISTANCE = 128
D_FF = 2816
EPS = 1e-6
NEG_INF = -1e30
SCALE = HEAD_DIM ** -0.5
LOG2E = math.log2(math.e)

LANES = 128
PAIR_W = 2 * HEAD_DIM
VMEM_LIMIT = 56 * 1024 * 1024

SWA_KEYS = 2 * BLOCK + N_META
SWA_TQ = 2048
SWA_ONES = 16
IN_TM = 512
META_TM = 128
GATE_ROWS = 16
FOX_VROWS = HEAD_DIM + 16
FOX_T = 512
FOX_GROUP = 8
FOX_SAFE_LOG2 = 40.0
FOX_BOUND_MARGIN = 1.001
FFN_TM = 512
FFN_PARTS = 2
SLAB_ALIGN = 16
N_PIECES = 3
FOX_K = HEAD_DIM + 16
AUG_ROWS = 16

F32 = jnp.float32
BF16 = jnp.bfloat16


def _dot(a, b):
    return jnp.dot(a, b, preferred_element_type=F32)


def _rms(t, g):
    return t * lax.rsqrt(jnp.mean(t * t, axis=-1, keepdims=True) + EPS) * g


def _bf16_pieces(v):
    pieces = []
    for _ in range(N_PIECES - 1):
        p = v.astype(BF16)
        pieces.append(p)
        v = v - p.astype(F32)
    pieces.append(v.astype(BF16))
    return pieces


def _t5_bucket_np(dist):
    n = np.maximum(dist, 0).astype(np.int32)
    max_exact = N_BUCKETS // 2
    nf = np.maximum(n, 1).astype(np.float32)
    large = max_exact + (np.log(nf / np.float32(max_exact)) / np.float32(math.log(MAX_DISTANCE / max_exact))
                         * np.float32(N_BUCKETS - max_exact)).astype(np.int32)
    large = np.minimum(large, N_BUCKETS - 1)
    return np.where(n < max_exact, n, large).astype(np.int32)


def _bias_kernel(tab_ref, bkt_ref, valid_ref, out_ref):
    bkt = bkt_ref[...]
    valid = valid_ref[...] > 0
    for h in range(SWA_Q_HEADS):
        acc = jnp.zeros(bkt.shape, F32)
        for b in range(N_BUCKETS):
            acc = jnp.where(bkt == b, tab_ref[b, h], acc)
        out_ref[:, h * BLOCK:(h + 1) * BLOCK] = jnp.where(valid, acc * LOG2E, NEG_INF)


def _bias_tables(rel_bias):
    ki = np.arange(2 * BLOCK)[:, None]
    qi = np.arange(BLOCK)[None, :]
    d_w = qi + BLOCK - ki
    bkt_w = _t5_bucket_np(d_w)
    in_window = (d_w >= 0) & (d_w < WINDOW)
    mi = np.arange(N_META)[:, None]
    bkt_m0 = _t5_bucket_np(N_META + qi - mi)
    far = _t5_bucket_np(np.asarray([[N_META + BLOCK - (N_META - 1)]]))
    assert far[0, 0] == N_BUCKETS - 1, "meta keys of later blocks must share the last bucket"
    bkt = np.stack([np.concatenate([bkt_w, bkt_m0]), np.concatenate([bkt_w, np.full_like(bkt_m0, far[0, 0])])])
    meta_ok = np.ones((N_META, BLOCK), bool)
    valid = np.stack([np.concatenate([in_window & (ki >= BLOCK), meta_ok]), np.concatenate([in_window, meta_ok])])
    spec = pl.BlockSpec((None, SWA_KEYS, BLOCK), lambda a: (a, 0, 0))
    return pl.pallas_call(
        _bias_kernel,
        grid=(2,),
        in_specs=[pl.BlockSpec(memory_space=pltpu.SMEM), spec, spec],
        out_specs=pl.BlockSpec((None, SWA_KEYS, SWA_Q_HEADS * BLOCK), lambda a: (a, 0, 0)),
        out_shape=jax.ShapeDtypeStruct((2, SWA_KEYS, SWA_Q_HEADS * BLOCK), F32),
        name="bias_tables",
    )(rel_bias.astype(F32), jnp.asarray(bkt.astype(np.int32)), jnp.asarray(valid.astype(np.int32)))


def _fox_aug_rows(val_rows, key_side):
    n = val_rows[0].shape[1]
    row = lax.broadcasted_iota(jnp.int32, (AUG_ROWS, n), 0)
    ones = (row >= N_PIECES) & (row < 3 * N_PIECES) if key_side else (row < N_PIECES)
    aug = jnp.where(ones, 1.0, 0.0)
    first = 0 if key_side else N_PIECES
    for g, val in enumerate(val_rows):
        for i, piece in enumerate(_bf16_pieces(val)):
            aug = jnp.where(row == first + g * N_PIECES + i, piece.astype(F32), aug)
    return aug


def _fox_head_t(own_t, aug, pad_to=None):
    parts = [own_t, aug]
    if pad_to:
        parts.append(jnp.zeros((pad_to - FOX_K, own_t.shape[1]), F32))
    return jnp.concatenate(parts, axis=0)


def _inproj_kernel(x_ref, g_ref, w_ref, wf_ref, bft_ref, tri_ref, c0_ref, k0_ref,
                   pa_ref, qt_ref, ka_ref, vt_ref, kpre_ref, rmax_ref, cmeta_ref, carry_ref, kpre_scr):
    @pl.when(pl.program_id(1) == 0)
    def _():
        carry_ref[...] = c0_ref[...]
        kpre_scr[...] = k0_ref[...]

    tm = x_ref.shape[0]
    half = tm // 2
    y_top = _rms(x_ref[0:half, :], g_ref[...]).astype(BF16)
    fox_top = _dot(y_top, w_ref[:, OFF_VA:OFF_VB])
    y = jnp.concatenate([y_top, _rms(x_ref[half:tm, :], g_ref[...]).astype(BF16)], axis=0)
    tile_lanes = lambda a: jnp.concatenate([a] * (tm // LANES), axis=1)
    f_t = lax.dot_general(wf_ref[...], y, (((0,), (1,)), ((), ())), preferred_element_type=F32) + tile_lanes(bft_ref[...])
    acc_fox = jnp.concatenate([fox_top, _dot(y[half:tm], w_ref[:, OFF_VA:OFF_VB])], axis=0)
    ls_t = jnp.minimum(f_t, 0.0) - jnp.log1p(jnp.exp(-jnp.abs(f_t)))
    cum_t = tile_lanes(carry_ref[...])
    for piece in _bf16_pieces(ls_t):
        cum_t = cum_t + _dot(piece, tri_ref[...])
    carry_ref[...] = jnp.broadcast_to(cum_t[:, tm - 1:tm], carry_ref.shape)
    cmeta_ref[...] = jnp.broadcast_to(cum_t[:, N_META - 1:N_META], cmeta_ref.shape)
    cb_t = cum_t * (-LOG2E)

    acc_swa = _dot(y, w_ref[:, 0:OFF_VA])
    pa_ref[:, 0:OFF_QA] = (acc_swa[:, 0:OFF_QA] * (SCALE * LOG2E)).astype(BF16)
    pa_ref[:, OFF_QA:OFF_VA] = acc_swa[:, OFF_QA:OFF_VA].astype(BF16)

    ones = jnp.ones((FOX_VROWS - HEAD_DIM, tm), F32)
    for p in range(FOX_HEADS // 2):
        cols = lambda off: slice(off - OFF_VA + p * PAIR_W, off - OFF_VA + (p + 1) * PAIR_W)
        q_t = (acc_fox[:, cols(OFF_VA)] * (SCALE * LOG2E)).astype(BF16).astype(F32).T
        k_t = acc_fox[:, cols(OFF_QB)].astype(BF16).astype(F32).T
        v_t = acc_fox[:, cols(OFF_KB)].astype(BF16).astype(F32).T
        for e in range(2):
            h = 2 * p + e
            own = slice(e * HEAD_DIM, (e + 1) * HEAD_DIM)
            qn2 = jnp.sum(q_t[own] * q_t[own], axis=0, keepdims=True)
            kn2 = jnp.max(jnp.sum(k_t[own] * k_t[own], axis=0, keepdims=True), axis=1, keepdims=True)
            kpre = jnp.maximum(kpre_scr[h:h + 1, :], kn2)
            kpre_scr[h:h + 1, :] = kpre
            kpre_ref[h:h + 1, :] = kpre
            r = jnp.sqrt(qn2 * kpre[:, 0:1]) * FOX_BOUND_MARGIN
            rmax_ref[h:h + 1, :] = jnp.broadcast_to(jnp.max(r, axis=1, keepdims=True), (1, LANES))
            ka_ref[h] = _fox_head_t(k_t[own], _fox_aug_rows([cb_t[h:h + 1]], True), PAIR_W).T.astype(BF16)
            qt_ref[h] = _fox_head_t(q_t[own], _fox_aug_rows([-cb_t[h:h + 1], -r], False)).astype(BF16)
            vt_ref[h] = jnp.concatenate([v_t[own], ones], axis=0).astype(BF16)


def _inproj(x3, g, w, wft, bft, c0, k0, tm):
    nb, rows, _ = x3.shape
    nt = rows // tm
    tri = jnp.asarray(np.triu(np.ones((tm, tm), np.float32)), BF16)
    const = lambda b, t: (0, 0)
    return pl.pallas_call(
        _inproj_kernel,
        grid=(nb, nt),
        in_specs=[
            pl.BlockSpec((None, tm, D_MODEL), lambda b, t: (b, t, 0)),
            pl.BlockSpec((1, D_MODEL), const),
            pl.BlockSpec(w.shape, const),
            pl.BlockSpec((D_MODEL, GATE_ROWS), const),
            pl.BlockSpec((GATE_ROWS, LANES), const),
            pl.BlockSpec((tm, tm), const),
            pl.BlockSpec((GATE_ROWS, LANES), const),
            pl.BlockSpec((FOX_HEADS, LANES), const),
        ],
        out_specs=[
            pl.BlockSpec((None, tm, OFF_VA), lambda b, t: (b, t, 0)),
            pl.BlockSpec((None, None, FOX_HEADS, FOX_K, tm), lambda b, t: (b, t, 0, 0, 0)),
            pl.BlockSpec((None, FOX_HEADS, tm, PAIR_W), lambda b, t: (b, 0, t, 0)),
            pl.BlockSpec((None, None, FOX_HEADS, FOX_VROWS, tm), lambda b, t: (b, t, 0, 0, 0)),
            pl.BlockSpec((None, None, FOX_HEADS, LANES), lambda b, t: (b, t, 0, 0)),
            pl.BlockSpec((None, None, FOX_HEADS, LANES), lambda b, t: (b, t, 0, 0)),
            pl.BlockSpec((None, None, GATE_ROWS, LANES), lambda b, t: (b, t, 0, 0)),
        ],
        out_shape=[
            jax.ShapeDtypeStruct((nb, rows, OFF_VA), BF16),
            jax.ShapeDtypeStruct((nb, nt, FOX_HEADS, FOX_K, tm), BF16),
            jax.ShapeDtypeStruct((nb, FOX_HEADS, rows, PAIR_W), BF16),
            jax.ShapeDtypeStruct((nb, nt, FOX_HEADS, FOX_VROWS, tm), BF16),
            jax.ShapeDtypeStruct((nb, nt, FOX_HEADS, LANES), F32),
            jax.ShapeDtypeStruct((nb, nt, FOX_HEADS, LANES), F32),
            jax.ShapeDtypeStruct((nb, nt, GATE_ROWS, LANES), F32),
        ],
        scratch_shapes=[pltpu.VMEM((GATE_ROWS, LANES), F32), pltpu.VMEM((FOX_HEADS, LANES), F32)],
        compiler_params=pltpu.CompilerParams(
            dimension_semantics=("arbitrary", "arbitrary"), vmem_limit_bytes=VMEM_LIMIT),
        name="inproj",
    )(x3, g, w, wft, bft, tri, c0, k0)


def _swa_scores(q, k_all, bias):
    nq = q.shape[0]
    zeros = jnp.zeros((HEAD_DIM, nq), F32)
    blocks = []
    for pair in range(SWA_Q_HEADS // 2):
        q_t = q[:, pair * PAIR_W:(pair + 1) * PAIR_W].astype(F32).T
        for hp in range(2):
            qh = q_t[hp * HEAD_DIM:(hp + 1) * HEAD_DIM]
            g = (2 * pair + hp) // SWA_GROUP
            blocks.append(jnp.concatenate([qh, zeros] if g == 0 else [zeros, qh], axis=0))
    q_bd = jnp.concatenate(blocks, axis=1).astype(BF16)
    return _dot(k_all, q_bd) + bias


def _swa_outputs(s, v_all, sink_ref):
    nq = s.shape[1] // SWA_Q_HEADS
    ps, sink_terms = [], []
    for h in range(SWA_Q_HEADS):
        sh = s[:, h * nq:(h + 1) * nq]
        sink = sink_ref[h] * LOG2E
        m = jnp.maximum(jnp.max(sh, axis=0, keepdims=True), sink)
        ps.append(jnp.exp2(sh - m).astype(BF16))
        sink_terms.append(jnp.exp2(sink - m))
    n = BLOCK
    v_t = jnp.concatenate([v_all[0:n].astype(F32).T, v_all[n:2 * n].astype(F32).T,
                           v_all[2 * n:].astype(F32).T], axis=1)
    ones = jnp.ones((SWA_ONES, v_t.shape[1]), F32)
    outs = []
    for g in range(SWA_KV_HEADS):
        v_aug = jnp.concatenate([v_t[g * HEAD_DIM:(g + 1) * HEAD_DIM], ones], axis=0).astype(BF16)
        p_g = jnp.concatenate(ps[g * SWA_GROUP:(g + 1) * SWA_GROUP], axis=1)
        o_g = _dot(v_aug, p_g)
        for j in range(SWA_GROUP):
            cols = slice(j * nq, (j + 1) * nq)
            l = o_g[HEAD_DIM:HEAD_DIM + 1, cols] + sink_terms[g * SWA_GROUP + j]
            outs.append(o_g[0:HEAD_DIM, cols] / l)
    pairs = [jnp.concatenate(outs[2 * p:2 * p + 2], axis=0).T for p in range(SWA_Q_HEADS // 2)]
    return jnp.concatenate(pairs, axis=1).astype(BF16)


def _swa_kernel(sink_ref, q_ref, kc_ref, kp_ref, vc_ref, vp_ref, km_ref, vm_ref, bias0_ref, bias1_ref, o_ref):
    n = BLOCK
    nsub = q_ref.shape[0] // n
    k_rows = [kp_ref[...]] + [kc_ref[c * n:(c + 1) * n] for c in range(nsub)]
    v_rows = [vp_ref[...]] + [vc_ref[c * n:(c + 1) * n] for c in range(nsub)]
    def block_scores(c):
        k_all = jnp.concatenate([k_rows[c], k_rows[c + 1], km_ref[...]], axis=0)
        bias = bias0_ref[...] if c == 0 else bias1_ref[...]
        return _swa_scores(q_ref[c * n:(c + 1) * n], k_all, bias)

    s_next = block_scores(0)
    for c in range(nsub):
        s_cur = s_next
        if c + 1 < nsub:
            s_next = block_scores(c + 1)
        v_all = jnp.concatenate([v_rows[c], v_rows[c + 1], vm_ref[...]], axis=0)
        o_ref[c * n:(c + 1) * n] = _swa_outputs(s_cur, v_all, sink_ref)


def _swa(proj, projm, bias, sinks):
    nb, rows, _ = proj.shape
    tq = SWA_TQ
    ratio = tq // BLOCK
    kcol, vcol = OFF_QA // SWA_KV_W, OFF_KA // SWA_KV_W
    cur = lambda c: (lambda b, n: (b, n, c))
    prev = lambda c: (lambda b, n: (b, jnp.maximum(n * ratio - 1, 0), c))
    bias_spec = lambda idx: pl.BlockSpec((None, SWA_KEYS, SWA_Q_HEADS * BLOCK), idx)
    return pl.pallas_call(
        _swa_kernel,
        grid=(nb, rows // tq),
        in_specs=[
            pl.BlockSpec(memory_space=pltpu.SMEM),
            pl.BlockSpec((None, tq, SWA_Q_W), lambda b, n: (b, n, 0)),
            pl.BlockSpec((None, tq, SWA_KV_W), cur(kcol)),
            pl.BlockSpec((None, BLOCK, SWA_KV_W), prev(kcol)),
            pl.BlockSpec((None, tq, SWA_KV_W), cur(vcol)),
            pl.BlockSpec((None, BLOCK, SWA_KV_W), prev(vcol)),
            pl.BlockSpec((None, N_META, SWA_KV_W), lambda b, n: (0, 0, kcol)),
            pl.BlockSpec((None, N_META, SWA_KV_W), lambda b, n: (0, 0, vcol)),
            bias_spec(lambda b, n: (jnp.minimum(n, 1), 0, 0)),
            bias_spec(lambda b, n: (1, 0, 0)),
        ],
        out_specs=pl.BlockSpec((None, tq, SWA_Q_W), lambda b, n: (b, n, 0)),
        out_shape=jax.ShapeDtypeStruct((nb, rows, SWA_Q_W), BF16),
        compiler_params=pltpu.CompilerParams(
            dimension_semantics=("arbitrary", "arbitrary"), vmem_limit_bytes=VMEM_LIMIT),
        name="swa",
    )(sinks, proj, proj, proj, proj, proj, projm, projm, bias, bias)


def _fox_kernel(flags_ref, qt_ref, ka_ref, vt_ref, kam_ref, vtm_ref,
                wo_ref, wgu_ref, wd_ref, acc_ref, wo_bf_ref, wgu_bf_ref, wd_bf_ref):
    wo_bf_ref[...] = wo_ref[...].astype(BF16)
    wgu_bf_ref[...] = wgu_ref[...].astype(BF16)
    wd_bf_ref[...] = wd_ref[...].astype(BF16)

    b, g, i = pl.program_id(0), pl.program_id(1), pl.program_id(2)
    t = acc_ref.shape[2]
    nh = acc_ref.shape[0]
    colmax = lambda x: jnp.max(x, axis=0, keepdims=True)
    tile_rows = lambda j: pl.ds(pl.multiple_of(j * t, t), t)

    safe = flags_ref[(b * pl.num_programs(2) + i) * pl.num_programs(1) + g] > 0
    qt = [qt_ref[h] for h in range(nh)]

    def scores(h, j, diagonal):
        s = _dot(ka_ref[h, tile_rows(j), 0:FOX_K], qt[h])
        if diagonal:
            key = lax.broadcasted_iota(jnp.int32, (t, t), 0)
            qry = lax.broadcasted_iota(jnp.int32, (t, t), 1)
            s = jnp.where(key <= qry, s, NEG_INF)
        return s

    @pl.when(safe)
    def _():
        def run(work):
            prev = None
            for score_fn, pv_fn in work:
                s = score_fn()
                if prev is not None:
                    prev[0](prev[1])
                prev = (pv_fn, s)
            prev[0](prev[1])

        def tile_work(j):
            def pv(h):
                def apply(s):
                    acc_ref[h] += _dot(vt_ref[j, h], jnp.exp2(s).astype(BF16))
                return apply
            return [(lambda h=h: scores(h, j, False), pv(h)) for h in range(nh)]

        def first_work():
            def sc(h):
                return jnp.concatenate([scores(h, i, True), _dot(kam_ref[h, :, 0:FOX_K], qt[h])], axis=0)
            def pv(h):
                def apply(s):
                    v_t = jnp.concatenate([vt_ref[i, h], vtm_ref[h, :, 0:N_META]], axis=1)
                    acc_ref[h] = _dot(v_t, jnp.exp2(s).astype(BF16))
                return apply
            return [(lambda h=h: sc(h), pv(h)) for h in range(nh)]

        run(first_work())

        def tile_pair(p, carry):
            run(tile_work(2 * p) + tile_work(2 * p + 1))
            return carry

        lax.fori_loop(0, i // 2, tile_pair, 0)

        @pl.when(i % 2 == 1)
        def _():
            run(tile_work(i - 1))

    @pl.when(jnp.logical_not(safe))
    def _():
        m0 = []
        for h in range(nh):
            s = _dot(kam_ref[h, :, 0:FOX_K], qt[h])
            m = colmax(s)
            acc_ref[h] = _dot(vtm_ref[h, :, 0:N_META], jnp.exp2(s - m).astype(BF16))
            m0.append(m)

        def tile(j, ms, diagonal):
            out = []
            ss = [scores(h, j, diagonal) for h in range(nh)]
            for h in range(nh):
                s = ss[h]
                m_next = jnp.maximum(ms[h], colmax(s))
                p = jnp.exp2(s - m_next).astype(BF16)
                acc_ref[h] = jnp.exp2(ms[h] - m_next) * acc_ref[h] + _dot(vt_ref[j, h], p)
                out.append(m_next)
            return tuple(out)

        ms = lax.fori_loop(0, i, lambda j, ms: tile(j, ms, False), tuple(m0))
        tile(i, ms, True)


def _fox(flags, qt, ka, vt, ka_m, vt_m, w_out, w_gate_up, w_down):
    nb, _, rows, _ = ka.shape
    t = FOX_T
    nh = FOX_GROUP
    ngroup = FOX_HEADS // nh
    nq = rows // t
    nsteps = nb * ngroup * nq
    slab = lambda w: min(r for r in range(SLAB_ALIGN, w.shape[0] + 1, SLAB_ALIGN)
                         if w.shape[0] % r == 0 and r * nsteps >= w.shape[0])
    step = lambda b, g, i: (b * ngroup + g) * nq + i
    w_spec = lambda w: pl.BlockSpec((slab(w), w.shape[1]),
                                    lambda b, g, i, f: (jnp.minimum(step(b, g, i), w.shape[0] // slab(w) - 1), 0))
    return pl.pallas_call(
        _fox_kernel,
        grid_spec=pltpu.PrefetchScalarGridSpec(
            num_scalar_prefetch=1,
            grid=(nb, ngroup, nq),
            in_specs=[
                pl.BlockSpec((None, None, nh, FOX_K, t), lambda b, g, i, f: (b, i, g, 0, 0)),
                pl.BlockSpec((None, nh, rows, PAIR_W), lambda b, g, i, f: (b, g, 0, 0)),
                pl.BlockSpec((None, nq, nh, FOX_VROWS, t), lambda b, g, i, f: (b, 0, g, 0, 0)),
                pl.BlockSpec((None, nh, N_META, PAIR_W), lambda b, g, i, f: (0, g, 0, 0)),
                pl.BlockSpec((None, None, nh, FOX_VROWS, META_TM), lambda b, g, i, f: (0, 0, g, 0, 0)),
                w_spec(w_out), w_spec(w_gate_up), w_spec(w_down),
            ],
            out_specs=[pl.BlockSpec((None, nh, FOX_VROWS, t), lambda b, g, i, f: (b, g, 0, i)),
                       w_spec(w_out), w_spec(w_gate_up), w_spec(w_down)],
        ),
        out_shape=[jax.ShapeDtypeStruct((nb, FOX_HEADS, FOX_VROWS, rows), F32)]
        + [jax.ShapeDtypeStruct(w.shape, BF16) for w in (w_out, w_gate_up, w_down)],
        compiler_params=pltpu.CompilerParams(
            dimension_semantics=("arbitrary", "arbitrary", "arbitrary"), vmem_limit_bytes=VMEM_LIMIT),
        name="fox",
    )(flags, qt, ka, vt, ka_m, vt_m, w_out, w_gate_up, w_down)


def _ffn_kernel(oa_ref, fox_ref, x_ref, wo_hbm, g1_ref, g2_ref, wgu_hbm, wd_hbm, g3_ref, out_ref,
                wo_ref, wg_ref, wu_ref, wd_ref, sems):
    copies = [pltpu.make_async_copy(wo_hbm, wo_ref, sems.at[0]),
              pltpu.make_async_copy(wgu_hbm.at[:, 0:D_FF], wg_ref, sems.at[1]),
              pltpu.make_async_copy(wgu_hbm.at[:, D_FF:2 * D_FF], wu_ref, sems.at[2]),
              pltpu.make_async_copy(wd_hbm, wd_ref, sems.at[3])]
    first = jnp.logical_and(pl.program_id(0) == 0, pl.program_id(1) == 0)

    @pl.when(first)
    def _():
        for c in copies:
            c.start()
        _ffn_tile(oa_ref, fox_ref, x_ref, wo_ref, g1_ref, g2_ref, wg_ref, wu_ref, wd_ref, g3_ref, out_ref,
                  lambda *ks: [copies[k].wait() for k in ks])

    @pl.when(jnp.logical_not(first))
    def _():
        _ffn_tile(oa_ref, fox_ref, x_ref, wo_ref, g1_ref, g2_ref, wg_ref, wu_ref, wd_ref, g3_ref, out_ref,
                  lambda *ks: None)


def _ffn_tile(oa_ref, fox_ref, x_ref, wo_ref, g1_ref, g2_ref, wg_ref, wu_ref, wd_ref, g3_ref, out_ref, wait):
    tm = x_ref.shape[0]
    parts = [slice(c * tm // FFN_PARTS, (c + 1) * tm // FFN_PARTS) for c in range(FFN_PARTS)]
    def fox_rows(r):
        pairs = []
        for p in range(FOX_HEADS // 2):
            o_t = jnp.concatenate([fox_ref[h, 0:HEAD_DIM, r] / fox_ref[h, HEAD_DIM:HEAD_DIM + 1, r]
                                   for h in (2 * p, 2 * p + 1)], axis=0)
            pairs.append(o_t.T.astype(BF16))
        return jnp.concatenate(pairs, axis=1)

    mix = [jnp.concatenate([oa_ref[r, :], fox_rows(r)], axis=1) for r in parts]
    wait(0)
    a = [_dot(m, wo_ref[...]) for m in mix]
    h1 = [x_ref[r, :] + _rms(ai, g1_ref[...]) for r, ai in zip(parts, a)]
    hn = [_rms(h, g2_ref[...]).astype(BF16) for h in h1]
    wait(1, 2)
    gate_up = [(_dot(h, wg_ref[...]), _dot(h, wu_ref[...])) for h in hn]
    act = [(g / (1.0 + jnp.exp(-g)) * u).astype(BF16) for g, u in gate_up]
    wait(3)
    ff = [_dot(ac, wd_ref[...]) for ac in act]
    for r, h, f in zip(parts, h1, ff):
        out_ref[r, :] = h + _rms(f, g3_ref[...])


def _ffn(o_a, o_b, x3, wo, g1, g2, wgu, wd, g3):
    nb, rows, _ = x3.shape
    tm = FFN_TM
    const = lambda b, t: (0, 0)
    resident = lambda shape: pl.BlockSpec(shape, const, pipeline_mode=pl.Buffered(1))
    row = lambda w: pl.BlockSpec((None, tm, w), lambda b, t: (b, t, 0))
    in_hbm = pl.BlockSpec(memory_space=pl.ANY)
    return pl.pallas_call(
        _ffn_kernel,
        grid=(nb, rows // tm),
        in_specs=[
            row(SWA_Q_W), pl.BlockSpec((None, FOX_HEADS, FOX_VROWS, tm), lambda b, t: (b, 0, 0, t)), row(D_MODEL),
            in_hbm, resident((1, D_MODEL)), resident((1, D_MODEL)),
            in_hbm, in_hbm,
            resident((1, D_MODEL)),
        ],
        out_specs=row(D_MODEL),
        out_shape=jax.ShapeDtypeStruct((nb, rows, D_MODEL), F32),
        scratch_shapes=[pltpu.VMEM((D_MODEL, D_MODEL), BF16), pltpu.VMEM((D_MODEL, D_FF), BF16),
                        pltpu.VMEM((D_MODEL, D_FF), BF16), pltpu.VMEM((D_FF, D_MODEL), BF16),
                        pltpu.SemaphoreType.DMA((4,))],
        compiler_params=pltpu.CompilerParams(
            dimension_semantics=("arbitrary", "arbitrary"), vmem_limit_bytes=VMEM_LIMIT),
        name="outproj_ffn",
    )(o_a, o_b, x3, wo, g1, g2, wgu, wd, g3)


def kernel(x, meta_tokens, rel_bias, ln_pre_mix, ln_post_mix, ln_pre_ffn, ln_post_ffn,
           w_in, b_forget, sinks, w_out, w_gate_up, w_down):
    nb, seq, d = x.shape
    assert IN_TM == FOX_T, "the in-proj tile's running key-norm maximum is the fox query tile's bound"
    assert d == D_MODEL and seq % IN_TM == 0 and seq % FOX_T == 0 and seq % FFN_TM == 0 and seq % SWA_TQ == 0
    assert w_in.shape[0] == 1, "single-layer block"
    assert meta_tokens.shape == (N_META, D_MODEL)
    x = x.astype(F32)

    w_all = w_in[0].astype(BF16)
    pad_gate = GATE_ROWS - FOX_HEADS
    w_ft = jnp.pad(w_in[0, :, OFF_VB:], ((0, 0), (0, pad_gate))).astype(BF16)
    lane_rep = lambda v: jnp.broadcast_to(jnp.pad(v.astype(F32), (0, pad_gate))[:, None], (GATE_ROWS, LANES))
    b_ft = lane_rep(b_forget[0])
    g_pre = ln_pre_mix[0].astype(F32).reshape(1, D_MODEL)

    swa_bias = _bias_tables(rel_bias)

    x_m = jnp.pad(meta_tokens.astype(F32), ((0, META_TM - N_META), (0, 0)))[None]
    pa_m, _, ka_m, vt_m, kpre_m, _, c_meta = _inproj(x_m, g_pre, w_all, w_ft, b_ft, jnp.zeros((GATE_ROWS, LANES), F32),
                                                     jnp.zeros((FOX_HEADS, LANES), F32), META_TM)
    pa, qt, ka, vt, _, rmax, _ = _inproj(x, g_pre, w_all, w_ft, b_ft, c_meta[0, 0], kpre_m[0, 0], IN_TM)
    shift = rmax[..., 0].reshape(nb, seq // IN_TM, FOX_HEADS // FOX_GROUP, FOX_GROUP)
    flags = (jnp.max(shift, axis=-1) <= FOX_SAFE_LOG2).astype(jnp.int32).reshape(-1)

    o_a = _swa(pa, pa_m, swa_bias, sinks[0].astype(F32))
    o_b, w_out_bf, w_gu_bf, w_down_bf = _fox(flags, qt, ka, vt, ka_m, vt_m,
                                             w_out[0].astype(F32), w_gate_up[0].astype(F32),
                                             w_down[0].astype(F32))

    return _ffn(o_a, o_b, x,
                w_out_bf,
                ln_post_mix[0].astype(F32).reshape(1, D_MODEL),
                ln_pre_ffn[0].astype(F32).reshape(1, D_MODEL),
                w_gu_bf,
                w_down_bf,
                ln_post_ffn[0].astype(F32).reshape(1, D_MODEL))
```

```python
import math

import numpy as np
import jax
import jax.numpy as jnp
from jax import lax
from jax.experimental import pallas as pl
from jax.experimental.pallas import tpu as pltpu

D_MODEL = 1024
N_META = 16
HEAD_DIM = 64
SWA_Q_HEADS = 8
SWA_KV_HEADS = 2
SWA_GROUP = SWA_Q_HEADS // SWA_KV_HEADS
FOX_HEADS = 8
SWA_Q_W = SWA_Q_HEADS * HEAD_DIM
SWA_KV_W = SWA_KV_HEADS * HEAD_DIM
FOX_W = FOX_HEADS * HEAD_DIM
OFF_QA = SWA_Q_W
OFF_KA = OFF_QA + SWA_KV_W
OFF_VA = OFF_KA + SWA_KV_W
OFF_QB = OFF_VA + FOX_W
OFF_KB = OFF_QB + FOX_W
OFF_VB = OFF_KB + FOX_W
WINDOW = 128
BLOCK = 128
N_BUCKETS = 32
MAX_DISTANCE = 128
D_FF = 2816
EPS = 1e-6
NEG_INF = -1e30
SCALE = HEAD_DIM ** -0.5
LOG2E = math.log2(math.e)

LANES = 128
PAIR_W = 2 * HEAD_DIM
VMEM_LIMIT = 56 * 1024 * 1024

SWA_KEYS = 2 * BLOCK + N_META
SWA_TQ = 2048
SWA_ONES = 16
IN_TM = 512
META_TM = 128
GATE_ROWS = 16
FOX_VROWS = HEAD_DIM + 16
FOX_T = 512
FOX_GROUP = 8
FOX_SAFE_LOG2 = 40.0
FOX_BOUND_MARGIN = 1.001
FFN_TM = 512
FFN_PARTS = 2
SLAB_ALIGN = 16
N_PIECES = 3
FOX_K = HEAD_DIM + 16
AUG_ROWS = 16

F32 = jnp.float32
BF16 = jnp.bfloat16


def _dot(a, b):
    return jnp.dot(a, b, preferred_element_type=F32)


def _rms(t, g):
    return t * lax.rsqrt(jnp.mean(t * t, axis=-1, keepdims=True) + EPS) * g


def _bf16_pieces(v):
    pieces = []
    for _ in range(N_PIECES - 1):
        p = v.astype(BF16)
        pieces.append(p)
        v = v - p.astype(F32)
    pieces.append(v.astype(BF16))
    return pieces


def _t5_bucket_np(dist):
    n = np.maximum(dist, 0).astype(np.int32)
    max_exact = N_BUCKETS // 2
    nf = np.maximum(n, 1).astype(np.float32)
    large = max_exact + (np.log(nf / np.float32(max_exact)) / np.float32(math.log(MAX_DISTANCE / max_exact))
                         * np.float32(N_BUCKETS - max_exact)).astype(np.int32)
    large = np.minimum(large, N_BUCKETS - 1)
    return np.where(n < max_exact, n, large).astype(np.int32)


def _bias_kernel(tab_ref, bkt_ref, valid_ref, out_ref):
    bkt = bkt_ref[...]
    valid = valid_ref[...] > 0
    for h in range(SWA_Q_HEADS):
        acc = jnp.zeros(bkt.shape, F32)
        for b in range(N_BUCKETS):
            acc = jnp.where(bkt == b, tab_ref[b, h], acc)
        out_ref[:, h * BLOCK:(h + 1) * BLOCK] = jnp.where(valid, acc * LOG2E, NEG_INF)


def _bias_tables(rel_bias):
    ki = np.arange(2 * BLOCK)[:, None]
    qi = np.arange(BLOCK)[None, :]
    d_w = qi + BLOCK - ki
    bkt_w = _t5_bucket_np(d_w)
    in_window = (d_w >= 0) & (d_w < WINDOW)
    mi = np.arange(N_META)[:, None]
    bkt_m0 = _t5_bucket_np(N_META + qi - mi)
    far = _t5_bucket_np(np.asarray([[N_META + BLOCK - (N_META - 1)]]))
    assert far[0, 0] == N_BUCKETS - 1, "meta keys of later blocks must share the last bucket"
    bkt = np.stack([np.concatenate([bkt_w, bkt_m0]), np.concatenate([bkt_w, np.full_like(bkt_m0, far[0, 0])])])
    meta_ok = np.ones((N_META, BLOCK), bool)
    valid = np.stack([np.concatenate([in_window & (ki >= BLOCK), meta_ok]), np.concatenate([in_window, meta_ok])])
    spec = pl.BlockSpec((None, SWA_KEYS, BLOCK), lambda a: (a, 0, 0))
    return pl.pallas_call(
        _bias_kernel,
        grid=(2,),
        in_specs=[pl.BlockSpec(memory_space=pltpu.SMEM), spec, spec],
        out_specs=pl.BlockSpec((None, SWA_KEYS, SWA_Q_HEADS * BLOCK), lambda a: (a, 0, 0)),
        out_shape=jax.ShapeDtypeStruct((2, SWA_KEYS, SWA_Q_HEADS * BLOCK), F32),
        name="bias_tables",
    )(rel_bias.astype(F32), jnp.asarray(bkt.astype(np.int32)), jnp.asarray(valid.astype(np.int32)))


def _fox_aug_rows(val_rows, key_side):
    n = val_rows[0].shape[1]
    row = lax.broadcasted_iota(jnp.int32, (AUG_ROWS, n), 0)
    ones = (row >= N_PIECES) & (row < 3 * N_PIECES) if key_side else (row < N_PIECES)
    aug = jnp.where(ones, 1.0, 0.0)
    first = 0 if key_side else N_PIECES
    for g, val in enumerate(val_rows):
        for i, piece in enumerate(_bf16_pieces(val)):
            aug = jnp.where(row == first + g * N_PIECES + i, piece.astype(F32), aug)
    return aug


def _fox_head_t(own_t, aug, pad_to=None):
    parts = [own_t, aug]
    if pad_to:
        parts.append(jnp.zeros((pad_to - FOX_K, own_t.shape[1]), F32))
    return jnp.concatenate(parts, axis=0)


def _inproj_kernel(x_ref, g_ref, w_ref, wf_ref, bft_ref, tri_ref, c0_ref, k0_ref,
                   pa_ref, qt_ref, ka_ref, vt_ref, stats_ref, carry_ref, kpre_scr):
    kpre_ref = stats_ref.at[0:FOX_HEADS]
    rmax_ref = stats_ref.at[FOX_HEADS:2 * FOX_HEADS]
    cmeta_ref = stats_ref.at[2 * FOX_HEADS:2 * FOX_HEADS + GATE_ROWS]

    @pl.when(pl.program_id(1) == 0)
    def _():
        carry_ref[...] = c0_ref[...]
        kpre_scr[...] = k0_ref[...]

    tm = x_ref.shape[0]
    half = tm // 2
    y_top = _rms(x_ref[0:half, :], g_ref[...]).astype(BF16)
    fox_top = _dot(y_top, w_ref[:, OFF_VA:OFF_VB])
    y = jnp.concatenate([y_top, _rms(x_ref[half:tm, :], g_ref[...]).astype(BF16)], axis=0)
    tile_lanes = lambda a: jnp.concatenate([a] * (tm // LANES), axis=1)
    f_t = lax.dot_general(wf_ref[...], y, (((0,), (1,)), ((), ())), preferred_element_type=F32) + tile_lanes(bft_ref[...])
    acc_fox = jnp.concatenate([fox_top, _dot(y[half:tm], w_ref[:, OFF_VA:OFF_VB])], axis=0)
    ls_t = jnp.minimum(f_t, 0.0) - jnp.log1p(jnp.exp(-jnp.abs(f_t)))
    cum_t = tile_lanes(carry_ref[...])
    for piece in _bf16_pieces(ls_t):
        cum_t = cum_t + _dot(piece, tri_ref[...])
    carry_ref[...] = jnp.broadcast_to(cum_t[:, tm - 1:tm], carry_ref.shape)
    cmeta_ref[...] = jnp.broadcast_to(cum_t[:, N_META - 1:N_META], cmeta_ref.shape)
    cb_t = cum_t * (-LOG2E)

    acc_swa = _dot(y, w_ref[:, 0:OFF_VA])
    pa_ref[:, 0:OFF_QA] = (acc_swa[:, 0:OFF_QA] * (SCALE * LOG2E)).astype(BF16)
    pa_ref[:, OFF_QA:OFF_VA] = acc_swa[:, OFF_QA:OFF_VA].astype(BF16)

    ones = jnp.ones((FOX_VROWS - HEAD_DIM, tm), F32)
    for p in range(FOX_HEADS // 2):
        cols = lambda off: slice(off - OFF_VA + p * PAIR_W, off - OFF_VA + (p + 1) * PAIR_W)
        q_t = (acc_fox[:, cols(OFF_VA)] * (SCALE * LOG2E)).astype(BF16).astype(F32).T
        k_t = acc_fox[:, cols(OFF_QB)].astype(BF16).astype(F32).T
        v_t = acc_fox[:, cols(OFF_KB)].astype(BF16).astype(F32).T
        for e in range(2):
            h = 2 * p + e
            own = slice(e * HEAD_DIM, (e + 1) * HEAD_DIM)
            qn2 = jnp.sum(q_t[own] * q_t[own], axis=0, keepdims=True)
            kn2 = jnp.max(jnp.sum(k_t[own] * k_t[own], axis=0, keepdims=True), axis=1, keepdims=True)
            kpre = jnp.maximum(kpre_scr[h:h + 1, :], kn2)
            kpre_scr[h:h + 1, :] = kpre
            kpre_ref[h:h + 1, :] = kpre
            r = jnp.sqrt(qn2 * kpre[:, 0:1]) * FOX_BOUND_MARGIN
            rmax_ref[h:h + 1, :] = jnp.broadcast_to(jnp.max(r, axis=1, keepdims=True), (1, LANES))
            ka_ref[h] = _fox_head_t(k_t[own], _fox_aug_rows([cb_t[h:h + 1]], True), PAIR_W).T.astype(BF16)
            qt_ref[h] = _fox_head_t(q_t[own], _fox_aug_rows([-cb_t[h:h + 1], -r], False)).astype(BF16)
            vt_ref[h] = jnp.concatenate([v_t[own], ones], axis=0).astype(BF16)


def _inproj(x3, g, w, wft, bft, c0, k0, tm):
    nb, rows, _ = x3.shape
    nt = rows // tm
    tri = jnp.asarray(np.triu(np.ones((tm, tm), np.float32)), BF16)
    const = lambda b, t: (0, 0)
    pa, qt, ka, vt, stats = pl.pallas_call(
        _inproj_kernel,
        grid=(nb, nt),
        in_specs=[
            pl.BlockSpec((None, tm, D_MODEL), lambda b, t: (b, t, 0)),
            pl.BlockSpec((1, D_MODEL), const),
            pl.BlockSpec(w.shape, const),
            pl.BlockSpec((D_MODEL, GATE_ROWS), const),
            pl.BlockSpec((GATE_ROWS, LANES), const),
            pl.BlockSpec((tm, tm), const),
            pl.BlockSpec((GATE_ROWS, LANES), const),
            pl.BlockSpec((FOX_HEADS, LANES), const),
        ],
        out_specs=[
            pl.BlockSpec((None, tm, OFF_VA), lambda b, t: (b, t, 0)),
            pl.BlockSpec((None, None, FOX_HEADS, FOX_K, tm), lambda b, t: (b, t, 0, 0, 0)),
            pl.BlockSpec((None, FOX_HEADS, tm, PAIR_W), lambda b, t: (b, 0, t, 0)),
            pl.BlockSpec((None, None, FOX_HEADS, FOX_VROWS, tm), lambda b, t: (b, t, 0, 0, 0)),
            pl.BlockSpec((None, None, 2 * FOX_HEADS + GATE_ROWS, LANES), lambda b, t: (b, t, 0, 0)),
        ],
        out_shape=[
            jax.ShapeDtypeStruct((nb, rows, OFF_VA), BF16),
            jax.ShapeDtypeStruct((nb, nt, FOX_HEADS, FOX_K, tm), BF16),
            jax.ShapeDtypeStruct((nb, FOX_HEADS, rows, PAIR_W), BF16),
            jax.ShapeDtypeStruct((nb, nt, FOX_HEADS, FOX_VROWS, tm), BF16),
            jax.ShapeDtypeStruct((nb, nt, 2 * FOX_HEADS + GATE_ROWS, LANES), F32),
        ],
        scratch_shapes=[pltpu.VMEM((GATE_ROWS, LANES), F32), pltpu.VMEM((FOX_HEADS, LANES), F32)],
        compiler_params=pltpu.CompilerParams(
            dimension_semantics=("arbitrary", "arbitrary"), vmem_limit_bytes=VMEM_LIMIT),
        name="inproj",
    )(x3, g, w, wft, bft, tri, c0, k0)
    return (pa, qt, ka, vt, stats[:, :, 0:FOX_HEADS], stats[:, :, FOX_HEADS:2 * FOX_HEADS],
            stats[:, :, 2 * FOX_HEADS:])


def _swa_scores(q, k_all, bias):
    nq = q.shape[0]
    zeros = jnp.zeros((HEAD_DIM, nq), F32)
    blocks = []
    for pair in range(SWA_Q_HEADS // 2):
        q_t = q[:, pair * PAIR_W:(pair + 1) * PAIR_W].astype(F32).T
        for hp in range(2):
            qh = q_t[hp * HEAD_DIM:(hp + 1) * HEAD_DIM]
            g = (2 * pair + hp) // SWA_GROUP
            blocks.append(jnp.concatenate([qh, zeros] if g == 0 else [zeros, qh], axis=0))
    q_bd = jnp.concatenate(blocks, axis=1).astype(BF16)
    return _dot(k_all, q_bd) + bias


def _swa_outputs(s, v_all, sink_ref):
    nq = s.shape[1] // SWA_Q_HEADS
    ps, sink_terms = [], []
    for h in range(SWA_Q_HEADS):
        sh = s[:, h * nq:(h + 1) * nq]
        sink = sink_ref[h] * LOG2E
        m = jnp.maximum(jnp.max(sh, axis=0, keepdims=True), sink)
        ps.append(jnp.exp2(sh - m).astype(BF16))
        sink_terms.append(jnp.exp2(sink - m))
    n = BLOCK
    v_t = jnp.concatenate([v_all[0:n].astype(F32).T, v_all[n:2 * n].astype(F32).T,
                           v_all[2 * n:].astype(F32).T], axis=1)
    ones = jnp.ones((SWA_ONES, v_t.shape[1]), F32)
    outs = []
    for g in range(SWA_KV_HEADS):
        v_aug = jnp.concatenate([v_t[g * HEAD_DIM:(g + 1) * HEAD_DIM], ones], axis=0).astype(BF16)
        p_g = jnp.concatenate(ps[g * SWA_GROUP:(g + 1) * SWA_GROUP], axis=1)
        o_g = _dot(v_aug, p_g)
        for j in range(SWA_GROUP):
            cols = slice(j * nq, (j + 1) * nq)
            l = o_g[HEAD_DIM:HEAD_DIM + 1, cols] + sink_terms[g * SWA_GROUP + j]
            outs.append(o_g[0:HEAD_DIM, cols] / l)
    pairs = [jnp.concatenate(outs[2 * p:2 * p + 2], axis=0).T for p in range(SWA_Q_HEADS // 2)]
    return jnp.concatenate(pairs, axis=1).astype(BF16)


def _swa_kernel(sink_ref, q_ref, kc_ref, kp_ref, vc_ref, vp_ref, km_ref, vm_ref, bias0_ref, bias1_ref, o_ref):
    n = BLOCK
    nsub = q_ref.shape[0] // n
    k_rows = [kp_ref[...]] + [kc_ref[c * n:(c + 1) * n] for c in range(nsub)]
    v_rows = [vp_ref[...]] + [vc_ref[c * n:(c + 1) * n] for c in range(nsub)]
    def block_scores(c):
        k_all = jnp.concatenate([k_rows[c], k_rows[c + 1], km_ref[...]], axis=0)
        bias = bias0_ref[...] if c == 0 else bias1_ref[...]
        return _swa_scores(q_ref[c * n:(c + 1) * n], k_all, bias)

    s_next = block_scores(0)
    for c in range(nsub):
        s_cur = s_next
        if c + 1 < nsub:
            s_next = block_scores(c + 1)
        v_all = jnp.concatenate([v_rows[c], v_rows[c + 1], vm_ref[...]], axis=0)
        o_ref[c * n:(c + 1) * n] = _swa_outputs(s_cur, v_all, sink_ref)


def _swa(proj, projm, bias, sinks):
    nb, rows, _ = proj.shape
    tq = SWA_TQ
    ratio = tq // BLOCK
    kcol, vcol = OFF_QA // SWA_KV_W, OFF_KA // SWA_KV_W
    cur = lambda c: (lambda b, n: (b, n, c))
    prev = lambda c: (lambda b, n: (b, jnp.maximum(n * ratio - 1, 0), c))
    bias_spec = lambda idx: pl.BlockSpec((None, SWA_KEYS, SWA_Q_HEADS * BLOCK), idx)
    return pl.pallas_call(
        _swa_kernel,
        grid=(nb, rows // tq),
        in_specs=[
            pl.BlockSpec(memory_space=pltpu.SMEM),
            pl.BlockSpec((None, tq, SWA_Q_W), lambda b, n: (b, n, 0)),
            pl.BlockSpec((None, tq, SWA_KV_W), cur(kcol)),
            pl.BlockSpec((None, BLOCK, SWA_KV_W), prev(kcol)),
            pl.BlockSpec((None, tq, SWA_KV_W), cur(vcol)),
            pl.BlockSpec((None, BLOCK, SWA_KV_W), prev(vcol)),
            pl.BlockSpec((None, N_META, SWA_KV_W), lambda b, n: (0, 0, kcol)),
            pl.BlockSpec((None, N_META, SWA_KV_W), lambda b, n: (0, 0, vcol)),
            bias_spec(lambda b, n: (jnp.minimum(n, 1), 0, 0)),
            bias_spec(lambda b, n: (1, 0, 0)),
        ],
        out_specs=pl.BlockSpec((None, tq, SWA_Q_W), lambda b, n: (b, n, 0)),
        out_shape=jax.ShapeDtypeStruct((nb, rows, SWA_Q_W), BF16),
        compiler_params=pltpu.CompilerParams(
            dimension_semantics=("arbitrary", "arbitrary"), vmem_limit_bytes=VMEM_LIMIT),
        name="swa",
    )(sinks, proj, proj, proj, proj, proj, projm, projm, bias, bias)


def _fox_kernel(flags_ref, qt_ref, ka_ref, vt_ref, kam_ref, vtm_ref,
                wo_ref, wgu_ref, wd_ref, acc_ref, wo_bf_ref, wgu_bf_ref, wd_bf_ref):
    wo_bf_ref[...] = wo_ref[...].astype(BF16)
    wgu_bf_ref[...] = wgu_ref[...].astype(BF16)
    wd_bf_ref[...] = wd_ref[...].astype(BF16)

    b, g, i = pl.program_id(0), pl.program_id(1), pl.program_id(2)
    t = acc_ref.shape[2]
    nh = acc_ref.shape[0]
    colmax = lambda x: jnp.max(x, axis=0, keepdims=True)
    tile_rows = lambda j: pl.ds(pl.multiple_of(j * t, t), t)

    safe = flags_ref[(b * pl.num_programs(2) + i) * pl.num_programs(1) + g] > 0
    qt = [qt_ref[h] for h in range(nh)]

    def scores(h, j, diagonal):
        s = _dot(ka_ref[h, tile_rows(j), 0:FOX_K], qt[h])
        if diagonal:
            key = lax.broadcasted_iota(jnp.int32, (t, t), 0)
            qry = lax.broadcasted_iota(jnp.int32, (t, t), 1)
            s = jnp.where(key <= qry, s, NEG_INF)
        return s

    @pl.when(safe)
    def _():
        def run(work):
            prev = None
            for score_fn, pv_fn in work:
                s = score_fn()
                if prev is not None:
                    prev[0](prev[1])
                prev = (pv_fn, s)
            prev[0](prev[1])

        def tile_work(j):
            def pv(h):
                def apply(s):
                    acc_ref[h] += _dot(vt_ref[j, h], jnp.exp2(s).astype(BF16))
                return apply
            return [(lambda h=h: scores(h, j, False), pv(h)) for h in range(nh)]

        def first_work():
            def sc(h):
                return jnp.concatenate([scores(h, i, True), _dot(kam_ref[h, :, 0:FOX_K], qt[h])], axis=0)
            def pv(h):
                def apply(s):
                    v_t = jnp.concatenate([vt_ref[i, h], vtm_ref[h, :, 0:N_META]], axis=1)
                    acc_ref[h] = _dot(v_t, jnp.exp2(s).astype(BF16))
                return apply
            return [(lambda h=h: sc(h), pv(h)) for h in range(nh)]

        run(first_work())

        def tile_pair(p, carry):
            run(tile_work(2 * p) + tile_work(2 * p + 1))
            return carry

        lax.fori_loop(0, i // 2, tile_pair, 0)

        @pl.when(i % 2 == 1)
        def _():
            run(tile_work(i - 1))

    @pl.when(jnp.logical_not(safe))
    def _():
        m0 = []
        for h in range(nh):
            s = _dot(kam_ref[h, :, 0:FOX_K], qt[h])
            m = colmax(s)
            acc_ref[h] = _dot(vtm_ref[h, :, 0:N_META], jnp.exp2(s - m).astype(BF16))
            m0.append(m)

        def tile(j, ms, diagonal):
            out = []
            ss = [scores(h, j, diagonal) for h in range(nh)]
            for h in range(nh):
                s = ss[h]
                m_next = jnp.maximum(ms[h], colmax(s))
                p = jnp.exp2(s - m_next).astype(BF16)
                acc_ref[h] = jnp.exp2(ms[h] - m_next) * acc_ref[h] + _dot(vt_ref[j, h], p)
                out.append(m_next)
            return tuple(out)

        ms = lax.fori_loop(0, i, lambda j, ms: tile(j, ms, False), tuple(m0))
        tile(i, ms, True)


def _fox(flags, qt, ka, vt, ka_m, vt_m, w_out, w_gate_up, w_down):
    nb, _, rows, _ = ka.shape
    t = FOX_T
    nh = FOX_GROUP
    ngroup = FOX_HEADS // nh
    nq = rows // t
    nsteps = nb * ngroup * nq
    slab = lambda w: min(r for r in range(SLAB_ALIGN, w.shape[0] + 1, SLAB_ALIGN)
                         if w.shape[0] % r == 0 and r * nsteps >= w.shape[0])
    step = lambda b, g, i: (b * ngroup + g) * nq + i
    w_spec = lambda w: pl.BlockSpec((slab(w), w.shape[1]),
                                    lambda b, g, i, f: (jnp.minimum(step(b, g, i), w.shape[0] // slab(w) - 1), 0))
    return pl.pallas_call(
        _fox_kernel,
        grid_spec=pltpu.PrefetchScalarGridSpec(
            num_scalar_prefetch=1,
            grid=(nb, ngroup, nq),
            in_specs=[
                pl.BlockSpec((None, None, nh, FOX_K, t), lambda b, g, i, f: (b, i, g, 0, 0)),
                pl.BlockSpec((None, nh, rows, PAIR_W), lambda b, g, i, f: (b, g, 0, 0)),
                pl.BlockSpec((None, nq, nh, FOX_VROWS, t), lambda b, g, i, f: (b, 0, g, 0, 0)),
                pl.BlockSpec((None, nh, N_META, PAIR_W), lambda b, g, i, f: (0, g, 0, 0)),
                pl.BlockSpec((None, None, nh, FOX_VROWS, META_TM), lambda b, g, i, f: (0, 0, g, 0, 0)),
                w_spec(w_out), w_spec(w_gate_up), w_spec(w_down),
            ],
            out_specs=[pl.BlockSpec((None, nh, FOX_VROWS, t), lambda b, g, i, f: (b, g, 0, i)),
                       w_spec(w_out), w_spec(w_gate_up), w_spec(w_down)],
        ),
        out_shape=[jax.ShapeDtypeStruct((nb, FOX_HEADS, FOX_VROWS, rows), F32)]
        + [jax.ShapeDtypeStruct(w.shape, BF16) for w in (w_out, w_gate_up, w_down)],
        compiler_params=pltpu.CompilerParams(
            dimension_semantics=("arbitrary", "arbitrary", "arbitrary"), vmem_limit_bytes=VMEM_LIMIT),
        name="fox",
    )(flags, qt, ka, vt, ka_m, vt_m, w_out, w_gate_up, w_down)


def _ffn_kernel(oa_ref, fox_ref, x_ref, wo_ref, g1_ref, g2_ref, wg_ref, wu_ref, wd_ref, g3_ref, out_ref):
    tm = x_ref.shape[0]
    parts = [slice(c * tm // FFN_PARTS, (c + 1) * tm // FFN_PARTS) for c in range(FFN_PARTS)]
    def fox_rows(r):
        pairs = []
        for p in range(FOX_HEADS // 2):
            o_t = jnp.concatenate([fox_ref[h, 0:HEAD_DIM, r] / fox_ref[h, HEAD_DIM:HEAD_DIM + 1, r]
                                   for h in (2 * p, 2 * p + 1)], axis=0)
            pairs.append(o_t.T.astype(BF16))
        return jnp.concatenate(pairs, axis=1)

    mix = [jnp.concatenate([oa_ref[r, :], fox_rows(r)], axis=1) for r in parts]
    a = [_dot(m, wo_ref[...]) for m in mix]
    h1 = [x_ref[r, :] + _rms(ai, g1_ref[...]) for r, ai in zip(parts, a)]
    hn = [_rms(h, g2_ref[...]).astype(BF16) for h in h1]
    gate_up = [(_dot(h, wg_ref[...]), _dot(h, wu_ref[...])) for h in hn]
    act = [(g / (1.0 + jnp.exp(-g)) * u).astype(BF16) for g, u in gate_up]
    ff = [_dot(ac, wd_ref[...]) for ac in act]
    for r, h, f in zip(parts, h1, ff):
        out_ref[r, :] = h + _rms(f, g3_ref[...])


def _ffn(o_a, o_b, x3, wo, g1, g2, wgu, wd, g3):
    nb, rows, _ = x3.shape
    tm = FFN_TM
    const = lambda b, t: (0, 0)
    resident = lambda shape: pl.BlockSpec(shape, const, pipeline_mode=pl.Buffered(1))
    row = lambda w: pl.BlockSpec((None, tm, w), lambda b, t: (b, t, 0))
    return pl.pallas_call(
        _ffn_kernel,
        grid=(nb, rows // tm),
        in_specs=[
            row(SWA_Q_W), pl.BlockSpec((None, FOX_HEADS, FOX_VROWS, tm), lambda b, t: (b, 0, 0, t)), row(D_MODEL),
            resident((D_MODEL, D_MODEL)), resident((1, D_MODEL)), resident((1, D_MODEL)),
            pl.BlockSpec((D_MODEL, D_FF), lambda b, t: (0, 0), pipeline_mode=pl.Buffered(1)),
            pl.BlockSpec((D_MODEL, D_FF), lambda b, t: (0, 1), pipeline_mode=pl.Buffered(1)),
            resident((D_FF, D_MODEL)),
            resident((1, D_MODEL)),
        ],
        out_specs=row(D_MODEL),
        out_shape=jax.ShapeDtypeStruct((nb, rows, D_MODEL), F32),
        compiler_params=pltpu.CompilerParams(
            dimension_semantics=("arbitrary", "arbitrary"), vmem_limit_bytes=VMEM_LIMIT),
        name="outproj_ffn",
    )(o_a, o_b, x3, wo, g1, g2, wgu, wgu, wd, g3)


def kernel(x, meta_tokens, rel_bias, ln_pre_mix, ln_post_mix, ln_pre_ffn, ln_post_ffn,
           w_in, b_forget, sinks, w_out, w_gate_up, w_down):
    nb, seq, d = x.shape
    assert IN_TM == FOX_T, "the in-proj tile's running key-norm maximum is the fox query tile's bound"
    assert d == D_MODEL and seq % IN_TM == 0 and seq % FOX_T == 0 and seq % FFN_TM == 0 and seq % SWA_TQ == 0
    assert w_in.shape[0] == 1, "single-layer block"
    assert meta_tokens.shape == (N_META, D_MODEL)
    x = x.astype(F32)

    w_all = w_in[0].astype(BF16)
    pad_gate = GATE_ROWS - FOX_HEADS
    w_ft = jnp.pad(w_in[0, :, OFF_VB:], ((0, 0), (0, pad_gate))).astype(BF16)
    lane_rep = lambda v: jnp.broadcast_to(jnp.pad(v.astype(F32), (0, pad_gate))[:, None], (GATE_ROWS, LANES))
    b_ft = lane_rep(b_forget[0])
    g_pre = ln_pre_mix[0].astype(F32).reshape(1, D_MODEL)

    swa_bias = _bias_tables(rel_bias)

    x_m = jnp.pad(meta_tokens.astype(F32), ((0, META_TM - N_META), (0, 0)))[None]
    pa_m, _, ka_m, vt_m, kpre_m, _, c_meta = _inproj(x_m, g_pre, w_all, w_ft, b_ft, jnp.zeros((GATE_ROWS, LANES), F32),
                                                     jnp.zeros((FOX_HEADS, LANES), F32), META_TM)
    pa, qt, ka, vt, _, rmax, _ = _inproj(x, g_pre, w_all, w_ft, b_ft, c_meta[0, 0], kpre_m[0, 0], IN_TM)
    shift = rmax[..., 0].reshape(nb, seq // IN_TM, FOX_HEADS // FOX_GROUP, FOX_GROUP)
    flags = (jnp.max(shift, axis=-1) <= FOX_SAFE_LOG2).astype(jnp.int32).reshape(-1)

    o_a = _swa(pa, pa_m, swa_bias, sinks[0].astype(F32))
    o_b, w_out_bf, w_gu_bf, w_down_bf = _fox(flags, qt, ka, vt, ka_m, vt_m,
                                             w_out[0].astype(F32), w_gate_up[0].astype(F32),
                                             w_down[0].astype(F32))

    return _ffn(o_a, o_b, x,
                w_out_bf,
                ln_post_mix[0].astype(F32).reshape(1, D_MODEL),
                ln_pre_ffn[0].astype(F32).reshape(1, D_MODEL),
                w_gu_bf,
                w_down_bf,
                ln_post_ffn[0].astype(F32).reshape(1, D_MODEL))
```
